```python
import jax, jax.numpy as jnp
from jax import lax
import numpy as np

D_MODEL = 4096
BATCH = 8
SEQ = 4096
DEPTH = 1

ATT_HEAD_DIM = 128
ATT_Q_HEADS = 32
ATT_KV_HEADS = 8
ATT_GROUP = ATT_Q_HEADS // ATT_KV_HEADS
ATT_Q_DIM = ATT_Q_HEADS * ATT_HEAD_DIM
ATT_KV_DIM = ATT_KV_HEADS * ATT_HEAD_DIM
ATT_QKV_DIM = ATT_Q_DIM + 2 * ATT_KV_DIM
WINDOW = 128
BLOCK = 128
ROPE_THETA = 10000.0
NEG_INF = -1e30

RWKV_HEAD = 64
RWKV_DIM = D_MODEL
RWKV_HEADS = RWKV_DIM // RWKV_HEAD
D_DECAY = max(32, int(round(1.8 * RWKV_DIM ** 0.5 / 32)) * 32)
D_AAA = max(32, int(round(1.8 * RWKV_DIM ** 0.5 / 32)) * 32)
D_GATE = max(32, int(round(0.6 * RWKV_DIM ** 0.8 / 32)) * 32)
RWKV_SHIFT_DIM = 3 * RWKV_DIM + D_DECAY + D_AAA + D_GATE
RWKV_SPLITS = [RWKV_DIM, 2 * RWKV_DIM, 3 * RWKV_DIM, 3 * RWKV_DIM + D_DECAY, 3 * RWKV_DIM + D_DECAY + D_AAA]
GN_EPS = 64e-5

IN_DIM = ATT_QKV_DIM + RWKV_SHIFT_DIM + 2 * D_MODEL

FFN_DIM = ((8 * D_MODEL + 3 * 256 - 1) // (3 * 256)) * 256
RMS_EPS = 1e-6

kernel_name = 'hybrid_swa_sinks_rwkv7_gated_block'


def _rmsnorm(x, g):
    xf = x.astype(jnp.float32)
    y = xf * lax.rsqrt(jnp.mean(xf * xf, axis=-1, keepdims=True) + RMS_EPS)
    return (y * g.astype(jnp.float32)).astype(x.dtype)


def _rope_tables(seq, dtype):
    pos = jnp.arange(seq, dtype=jnp.float32)
    inv_freq = ROPE_THETA ** (-jnp.arange(0, ATT_HEAD_DIM, 2, dtype=jnp.float32) / ATT_HEAD_DIM)
    ang = pos[:, None] * inv_freq[None, :]
    return jnp.cos(ang).astype(dtype), jnp.sin(ang).astype(dtype)


def _rope(t, cos, sin):
    t1, t2 = jnp.split(t, 2, axis=-1)
    c = cos[None, :, None, :]
    s = sin[None, :, None, :]
    return jnp.concatenate([t1 * c - t2 * s, t2 * c + t1 * s], axis=-1)


def _swa_sinks(q, k, v, sinks):
    B, S = q.shape[0], q.shape[1]
    nb = S // BLOCK
    qb = q.reshape(B, nb, BLOCK, ATT_KV_HEADS, ATT_GROUP, ATT_HEAD_DIM)

    def band(t):
        tb = t.reshape(B, nb, BLOCK, ATT_KV_HEADS, ATT_HEAD_DIM)
        prev = jnp.pad(tb[:, :-1], ((0, 0), (1, 0), (0, 0), (0, 0), (0, 0)))
        return jnp.concatenate([prev, tb], axis=2)

    kb, vb = band(k), band(v)
    scores = jnp.einsum('bnqhgd,bnkhd->bnhgqk', qb, kb,
                        preferred_element_type=jnp.float32) * (ATT_HEAD_DIM ** -0.5)
    qi = jnp.arange(BLOCK)[:, None]
    kj = jnp.arange(2 * BLOCK)[None, :]
    rel = qi + BLOCK - kj
    key_pos = jnp.arange(nb)[:, None, None] * BLOCK + kj[None] - BLOCK
    mask = (rel >= 0) & (rel < WINDOW) & (key_pos >= 0)
    scores = jnp.where(mask[None, :, None, None], scores, NEG_INF)
    sink = sinks.astype(jnp.float32).reshape(ATT_KV_HEADS, ATT_GROUP)[None, None, :, :, None, None]
    m = jnp.maximum(jnp.max(scores, axis=-1, keepdims=True), sink)
    p = jnp.exp(scores - m)
    probs = p / (jnp.sum(p, axis=-1, keepdims=True) + jnp.exp(sink - m))
    o = jnp.einsum('bnhgqk,bnkhd->bnqhgd', probs.astype(v.dtype), vb)
    return o.reshape(B, S, ATT_Q_DIM)


def _wkv7_step(state, inp):
    r, w, k, v, a, b = inp
    sa = jnp.einsum('bhvk,bhk->bhv', state, a)
    state = state * w[:, :, None, :] + sa[..., None] * b[:, :, None, :] + v[..., None] * k[:, :, None, :]
    y = jnp.einsum('bhvk,bhk->bhv', state, r)
    return state, y


def _rwkv7(p_r, p_k, p_v, p_w, p_a, p_g, w0, w2, a0, a2, g2, k_k, k_a, r_k, ln_w, ln_b):
    B, S, C = p_r.shape
    H, N = RWKV_HEADS, RWKV_HEAD
    f32 = jnp.float32
    heads = lambda t: t.astype(f32).reshape(B, S, H, N)
    w = -jax.nn.softplus(-(w0 + jnp.tanh(p_w) @ w2)) - 0.5
    a = jax.nn.sigmoid(a0 + p_a @ a2)
    g = jax.nn.sigmoid(p_g) @ g2
    kk = heads(p_k * k_k)
    kk = kk / jnp.maximum(jnp.sqrt(jnp.sum(kk * kk, axis=-1, keepdims=True)), 1e-12)
    k = heads(p_k * (1.0 + (a - 1.0) * k_a))
    r = heads(p_r)
    v = heads(p_v)
    decay = jnp.exp(-jnp.exp(heads(w)))
    a_vec = -kk
    b_vec = kk * heads(a)
    tm = lambda t: jnp.moveaxis(t, 1, 0)
    state0 = jnp.zeros((B, H, N, N), f32)
    _, y = lax.scan(_wkv7_step, state0, (tm(r), tm(decay), tm(k), tm(v), tm(a_vec), tm(b_vec)))
    y = jnp.moveaxis(y, 0, 1)
    mu = jnp.mean(y, axis=-1, keepdims=True)
    var = jnp.mean(jnp.square(y - mu), axis=-1, keepdims=True)
    y = ((y - mu) * lax.rsqrt(var + GN_EPS)).reshape(B, S, C)
    y = y * ln_w.astype(f32) + ln_b.astype(f32)
    bonus = (jnp.sum(r * k * r_k.astype(f32), axis=-1, keepdims=True) * v).reshape(B, S, C)
    return ((y + bonus) * g.astype(f32)).astype(p_r.dtype)


def _layer(x, norm_mix_pre, norm_mix_post, norm_ffn_pre, norm_ffn_post, w_in, b_qkv, att_sinks,
           mu_shift, w0, w2, a0, a2, g2, k_k, k_a, r_k, ln_x_w, ln_x_b,
           w_att_branch, w_rwkv_branch, w_out, w_ffn_gate, w_ffn_up, w_ffn_down):
    B, S, _ = x.shape
    h = _rmsnorm(x, norm_mix_pre)
    proj = h @ w_in
    att_cols, rwkv_cols, gate_cols = jnp.split(
        proj, [ATT_QKV_DIM, ATT_QKV_DIM + RWKV_SHIFT_DIM], axis=-1)

    att_cols = att_cols + b_qkv
    q, k, v = jnp.split(att_cols, [ATT_Q_DIM, ATT_Q_DIM + ATT_KV_DIM], axis=-1)
    q = q.reshape(B, S, ATT_Q_HEADS, ATT_HEAD_DIM)
    k = k.reshape(B, S, ATT_KV_HEADS, ATT_HEAD_DIM)
    v = v.reshape(B, S, ATT_KV_HEADS, ATT_HEAD_DIM)
    cos, sin = _rope_tables(S, q.dtype)
    o_att = _swa_sinks(_rope(q, cos, sin), _rope(k, cos, sin), v, att_sinks)

    prev = jnp.pad(rwkv_cols[:, :-1], ((0, 0), (1, 0), (0, 0)))
    rwkv_cols = rwkv_cols + (prev - rwkv_cols) * mu_shift
    p_r, p_k, p_v, p_w, p_a, p_g = jnp.split(rwkv_cols, RWKV_SPLITS, axis=-1)
    o_rwkv = _rwkv7(p_r, p_k, p_v, p_w, p_a, p_g, w0, w2, a0, a2, g2, k_k, k_a, r_k, ln_x_w, ln_x_b)

    g_att, g_rwkv = jnp.split(gate_cols, 2, axis=-1)
    merged = (jax.nn.sigmoid(g_att) * (o_att @ w_att_branch)
              + jax.nn.sigmoid(g_rwkv) * (o_rwkv @ w_rwkv_branch))
    x = x + _rmsnorm(merged @ w_out, norm_mix_post)

    h = _rmsnorm(x, norm_ffn_pre)
    f = (jax.nn.silu(h @ w_ffn_gate) * (h @ w_ffn_up)) @ w_ffn_down
    return x + _rmsnorm(f, norm_ffn_post)


def _normal(k, shape, scale):
    return jax.random.normal(k, shape, jnp.float32) * scale


def _fwd_setup_inputs(seed: int = 0) -> dict:
    key = jax.random.key(seed)
    ks = jax.random.split(key, 26)
    L = DEPTH
    gain = lambda k: 1.0 + _normal(k, (L, D_MODEL), 0.02)
    return {
        'x': _normal(ks[0], (BATCH, SEQ, D_MODEL), 1.0),
        'norm_mix_pre': gain(ks[1]),
        'norm_mix_post': gain(ks[2]),
        'norm_ffn_pre': gain(ks[3]),
        'norm_ffn_post': gain(ks[4]),
        'w_in': _normal(ks[5], (L, D_MODEL, IN_DIM), D_MODEL ** -0.5),
        'b_qkv': _normal(ks[6], (L, ATT_QKV_DIM), 0.02),
        'att_sinks': _normal(ks[7], (L, ATT_Q_HEADS), 0.5),
        'mu_shift': jax.random.uniform(ks[8], (L, RWKV_SHIFT_DIM), jnp.float32, 0.0, 1.0),
        'w0': jax.random.uniform(ks[9], (L, RWKV_DIM), jnp.float32, -3.0, 0.0),
        'w2': _normal(ks[10], (L, D_DECAY, RWKV_DIM), 0.1 * D_DECAY ** -0.5),
        'a0': _normal(ks[11], (L, RWKV_DIM), 0.1),
        'a2': _normal(ks[12], (L, D_AAA, RWKV_DIM), 0.5 * D_AAA ** -0.5),
        'g2': _normal(ks[13], (L, D_GATE, RWKV_DIM), D_GATE ** -0.5),
        'k_k': 0.85 + _normal(ks[14], (L, RWKV_DIM), 0.02),
        'k_a': 1.0 + _normal(ks[15], (L, RWKV_DIM), 0.02),
        'r_k': _normal(ks[16], (L, RWKV_HEADS, RWKV_HEAD), 0.1),
        'ln_x_w': 1.0 + _normal(ks[17], (L, RWKV_DIM), 0.02),
        'ln_x_b': _normal(ks[18], (L, RWKV_DIM), 0.02),
        'w_att_branch': _normal(ks[19], (L, ATT_Q_DIM, D_MODEL), ATT_Q_DIM ** -0.5),
        'w_rwkv_branch': _normal(ks[20], (L, RWKV_DIM, D_MODEL), RWKV_DIM ** -0.5),
        'w_out': _normal(ks[21], (L, D_MODEL, D_MODEL), D_MODEL ** -0.5),
        'w_ffn_gate': _normal(ks[22], (L, D_MODEL, FFN_DIM), D_MODEL ** -0.5),
        'w_ffn_up': _normal(ks[23], (L, D_MODEL, FFN_DIM), D_MODEL ** -0.5),
        'w_ffn_down': _normal(ks[24], (L, FFN_DIM, D_MODEL), FFN_DIM ** -0.5),
    }


def _fwd_reference(x, norm_mix_pre, norm_mix_post, norm_ffn_pre, norm_ffn_post, w_in, b_qkv, att_sinks,
              mu_shift, w0, w2, a0, a2, g2, k_k, k_a, r_k, ln_x_w, ln_x_b,
              w_att_branch, w_rwkv_branch, w_out, w_ffn_gate, w_ffn_up, w_ffn_down):
    for l in range(DEPTH):
        x = _layer(x, norm_mix_pre[l], norm_mix_post[l], norm_ffn_pre[l], norm_ffn_post[l],
                   w_in[l], b_qkv[l], att_sinks[l], mu_shift[l], w0[l], w2[l], a0[l], a2[l],
                   g2[l], k_k[l], k_a[l], r_k[l], ln_x_w[l], ln_x_b[l],
                   w_att_branch[l], w_rwkv_branch[l], w_out[l],
                   w_ffn_gate[l], w_ffn_up[l], w_ffn_down[l])
    return x


import jax as _jax
import jax.numpy as _jnp

TWIN_FORMAT = 'train_step'
FWD_PARAMS = ['x', 'norm_mix_pre', 'norm_mix_post', 'norm_ffn_pre', 'norm_ffn_post', 'w_in', 'b_qkv', 'att_sinks', 'mu_shift', 'w0', 'w2', 'a0', 'a2', 'g2', 'k_k', 'k_a', 'r_k', 'ln_x_w', 'ln_x_b', 'w_att_branch', 'w_rwkv_branch', 'w_out', 'w_ffn_gate', 'w_ffn_up', 'w_ffn_down']
TWIN_WEIGHTS = ['norm_mix_pre', 'norm_mix_post', 'norm_ffn_pre', 'norm_ffn_post', 'w_in', 'b_qkv', 'att_sinks', 'mu_shift', 'w0', 'w2', 'a0', 'a2', 'g2', 'k_k', 'k_a', 'r_k', 'ln_x_w', 'ln_x_b', 'w_att_branch', 'w_rwkv_branch', 'w_out', 'w_ffn_gate', 'w_ffn_up', 'w_ffn_down']
TWIN_DIFF_INPUT = 'x'
TWIN_INPUTS = ['x', 'norm_mix_pre', 'norm_mix_post', 'norm_ffn_pre', 'norm_ffn_post', 'w_in', 'b_qkv', 'att_sinks', 'mu_shift', 'w0', 'w2', 'a0', 'a2', 'g2', 'k_k', 'k_a', 'r_k', 'ln_x_w', 'ln_x_b', 'w_att_branch', 'w_rwkv_branch', 'w_out', 'w_ffn_gate', 'w_ffn_up', 'w_ffn_down', 'loss_target', 'm_norm_mix_pre', 'm_norm_mix_post', 'm_norm_ffn_pre', 'm_norm_ffn_post', 'm_w_in', 'm_b_qkv', 'm_att_sinks', 'm_mu_shift', 'm_w0', 'm_w2', 'm_a0', 'm_a2', 'm_g2', 'm_k_k', 'm_k_a', 'm_r_k', 'm_ln_x_w', 'm_ln_x_b', 'm_w_att_branch', 'm_w_rwkv_branch', 'm_w_out', 'm_w_ffn_gate', 'm_w_ffn_up', 'm_w_ffn_down', 'v_norm_mix_pre', 'v_norm_mix_post', 'v_norm_ffn_pre', 'v_norm_ffn_post', 'v_w_in', 'v_b_qkv', 'v_att_sinks', 'v_mu_shift', 'v_w0', 'v_w2', 'v_a0', 'v_a2', 'v_g2', 'v_k_k', 'v_k_a', 'v_r_k', 'v_ln_x_w', 'v_ln_x_b', 'v_w_att_branch', 'v_w_rwkv_branch', 'v_w_out', 'v_w_ffn_gate', 'v_w_ffn_up', 'v_w_ffn_down']
TWIN_OUTPUTS = ['loss', 'grad_x', 'grad_norm_mix_pre', 'grad_norm_mix_post', 'grad_norm_ffn_pre', 'grad_norm_ffn_post', 'grad_w_in', 'grad_b_qkv', 'grad_att_sinks', 'grad_mu_shift', 'grad_w0', 'grad_w2', 'grad_a0', 'grad_a2', 'grad_g2', 'grad_k_k', 'grad_k_a', 'grad_r_k', 'grad_ln_x_w', 'grad_ln_x_b', 'grad_w_att_branch', 'grad_w_rwkv_branch', 'grad_w_out', 'grad_w_ffn_gate', 'grad_w_ffn_up', 'grad_w_ffn_down', 'delta_norm_mix_pre', 'delta_norm_mix_post', 'delta_norm_ffn_pre', 'delta_norm_ffn_post', 'delta_w_in', 'delta_b_qkv', 'delta_att_sinks', 'delta_mu_shift', 'delta_w0', 'delta_w2', 'delta_a0', 'delta_a2', 'delta_g2', 'delta_k_k', 'delta_k_a', 'delta_r_k', 'delta_ln_x_w', 'delta_ln_x_b', 'delta_w_att_branch', 'delta_w_rwkv_branch', 'delta_w_out', 'delta_w_ffn_gate', 'delta_w_ffn_up', 'delta_w_ffn_down', 'new_m_norm_mix_pre', 'new_m_norm_mix_post', 'new_m_norm_ffn_pre', 'new_m_norm_ffn_post', 'new_m_w_in', 'new_m_b_qkv', 'new_m_att_sinks', 'new_m_mu_shift', 'new_m_w0', 'new_m_w2', 'new_m_a0', 'new_m_a2', 'new_m_g2', 'new_m_k_k', 'new_m_k_a', 'new_m_r_k', 'new_m_ln_x_w', 'new_m_ln_x_b', 'new_m_w_att_branch', 'new_m_w_rwkv_branch', 'new_m_w_out', 'new_m_w_ffn_gate', 'new_m_w_ffn_up', 'new_m_w_ffn_down', 'new_v_norm_mix_pre', 'new_v_norm_mix_post', 'new_v_norm_ffn_pre', 'new_v_norm_ffn_post', 'new_v_w_in', 'new_v_b_qkv', 'new_v_att_sinks', 'new_v_mu_shift', 'new_v_w0', 'new_v_w2', 'new_v_a0', 'new_v_a2', 'new_v_g2', 'new_v_k_k', 'new_v_k_a', 'new_v_r_k', 'new_v_ln_x_w', 'new_v_ln_x_b', 'new_v_w_att_branch', 'new_v_w_rwkv_branch', 'new_v_w_out', 'new_v_w_ffn_gate', 'new_v_w_ffn_up', 'new_v_w_ffn_down']
TWIN_LEAF_KINDS = {'loss': 'loss', 'grad_x': 'grad_x', 'grad_norm_mix_pre': 'grad_w', 'grad_norm_mix_post': 'grad_w', 'grad_norm_ffn_pre': 'grad_w', 'grad_norm_ffn_post': 'grad_w', 'grad_w_in': 'grad_w', 'grad_b_qkv': 'grad_w', 'grad_att_sinks': 'grad_w', 'grad_mu_shift': 'grad_w', 'grad_w0': 'grad_w', 'grad_w2': 'grad_w', 'grad_a0': 'grad_w', 'grad_a2': 'grad_w', 'grad_g2': 'grad_w', 'grad_k_k': 'grad_w', 'grad_k_a': 'grad_w', 'grad_r_k': 'grad_w', 'grad_ln_x_w': 'grad_w', 'grad_ln_x_b': 'grad_w', 'grad_w_att_branch': 'grad_w', 'grad_w_rwkv_branch': 'grad_w', 'grad_w_out': 'grad_w', 'grad_w_ffn_gate': 'grad_w', 'grad_w_ffn_up': 'grad_w', 'grad_w_ffn_down': 'grad_w', 'delta_norm_mix_pre': 'delta_w', 'delta_norm_mix_post': 'delta_w', 'delta_norm_ffn_pre': 'delta_w', 'delta_norm_ffn_post': 'delta_w', 'delta_w_in': 'delta_w', 'delta_b_qkv': 'delta_w', 'delta_att_sinks': 'delta_w', 'delta_mu_shift': 'delta_w', 'delta_w0': 'delta_w', 'delta_w2': 'delta_w', 'delta_a0': 'delta_w', 'delta_a2': 'delta_w', 'delta_g2': 'delta_w', 'delta_k_k': 'delta_w', 'delta_k_a': 'delta_w', 'delta_r_k': 'delta_w', 'delta_ln_x_w': 'delta_w', 'delta_ln_x_b': 'delta_w', 'delta_w_att_branch': 'delta_w', 'delta_w_rwkv_branch': 'delta_w', 'delta_w_out': 'delta_w', 'delta_w_ffn_gate': 'delta_w', 'delta_w_ffn_up': 'delta_w', 'delta_w_ffn_down': 'delta_w', 'new_m_norm_mix_pre': 'new_m', 'new_m_norm_mix_post': 'new_m', 'new_m_norm_ffn_pre': 'new_m', 'new_m_norm_ffn_post': 'new_m', 'new_m_w_in': 'new_m', 'new_m_b_qkv': 'new_m', 'new_m_att_sinks': 'new_m', 'new_m_mu_shift': 'new_m', 'new_m_w0': 'new_m', 'new_m_w2': 'new_m', 'new_m_a0': 'new_m', 'new_m_a2': 'new_m', 'new_m_g2': 'new_m', 'new_m_k_k': 'new_m', 'new_m_k_a': 'new_m', 'new_m_r_k': 'new_m', 'new_m_ln_x_w': 'new_m', 'new_m_ln_x_b': 'new_m', 'new_m_w_att_branch': 'new_m', 'new_m_w_rwkv_branch': 'new_m', 'new_m_w_out': 'new_m', 'new_m_w_ffn_gate': 'new_m', 'new_m_w_ffn_up': 'new_m', 'new_m_w_ffn_down': 'new_m', 'new_v_norm_mix_pre': 'new_v', 'new_v_norm_mix_post': 'new_v', 'new_v_norm_ffn_pre': 'new_v', 'new_v_norm_ffn_post': 'new_v', 'new_v_w_in': 'new_v', 'new_v_b_qkv': 'new_v', 'new_v_att_sinks': 'new_v', 'new_v_mu_shift': 'new_v', 'new_v_w0': 'new_v', 'new_v_w2': 'new_v', 'new_v_a0': 'new_v', 'new_v_a2': 'new_v', 'new_v_g2': 'new_v', 'new_v_k_k': 'new_v', 'new_v_k_a': 'new_v', 'new_v_r_k': 'new_v', 'new_v_ln_x_w': 'new_v', 'new_v_ln_x_b': 'new_v', 'new_v_w_att_branch': 'new_v', 'new_v_w_rwkv_branch': 'new_v', 'new_v_w_out': 'new_v', 'new_v_w_ffn_gate': 'new_v', 'new_v_w_ffn_up': 'new_v', 'new_v_w_ffn_down': 'new_v'}


def _forward(args):
    return _fwd_reference(*[args[k] for k in FWD_PARAMS])


def _output_shape():
    out = _jax.eval_shape(lambda: _forward(_fwd_setup_inputs(0)))
    return out.shape, out.dtype

N_MICROBATCH = 1
ADAM_LR = 0.001
ADAM_B1 = 0.9
ADAM_B2 = 0.999
ADAM_EPS = 1e-08
ADAM_WD = 0.01
ADAM_STEP = 10
PER_EXAMPLE_BATCH_AXIS = {'x': 0, 'loss_target': 0}
SHARED_INPUTS = []
_WEIGHT_DTYPES = {'norm_mix_pre': _jnp.float32, 'norm_mix_post': _jnp.float32, 'norm_ffn_pre': _jnp.float32, 'norm_ffn_post': _jnp.float32, 'w_in': _jnp.float32, 'b_qkv': _jnp.float32, 'att_sinks': _jnp.float32, 'mu_shift': _jnp.float32, 'w0': _jnp.float32, 'w2': _jnp.float32, 'a0': _jnp.float32, 'a2': _jnp.float32, 'g2': _jnp.float32, 'k_k': _jnp.float32, 'k_a': _jnp.float32, 'r_k': _jnp.float32, 'ln_x_w': _jnp.float32, 'ln_x_b': _jnp.float32, 'w_att_branch': _jnp.float32, 'w_rwkv_branch': _jnp.float32, 'w_out': _jnp.float32, 'w_ffn_gate': _jnp.float32, 'w_ffn_up': _jnp.float32, 'w_ffn_down': _jnp.float32}
MOMENT_SCALE = {'norm_mix_pre': 2.085176e-01, 'norm_mix_post': 7.994451e+00, 'norm_ffn_pre': 1.133512e-01, 'norm_ffn_post': 8.002339e+00, 'w_in': 8.058290e-02, 'b_qkv': 2.411864e-01, 'att_sinks': 2.165270e-02, 'mu_shift': 1.904664e-01, 'w0': 5.614357e-02, 'w2': 1.126735e-02, 'a0': 4.854282e-02, 'a2': 4.266424e-02, 'g2': 1.066008e-01, 'k_k': 4.017092e-02, 'k_a': 1.153794e-01, 'r_k': 2.363971e-01, 'ln_x_w': 1.107765e-01, 'ln_x_b': 1.691995e-01, 'w_att_branch': 3.671471e-02, 'w_rwkv_branch': 1.122281e-01, 'w_out': 1.156782e-01, 'w_ffn_gate': 4.837338e-02, 'w_ffn_up': 5.326521e-02, 'w_ffn_down': 8.754863e-02}


def _to_microbatches(a, axis):
    t = _jnp.moveaxis(a, axis, 0)
    t = t.reshape((N_MICROBATCH, t.shape[0] // N_MICROBATCH) + t.shape[1:])
    return _jnp.moveaxis(t, 1, axis + 1)


def setup_inputs(seed: int = 0) -> dict:
    inp = _fwd_setup_inputs(seed)
    key = _jax.random.fold_in(_jax.random.key(seed), 7919)
    shape, _ = _output_shape()
    out = dict(inp)
    out["loss_target"] = _jax.random.normal(_jax.random.fold_in(key, 0), shape, _jnp.float32)
    for i, name in enumerate(TWIN_WEIGHTS):
        w = inp[name].astype(_jnp.float32)
        if MOMENT_SCALE is None:
            s = _jnp.sqrt(_jnp.mean(_jnp.square(w)) + 1e-30)
        else:
            s = MOMENT_SCALE[name]
        km, kv = _jax.random.split(_jax.random.fold_in(key, i + 1))
        out[name] = w
        out["m_" + name] = s * _jax.random.normal(km, w.shape, _jnp.float32)
        out["v_" + name] = (s * s) * _jax.random.uniform(kv, w.shape, _jnp.float32, 0.5, 1.5)
    if N_MICROBATCH > 1:
        for name, axis in PER_EXAMPLE_BATCH_AXIS.items():
            out[name] = _to_microbatches(out[name], axis)
    return {'x': out['x'], 'norm_mix_pre': out['norm_mix_pre'], 'norm_mix_post': out['norm_mix_post'], 'norm_ffn_pre': out['norm_ffn_pre'], 'norm_ffn_post': out['norm_ffn_post'], 'w_in': out['w_in'], 'b_qkv': out['b_qkv'], 'att_sinks': out['att_sinks'], 'mu_shift': out['mu_shift'], 'w0': out['w0'], 'w2': out['w2'], 'a0': out['a0'], 'a2': out['a2'], 'g2': out['g2'], 'k_k': out['k_k'], 'k_a': out['k_a'], 'r_k': out['r_k'], 'ln_x_w': out['ln_x_w'], 'ln_x_b': out['ln_x_b'], 'w_att_branch': out['w_att_branch'], 'w_rwkv_branch': out['w_rwkv_branch'], 'w_out': out['w_out'], 'w_ffn_gate': out['w_ffn_gate'], 'w_ffn_up': out['w_ffn_up'], 'w_ffn_down': out['w_ffn_down'], 'loss_target': out['loss_target'], 'm_norm_mix_pre': out['m_norm_mix_pre'], 'm_norm_mix_post': out['m_norm_mix_post'], 'm_norm_ffn_pre': out['m_norm_ffn_pre'], 'm_norm_ffn_post': out['m_norm_ffn_post'], 'm_w_in': out['m_w_in'], 'm_b_qkv': out['m_b_qkv'], 'm_att_sinks': out['m_att_sinks'], 'm_mu_shift': out['m_mu_shift'], 'm_w0': out['m_w0'], 'm_w2': out['m_w2'], 'm_a0': out['m_a0'], 'm_a2': out['m_a2'], 'm_g2': out['m_g2'], 'm_k_k': out['m_k_k'], 'm_k_a': out['m_k_a'], 'm_r_k': out['m_r_k'], 'm_ln_x_w': out['m_ln_x_w'], 'm_ln_x_b': out['m_ln_x_b'], 'm_w_att_branch': out['m_w_att_branch'], 'm_w_rwkv_branch': out['m_w_rwkv_branch'], 'm_w_out': out['m_w_out'], 'm_w_ffn_gate': out['m_w_ffn_gate'], 'm_w_ffn_up': out['m_w_ffn_up'], 'm_w_ffn_down': out['m_w_ffn_down'], 'v_norm_mix_pre': out['v_norm_mix_pre'], 'v_norm_mix_post': out['v_norm_mix_post'], 'v_norm_ffn_pre': out['v_norm_ffn_pre'], 'v_norm_ffn_post': out['v_norm_ffn_post'], 'v_w_in': out['v_w_in'], 'v_b_qkv': out['v_b_qkv'], 'v_att_sinks': out['v_att_sinks'], 'v_mu_shift': out['v_mu_shift'], 'v_w0': out['v_w0'], 'v_w2': out['v_w2'], 'v_a0': out['v_a0'], 'v_a2': out['v_a2'], 'v_g2': out['v_g2'], 'v_k_k': out['v_k_k'], 'v_k_a': out['v_k_a'], 'v_r_k': out['v_r_k'], 'v_ln_x_w': out['v_ln_x_w'], 'v_ln_x_b': out['v_ln_x_b'], 'v_w_att_branch': out['v_w_att_branch'], 'v_w_rwkv_branch': out['v_w_rwkv_branch'], 'v_w_out': out['v_w_out'], 'v_w_ffn_gate': out['v_w_ffn_gate'], 'v_w_ffn_up': out['v_w_ffn_up'], 'v_w_ffn_down': out['v_w_ffn_down']}


def _loss(weights, diff, rest, loss_target):
    with _jax.named_scope("forward"):
        args = {**rest, TWIN_DIFF_INPUT: diff, **{k: w.astype(_WEIGHT_DTYPES[k]) for k, w in weights.items()}}
        y = _forward(args)
    with _jax.named_scope("loss_head"):
        err = _jnp.square(y.astype(_jnp.float32) - loss_target)
        return 0.5 * _jnp.sum(_jnp.mean(err, axis=-1)) if err.ndim else 0.5 * err


def _adamw(w, g, m, v):
    m = ADAM_B1 * m + (1.0 - ADAM_B1) * g
    v = ADAM_B2 * v + (1.0 - ADAM_B2) * _jnp.square(g)
    m_hat = m / (1.0 - ADAM_B1 ** ADAM_STEP)
    v_hat = v / (1.0 - ADAM_B2 ** ADAM_STEP)
    delta = -ADAM_LR * (m_hat / (_jnp.sqrt(v_hat) + ADAM_EPS) + ADAM_WD * w)
    return delta, m, v


def reference(x, norm_mix_pre, norm_mix_post, norm_ffn_pre, norm_ffn_post, w_in, b_qkv, att_sinks, mu_shift, w0, w2, a0, a2, g2, k_k, k_a, r_k, ln_x_w, ln_x_b, w_att_branch, w_rwkv_branch, w_out, w_ffn_gate, w_ffn_up, w_ffn_down, loss_target, m_norm_mix_pre, m_norm_mix_post, m_norm_ffn_pre, m_norm_ffn_post, m_w_in, m_b_qkv, m_att_sinks, m_mu_shift, m_w0, m_w2, m_a0, m_a2, m_g2, m_k_k, m_k_a, m_r_k, m_ln_x_w, m_ln_x_b, m_w_att_branch, m_w_rwkv_branch, m_w_out, m_w_ffn_gate, m_w_ffn_up, m_w_ffn_down, v_norm_mix_pre, v_norm_mix_post, v_norm_ffn_pre, v_norm_ffn_post, v_w_in, v_b_qkv, v_att_sinks, v_mu_shift, v_w0, v_w2, v_a0, v_a2, v_g2, v_k_k, v_k_a, v_r_k, v_ln_x_w, v_ln_x_b, v_w_att_branch, v_w_rwkv_branch, v_w_out, v_w_ffn_gate, v_w_ffn_up, v_w_ffn_down):
    given = dict(x=x, norm_mix_pre=norm_mix_pre, norm_mix_post=norm_mix_post, norm_ffn_pre=norm_ffn_pre, norm_ffn_post=norm_ffn_post, w_in=w_in, b_qkv=b_qkv, att_sinks=att_sinks, mu_shift=mu_shift, w0=w0, w2=w2, a0=a0, a2=a2, g2=g2, k_k=k_k, k_a=k_a, r_k=r_k, ln_x_w=ln_x_w, ln_x_b=ln_x_b, w_att_branch=w_att_branch, w_rwkv_branch=w_rwkv_branch, w_out=w_out, w_ffn_gate=w_ffn_gate, w_ffn_up=w_ffn_up, w_ffn_down=w_ffn_down, loss_target=loss_target, m_norm_mix_pre=m_norm_mix_pre, m_norm_mix_post=m_norm_mix_post, m_norm_ffn_pre=m_norm_ffn_pre, m_norm_ffn_post=m_norm_ffn_post, m_w_in=m_w_in, m_b_qkv=m_b_qkv, m_att_sinks=m_att_sinks, m_mu_shift=m_mu_shift, m_w0=m_w0, m_w2=m_w2, m_a0=m_a0, m_a2=m_a2, m_g2=m_g2, m_k_k=m_k_k, m_k_a=m_k_a, m_r_k=m_r_k, m_ln_x_w=m_ln_x_w, m_ln_x_b=m_ln_x_b, m_w_att_branch=m_w_att_branch, m_w_rwkv_branch=m_w_rwkv_branch, m_w_out=m_w_out, m_w_ffn_gate=m_w_ffn_gate, m_w_ffn_up=m_w_ffn_up, m_w_ffn_down=m_w_ffn_down, v_norm_mix_pre=v_norm_mix_pre, v_norm_mix_post=v_norm_mix_post, v_norm_ffn_pre=v_norm_ffn_pre, v_norm_ffn_post=v_norm_ffn_post, v_w_in=v_w_in, v_b_qkv=v_b_qkv, v_att_sinks=v_att_sinks, v_mu_shift=v_mu_shift, v_w0=v_w0, v_w2=v_w2, v_a0=v_a0, v_a2=v_a2, v_g2=v_g2, v_k_k=v_k_k, v_k_a=v_k_a, v_r_k=v_r_k, v_ln_x_w=v_ln_x_w, v_ln_x_b=v_ln_x_b, v_w_att_branch=v_w_att_branch, v_w_rwkv_branch=v_w_rwkv_branch, v_w_out=v_w_out, v_w_ffn_gate=v_w_ffn_gate, v_w_ffn_up=v_w_ffn_up, v_w_ffn_down=v_w_ffn_down)
    weights = {n: given[n] for n in TWIN_WEIGHTS}
    shared = {n: given[n] for n in SHARED_INPUTS}
    per_example = {n: given[n] for n in ['x']}
    grad_fn = _jax.value_and_grad(_loss, argnums=(0, 1))

    def one_microbatch(ex, loss_target):
        ex = dict(ex)
        diff = ex.pop(TWIN_DIFF_INPUT)
        return grad_fn(weights, diff, {**shared, **ex}, loss_target)

    if N_MICROBATCH == 1:
        loss, (grad_w, grad_x) = one_microbatch(per_example, given["loss_target"])
    else:
        def body(carry, xs):
            loss_sum, grad_sum = carry
            l_k, (gw_k, gx_k) = one_microbatch(xs[0], xs[1])
            with _jax.named_scope("update"):
                return (loss_sum + l_k, _jax.tree.map(_jnp.add, grad_sum, gw_k)), gx_k

        init = (_jnp.zeros((), _jnp.float32), _jax.tree.map(_jnp.zeros_like, weights))
        (loss, grad_w), grad_x = _jax.lax.scan(body, init, (per_example, given["loss_target"]))
    with _jax.named_scope("update"):
        delta_w, new_m, new_v = {}, {}, {}
        for n in TWIN_WEIGHTS:
            delta_w[n], new_m[n], new_v[n] = _adamw(weights[n], grad_w[n], given["m_" + n], given["v_" + n])
    return (loss, grad_x, *[grad_w[n] for n in TWIN_WEIGHTS], *[delta_w[n] for n in TWIN_WEIGHTS],
            *[new_m[n] for n in TWIN_WEIGHTS], *[new_v[n] for n in TWIN_WEIGHTS])
```

```python
import jax
import jax.numpy as jnp
from jax import lax
from jax.experimental import pallas as pl
from jax.experimental.pallas import tpu as pltpu

F32 = jnp.float32
BF16 = jnp.bfloat16

LANES = 128
SUBLANES = 8
VMEM_LIMIT = 56 * 1024 * 1024

RW_H = 64
RW_N = 64
RW_C = RW_H * RW_N
RW_NB = RW_C // LANES
SCAN_CHUNK = 8


def _cparams(sem=None):
    return pltpu.CompilerParams(dimension_semantics=sem, vmem_limit_bytes=VMEM_LIMIT)


def _fold(x):
    return x + pltpu.roll(x, 64, axis=x.ndim - 1)


def _scan_step_fwd(t, src_ref, dst_ref, r_ref, w_ref, k_ref, a_ref, b_ref, v_ref):
    vt = v_ref[t]
    acc = jnp.zeros((RW_N, LANES), F32)
    for j in range(RW_NB):
        ls = slice(j * LANES, (j + 1) * LANES)
        acc = acc + src_ref[j] * a_ref[t:t + 1, ls]
    sa = _fold(acc)
    yacc = jnp.zeros((RW_N, LANES), F32)
    for j in range(RW_NB):
        ls = slice(j * LANES, (j + 1) * LANES)
        s_new = src_ref[j] * w_ref[t:t + 1, ls] + sa * b_ref[t:t + 1, ls] + vt * k_ref[t:t + 1, ls]
        dst_ref[j] = s_new
        yacc = yacc + s_new * r_ref[t:t + 1, ls]
    return _fold(yacc), sa


def rwkv_scan_fwd(r, w, k, a, b, v3):
    s_len = r.shape[0]
    nchunk = s_len // SCAN_CHUNK

    def body(r_ref, w_ref, k_ref, a_ref, b_ref, v_ref, y_ref, ck_ref, st_ref):
        @pl.when(pl.program_id(0) == 0)
        def _():
            st_ref[...] = jnp.zeros_like(st_ref)

        ck_ref[0] = st_ref[...]

        for t in range(SCAN_CHUNK):
            y, _ = _scan_step_fwd(t, st_ref, st_ref, r_ref, w_ref, k_ref, a_ref, b_ref, v_ref)
            y_ref[t] = y

    row = pl.BlockSpec((SCAN_CHUNK, RW_C), lambda i: (i, 0))
    til = pl.BlockSpec((SCAN_CHUNK, RW_N, LANES), lambda i: (i, 0, 0))
    return pl.pallas_call(
        body,
        name="rwkv_scan_fwd",
        grid=(nchunk,),
        in_specs=[row, row, row, row, row, til],
        out_specs=[til, pl.BlockSpec((1, RW_NB, RW_N, LANES), lambda i: (i, 0, 0, 0))],
        out_shape=[
            jax.ShapeDtypeStruct((s_len, RW_N, LANES), F32),
            jax.ShapeDtypeStruct((nchunk, RW_NB, RW_N, LANES), F32),
        ],
        scratch_shapes=[pltpu.VMEM((RW_NB, RW_N, LANES), F32)],
        compiler_params=_cparams(("arbitrary",)),
    )(r, w, k, a, b, v3)


def rwkv_scan_bwd(r, w, k, a, b, v3, ck, dy3):
    s_len = r.shape[0]
    nchunk = s_len // SCAN_CHUNK

    def body(r_ref, w_ref, k_ref, a_ref, b_ref, v_ref, ck_ref, dy_ref,
             dr_ref, dw_ref, dk_ref, da_ref, db_ref, dv_ref, hist_ref, sa_ref, ds_ref):
        @pl.when(pl.program_id(0) == 0)
        def _():
            ds_ref[...] = jnp.zeros_like(ds_ref)

        hist_ref[0] = ck_ref[0]

        for t in range(SCAN_CHUNK):
            _, sa = _scan_step_fwd(t, hist_ref.at[t], hist_ref.at[t + 1], r_ref, w_ref, k_ref, a_ref, b_ref, v_ref)
            sa_ref[t] = sa

        for t in reversed(range(SCAN_CHUNK)):
            vt = v_ref[t]
            dyt = dy_ref[t]
            sat = sa_ref[t]
            dv_acc = jnp.zeros((RW_N, LANES), F32)
            dsa_acc = jnp.zeros((RW_N, LANES), F32)
            for j in range(RW_NB):
                ls = slice(j * LANES, (j + 1) * LANES)
                row = (slice(t, t + 1), ls)
                ds_j = ds_ref[j] + dyt * r_ref[row]
                ds_ref[j] = ds_j
                dr_ref[row] = jnp.sum(hist_ref[t + 1, j] * dyt, axis=0, keepdims=True)
                dv_acc = dv_acc + ds_j * k_ref[row]
                dk_ref[row] = jnp.sum(ds_j * vt, axis=0, keepdims=True)
                dsa_acc = dsa_acc + ds_j * b_ref[row]
                db_ref[row] = jnp.sum(ds_j * sat, axis=0, keepdims=True)
                dw_ref[row] = jnp.sum(ds_j * hist_ref[t, j], axis=0, keepdims=True)
            dv_ref[t] = _fold(dv_acc)
            dsa = _fold(dsa_acc)
            for j in range(RW_NB):
                ls = slice(j * LANES, (j + 1) * LANES)
                row = (slice(t, t + 1), ls)
                da_ref[row] = jnp.sum(hist_ref[t, j] * dsa, axis=0, keepdims=True)
                ds_ref[j] = ds_ref[j] * w_ref[row] + dsa * a_ref[row]

    rev = lambda i: (nchunk - 1 - i, 0)
    rev3 = lambda i: (nchunk - 1 - i, 0, 0)
    row = pl.BlockSpec((SCAN_CHUNK, RW_C), rev)
    til = pl.BlockSpec((SCAN_CHUNK, RW_N, LANES), rev3)
    rows = jax.ShapeDtypeStruct((s_len, RW_C), F32)
    return pl.pallas_call(
        body,
        name="rwkv_scan_bwd",
        grid=(nchunk,),
        in_specs=[row, row, row, row, row, til,
                  pl.BlockSpec((1, RW_NB, RW_N, LANES), lambda i: (nchunk - 1 - i, 0, 0, 0)), til],
        out_specs=[row, row, row, row, row, til],
        out_shape=[rows, rows, rows, rows, rows, jax.ShapeDtypeStruct((s_len, RW_N, LANES), F32)],
        scratch_shapes=[
            pltpu.VMEM((SCAN_CHUNK + 1, RW_NB, RW_N, LANES), F32),
            pltpu.VMEM((SCAN_CHUNK, RW_N, LANES), F32),
            pltpu.VMEM((RW_NB, RW_N, LANES), F32),
        ],
        compiler_params=_cparams(("arbitrary",)),
    )(r, w, k, a, b, v3, ck, dy3)


def _pick(n, cands):
    for c in cands:
        if n % c == 0:
            return c
    return n


def matmul(a, b, *, ta=False, tb=False, out_dtype=F32, name="matmul"):
    m, kdim = (a.shape[1], a.shape[0]) if ta else a.shape
    n = b.shape[0] if tb else b.shape[1]
    assert (b.shape[1] if tb else b.shape[0]) == kdim
    tm = _pick(m, (1024, 768, 512, 256, 128))
    tn = _pick(n, (1024, 768, 512, 256, 128))
    tk = _pick(kdim, (1024, 768, 512, 256, 128))
    nk = kdim // tk
    dims = (((0 if ta else 1,), (1 if tb else 0,)), ((), ()))

    def body(a_ref, b_ref, o_ref, acc_ref):
        kk = pl.program_id(2)

        @pl.when(kk == 0)
        def _():
            acc_ref[...] = jnp.zeros_like(acc_ref)

        acc_ref[...] += lax.dot_general(a_ref[...].astype(BF16), b_ref[...].astype(BF16), dims,
                                        preferred_element_type=F32)

        @pl.when(kk == nk - 1)
        def _():
            o_ref[...] = acc_ref[...].astype(o_ref.dtype)

    a_spec = pl.BlockSpec((tk, tm), lambda i, j, k: (k, i)) if ta else pl.BlockSpec((tm, tk), lambda i, j, k: (i, k))
    b_spec = pl.BlockSpec((tn, tk), lambda i, j, k: (j, k)) if tb else pl.BlockSpec((tk, tn), lambda i, j, k: (k, j))
    return pl.pallas_call(
        body,
        name=name,
        grid=(m // tm, n // tn, nk),
        in_specs=[a_spec, b_spec],
        out_specs=pl.BlockSpec((tm, tn), lambda i, j, k: (i, j)),
        out_shape=jax.ShapeDtypeStruct((m, n), out_dtype),
        scratch_shapes=[pltpu.VMEM((tm, tn), F32)],
        compiler_params=_cparams(("parallel", "parallel", "arbitrary")),
    )(a, b)


@jax.custom_vjp
def hsum(x):
    acc = x[:, 0:LANES]
    for j in range(1, RW_NB):
        acc = acc + x[:, j * LANES:(j + 1) * LANES]
    return _fold(acc)


def _hsum_fwd(x):
    return hsum(x), None


def _hsum_bwd(_, ct):
    return (jnp.concatenate([_fold(ct)] * RW_NB, axis=1),)


hsum.defvjp(_hsum_fwd, _hsum_bwd)


@jax.custom_vjp
def hbcast(s):
    return jnp.concatenate([s] * RW_NB, axis=1)


def _hbcast_fwd(s):
    return hbcast(s), None


def _hbcast_bwd(_, ct):
    acc = ct[:, 0:LANES]
    for j in range(1, RW_NB):
        acc = acc + ct[:, j * LANES:(j + 1) * LANES]
    return (acc,)


hbcast.defvjp(_hbcast_fwd, _hbcast_bwd)


@jax.custom_vjp
def bdot(x, w):
    return jnp.dot(x.astype(BF16), w, preferred_element_type=F32)


def _bdot_fwd(x, w):
    return bdot(x, w), w


def _bdot_bwd(w, ct):
    dx = lax.dot_general(ct.astype(BF16), w, (((1,), (1,)), ((), ())), preferred_element_type=F32)
    return dx, jnp.zeros_like(w)


bdot.defvjp(_bdot_fwd, _bdot_bwd)

RMS_EPS = 1e-6
GN_EPS = 64e-5


def f_rms(x, g):
    return x * lax.rsqrt(jnp.mean(x * x, axis=-1, keepdims=True) + RMS_EPS) * g


def _softplus(z):
    return jnp.maximum(z, 0.0) + jnp.log1p(jnp.exp(-jnp.abs(z)))


def f_pre(xk, xg, xw, xa, ew, ea, w0, a0, k_k, k_a, w2, a2, g2):
    tw = jnp.tanh(xw)
    sg = jax.nn.sigmoid(xg)
    wlog = -_softplus(-(w0 + bdot(tw, w2) + ew)) - 0.5
    decay = jnp.exp(-jnp.exp(wlog))
    a = jax.nn.sigmoid(a0 + bdot(xa, a2) + ea)
    g = bdot(sg, g2)
    kk0 = xk * k_k
    nrm = jnp.sqrt(hbcast(hsum(kk0 * kk0)))
    kk = kk0 / jnp.maximum(nrm, 1e-12)
    k = xk * (1.0 + (a - 1.0) * k_a)
    return decay, k, -kk, kk * a, g, tw, sg


def f_post(y, r, k, v, g, ln_w, ln_b, r_k):
    mu = hbcast(hsum(y)) * (1.0 / RW_N)
    yc = y - mu
    var = hbcast(hsum(yc * yc)) * (1.0 / RW_N)
    yn = yc * lax.rsqrt(var + GN_EPS) * ln_w + ln_b
    bonus = hbcast(hsum(r * k * r_k)) * v
    return (yn + bonus) * g


def f_merge(ga, gr, ab, rb):
    return jax.nn.sigmoid(ga) * ab + jax.nn.sigmoid(gr) * rb


def f_swiglu(gg, uu):
    return gg * jax.nn.sigmoid(gg) * uu


def _row(tt, width, cb=0, rev_n=None):
    if rev_n is None:
        return pl.BlockSpec((tt, width), lambda i: (i, cb))
    return pl.BlockSpec((tt, width), lambda i: (rev_n - 1 - i, cb))


def _full(arr):
    nd = arr.ndim
    return pl.BlockSpec(arr.shape, lambda i: (0,) * nd)


def _acc_init(i_first, *refs):
    @pl.when(i_first)
    def _():
        for r in refs:
            r[...] = jnp.zeros_like(r)


def rms_fwd(x, g, *, tt=128):
    s_len, d = x.shape

    def body(x_ref, g_ref, o_ref):
        o_ref[...] = f_rms(x_ref[...], g_ref[...]).astype(BF16)

    return pl.pallas_call(
        body, name="rms_fwd", grid=(s_len // tt,),
        in_specs=[_row(tt, d), _full(g)], out_specs=_row(tt, d),
        out_shape=jax.ShapeDtypeStruct((s_len, d), BF16),
        compiler_params=_cparams(("parallel",)),
    )(x, g)


def rwkv_pre_fwd(proj, mu, w0, a0, k_k, k_a, w2, a2, g2, *, tt=64):
    s_len, c = proj.shape
    nt = s_len // tt
    sub = tt // SUBLANES

    def body(p_ref, pb_ref, mu_ref, w0_ref, a0_ref, kk_ref, ka_ref, w2_ref, a2_ref, g2_ref,
             r_ref, dec_ref, k_ref, v_ref, av_ref, bv_ref, g_ref, tw_ref, xa_ref, sg_ref):
        i = pl.program_id(0)
        cur = p_ref[...]
        edge = jnp.where(i > 0, pb_ref[SUBLANES - 1:SUBLANES, :], 0.0)
        rows = lax.broadcasted_iota(jnp.int32, cur.shape, 0)
        prev = jnp.where(rows == 0, edge, pltpu.roll(cur, 1, axis=0))
        xs = cur + (prev - cur) * mu_ref[...]
        xr, xk, xv = xs[:, 0:RW_C], xs[:, RW_C:2 * RW_C], xs[:, 2 * RW_C:3 * RW_C]
        xg = xs[:, 3 * RW_C:3 * RW_C + 512]
        xw = xs[:, 3 * RW_C + 512:3 * RW_C + 640]
        xa = xs[:, 3 * RW_C + 640:3 * RW_C + 768]
        zero = jnp.zeros((tt, RW_C), F32)
        dec, k, av, bv, g, tw, sg = f_pre(xk, xg, xw, xa, zero, zero, w0_ref[...], a0_ref[...], kk_ref[...],
                                          ka_ref[...], w2_ref[...], a2_ref[...], g2_ref[...])
        r_ref[...] = xr
        dec_ref[...] = dec
        k_ref[...] = k
        v_ref[...] = xv
        av_ref[...] = av
        bv_ref[...] = bv
        g_ref[...] = g
        tw_ref[...] = tw.astype(BF16)
        xa_ref[...] = xa.astype(BF16)
        sg_ref[...] = sg.astype(BF16)

    rows_f = jax.ShapeDtypeStruct((s_len, RW_C), F32)
    prev_spec = pl.BlockSpec((SUBLANES, c), lambda i: (jnp.maximum(i * sub - 1, 0), 0))
    params = [mu, w0, a0, k_k, k_a, w2, a2, g2]
    return pl.pallas_call(
        body, name="rwkv_pre_fwd", grid=(nt,),
        in_specs=[_row(tt, c), prev_spec] + [_full(p) for p in params],
        out_specs=[_row(tt, RW_C)] * 7 + [_row(tt, 128), _row(tt, 128), _row(tt, 512)],
        out_shape=[rows_f] * 7 + [jax.ShapeDtypeStruct((s_len, 128), BF16), jax.ShapeDtypeStruct((s_len, 128), BF16),
                                  jax.ShapeDtypeStruct((s_len, 512), BF16)],
        compiler_params=_cparams(("parallel",)),
    )(proj, proj, *params)


def rwkv_pre_bwd(proj, mu, w0, a0, k_k, k_a, w2, a2, g2, d_r, d_dec, d_k, d_v, d_av, d_bv, d_g, d_r2, d_k2, d_v2,
                 *, tt=32):
    s_len, c = proj.shape
    nt = s_len // tt
    sub = tt // SUBLANES

    def body(p_ref, pb_ref, mu_ref, w0_ref, a0_ref, kk_ref, ka_ref, w2_ref, a2_ref, g2_ref,
             dr_ref, ddec_ref, dk_ref, dv_ref, dav_ref, dbv_ref, dg_ref, dr2_ref, dk2_ref, dv2_ref,
             dp_ref, dzw_ref, dza_ref, dmu_ref, dw0_ref, da0_ref, dkk_ref, dka_ref, carry_ref):
        step = pl.program_id(0)
        i = nt - 1 - step
        _acc_init(step == 0, dmu_ref, dw0_ref, da0_ref, dkk_ref, dka_ref, carry_ref)
        cur = p_ref[...]
        edge = jnp.where(i > 0, pb_ref[SUBLANES - 1:SUBLANES, :], 0.0)
        rows = lax.broadcasted_iota(jnp.int32, cur.shape, 0)
        prev = jnp.where(rows == 0, edge, pltpu.roll(cur, 1, axis=0))
        mu_v = mu_ref[...]
        xs = cur + (prev - cur) * mu_v
        xk = xs[:, RW_C:2 * RW_C]
        xg = xs[:, 3 * RW_C:3 * RW_C + 512]
        xw = xs[:, 3 * RW_C + 512:3 * RW_C + 640]
        xa = xs[:, 3 * RW_C + 640:3 * RW_C + 768]
        zero = jnp.zeros((tt, RW_C), F32)
        w2_v, a2_v, g2_v = w2_ref[...], a2_ref[...], g2_ref[...]

        def core(xk, xg, xw, xa, ew, ea, w0, a0, k_k, k_a):
            return f_pre(xk, xg, xw, xa, ew, ea, w0, a0, k_k, k_a, w2_v, a2_v, g2_v)[:5]

        _, vjp = jax.vjp(core, xk, xg, xw, xa, zero, zero, w0_ref[...], a0_ref[...], kk_ref[...], ka_ref[...])
        dxk, dxg, dxw, dxa, dzw, dza, dw0, da0, dkk, dka = vjp(
            (ddec_ref[...], dk_ref[...] + dk2_ref[...], dav_ref[...], dbv_ref[...], dg_ref[...]))
        dzw_ref[...] = dzw.astype(BF16)
        dza_ref[...] = dza.astype(BF16)
        dw0_ref[...] += dw0
        da0_ref[...] += da0
        dkk_ref[...] += dkk
        dka_ref[...] += dka
        dxs = jnp.concatenate([dr_ref[...] + dr2_ref[...], dxk, dv_ref[...] + dv2_ref[...], dxg, dxw, dxa], axis=1)
        dmu_ref[...] += jnp.sum(dxs * (prev - cur), axis=0, keepdims=True)
        to_prev = dxs * mu_v
        nxt = jnp.where(rows == tt - 1, carry_ref[...], pltpu.roll(to_prev, tt - 1, axis=0))
        carry_ref[...] = to_prev[0:1, :]
        dp_ref[...] = (dxs * (1.0 - mu_v) + nxt).astype(BF16)

    prev_spec = pl.BlockSpec((SUBLANES, c), lambda s: (jnp.maximum((nt - 1 - s) * sub - 1, 0), 0))
    params = [mu, w0, a0, k_k, k_a, w2, a2, g2]
    cts = [d_r, d_dec, d_k, d_v, d_av, d_bv, d_g, d_r2, d_k2, d_v2]
    vec = jax.ShapeDtypeStruct((1, RW_C), F32)
    acc = pl.BlockSpec((1, RW_C), lambda s: (0, 0))
    return pl.pallas_call(
        body, name="rwkv_pre_bwd", grid=(nt,),
        in_specs=[_row(tt, c, rev_n=nt), prev_spec] + [_full(p) for p in params] + [_row(tt, RW_C, rev_n=nt)] * 10,
        out_specs=[_row(tt, c, rev_n=nt), _row(tt, RW_C, rev_n=nt), _row(tt, RW_C, rev_n=nt),
                   pl.BlockSpec((1, c), lambda s: (0, 0)), acc, acc, acc, acc],
        out_shape=[jax.ShapeDtypeStruct((s_len, c), BF16), jax.ShapeDtypeStruct((s_len, RW_C), BF16),
                   jax.ShapeDtypeStruct((s_len, RW_C), BF16), jax.ShapeDtypeStruct((1, c), F32), vec, vec, vec, vec],
        scratch_shapes=[pltpu.VMEM((1, c), F32)],
        compiler_params=_cparams(("arbitrary",)),
    )(proj, proj, *params, *cts)


def rwkv_post_fwd(y, r, k, v, g, ln_w, ln_b, r_k, *, tt=64):
    s_len = y.shape[0]

    def body(y_ref, r_ref, k_ref, v_ref, g_ref, lw_ref, lb_ref, rk_ref, o_ref):
        o_ref[...] = f_post(y_ref[...], r_ref[...], k_ref[...], v_ref[...], g_ref[...],
                            lw_ref[...], lb_ref[...], rk_ref[...]).astype(BF16)

    return pl.pallas_call(
        body, name="rwkv_post_fwd", grid=(s_len // tt,),
        in_specs=[_row(tt, RW_C)] * 5 + [_full(ln_w), _full(ln_b), _full(r_k)],
        out_specs=_row(tt, RW_C), out_shape=jax.ShapeDtypeStruct((s_len, RW_C), BF16),
        compiler_params=_cparams(("parallel",)),
    )(y, r, k, v, g, ln_w, ln_b, r_k)


def rwkv_post_bwd(y, r, k, v, g, ln_w, ln_b, r_k, d_o, *, tt=32):
    s_len = y.shape[0]

    def body(y_ref, r_ref, k_ref, v_ref, g_ref, lw_ref, lb_ref, rk_ref, do_ref,
             dy_ref, dr_ref, dk_ref, dv_ref, dg_ref, dlw_ref, dlb_ref, drk_ref):
        _acc_init(pl.program_id(0) == 0, dlw_ref, dlb_ref, drk_ref)
        _, vjp = jax.vjp(f_post, y_ref[...], r_ref[...], k_ref[...], v_ref[...], g_ref[...],
                         lw_ref[...], lb_ref[...], rk_ref[...])
        dy, dr, dk, dv, dg, dlw, dlb, drk = vjp(do_ref[...].astype(F32))
        dy_ref[...] = dy
        dr_ref[...] = dr
        dk_ref[...] = dk
        dv_ref[...] = dv
        dg_ref[...] = dg
        dlw_ref[...] += dlw
        dlb_ref[...] += dlb
        drk_ref[...] += drk

    rows_f = jax.ShapeDtypeStruct((s_len, RW_C), F32)
    vec = jax.ShapeDtypeStruct((1, RW_C), F32)
    acc = pl.BlockSpec((1, RW_C), lambda s: (0, 0))
    return pl.pallas_call(
        body, name="rwkv_post_bwd", grid=(s_len // tt,),
        in_specs=[_row(tt, RW_C)] * 5 + [_full(ln_w), _full(ln_b), _full(r_k), _row(tt, RW_C)],
        out_specs=[_row(tt, RW_C)] * 5 + [acc] * 3, out_shape=[rows_f] * 5 + [vec] * 3,
        compiler_params=_cparams(("arbitrary",)),
    )(y, r, k, v, g, ln_w, ln_b, r_k, d_o)


def merge_fwd(gate, ab, rb, *, tt=128):
    s_len, d = ab.shape

    def body(ga_ref, gr_ref, a_ref, r_ref, o_ref):
        o_ref[...] = f_merge(ga_ref[...], gr_ref[...], a_ref[...], r_ref[...]).astype(BF16)

    return pl.pallas_call(
        body, name="merge_fwd", grid=(s_len // tt,),
        in_specs=[_row(tt, d, 0), _row(tt, d, 1), _row(tt, d), _row(tt, d)],
        out_specs=_row(tt, d), out_shape=jax.ShapeDtypeStruct((s_len, d), BF16),
        compiler_params=_cparams(("parallel",)),
    )(gate, gate, ab, rb)


def merge_bwd(gate, ab, rb, d_m, *, tt=64):
    s_len, d = ab.shape

    def body(ga_ref, gr_ref, a_ref, r_ref, dm_ref, dgate_ref, da_ref, dr_ref):
        _, vjp = jax.vjp(f_merge, ga_ref[...], gr_ref[...], a_ref[...], r_ref[...])
        dga, dgr, da, dr = vjp(dm_ref[...].astype(F32))
        dgate_ref[:, 0:d] = dga.astype(BF16)
        dgate_ref[:, d:2 * d] = dgr.astype(BF16)
        da_ref[...] = da.astype(BF16)
        dr_ref[...] = dr.astype(BF16)

    return pl.pallas_call(
        body, name="merge_bwd", grid=(s_len // tt,),
        in_specs=[_row(tt, d, 0), _row(tt, d, 1), _row(tt, d), _row(tt, d), _row(tt, d)],
        out_specs=[_row(tt, 2 * d), _row(tt, d), _row(tt, d)],
        out_shape=[jax.ShapeDtypeStruct((s_len, 2 * d), BF16), jax.ShapeDtypeStruct((s_len, d), BF16),
                   jax.ShapeDtypeStruct((s_len, d), BF16)],
        compiler_params=_cparams(("parallel",)),
    )(gate, gate, ab, rb, d_m)


def swiglu_fwd(gg, uu, *, tt=64):
    s_len, f = gg.shape

    def body(g_ref, u_ref, o_ref):
        o_ref[...] = f_swiglu(g_ref[...], u_ref[...]).astype(BF16)

    return pl.pallas_call(
        body, name="swiglu_fwd", grid=(s_len // tt,),
        in_specs=[_row(tt, f), _row(tt, f)], out_specs=_row(tt, f),
        out_shape=jax.ShapeDtypeStruct((s_len, f), BF16),
        compiler_params=_cparams(("parallel",)),
    )(gg, uu)


def swiglu_bwd(gg, uu, d_act, *, tt=32):
    s_len, f = gg.shape

    def body(g_ref, u_ref, d_ref, dg_ref, du_ref):
        _, vjp = jax.vjp(f_swiglu, g_ref[...], u_ref[...])
        dg, du = vjp(d_ref[...].astype(F32))
        dg_ref[...] = dg.astype(BF16)
        du_ref[...] = du.astype(BF16)

    out = jax.ShapeDtypeStruct((s_len, f), BF16)
    return pl.pallas_call(
        body, name="swiglu_bwd", grid=(s_len // tt,),
        in_specs=[_row(tt, f)] * 3, out_specs=[_row(tt, f)] * 2, out_shape=[out, out],
        compiler_params=_cparams(("parallel",)),
    )(gg, uu, d_act)


def resid_norm_fwd(x, m2, g_post, g_pre, *, tt=128):
    s_len, d = x.shape

    def body(x_ref, m_ref, gp_ref, gn_ref, x1_ref, h_ref):
        x1 = x_ref[...] + f_rms(m_ref[...], gp_ref[...])
        x1_ref[...] = x1
        h_ref[...] = f_rms(x1, gn_ref[...]).astype(BF16)

    return pl.pallas_call(
        body, name="resid_norm_fwd", grid=(s_len // tt,),
        in_specs=[_row(tt, d), _row(tt, d), _full(g_post), _full(g_pre)],
        out_specs=[_row(tt, d), _row(tt, d)],
        out_shape=[jax.ShapeDtypeStruct((s_len, d), F32), jax.ShapeDtypeStruct((s_len, d), BF16)],
        compiler_params=_cparams(("parallel",)),
    )(x, m2, g_post, g_pre)


def loss_head(x1, ff, tgt, g_post, *, tt=64):
    s_len, d = x1.shape

    def body(x1_ref, f_ref, t_ref, g_ref, loss_ref, dy_ref, df_ref, dg_ref):
        _acc_init(pl.program_id(0) == 0, loss_ref, dg_ref)
        nrm, vjp = jax.vjp(f_rms, f_ref[...], g_ref[...])
        err = x1_ref[...] + nrm - t_ref[...]
        per_tok = jnp.mean(err * err, axis=-1, keepdims=True)
        loss_ref[...] += 0.5 * jnp.sum(per_tok, axis=0, keepdims=True)
        dy = err * (1.0 / d)
        dff, dg = vjp(dy)
        dy_ref[...] = dy
        df_ref[...] = dff.astype(BF16)
        dg_ref[...] += dg

    return pl.pallas_call(
        body, name="loss_head", grid=(s_len // tt,),
        in_specs=[_row(tt, d)] * 3 + [_full(g_post)],
        out_specs=[pl.BlockSpec((1, LANES), lambda s: (0, 0)), _row(tt, d), _row(tt, d),
                   pl.BlockSpec((1, d), lambda s: (0, 0))],
        out_shape=[jax.ShapeDtypeStruct((1, LANES), F32), jax.ShapeDtypeStruct((s_len, d), F32),
                   jax.ShapeDtypeStruct((s_len, d), BF16), jax.ShapeDtypeStruct((1, d), F32)],
        compiler_params=_cparams(("arbitrary",)),
    )(x1, ff, tgt, g_post)


def resid_norm_bwd(x1, dh_a, dh_b, g_pre, m2, g_post, dy, *, tt=64):
    s_len, d = x1.shape

    def body(x1_ref, da_ref, db_ref, gn_ref, m_ref, gp_ref, dy_ref, dx1_ref, dm_ref, dgn_ref, dgp_ref):
        _acc_init(pl.program_id(0) == 0, dgn_ref, dgp_ref)
        _, vjp_n = jax.vjp(f_rms, x1_ref[...], gn_ref[...])
        dx1_n, dgn = vjp_n(da_ref[...] + db_ref[...])
        dx1 = dy_ref[...] + dx1_n
        _, vjp_p = jax.vjp(f_rms, m_ref[...], gp_ref[...])
        dm, dgp = vjp_p(dx1)
        dx1_ref[...] = dx1
        dm_ref[...] = dm.astype(BF16)
        dgn_ref[...] += dgn
        dgp_ref[...] += dgp

    acc = pl.BlockSpec((1, d), lambda s: (0, 0))
    vec = jax.ShapeDtypeStruct((1, d), F32)
    return pl.pallas_call(
        body, name="resid_norm_bwd", grid=(s_len // tt,),
        in_specs=[_row(tt, d)] * 3 + [_full(g_pre), _row(tt, d), _full(g_post), _row(tt, d)],
        out_specs=[_row(tt, d), _row(tt, d), acc, acc],
        out_shape=[jax.ShapeDtypeStruct((s_len, d), F32), jax.ShapeDtypeStruct((s_len, d), BF16), vec, vec],
        compiler_params=_cparams(("arbitrary",)),
    )(x1, dh_a, dh_b, g_pre, m2, g_post, dy)


def rms_bwd(x, g, dh_a, dh_b, dh_c, dres, *, tt=64):
    s_len, d = x.shape

    def body(x_ref, g_ref, a_ref, b_ref, c_ref, r_ref, dx_ref, dg_ref):
        _acc_init(pl.program_id(0) == 0, dg_ref)
        _, vjp = jax.vjp(f_rms, x_ref[...], g_ref[...])
        dx, dg = vjp(a_ref[...] + b_ref[...] + c_ref[...])
        dx_ref[...] = r_ref[...] + dx
        dg_ref[...] += dg

    return pl.pallas_call(
        body, name="rms_bwd", grid=(s_len // tt,),
        in_specs=[_row(tt, d), _full(g)] + [_row(tt, d)] * 4,
        out_specs=[_row(tt, d), pl.BlockSpec((1, d), lambda s: (0, 0))],
        out_shape=[jax.ShapeDtypeStruct((s_len, d), F32), jax.ShapeDtypeStruct((1, d), F32)],
        compiler_params=_cparams(("arbitrary",)),
    )(x, g, dh_a, dh_b, dh_c, dres)


def colsum(a, *, tt=256):
    s_len, c = a.shape

    def body(a_ref, o_ref):
        _acc_init(pl.program_id(0) == 0, o_ref)
        o_ref[...] += jnp.sum(a_ref[...].astype(F32), axis=0, keepdims=True)

    return pl.pallas_call(
        body, name="colsum", grid=(s_len // tt,),
        in_specs=[_row(tt, c)], out_specs=pl.BlockSpec((1, c), lambda s: (0, 0)),
        out_shape=jax.ShapeDtypeStruct((1, c), F32),
        compiler_params=_cparams(("arbitrary",)),
    )(a)


AT_HD = 128
AT_GROUP = 4
AT_KVH = 8
AT_BLK = 128
AT_QW = AT_GROUP * AT_HD
AT_KCOL = AT_KVH * AT_GROUP
AT_VCOL = AT_KCOL + AT_KVH
NEG_INF = -1e30
AT_SCALE = AT_HD ** -0.5


def _rope(t, cos2, sin2):
    return t * cos2 + pltpu.roll(t, AT_HD // 2, axis=1) * sin2


def _rope_t(d, cos2, sin2):
    return d * cos2 + pltpu.roll(d * sin2, AT_HD // 2, axis=1)


def _att_specs():
    prev = lambda i: jnp.maximum(i - 1, 0)
    blk = (AT_BLK, AT_HD)
    return [
        pl.BlockSpec((AT_BLK, AT_QW), lambda h, i: (i, h)),
        pl.BlockSpec(blk, lambda h, i: (i, AT_KCOL + h)),
        pl.BlockSpec(blk, lambda h, i: (prev(i), AT_KCOL + h)),
        pl.BlockSpec(blk, lambda h, i: (i, AT_VCOL + h)),
        pl.BlockSpec(blk, lambda h, i: (prev(i), AT_VCOL + h)),
        pl.BlockSpec((1, AT_QW), lambda h, i: (0, h)),
        pl.BlockSpec((1, AT_HD), lambda h, i: (0, AT_KCOL + h)),
        pl.BlockSpec((1, AT_HD), lambda h, i: (0, AT_VCOL + h)),
        pl.BlockSpec((1, AT_GROUP, AT_HD), lambda h, i: (h, 0, 0)),
        pl.BlockSpec(blk, lambda h, i: (i, 0)),
        pl.BlockSpec(blk, lambda h, i: (i, 0)),
        pl.BlockSpec(blk, lambda h, i: (prev(i), 0)),
        pl.BlockSpec(blk, lambda h, i: (prev(i), 0)),
    ]


def _att_load(i, q_ref, kc_ref, kp_ref, vc_ref, vp_ref, bq_ref, bk_ref, bv_ref, cc_ref, sc_ref, cp_ref, sp_ref):
    cosc, sinc = cc_ref[...], sc_ref[...]
    q = q_ref[...] + bq_ref[...]
    kc = _rope(kc_ref[...] + bk_ref[...], cosc, sinc)
    kp = _rope(kp_ref[...] + bk_ref[...], cp_ref[...], sp_ref[...])
    kcat = jnp.concatenate([kp, kc], axis=0).astype(BF16)
    vcat = jnp.concatenate([vp_ref[...] + bv_ref[...], vc_ref[...] + bv_ref[...]], axis=0).astype(BF16)
    qi = lax.broadcasted_iota(jnp.int32, (AT_BLK, 2 * AT_BLK), 0)
    kj = lax.broadcasted_iota(jnp.int32, (AT_BLK, 2 * AT_BLK), 1)
    rel = qi + AT_BLK - kj
    mask = (rel >= 0) & (rel < AT_BLK) & ((kj >= AT_BLK) | (i > 0))
    return q, kcat, vcat, mask, cosc, sinc


def _att_probs(qg, kcat, mask, sink):
    s = lax.dot_general(qg.astype(BF16), kcat, (((1,), (1,)), ((), ())), preferred_element_type=F32) * AT_SCALE
    s = jnp.where(mask, s, NEG_INF)
    m = jnp.maximum(jnp.max(s, axis=-1, keepdims=True), sink)
    p = jnp.exp(s - m)
    es = jnp.exp(sink - m)
    inv = 1.0 / (jnp.sum(p, axis=-1, keepdims=True) + es)
    return p * inv, es * inv


def attention_fwd(qkv, bias, sinks_b, cos2, sin2):
    s_len = qkv.shape[0]
    nb = s_len // AT_BLK

    def body(q_ref, kc_ref, kp_ref, vc_ref, vp_ref, bq_ref, bk_ref, bv_ref, sk_ref, cc_ref, sc_ref, cp_ref, sp_ref,
             o_ref):
        i = pl.program_id(1)
        q, kcat, vcat, mask, cosc, sinc = _att_load(i, q_ref, kc_ref, kp_ref, vc_ref, vp_ref, bq_ref, bk_ref,
                                                    bv_ref, cc_ref, sc_ref, cp_ref, sp_ref)
        for g in range(AT_GROUP):
            ls = slice(g * AT_HD, (g + 1) * AT_HD)
            qg = _rope(q[:, ls], cosc, sinc)
            probs, _ = _att_probs(qg, kcat, mask, sk_ref[0, g:g + 1, 0:1])
            o_ref[:, ls] = jnp.dot(probs.astype(BF16), vcat, preferred_element_type=F32).astype(BF16)

    return pl.pallas_call(
        body, name="attention_fwd", grid=(AT_KVH, nb),
        in_specs=_att_specs(),
        out_specs=pl.BlockSpec((AT_BLK, AT_QW), lambda h, i: (i, h)),
        out_shape=jax.ShapeDtypeStruct((s_len, AT_KVH * AT_QW), BF16),
        compiler_params=_cparams(("parallel", "parallel")),
    )(qkv, qkv, qkv, qkv, qkv, bias, bias, bias, sinks_b, cos2, sin2, cos2, sin2)


def attention_bwd(qkv, bias, sinks_b, cos2, sin2, d_o):
    s_len = qkv.shape[0]
    nb = s_len // AT_BLK

    def body(q_ref, kc_ref, kp_ref, vc_ref, vp_ref, bq_ref, bk_ref, bv_ref, sk_ref, cc_ref, sc_ref, cp_ref, sp_ref,
             do_ref, dq_ref, dk_ref, dv_ref, dsk_ref):
        i = pl.program_id(1)
        _acc_init(i == 0, dsk_ref)
        q, kcat, vcat, mask, cosc, sinc = _att_load(i, q_ref, kc_ref, kp_ref, vc_ref, vp_ref, bq_ref, bk_ref,
                                                    bv_ref, cc_ref, sc_ref, cp_ref, sp_ref)
        dk_cat = jnp.zeros((2 * AT_BLK, AT_HD), F32)
        dv_cat = jnp.zeros((2 * AT_BLK, AT_HD), F32)
        lane = lax.broadcasted_iota(jnp.int32, (1, AT_HD), 1)
        dsk = jnp.zeros((1, AT_HD), F32)
        for g in range(AT_GROUP):
            ls = slice(g * AT_HD, (g + 1) * AT_HD)
            qg = _rope(q[:, ls], cosc, sinc).astype(BF16)
            probs, psink = _att_probs(qg, kcat, mask, sk_ref[0, g:g + 1, 0:1])
            pb = probs.astype(BF16)
            do_g = do_ref[:, ls].astype(F32)
            do_b = do_g.astype(BF16)
            o_g = jnp.dot(pb, vcat, preferred_element_type=F32)
            dsum = jnp.sum(do_g * o_g, axis=-1, keepdims=True)
            dp = lax.dot_general(do_b, vcat, (((1,), (1,)), ((), ())), preferred_element_type=F32)
            ds = (probs * (dp - dsum) * AT_SCALE).astype(BF16)
            dsk = dsk + jnp.where(lane == g, -jnp.sum(psink * dsum, axis=0, keepdims=True), 0.0)
            dv_cat = dv_cat + lax.dot_general(pb, do_b, (((0,), (0,)), ((), ())), preferred_element_type=F32)
            dk_cat = dk_cat + lax.dot_general(ds, qg, (((0,), (0,)), ((), ())), preferred_element_type=F32)
            dq_ref[:, ls] = _rope_t(jnp.dot(ds, kcat, preferred_element_type=F32), cosc, sinc).astype(BF16)
        dsk_ref[0] += dsk
        cur = pl.ds(pl.multiple_of(i * AT_BLK, AT_BLK), AT_BLK)
        dk_ref[cur, :] = _rope_t(dk_cat[AT_BLK:], cosc, sinc)
        dv_ref[cur, :] = dv_cat[AT_BLK:]

        @pl.when(i > 0)
        def _():
            prv = pl.ds(pl.multiple_of((i - 1) * AT_BLK, AT_BLK), AT_BLK)
            dk_ref[prv, :] += _rope_t(dk_cat[:AT_BLK], cp_ref[...], sp_ref[...])
            dv_ref[prv, :] += dv_cat[:AT_BLK]

    kv_out = pl.BlockSpec((s_len, AT_HD), lambda h, i: (0, h))
    return pl.pallas_call(
        body, name="attention_bwd", grid=(AT_KVH, nb),
        in_specs=_att_specs() + [pl.BlockSpec((AT_BLK, AT_QW), lambda h, i: (i, h))],
        out_specs=[pl.BlockSpec((AT_BLK, AT_QW), lambda h, i: (i, h)), kv_out, kv_out,
                   pl.BlockSpec((1, 1, AT_HD), lambda h, i: (h, 0, 0))],
        out_shape=[jax.ShapeDtypeStruct((s_len, AT_KVH * AT_QW), BF16),
                   jax.ShapeDtypeStruct((s_len, AT_KVH * AT_HD), F32),
                   jax.ShapeDtypeStruct((s_len, AT_KVH * AT_HD), F32),
                   jax.ShapeDtypeStruct((AT_KVH, 1, AT_HD), F32)],
        compiler_params=_cparams(("arbitrary", "arbitrary")),
    )(qkv, qkv, qkv, qkv, qkv, bias, bias, bias, sinks_b, cos2, sin2, cos2, sin2, d_o)


ATT_QKV = 6144
RW_SHIFT = 13024
RW_PAD = 13056
D_GATE = 480
ROPE_THETA = 10000.0


def perm_cols(a):
    lead = a.shape[:-1]
    return jnp.swapaxes(a.reshape(lead + (RW_H, RW_N)), -1, -2).reshape(lead + (RW_C,))


def rw_reorder(a, pad_value=0):
    r, k, v = (perm_cols(a[..., i * RW_C:(i + 1) * RW_C]) for i in range(3))
    wd = a[..., 3 * RW_C:3 * RW_C + 128]
    ad = a[..., 3 * RW_C + 128:3 * RW_C + 256]
    gd = a[..., 3 * RW_C + 256:]
    pad = jnp.full(a.shape[:-1] + (512 - D_GATE,), pad_value, a.dtype)
    return jnp.concatenate([r, k, v, gd, pad, wd, ad], axis=-1)


def rw_restore(a):
    r, k, v = (perm_cols(a[..., i * RW_C:(i + 1) * RW_C]) for i in range(3))
    gd = a[..., 3 * RW_C:3 * RW_C + D_GATE]
    wd = a[..., 3 * RW_C + 512:3 * RW_C + 640]
    ad = a[..., 3 * RW_C + 640:3 * RW_C + 768]
    return jnp.concatenate([r, k, v, wd, ad, gd], axis=-1)


def to_tiles(a):
    t = a.reshape(a.shape[0], RW_N, RW_H)
    return jnp.concatenate([t, t], axis=-1)


def from_tiles(t):
    return t[:, :, :RW_H].reshape(t.shape[0], RW_C)


def rope_tables(s_len):
    pos = jnp.arange(s_len, dtype=F32)
    inv_freq = ROPE_THETA ** (-jnp.arange(0, AT_HD, 2, dtype=F32) / AT_HD)
    ang = pos[:, None] * inv_freq[None, :]
    cos, sin = jnp.cos(ang), jnp.sin(ang)
    return jnp.concatenate([cos, cos], axis=1), jnp.concatenate([-sin, sin], axis=1)


def local_step(x, tgt, small, big):
    s_len, d = x.shape
    w_in = big["w_in"]
    w_qkv = w_in[:, :ATT_QKV]
    w_rw = rw_reorder(w_in[:, ATT_QKV:ATT_QKV + RW_SHIFT])
    w_gate = w_in[:, ATT_QKV + RW_SHIFT:]
    w2 = perm_cols(big["w2"])
    a2 = perm_cols(big["a2"])
    g2 = jnp.pad(perm_cols(big["g2"]), ((0, 512 - D_GATE), (0, 0)))
    w_rb = big["w_rwkv_branch"].reshape(RW_H, RW_N, d).swapaxes(0, 1).reshape(RW_C, d)
    mu = rw_reorder(small["mu_shift"])
    w0, a0, k_k, k_a, ln_w, ln_b = (perm_cols(small[n]) for n in ("w0", "a0", "k_k", "k_a", "ln_x_w", "ln_x_b"))
    r_k = small["r_k"].reshape(RW_H, RW_N).T.reshape(1, RW_C)
    sinks_b = jnp.broadcast_to(small["att_sinks"].reshape(AT_KVH, AT_GROUP, 1), (AT_KVH, AT_GROUP, AT_HD))
    cos2, sin2 = rope_tables(s_len)
    bias = small["b_qkv"]

    h = rms_fwd(x, small["norm_mix_pre"])
    qkv = matmul(h, w_qkv, name="mm_qkv")
    prw = matmul(h, w_rw, name="mm_rw")
    gate = matmul(h, w_gate, name="mm_gate")
    o_att = attention_fwd(qkv, bias, sinks_b, cos2, sin2)
    pre_params = (mu, w0, a0, k_k, k_a, w2, a2, g2)
    r, dec, k, v, av, bv, g, tw, xa, sg = rwkv_pre_fwd(prw, *pre_params)
    v3 = to_tiles(v)
    y3, ck = rwkv_scan_fwd(r, dec, k, av, bv, v3)
    y = from_tiles(y3)
    o_rw = rwkv_post_fwd(y, r, k, v, g, ln_w, ln_b, r_k)
    ab = matmul(o_att, big["w_att_branch"], name="mm_ab")
    rb = matmul(o_rw, w_rb, name="mm_rb")
    merged = merge_fwd(gate, ab, rb)
    m2 = matmul(merged, big["w_out"], name="mm_out")
    x1, h2 = resid_norm_fwd(x, m2, small["norm_mix_post"], small["norm_ffn_pre"])
    gg = matmul(h2, big["w_ffn_gate"], name="mm_fg")
    uu = matmul(h2, big["w_ffn_up"], name="mm_fu")
    act = swiglu_fwd(gg, uu)
    ff = matmul(act, big["w_ffn_down"], name="mm_fd")
    loss, dy, dff, d_nfp = loss_head(x1, ff, tgt, small["norm_ffn_post"])

    dact = matmul(dff, big["w_ffn_down"], tb=True, name="mm_dact")
    g_fd = matmul(act, dff, ta=True, out_dtype=BF16, name="mm_gfd")
    dgg, duu = swiglu_bwd(gg, uu, dact)
    g_fg = matmul(h2, dgg, ta=True, out_dtype=BF16, name="mm_gfg")
    g_fu = matmul(h2, duu, ta=True, out_dtype=BF16, name="mm_gfu")
    dh2a = matmul(dgg, big["w_ffn_gate"], tb=True, name="mm_dh2a")
    dh2b = matmul(duu, big["w_ffn_up"], tb=True, name="mm_dh2b")
    dx1, dm2, d_nfpre, d_nmpost = resid_norm_bwd(x1, dh2a, dh2b, small["norm_ffn_pre"], m2, small["norm_mix_post"], dy)
    dmerged = matmul(dm2, big["w_out"], tb=True, name="mm_dmerged")
    g_out = matmul(merged, dm2, ta=True, out_dtype=BF16, name="mm_gout")
    dgate, dab, drb = merge_bwd(gate, ab, rb, dmerged)
    do_att = matmul(dab, big["w_att_branch"], tb=True, out_dtype=BF16, name="mm_doatt")
    g_ab = matmul(o_att, dab, ta=True, out_dtype=BF16, name="mm_gab")
    do_rw = matmul(drb, w_rb, tb=True, name="mm_dorw")
    g_rb = matmul(o_rw, drb, ta=True, out_dtype=BF16, name="mm_grb")
    dq, dk_att, dv_att, dsk = attention_bwd(qkv, bias, sinks_b, cos2, sin2, do_att)
    dqkv = jnp.concatenate([dq, dk_att.astype(BF16), dv_att.astype(BF16)], axis=1)
    dy_s, dr_p, dk_p, dv_p, dg, d_lnw, d_lnb, d_rk = rwkv_post_bwd(y, r, k, v, g, ln_w, ln_b, r_k, do_rw)
    dr_s, ddec, dk_s, dav, dbv, dv3 = rwkv_scan_bwd(r, dec, k, av, bv, v3, ck, to_tiles(dy_s))
    dprw, dzw, dza, dmu, dw0, da0, dkk, dka = rwkv_pre_bwd(
        prw, *pre_params, dr_p, ddec, dk_p, dv_p, dav, dbv, dg, dr_s, dk_s, from_tiles(dv3))
    g_w2 = matmul(tw, dzw, ta=True, out_dtype=BF16, name="mm_gw2")
    g_a2 = matmul(xa, dza, ta=True, out_dtype=BF16, name="mm_ga2")
    g_g2 = matmul(sg, dg.astype(BF16), ta=True, out_dtype=BF16, name="mm_gg2")
    g_qkv = matmul(h, dqkv, ta=True, out_dtype=BF16, name="mm_gqkv")
    g_rw = matmul(h, dprw, ta=True, out_dtype=BF16, name="mm_grw")
    g_gate = matmul(h, dgate, ta=True, out_dtype=BF16, name="mm_ggate")
    dh_a = matmul(dqkv, w_qkv, tb=True, name="mm_dha")
    dh_b = matmul(dprw, w_rw, tb=True, name="mm_dhb")
    dh_c = matmul(dgate, w_gate, tb=True, name="mm_dhc")
    grad_x, d_nmpre = rms_bwd(x, small["norm_mix_pre"], dh_a, dh_b, dh_c, dx1)
    d_bias = colsum(dqkv)

    gsmall = {
        "norm_mix_pre": d_nmpre, "norm_mix_post": d_nmpost, "norm_ffn_pre": d_nfpre, "norm_ffn_post": d_nfp,
        "b_qkv": d_bias, "att_sinks": dsk[:, 0, :AT_GROUP].reshape(1, AT_KVH * AT_GROUP),
        "mu_shift": rw_restore(dmu), "w0": perm_cols(dw0), "a0": perm_cols(da0), "k_k": perm_cols(dkk),
        "k_a": perm_cols(dka), "r_k": d_rk.reshape(RW_N, RW_H).T.reshape(1, RW_C),
        "ln_x_w": perm_cols(d_lnw), "ln_x_b": perm_cols(d_lnb),
    }
    gbig = {
        "w_in": jnp.concatenate([g_qkv, rw_restore(g_rw), g_gate], axis=1),
        "w2": perm_cols(g_w2), "a2": perm_cols(g_a2), "g2": perm_cols(g_g2)[:D_GATE],
        "w_att_branch": g_ab,
        "w_rwkv_branch": g_rb.reshape(RW_N, RW_H, d).swapaxes(0, 1).reshape(RW_C, d),
        "w_out": g_out, "w_ffn_gate": g_fg, "w_ffn_up": g_fu, "w_ffn_down": g_fd,
    }
    return loss, grad_x, gsmall, gbig


MESH = pl.DeviceIdType.MESH
ANY = pl.BlockSpec(memory_space=pl.ANY)


def _place():
    x, y, c = lax.axis_index("x"), lax.axis_index("y"), lax.axis_index("c")
    chips = [(1 - x, y), (x, 1 - y), (1 - x, 1 - y)]
    return x, y, c, chips


def _remote(src, dst, send_sems, recv_sems, k, dev):
    return pltpu.make_async_remote_copy(src_ref=src, dst_ref=dst, send_sem=send_sems.at[k], recv_sem=recv_sems.at[k],
                                        device_id=dev, device_id_type=MESH)


def gather_weights(shards):
    n = len(shards)

    def body(*refs):
        ins, outs = refs[:n], refs[n:2 * n]
        send_sems, recv_sems, loc_sems = refs[2 * n:]
        x, y, c, chips = _place()
        me = 2 * x + y
        sib = (x, y, 1 - c)

        def half(ref, which):
            hr = ref.shape[0] // 2
            return ref.at[pl.ds(which * hr, hr), :]

        local = [pltpu.make_async_copy(ins[i], outs[i].at[me], loc_sems.at[i]) for i in range(n)]
        for cp in local:
            cp.start()
        sends = []
        for i in range(n):
            for j, chip in enumerate(chips):
                sends.append(_remote(half(ins[i], c), half(outs[i].at[me], c), send_sems, recv_sems, 6 * i + j,
                                     (*chip, c)))
        for cp in sends:
            cp.start()
        passed = []
        for i in range(n):
            for j, chip in enumerate(chips):
                got = half(outs[i].at[2 * chip[0] + chip[1]], c)
                _remote(got, got, send_sems, recv_sems, 6 * i + j, sib).wait_recv()
                cp = _remote(got, got, send_sems, recv_sems, 6 * i + 3 + j, sib)
                cp.start()
                passed.append(cp)
        for i in range(n):
            for j, chip in enumerate(chips):
                got = half(outs[i].at[2 * chip[0] + chip[1]], 1 - c)
                _remote(got, got, send_sems, recv_sems, 6 * i + 3 + j, sib).wait_recv()
        for cp in sends + passed:
            cp.wait_send()
        for cp in local:
            cp.wait()

    return pl.pallas_call(
        body, name="gather_weights",
        in_specs=[ANY] * n, out_specs=[ANY] * n,
        out_shape=[jax.ShapeDtypeStruct((4,) + s.shape, s.dtype) for s in shards],
        scratch_shapes=[pltpu.SemaphoreType.DMA((6 * n,)), pltpu.SemaphoreType.DMA((6 * n,)),
                        pltpu.SemaphoreType.DMA((n,))],
    )(*shards)


def swap_with_sibling(blocks, name):
    n = len(blocks)

    def body(*refs):
        ins, outs = refs[:n], refs[n:2 * n]
        send_sems, recv_sems = refs[2 * n:]
        x, y, c, _ = _place()
        cps = [_remote(ins[i], outs[i], send_sems, recv_sems, i, (x, y, 1 - c)) for i in range(n)]
        for cp in cps:
            cp.start()
        for cp in cps:
            cp.wait()

    return pl.pallas_call(
        body, name=name, in_specs=[ANY] * n, out_specs=[ANY] * n,
        out_shape=[jax.ShapeDtypeStruct(b.shape, b.dtype) for b in blocks],
        scratch_shapes=[pltpu.SemaphoreType.DMA((n,)), pltpu.SemaphoreType.DMA((n,))],
    )(*blocks)


def scatter_to_owners(parts):
    n = len(parts)

    def body(*refs):
        ins, outs = refs[:n], refs[n:2 * n]
        send_sems, recv_sems = refs[2 * n:]
        x, y, c, chips = _place()
        cps = []
        for i in range(n):
            for j, chip in enumerate(chips):
                cps.append(_remote(ins[i].at[2 * chip[0] + chip[1]], outs[i].at[j], send_sems, recv_sems, 3 * i + j,
                                   (*chip, c)))
        for cp in cps:
            cp.start()
        for cp in cps:
            cp.wait()

    return pl.pallas_call(
        body, name="scatter_to_owners", in_specs=[ANY] * n, out_specs=[ANY] * n,
        out_shape=[jax.ShapeDtypeStruct((3,) + p.shape[1:], p.dtype) for p in parts],
        scratch_shapes=[pltpu.SemaphoreType.DMA((3 * n,)), pltpu.SemaphoreType.DMA((3 * n,))],
    )(*parts)


def share_halves(halves):
    n = len(halves)

    def body(*refs):
        ins, outs = refs[:n], refs[n:2 * n]
        send_sems, recv_sems, loc_sems = refs[2 * n:]
        x, y, c, _ = _place()
        local = [pltpu.make_async_copy(ins[i], outs[i].at[c], loc_sems.at[i]) for i in range(n)]
        cps = [_remote(ins[i], outs[i].at[c], send_sems, recv_sems, i, (x, y, 1 - c)) for i in range(n)]
        for cp in local + cps:
            cp.start()
        for i in range(n):
            other = outs[i].at[1 - c]
            _remote(other, other, send_sems, recv_sems, i, (x, y, 1 - c)).wait_recv()
        for cp in cps:
            cp.wait_send()
        for cp in local:
            cp.wait()

    return pl.pallas_call(
        body, name="share_halves", in_specs=[ANY] * n, out_specs=[ANY] * n,
        out_shape=[jax.ShapeDtypeStruct((2,) + h.shape, h.dtype) for h in halves],
        scratch_shapes=[pltpu.SemaphoreType.DMA((n,)), pltpu.SemaphoreType.DMA((n,)), pltpu.SemaphoreType.DMA((n,))],
    )(*halves)


def allreduce_small(v):
    rows = v.shape[0]

    def body(v_ref, o_ref, buf, send_sems, recv_sems):
        x, y, c, chips = _place()
        me, sib = (x, y, c), (x, y, 1 - c)

        def slot(px, py, pc):
            return buf.at[4 * px + 2 * py + pc]

        def copy(k, block, to, src=None):
            return _remote(slot(*block) if src is None else src, slot(*block), send_sems, recv_sems, k, to)

        buf[4 * x + 2 * y + c] = v_ref[...]
        first = [copy(0, me, sib, src=v_ref)]
        first += [copy(1 + j, me, (*chip, c), src=v_ref) for j, chip in enumerate(chips)]
        for cp in first:
            cp.start()
        passed = [copy(4 + j, (*chip, c), sib) for j, chip in enumerate(chips)]
        for j, chip in enumerate(chips):
            copy(1 + j, (*chip, c), me).wait_recv()
            passed[j].start()
        copy(0, sib, me).wait_recv()
        for j, chip in enumerate(chips):
            copy(4 + j, (*chip, 1 - c), me).wait_recv()
        for cp in first + passed:
            cp.wait_send()
        acc = buf[0]
        for k in range(1, 8):
            acc = acc + buf[k]
        o_ref[...] = acc

    vm = pl.BlockSpec(memory_space=pltpu.VMEM)
    return pl.pallas_call(
        body, name="allreduce_small", in_specs=[vm], out_specs=vm,
        out_shape=jax.ShapeDtypeStruct(v.shape, F32),
        scratch_shapes=[pltpu.VMEM((8, rows, LANES), F32), pltpu.SemaphoreType.DMA((7,)),
                        pltpu.SemaphoreType.DMA((7,))],
    )(v)


def _rows_tile(r):
    return _pick(r, (64, 32, 16, 8))


def add_pairs(a, b):
    _, r, c = a.shape
    tr = _rows_tile(r)
    spec = pl.BlockSpec((1, tr, c), lambda s, i: (s, i, 0))

    def body(a_ref, b_ref, o_ref):
        o_ref[...] = (a_ref[...].astype(F32) + b_ref[...].astype(F32)).astype(o_ref.dtype)

    return pl.pallas_call(
        body, name="add_pairs", grid=(4, r // tr), in_specs=[spec, spec], out_specs=spec,
        out_shape=jax.ShapeDtypeStruct(a.shape, a.dtype), compiler_params=_cparams(("parallel", "parallel")),
    )(a, b)


def add_four(mine, others):
    r, c = mine.shape
    tr = _rows_tile(r)

    def body(m_ref, o_ref, out_ref):
        acc = m_ref[...].astype(F32)
        for j in range(3):
            acc = acc + o_ref[j].astype(F32)
        out_ref[...] = acc

    return pl.pallas_call(
        body, name="add_four", grid=(r // tr,),
        in_specs=[pl.BlockSpec((tr, c), lambda i: (i, 0)), pl.BlockSpec((3, tr, c), lambda i: (0, i, 0))],
        out_specs=pl.BlockSpec((tr, c), lambda i: (i, 0)),
        out_shape=jax.ShapeDtypeStruct((r, c), F32), compiler_params=_cparams(("parallel",)),
    )(mine, others)


def reduce_sharded(blocks):
    cx, cy, cc = lax.axis_index("x"), lax.axis_index("y"), lax.axis_index("c")
    me = 2 * cx + cy
    mine, other = [], []
    for g4 in blocks:
        hr = g4.shape[1] // 2
        mine.append(lax.dynamic_slice_in_dim(g4, cc * hr, hr, axis=1))
        other.append(lax.dynamic_slice_in_dim(g4, (1 - cc) * hr, hr, axis=1))
    from_sibling = swap_with_sibling(other, "swap_halves")
    pair = [add_pairs(a, b) for a, b in zip(mine, from_sibling)]
    from_chips = scatter_to_owners(pair)
    sums = [add_four(lax.dynamic_index_in_dim(p, me, 0, keepdims=False), t) for p, t in zip(pair, from_chips)]
    return share_halves(sums)


ADAM_LR = 0.001
ADAM_B1 = 0.9
ADAM_B2 = 0.999
ADAM_EPS = 1e-08
ADAM_WD = 0.01
ADAM_STEP = 10


def adamw(w, g, m, v):
    r, c = w.shape
    tr = _rows_tile(r)
    spec = pl.BlockSpec((tr, c), lambda i: (i, 0))

    def body(w_ref, g_ref, m_ref, v_ref, d_ref, nm_ref, nv_ref):
        gv = g_ref[...]
        nm = ADAM_B1 * m_ref[...] + (1.0 - ADAM_B1) * gv
        nv = ADAM_B2 * v_ref[...] + (1.0 - ADAM_B2) * jnp.square(gv)
        m_hat = nm / (1.0 - ADAM_B1 ** ADAM_STEP)
        v_hat = nv / (1.0 - ADAM_B2 ** ADAM_STEP)
        d_ref[...] = -ADAM_LR * (m_hat / (jnp.sqrt(v_hat) + ADAM_EPS) + ADAM_WD * w_ref[...])
        nm_ref[...] = nm
        nv_ref[...] = nv

    out = jax.ShapeDtypeStruct((r, c), F32)
    return pl.pallas_call(
        body, name="adamw", grid=(r // tr,), in_specs=[spec] * 4, out_specs=[spec] * 3, out_shape=[out] * 3,
        compiler_params=_cparams(("parallel",)),
    )(w, g, m, v)


WEIGHTS = ["norm_mix_pre", "norm_mix_post", "norm_ffn_pre", "norm_ffn_post", "w_in", "b_qkv", "att_sinks", "mu_shift",
           "w0", "w2", "a0", "a2", "g2", "k_k", "k_a", "r_k", "ln_x_w", "ln_x_b", "w_att_branch", "w_rwkv_branch",
           "w_out", "w_ffn_gate", "w_ffn_up", "w_ffn_down"]
BIG = {"w_in": 1, "w2": 1, "a2": 1, "g2": 1, "w_att_branch": 0, "w_rwkv_branch": 0, "w_out": 0, "w_ffn_gate": 1,
       "w_ffn_up": 1, "w_ffn_down": 0}
SMALL = [n for n in WEIGHTS if n not in BIG]
N_CHIPS = 4


def _whole(g4, axis):
    if axis == 0:
        return g4.reshape(g4.shape[0] * g4.shape[1], g4.shape[2])
    return jnp.swapaxes(g4, 0, 1).reshape(g4.shape[1], g4.shape[0] * g4.shape[2])


def _by_shard(w, axis):
    if axis == 0:
        return w.reshape(N_CHIPS, w.shape[0] // N_CHIPS, w.shape[1])
    return jnp.swapaxes(w.reshape(w.shape[0], N_CHIPS, w.shape[1] // N_CHIPS), 0, 1)


def _pack(parts):
    flat = jnp.concatenate([parts[n].reshape(-1) for n in SMALL])
    rows = -(-flat.shape[0] // (LANES * SUBLANES)) * SUBLANES
    return jnp.pad(flat, (0, rows * LANES - flat.shape[0])).reshape(rows, LANES)


def _unpack(packed, like):
    flat = packed.reshape(-1)
    out, off = {}, 0
    for n in SMALL:
        size = like[n].size
        out[n] = flat[off:off + size].reshape(like[n].shape)
        off += size
    return out


def kernel(x, norm_mix_pre, norm_mix_post, norm_ffn_pre, norm_ffn_post, w_in, b_qkv, att_sinks, mu_shift, w0, w2, a0, a2, g2, k_k, k_a, r_k, ln_x_w, ln_x_b, w_att_branch, w_rwkv_branch, w_out, w_ffn_gate, w_ffn_up, w_ffn_down, loss_target, m_norm_mix_pre, m_norm_mix_post, m_norm_ffn_pre, m_norm_ffn_post, m_w_in, m_b_qkv, m_att_sinks, m_mu_shift, m_w0, m_w2, m_a0, m_a2, m_g2, m_k_k, m_k_a, m_r_k, m_ln_x_w, m_ln_x_b, m_w_att_branch, m_w_rwkv_branch, m_w_out, m_w_ffn_gate, m_w_ffn_up, m_w_ffn_down, v_norm_mix_pre, v_norm_mix_post, v_norm_ffn_pre, v_norm_ffn_post, v_w_in, v_b_qkv, v_att_sinks, v_mu_shift, v_w0, v_w2, v_a0, v_a2, v_g2, v_k_k, v_k_a, v_r_k, v_ln_x_w, v_ln_x_b, v_w_att_branch, v_w_rwkv_branch, v_w_out, v_w_ffn_gate, v_w_ffn_up, v_w_ffn_down):
    given = dict(locals())
    wts = {n: given[n] for n in WEIGHTS}
    mom1 = {n: given["m_" + n] for n in WEIGHTS}
    mom2 = {n: given["v_" + n] for n in WEIGHTS}
    names = list(BIG)
    gathered = gather_weights([wts[n][0].astype(BF16) for n in names])
    big = {n: _whole(g4, BIG[n]) for n, g4 in zip(names, gathered)}
    small = {n: wts[n].reshape(1, -1) for n in SMALL}

    loss, grad_x, gsmall, gbig = local_step(x[0], loss_target[0], small, big)

    wholes = reduce_sharded([_by_shard(gbig[n], BIG[n]) for n in names])
    grads = {n: w2h.reshape(wts[n].shape[1:]) for n, w2h in zip(names, wholes)}

    gsum = _unpack(allreduce_small(_pack(gsmall)), small)

    outs_g, outs_d, outs_m, outs_v = {}, {}, {}, {}
    for n in names:
        d, nm, nv = adamw(wts[n][0], grads[n], mom1[n][0], mom2[n][0])
        outs_g[n], outs_d[n], outs_m[n], outs_v[n] = (t[None] for t in (grads[n], d, nm, nv))
    pk = lambda src: _pack({n: src[n] for n in SMALL})
    d, nm, nv = adamw(pk(wts), _pack(gsum), pk(mom1), pk(mom2))
    du, mu, vu = _unpack(d, small), _unpack(nm, small), _unpack(nv, small)
    for n in SMALL:
        outs_g[n], outs_d[n], outs_m[n], outs_v[n] = (t[n].reshape(wts[n].shape) for t in (gsum, du, mu, vu))

    total = lax.psum(loss[0, 0], ("x", "y", "c"))
    return (total, grad_x[None], *[outs_g[n] for n in WEIGHTS], *[outs_d[n] for n in WEIGHTS],
            *[outs_m[n] for n in WEIGHTS], *[outs_v[n] for n in WEIGHTS])
```

```python
import jax
import jax.numpy as jnp
from jax import lax
from jax.experimental import pallas as pl
from jax.experimental.pallas import tpu as pltpu

F32 = jnp.float32
BF16 = jnp.bfloat16

LANES = 128
SUBLANES = 8
VMEM_LIMIT = 56 * 1024 * 1024

RW_H = 64
RW_N = 64
RW_C = RW_H * RW_N
RW_NB = RW_C // LANES
SCAN_CHUNK = 8


def _cparams(sem=None):
    return pltpu.CompilerParams(dimension_semantics=sem, vmem_limit_bytes=VMEM_LIMIT)


def _fold(x):
    return x + pltpu.roll(x, 64, axis=x.ndim - 1)


def _scan_step_fwd(t, src_ref, dst_ref, r_ref, w_ref, k_ref, a_ref, b_ref, v_ref):
    vt = v_ref[t]
    acc = jnp.zeros((RW_N, LANES), F32)
    for j in range(RW_NB):
        ls = slice(j * LANES, (j + 1) * LANES)
        acc = acc + src_ref[j] * a_ref[t:t + 1, ls]
    sa = _fold(acc)
    yacc = jnp.zeros((RW_N, LANES), F32)
    for j in range(RW_NB):
        ls = slice(j * LANES, (j + 1) * LANES)
        s_new = src_ref[j] * w_ref[t:t + 1, ls] + sa * b_ref[t:t + 1, ls] + vt * k_ref[t:t + 1, ls]
        dst_ref[j] = s_new
        yacc = yacc + s_new * r_ref[t:t + 1, ls]
    return _fold(yacc), sa


def rwkv_scan_fwd(r, w, k, a, b, v3):
    s_len = r.shape[0]
    nchunk = s_len // SCAN_CHUNK

    def body(r_ref, w_ref, k_ref, a_ref, b_ref, v_ref, y_ref, ck_ref, st_ref):
        @pl.when(pl.program_id(0) == 0)
        def _():
            st_ref[...] = jnp.zeros_like(st_ref)

        ck_ref[0] = st_ref[...]

        for t in range(SCAN_CHUNK):
            y, _ = _scan_step_fwd(t, st_ref, st_ref, r_ref, w_ref, k_ref, a_ref, b_ref, v_ref)
            y_ref[t] = y

    row = pl.BlockSpec((SCAN_CHUNK, RW_C), lambda i: (i, 0))
    til = pl.BlockSpec((SCAN_CHUNK, RW_N, LANES), lambda i: (i, 0, 0))
    return pl.pallas_call(
        body,
        name="rwkv_scan_fwd",
        grid=(nchunk,),
        in_specs=[row, row, row, row, row, til],
        out_specs=[til, pl.BlockSpec((1, RW_NB, RW_N, LANES), lambda i: (i, 0, 0, 0))],
        out_shape=[
            jax.ShapeDtypeStruct((s_len, RW_N, LANES), F32),
            jax.ShapeDtypeStruct((nchunk, RW_NB, RW_N, LANES), F32),
        ],
        scratch_shapes=[pltpu.VMEM((RW_NB, RW_N, LANES), F32)],
        compiler_params=_cparams(("arbitrary",)),
    )(r, w, k, a, b, v3)


def rwkv_scan_bwd(r, w, k, a, b, v3, ck, dy3):
    s_len = r.shape[0]
    nchunk = s_len // SCAN_CHUNK

    def body(r_ref, w_ref, k_ref, a_ref, b_ref, v_ref, ck_ref, dy_ref,
             dr_ref, dw_ref, dk_ref, da_ref, db_ref, dv_ref, hist_ref, sa_ref, ds_ref):
        @pl.when(pl.program_id(0) == 0)
        def _():
            ds_ref[...] = jnp.zeros_like(ds_ref)

        hist_ref[0] = ck_ref[0]

        for t in range(SCAN_CHUNK):
            _, sa = _scan_step_fwd(t, hist_ref.at[t], hist_ref.at[t + 1], r_ref, w_ref, k_ref, a_ref, b_ref, v_ref)
            sa_ref[t] = sa

        for t in reversed(range(SCAN_CHUNK)):
            vt = v_ref[t]
            dyt = dy_ref[t]
            sat = sa_ref[t]
            dv_acc = jnp.zeros((RW_N, LANES), F32)
            dsa_acc = jnp.zeros((RW_N, LANES), F32)
            for j in range(RW_NB):
                ls = slice(j * LANES, (j + 1) * LANES)
                row = (slice(t, t + 1), ls)
                ds_j = ds_ref[j] + dyt * r_ref[row]
                ds_ref[j] = ds_j
                dr_ref[row] = jnp.sum(hist_ref[t + 1, j] * dyt, axis=0, keepdims=True)
                dv_acc = dv_acc + ds_j * k_ref[row]
                dk_ref[row] = jnp.sum(ds_j * vt, axis=0, keepdims=True)
                dsa_acc = dsa_acc + ds_j * b_ref[row]
                db_ref[row] = jnp.sum(ds_j * sat, axis=0, keepdims=True)
                dw_ref[row] = jnp.sum(ds_j * hist_ref[t, j], axis=0, keepdims=True)
            dv_ref[t] = _fold(dv_acc)
            dsa = _fold(dsa_acc)
            for j in range(RW_NB):
                ls = slice(j * LANES, (j + 1) * LANES)
                row = (slice(t, t + 1), ls)
                da_ref[row] = jnp.sum(hist_ref[t, j] * dsa, axis=0, keepdims=True)
                ds_ref[j] = ds_ref[j] * w_ref[row] + dsa * a_ref[row]

    rev = lambda i: (nchunk - 1 - i, 0)
    rev3 = lambda i: (nchunk - 1 - i, 0, 0)
    row = pl.BlockSpec((SCAN_CHUNK, RW_C), rev)
    til = pl.BlockSpec((SCAN_CHUNK, RW_N, LANES), rev3)
    rows = jax.ShapeDtypeStruct((s_len, RW_C), F32)
    return pl.pallas_call(
        body,
        name="rwkv_scan_bwd",
        grid=(nchunk,),
        in_specs=[row, row, row, row, row, til,
                  pl.BlockSpec((1, RW_NB, RW_N, LANES), lambda i: (nchunk - 1 - i, 0, 0, 0)), til],
        out_specs=[row, row, row, row, row, til],
        out_shape=[rows, rows, rows, rows, rows, jax.ShapeDtypeStruct((s_len, RW_N, LANES), F32)],
        scratch_shapes=[
            pltpu.VMEM((SCAN_CHUNK + 1, RW_NB, RW_N, LANES), F32),
            pltpu.VMEM((SCAN_CHUNK, RW_N, LANES), F32),
            pltpu.VMEM((RW_NB, RW_N, LANES), F32),
        ],
        compiler_params=_cparams(("arbitrary",)),
    )(r, w, k, a, b, v3, ck, dy3)


def _pick(n, cands):
    for c in cands:
        if n % c == 0:
            return c
    return n


MM_VMEM_BUDGET = 40 * 1024 * 1024
MM_FLOPS = 8.5e14
MM_HBM = 2.2e12
MM_STEP = 0.4e-6


def _mm_plan(m, n, k, out_bytes):
    divs = lambda d: [t for t in range(LANES, d + 1, LANES) if d % t == 0] or [d]
    best = None
    for tm in divs(m):
        for tn in divs(n):
            for tk in divs(k):
                nk = k // tk
                vmem = 4 * (tm * tk + tk * tn) + (4 * tm * tn if nk > 1 else 0) + 2 * tm * tn * out_bytes
                if vmem > MM_VMEM_BUDGET:
                    continue
                steps = (m // tm) * (n // tn) * nk
                for n_outer in (False, True):
                    if nk > 1:
                        traffic = steps * (tm * tk + tk * tn) * 2
                    elif n_outer:
                        traffic = (n // tn) * (k * tn + m * k) * 2
                    else:
                        traffic = (m // tm) * (tm * k + k * n) * 2
                    cost = max(2.0 * m * n * k / MM_FLOPS, (traffic + m * n * out_bytes) / MM_HBM) + steps * MM_STEP
                    if best is None or cost < best[0]:
                        best = (cost, tm, tn, tk, n_outer)
    return best[1:]


def matmul(a, b, *, ta=False, tb=False, out_dtype=F32, name="matmul", plan=None):
    m, kdim = (a.shape[1], a.shape[0]) if ta else a.shape
    n = b.shape[0] if tb else b.shape[1]
    assert (b.shape[1] if tb else b.shape[0]) == kdim
    tm, tn, tk, n_outer = plan or _mm_plan(m, n, kdim, jnp.dtype(out_dtype).itemsize)
    nk = kdim // tk
    dims = (((0 if ta else 1,), (1 if tb else 0,)), ((), ()))

    def body(a_ref, b_ref, o_ref, *acc):
        prod = lax.dot_general(a_ref[...].astype(BF16), b_ref[...].astype(BF16), dims, preferred_element_type=F32)
        if nk == 1:
            o_ref[...] = prod.astype(o_ref.dtype)
            return
        acc_ref, = acc
        kk = pl.program_id(2)

        @pl.when(kk == 0)
        def _():
            acc_ref[...] = prod

        @pl.when(kk > 0)
        def _():
            acc_ref[...] += prod

        @pl.when(kk == nk - 1)
        def _():
            o_ref[...] = acc_ref[...].astype(o_ref.dtype)

    ij = (lambda p, q: (q, p)) if n_outer else (lambda p, q: (p, q))
    a_map = (lambda p, q, k: (k, ij(p, q)[0])) if ta else (lambda p, q, k: (ij(p, q)[0], k))
    b_map = (lambda p, q, k: (ij(p, q)[1], k)) if tb else (lambda p, q, k: (k, ij(p, q)[1]))
    grid = (n // tn, m // tm, nk) if n_outer else (m // tm, n // tn, nk)
    return pl.pallas_call(
        body,
        name=name,
        grid=grid,
        in_specs=[pl.BlockSpec((tk, tm) if ta else (tm, tk), a_map), pl.BlockSpec((tn, tk) if tb else (tk, tn), b_map)],
        out_specs=pl.BlockSpec((tm, tn), lambda p, q, k: ij(p, q)),
        out_shape=jax.ShapeDtypeStruct((m, n), out_dtype),
        scratch_shapes=[pltpu.VMEM((tm, tn), F32)] if nk > 1 else [],
        compiler_params=_cparams(("parallel", "parallel", "arbitrary")),
    )(a, b)


@jax.custom_vjp
def hsum(x):
    acc = x[:, 0:LANES]
    for j in range(1, RW_NB):
        acc = acc + x[:, j * LANES:(j + 1) * LANES]
    return _fold(acc)


def _hsum_fwd(x):
    return hsum(x), None


def _hsum_bwd(_, ct):
    return (jnp.concatenate([_fold(ct)] * RW_NB, axis=1),)


hsum.defvjp(_hsum_fwd, _hsum_bwd)


@jax.custom_vjp
def hbcast(s):
    return jnp.concatenate([s] * RW_NB, axis=1)


def _hbcast_fwd(s):
    return hbcast(s), None


def _hbcast_bwd(_, ct):
    acc = ct[:, 0:LANES]
    for j in range(1, RW_NB):
        acc = acc + ct[:, j * LANES:(j + 1) * LANES]
    return (acc,)


hbcast.defvjp(_hbcast_fwd, _hbcast_bwd)


@jax.custom_vjp
def bdot(x, w):
    return jnp.dot(x.astype(BF16), w, preferred_element_type=F32)


def _bdot_fwd(x, w):
    return bdot(x, w), w


def _bdot_bwd(w, ct):
    dx = lax.dot_general(ct.astype(BF16), w, (((1,), (1,)), ((), ())), preferred_element_type=F32)
    return dx, jnp.zeros_like(w)


bdot.defvjp(_bdot_fwd, _bdot_bwd)

RMS_EPS = 1e-6
GN_EPS = 64e-5


def f_rms(x, g):
    return x * lax.rsqrt(jnp.mean(x * x, axis=-1, keepdims=True) + RMS_EPS) * g


def _softplus(z):
    return jnp.maximum(z, 0.0) + jnp.log1p(jnp.exp(-jnp.abs(z)))


def f_pre(xk, xg, xw, xa, ew, ea, w0, a0, k_k, k_a, w2, a2, g2):
    tw = jnp.tanh(xw)
    sg = jax.nn.sigmoid(xg)
    wlog = -_softplus(-(w0 + bdot(tw, w2) + ew)) - 0.5
    decay = jnp.exp(-jnp.exp(wlog))
    a = jax.nn.sigmoid(a0 + bdot(xa, a2) + ea)
    g = bdot(sg, g2)
    kk0 = xk * k_k
    nrm = jnp.sqrt(hbcast(hsum(kk0 * kk0)))
    kk = kk0 / jnp.maximum(nrm, 1e-12)
    k = xk * (1.0 + (a - 1.0) * k_a)
    return decay, k, -kk, kk * a, g, tw, sg


def f_post(y, r, k, v, g, ln_w, ln_b, r_k):
    mu = hbcast(hsum(y)) * (1.0 / RW_N)
    yc = y - mu
    var = hbcast(hsum(yc * yc)) * (1.0 / RW_N)
    yn = yc * lax.rsqrt(var + GN_EPS) * ln_w + ln_b
    bonus = hbcast(hsum(r * k * r_k)) * v
    return (yn + bonus) * g


def f_merge(ga, gr, ab, rb):
    return jax.nn.sigmoid(ga) * ab + jax.nn.sigmoid(gr) * rb


def f_swiglu(gg, uu):
    return gg * jax.nn.sigmoid(gg) * uu


def _row(tt, width, cb=0, rev_n=None):
    if rev_n is None:
        return pl.BlockSpec((tt, width), lambda i: (i, cb))
    return pl.BlockSpec((tt, width), lambda i: (rev_n - 1 - i, cb))


def _full(arr):
    nd = arr.ndim
    return pl.BlockSpec(arr.shape, lambda i: (0,) * nd)


def _acc_init(i_first, *refs):
    @pl.when(i_first)
    def _():
        for r in refs:
            r[...] = jnp.zeros_like(r)


def rms_fwd(x, g, *, tt=128):
    s_len, d = x.shape

    def body(x_ref, g_ref, o_ref):
        o_ref[...] = f_rms(x_ref[...], g_ref[...]).astype(BF16)

    return pl.pallas_call(
        body, name="rms_fwd", grid=(s_len // tt,),
        in_specs=[_row(tt, d), _full(g)], out_specs=_row(tt, d),
        out_shape=jax.ShapeDtypeStruct((s_len, d), BF16),
        compiler_params=_cparams(("parallel",)),
    )(x, g)


def rwkv_pre_fwd(proj, mu, w0, a0, k_k, k_a, w2, a2, g2, *, tt=64):
    s_len, c = proj.shape
    nt = s_len // tt
    sub = tt // SUBLANES

    def body(p_ref, pb_ref, mu_ref, w0_ref, a0_ref, kk_ref, ka_ref, w2_ref, a2_ref, g2_ref,
             r_ref, dec_ref, k_ref, v_ref, av_ref, bv_ref, g_ref, tw_ref, xa_ref, sg_ref):
        i = pl.program_id(0)
        cur = p_ref[...]
        edge = jnp.where(i > 0, pb_ref[SUBLANES - 1:SUBLANES, :], 0.0)
        rows = lax.broadcasted_iota(jnp.int32, cur.shape, 0)
        prev = jnp.where(rows == 0, edge, pltpu.roll(cur, 1, axis=0))
        xs = cur + (prev - cur) * mu_ref[...]
        xr, xk, xv = xs[:, 0:RW_C], xs[:, RW_C:2 * RW_C], xs[:, 2 * RW_C:3 * RW_C]
        xg = xs[:, 3 * RW_C:3 * RW_C + 512]
        xw = xs[:, 3 * RW_C + 512:3 * RW_C + 640]
        xa = xs[:, 3 * RW_C + 640:3 * RW_C + 768]
        zero = jnp.zeros((tt, RW_C), F32)
        dec, k, av, bv, g, tw, sg = f_pre(xk, xg, xw, xa, zero, zero, w0_ref[...], a0_ref[...], kk_ref[...],
                                          ka_ref[...], w2_ref[...], a2_ref[...], g2_ref[...])
        r_ref[...] = xr
        dec_ref[...] = dec
        k_ref[...] = k
        v_ref[...] = xv
        av_ref[...] = av
        bv_ref[...] = bv
        g_ref[...] = g
        tw_ref[...] = tw.astype(BF16)
        xa_ref[...] = xa.astype(BF16)
        sg_ref[...] = sg.astype(BF16)

    rows_f = jax.ShapeDtypeStruct((s_len, RW_C), F32)
    prev_spec = pl.BlockSpec((SUBLANES, c), lambda i: (jnp.maximum(i * sub - 1, 0), 0))
    params = [mu, w0, a0, k_k, k_a, w2, a2, g2]
    return pl.pallas_call(
        body, name="rwkv_pre_fwd", grid=(nt,),
        in_specs=[_row(tt, c), prev_spec] + [_full(p) for p in params],
        out_specs=[_row(tt, RW_C)] * 7 + [_row(tt, 128), _row(tt, 128), _row(tt, 512)],
        out_shape=[rows_f] * 7 + [jax.ShapeDtypeStruct((s_len, 128), BF16), jax.ShapeDtypeStruct((s_len, 128), BF16),
                                  jax.ShapeDtypeStruct((s_len, 512), BF16)],
        compiler_params=_cparams(("parallel",)),
    )(proj, proj, *params)


def rwkv_pre_bwd(proj, mu, w0, a0, k_k, k_a, w2, a2, g2, d_r, d_dec, d_k, d_v, d_av, d_bv, d_g, d_r2, d_k2, d_v2,
                 *, tt=32):
    s_len, c = proj.shape
    nt = s_len // tt
    sub = tt // SUBLANES

    def body(p_ref, pb_ref, mu_ref, w0_ref, a0_ref, kk_ref, ka_ref, w2_ref, a2_ref, g2_ref,
             dr_ref, ddec_ref, dk_ref, dv_ref, dav_ref, dbv_ref, dg_ref, dr2_ref, dk2_ref, dv2_ref,
             dp_ref, dzw_ref, dza_ref, dmu_ref, dw0_ref, da0_ref, dkk_ref, dka_ref, carry_ref):
        step = pl.program_id(0)
        i = nt - 1 - step
        _acc_init(step == 0, dmu_ref, dw0_ref, da0_ref, dkk_ref, dka_ref, carry_ref)
        cur = p_ref[...]
        edge = jnp.where(i > 0, pb_ref[SUBLANES - 1:SUBLANES, :], 0.0)
        rows = lax.broadcasted_iota(jnp.int32, cur.shape, 0)
        prev = jnp.where(rows == 0, edge, pltpu.roll(cur, 1, axis=0))
        mu_v = mu_ref[...]
        xs = cur + (prev - cur) * mu_v
        xk = xs[:, RW_C:2 * RW_C]
        xg = xs[:, 3 * RW_C:3 * RW_C + 512]
        xw = xs[:, 3 * RW_C + 512:3 * RW_C + 640]
        xa = xs[:, 3 * RW_C + 640:3 * RW_C + 768]
        zero = jnp.zeros((tt, RW_C), F32)
        w2_v, a2_v, g2_v = w2_ref[...], a2_ref[...], g2_ref[...]

        def core(xk, xg, xw, xa, ew, ea, w0, a0, k_k, k_a):
            return f_pre(xk, xg, xw, xa, ew, ea, w0, a0, k_k, k_a, w2_v, a2_v, g2_v)[:5]

        _, vjp = jax.vjp(core, xk, xg, xw, xa, zero, zero, w0_ref[...], a0_ref[...], kk_ref[...], ka_ref[...])
        dxk, dxg, dxw, dxa, dzw, dza, dw0, da0, dkk, dka = vjp(
            (ddec_ref[...], dk_ref[...] + dk2_ref[...], dav_ref[...], dbv_ref[...], dg_ref[...]))
        dzw_ref[...] = dzw.astype(BF16)
        dza_ref[...] = dza.astype(BF16)
        dw0_ref[...] += dw0
        da0_ref[...] += da0
        dkk_ref[...] += dkk
        dka_ref[...] += dka
        dxs = jnp.concatenate([dr_ref[...] + dr2_ref[...], dxk, dv_ref[...] + dv2_ref[...], dxg, dxw, dxa], axis=1)
        dmu_ref[...] += jnp.sum(dxs * (prev - cur), axis=0, keepdims=True)
        to_prev = dxs * mu_v
        nxt = jnp.where(rows == tt - 1, carry_ref[...], pltpu.roll(to_prev, tt - 1, axis=0))
        carry_ref[...] = to_prev[0:1, :]
        dp_ref[...] = (dxs * (1.0 - mu_v) + nxt).astype(BF16)

    prev_spec = pl.BlockSpec((SUBLANES, c), lambda s: (jnp.maximum((nt - 1 - s) * sub - 1, 0), 0))
    params = [mu, w0, a0, k_k, k_a, w2, a2, g2]
    cts = [d_r, d_dec, d_k, d_v, d_av, d_bv, d_g, d_r2, d_k2, d_v2]
    vec = jax.ShapeDtypeStruct((1, RW_C), F32)
    acc = pl.BlockSpec((1, RW_C), lambda s: (0, 0))
    return pl.pallas_call(
        body, name="rwkv_pre_bwd", grid=(nt,),
        in_specs=[_row(tt, c, rev_n=nt), prev_spec] + [_full(p) for p in params] + [_row(tt, RW_C, rev_n=nt)] * 10,
        out_specs=[_row(tt, c, rev_n=nt), _row(tt, RW_C, rev_n=nt), _row(tt, RW_C, rev_n=nt),
                   pl.BlockSpec((1, c), lambda s: (0, 0)), acc, acc, acc, acc],
        out_shape=[jax.ShapeDtypeStruct((s_len, c), BF16), jax.ShapeDtypeStruct((s_len, RW_C), BF16),
                   jax.ShapeDtypeStruct((s_len, RW_C), BF16), jax.ShapeDtypeStruct((1, c), F32), vec, vec, vec, vec],
        scratch_shapes=[pltpu.VMEM((1, c), F32)],
        compiler_params=_cparams(("arbitrary",)),
    )(proj, proj, *params, *cts)


def rwkv_post_fwd(y, r, k, v, g, ln_w, ln_b, r_k, *, tt=64):
    s_len = y.shape[0]

    def body(y_ref, r_ref, k_ref, v_ref, g_ref, lw_ref, lb_ref, rk_ref, o_ref):
        o_ref[...] = f_post(y_ref[...], r_ref[...], k_ref[...], v_ref[...], g_ref[...],
                            lw_ref[...], lb_ref[...], rk_ref[...]).astype(BF16)

    return pl.pallas_call(
        body, name="rwkv_post_fwd", grid=(s_len // tt,),
        in_specs=[_row(tt, RW_C)] * 5 + [_full(ln_w), _full(ln_b), _full(r_k)],
        out_specs=_row(tt, RW_C), out_shape=jax.ShapeDtypeStruct((s_len, RW_C), BF16),
        compiler_params=_cparams(("parallel",)),
    )(y, r, k, v, g, ln_w, ln_b, r_k)


def rwkv_post_bwd(y, r, k, v, g, ln_w, ln_b, r_k, d_o, *, tt=32):
    s_len = y.shape[0]

    def body(y_ref, r_ref, k_ref, v_ref, g_ref, lw_ref, lb_ref, rk_ref, do_ref,
             dy_ref, dr_ref, dk_ref, dv_ref, dg_ref, dlw_ref, dlb_ref, drk_ref):
        _acc_init(pl.program_id(0) == 0, dlw_ref, dlb_ref, drk_ref)
        _, vjp = jax.vjp(f_post, y_ref[...], r_ref[...], k_ref[...], v_ref[...], g_ref[...],
                         lw_ref[...], lb_ref[...], rk_ref[...])
        dy, dr, dk, dv, dg, dlw, dlb, drk = vjp(do_ref[...].astype(F32))
        dy_ref[...] = dy
        dr_ref[...] = dr
        dk_ref[...] = dk
        dv_ref[...] = dv
        dg_ref[...] = dg
        dlw_ref[...] += dlw
        dlb_ref[...] += dlb
        drk_ref[...] += drk

    rows_f = jax.ShapeDtypeStruct((s_len, RW_C), F32)
    vec = jax.ShapeDtypeStruct((1, RW_C), F32)
    acc = pl.BlockSpec((1, RW_C), lambda s: (0, 0))
    return pl.pallas_call(
        body, name="rwkv_post_bwd", grid=(s_len // tt,),
        in_specs=[_row(tt, RW_C)] * 5 + [_full(ln_w), _full(ln_b), _full(r_k), _row(tt, RW_C)],
        out_specs=[_row(tt, RW_C)] * 5 + [acc] * 3, out_shape=[rows_f] * 5 + [vec] * 3,
        compiler_params=_cparams(("arbitrary",)),
    )(y, r, k, v, g, ln_w, ln_b, r_k, d_o)


def merge_fwd(gate, ab, rb, *, tt=128):
    s_len, d = ab.shape

    def body(ga_ref, gr_ref, a_ref, r_ref, o_ref):
        o_ref[...] = f_merge(ga_ref[...], gr_ref[...], a_ref[...], r_ref[...]).astype(BF16)

    return pl.pallas_call(
        body, name="merge_fwd", grid=(s_len // tt,),
        in_specs=[_row(tt, d, 0), _row(tt, d, 1), _row(tt, d), _row(tt, d)],
        out_specs=_row(tt, d), out_shape=jax.ShapeDtypeStruct((s_len, d), BF16),
        compiler_params=_cparams(("parallel",)),
    )(gate, gate, ab, rb)


def merge_bwd(gate, ab, rb, d_m, *, tt=64):
    s_len, d = ab.shape

    def body(ga_ref, gr_ref, a_ref, r_ref, dm_ref, dgate_ref, da_ref, dr_ref):
        _, vjp = jax.vjp(f_merge, ga_ref[...], gr_ref[...], a_ref[...], r_ref[...])
        dga, dgr, da, dr = vjp(dm_ref[...].astype(F32))
        dgate_ref[:, 0:d] = dga.astype(BF16)
        dgate_ref[:, d:2 * d] = dgr.astype(BF16)
        da_ref[...] = da.astype(BF16)
        dr_ref[...] = dr.astype(BF16)

    return pl.pallas_call(
        body, name="merge_bwd", grid=(s_len // tt,),
        in_specs=[_row(tt, d, 0), _row(tt, d, 1), _row(tt, d), _row(tt, d), _row(tt, d)],
        out_specs=[_row(tt, 2 * d), _row(tt, d), _row(tt, d)],
        out_shape=[jax.ShapeDtypeStruct((s_len, 2 * d), BF16), jax.ShapeDtypeStruct((s_len, d), BF16),
                   jax.ShapeDtypeStruct((s_len, d), BF16)],
        compiler_params=_cparams(("parallel",)),
    )(gate, gate, ab, rb, d_m)


def swiglu_fwd(gg, uu, *, tt=64):
    s_len, f = gg.shape

    def body(g_ref, u_ref, o_ref):
        o_ref[...] = f_swiglu(g_ref[...], u_ref[...]).astype(BF16)

    return pl.pallas_call(
        body, name="swiglu_fwd", grid=(s_len // tt,),
        in_specs=[_row(tt, f), _row(tt, f)], out_specs=_row(tt, f),
        out_shape=jax.ShapeDtypeStruct((s_len, f), BF16),
        compiler_params=_cparams(("parallel",)),
    )(gg, uu)


def swiglu_bwd(gg, uu, d_act, *, tt=32):
    s_len, f = gg.shape

    def body(g_ref, u_ref, d_ref, dg_ref, du_ref):
        _, vjp = jax.vjp(f_swiglu, g_ref[...], u_ref[...])
        dg, du = vjp(d_ref[...].astype(F32))
        dg_ref[...] = dg.astype(BF16)
        du_ref[...] = du.astype(BF16)

    out = jax.ShapeDtypeStruct((s_len, f), BF16)
    return pl.pallas_call(
        body, name="swiglu_bwd", grid=(s_len // tt,),
        in_specs=[_row(tt, f)] * 3, out_specs=[_row(tt, f)] * 2, out_shape=[out, out],
        compiler_params=_cparams(("parallel",)),
    )(gg, uu, d_act)


def resid_norm_fwd(x, m2, g_post, g_pre, *, tt=128):
    s_len, d = x.shape

    def body(x_ref, m_ref, gp_ref, gn_ref, x1_ref, h_ref):
        x1 = x_ref[...] + f_rms(m_ref[...], gp_ref[...])
        x1_ref[...] = x1
        h_ref[...] = f_rms(x1, gn_ref[...]).astype(BF16)

    return pl.pallas_call(
        body, name="resid_norm_fwd", grid=(s_len // tt,),
        in_specs=[_row(tt, d), _row(tt, d), _full(g_post), _full(g_pre)],
        out_specs=[_row(tt, d), _row(tt, d)],
        out_shape=[jax.ShapeDtypeStruct((s_len, d), F32), jax.ShapeDtypeStruct((s_len, d), BF16)],
        compiler_params=_cparams(("parallel",)),
    )(x, m2, g_post, g_pre)


def loss_head(x1, ff, tgt, g_post, *, tt=64):
    s_len, d = x1.shape

    def body(x1_ref, f_ref, t_ref, g_ref, loss_ref, dy_ref, df_ref, dg_ref):
        _acc_init(pl.program_id(0) == 0, loss_ref, dg_ref)
        nrm, vjp = jax.vjp(f_rms, f_ref[...], g_ref[...])
        err = x1_ref[...] + nrm - t_ref[...]
        per_tok = jnp.mean(err * err, axis=-1, keepdims=True)
        loss_ref[...] += 0.5 * jnp.sum(per_tok, axis=0, keepdims=True)
        dy = err * (1.0 / d)
        dff, dg = vjp(dy)
        dy_ref[...] = dy
        df_ref[...] = dff.astype(BF16)
        dg_ref[...] += dg

    return pl.pallas_call(
        body, name="loss_head", grid=(s_len // tt,),
        in_specs=[_row(tt, d)] * 3 + [_full(g_post)],
        out_specs=[pl.BlockSpec((1, LANES), lambda s: (0, 0)), _row(tt, d), _row(tt, d),
                   pl.BlockSpec((1, d), lambda s: (0, 0))],
        out_shape=[jax.ShapeDtypeStruct((1, LANES), F32), jax.ShapeDtypeStruct((s_len, d), F32),
                   jax.ShapeDtypeStruct((s_len, d), BF16), jax.ShapeDtypeStruct((1, d), F32)],
        compiler_params=_cparams(("arbitrary",)),
    )(x1, ff, tgt, g_post)


def resid_norm_bwd(x1, dh_a, dh_b, g_pre, m2, g_post, dy, *, tt=64):
    s_len, d = x1.shape

    def body(x1_ref, da_ref, db_ref, gn_ref, m_ref, gp_ref, dy_ref, dx1_ref, dm_ref, dgn_ref, dgp_ref):
        _acc_init(pl.program_id(0) == 0, dgn_ref, dgp_ref)
        _, vjp_n = jax.vjp(f_rms, x1_ref[...], gn_ref[...])
        dx1_n, dgn = vjp_n(da_ref[...] + db_ref[...])
        dx1 = dy_ref[...] + dx1_n
        _, vjp_p = jax.vjp(f_rms, m_ref[...], gp_ref[...])
        dm, dgp = vjp_p(dx1)
        dx1_ref[...] = dx1
        dm_ref[...] = dm.astype(BF16)
        dgn_ref[...] += dgn
        dgp_ref[...] += dgp

    acc = pl.BlockSpec((1, d), lambda s: (0, 0))
    vec = jax.ShapeDtypeStruct((1, d), F32)
    return pl.pallas_call(
        body, name="resid_norm_bwd", grid=(s_len // tt,),
        in_specs=[_row(tt, d)] * 3 + [_full(g_pre), _row(tt, d), _full(g_post), _row(tt, d)],
        out_specs=[_row(tt, d), _row(tt, d), acc, acc],
        out_shape=[jax.ShapeDtypeStruct((s_len, d), F32), jax.ShapeDtypeStruct((s_len, d), BF16), vec, vec],
        compiler_params=_cparams(("arbitrary",)),
    )(x1, dh_a, dh_b, g_pre, m2, g_post, dy)


def rms_bwd(x, g, dh_a, dh_b, dh_c, dres, *, tt=64):
    s_len, d = x.shape

    def body(x_ref, g_ref, a_ref, b_ref, c_ref, r_ref, dx_ref, dg_ref):
        _acc_init(pl.program_id(0) == 0, dg_ref)
        _, vjp = jax.vjp(f_rms, x_ref[...], g_ref[...])
        dx, dg = vjp(a_ref[...] + b_ref[...] + c_ref[...])
        dx_ref[...] = r_ref[...] + dx
        dg_ref[...] += dg

    return pl.pallas_call(
        body, name="rms_bwd", grid=(s_len // tt,),
        in_specs=[_row(tt, d), _full(g)] + [_row(tt, d)] * 4,
        out_specs=[_row(tt, d), pl.BlockSpec((1, d), lambda s: (0, 0))],
        out_shape=[jax.ShapeDtypeStruct((s_len, d), F32), jax.ShapeDtypeStruct((1, d), F32)],
        compiler_params=_cparams(("arbitrary",)),
    )(x, g, dh_a, dh_b, dh_c, dres)


def colsum(a, *, tt=256):
    s_len, c = a.shape

    def body(a_ref, o_ref):
        _acc_init(pl.program_id(0) == 0, o_ref)
        o_ref[...] += jnp.sum(a_ref[...].astype(F32), axis=0, keepdims=True)

    return pl.pallas_call(
        body, name="colsum", grid=(s_len // tt,),
        in_specs=[_row(tt, c)], out_specs=pl.BlockSpec((1, c), lambda s: (0, 0)),
        out_shape=jax.ShapeDtypeStruct((1, c), F32),
        compiler_params=_cparams(("arbitrary",)),
    )(a)


AT_HD = 128
AT_GROUP = 4
AT_KVH = 8
AT_BLK = 128
AT_QW = AT_GROUP * AT_HD
AT_KCOL = AT_KVH * AT_GROUP
AT_VCOL = AT_KCOL + AT_KVH
NEG_INF = -1e30
AT_SCALE = AT_HD ** -0.5


def _rope(t, cos2, sin2):
    return t * cos2 + pltpu.roll(t, AT_HD // 2, axis=1) * sin2


def _rope_t(d, cos2, sin2):
    return d * cos2 + pltpu.roll(d * sin2, AT_HD // 2, axis=1)


def _att_specs():
    prev = lambda i: jnp.maximum(i - 1, 0)
    blk = (AT_BLK, AT_HD)
    return [
        pl.BlockSpec((AT_BLK, AT_QW), lambda h, i: (i, h)),
        pl.BlockSpec(blk, lambda h, i: (i, AT_KCOL + h)),
        pl.BlockSpec(blk, lambda h, i: (prev(i), AT_KCOL + h)),
        pl.BlockSpec(blk, lambda h, i: (i, AT_VCOL + h)),
        pl.BlockSpec(blk, lambda h, i: (prev(i), AT_VCOL + h)),
        pl.BlockSpec((1, AT_QW), lambda h, i: (0, h)),
        pl.BlockSpec((1, AT_HD), lambda h, i: (0, AT_KCOL + h)),
        pl.BlockSpec((1, AT_HD), lambda h, i: (0, AT_VCOL + h)),
        pl.BlockSpec((1, AT_GROUP, AT_HD), lambda h, i: (h, 0, 0)),
        pl.BlockSpec(blk, lambda h, i: (i, 0)),
        pl.BlockSpec(blk, lambda h, i: (i, 0)),
        pl.BlockSpec(blk, lambda h, i: (prev(i), 0)),
        pl.BlockSpec(blk, lambda h, i: (prev(i), 0)),
    ]


def _att_load(i, q_ref, kc_ref, kp_ref, vc_ref, vp_ref, bq_ref, bk_ref, bv_ref, cc_ref, sc_ref, cp_ref, sp_ref):
    cosc, sinc = cc_ref[...], sc_ref[...]
    q = q_ref[...] + bq_ref[...]
    kc = _rope(kc_ref[...] + bk_ref[...], cosc, sinc)
    kp = _rope(kp_ref[...] + bk_ref[...], cp_ref[...], sp_ref[...])
    kcat = jnp.concatenate([kp, kc], axis=0).astype(BF16)
    vcat = jnp.concatenate([vp_ref[...] + bv_ref[...], vc_ref[...] + bv_ref[...]], axis=0).astype(BF16)
    qi = lax.broadcasted_iota(jnp.int32, (AT_BLK, 2 * AT_BLK), 0)
    kj = lax.broadcasted_iota(jnp.int32, (AT_BLK, 2 * AT_BLK), 1)
    rel = qi + AT_BLK - kj
    mask = (rel >= 0) & (rel < AT_BLK) & ((kj >= AT_BLK) | (i > 0))
    return q, kcat, vcat, mask, cosc, sinc


def _att_probs(qg, kcat, mask, sink):
    s = lax.dot_general(qg.astype(BF16), kcat, (((1,), (1,)), ((), ())), preferred_element_type=F32) * AT_SCALE
    s = jnp.where(mask, s, NEG_INF)
    m = jnp.maximum(jnp.max(s, axis=-1, keepdims=True), sink)
    p = jnp.exp(s - m)
    es = jnp.exp(sink - m)
    inv = 1.0 / (jnp.sum(p, axis=-1, keepdims=True) + es)
    return p * inv, es * inv


def attention_fwd(qkv, bias, sinks_b, cos2, sin2):
    s_len = qkv.shape[0]
    nb = s_len // AT_BLK

    def body(q_ref, kc_ref, kp_ref, vc_ref, vp_ref, bq_ref, bk_ref, bv_ref, sk_ref, cc_ref, sc_ref, cp_ref, sp_ref,
             o_ref):
        i = pl.program_id(1)
        q, kcat, vcat, mask, cosc, sinc = _att_load(i, q_ref, kc_ref, kp_ref, vc_ref, vp_ref, bq_ref, bk_ref,
                                                    bv_ref, cc_ref, sc_ref, cp_ref, sp_ref)
        for g in range(AT_GROUP):
            ls = slice(g * AT_HD, (g + 1) * AT_HD)
            qg = _rope(q[:, ls], cosc, sinc)
            probs, _ = _att_probs(qg, kcat, mask, sk_ref[0, g:g + 1, 0:1])
            o_ref[:, ls] = jnp.dot(probs.astype(BF16), vcat, preferred_element_type=F32).astype(BF16)

    return pl.pallas_call(
        body, name="attention_fwd", grid=(AT_KVH, nb),
        in_specs=_att_specs(),
        out_specs=pl.BlockSpec((AT_BLK, AT_QW), lambda h, i: (i, h)),
        out_shape=jax.ShapeDtypeStruct((s_len, AT_KVH * AT_QW), BF16),
        compiler_params=_cparams(("parallel", "parallel")),
    )(qkv, qkv, qkv, qkv, qkv, bias, bias, bias, sinks_b, cos2, sin2, cos2, sin2)


def attention_bwd(qkv, bias, sinks_b, cos2, sin2, d_o):
    s_len = qkv.shape[0]
    nb = s_len // AT_BLK

    def body(q_ref, kc_ref, kp_ref, vc_ref, vp_ref, bq_ref, bk_ref, bv_ref, sk_ref, cc_ref, sc_ref, cp_ref, sp_ref,
             do_ref, dq_ref, dk_ref, dv_ref, dsk_ref):
        i = pl.program_id(1)
        _acc_init(i == 0, dsk_ref)
        q, kcat, vcat, mask, cosc, sinc = _att_load(i, q_ref, kc_ref, kp_ref, vc_ref, vp_ref, bq_ref, bk_ref,
                                                    bv_ref, cc_ref, sc_ref, cp_ref, sp_ref)
        dk_cat = jnp.zeros((2 * AT_BLK, AT_HD), F32)
        dv_cat = jnp.zeros((2 * AT_BLK, AT_HD), F32)
        lane = lax.broadcasted_iota(jnp.int32, (1, AT_HD), 1)
        dsk = jnp.zeros((1, AT_HD), F32)
        for g in range(AT_GROUP):
            ls = slice(g * AT_HD, (g + 1) * AT_HD)
            qg = _rope(q[:, ls], cosc, sinc).astype(BF16)
            probs, psink = _att_probs(qg, kcat, mask, sk_ref[0, g:g + 1, 0:1])
            pb = probs.astype(BF16)
            do_g = do_ref[:, ls].astype(F32)
            do_b = do_g.astype(BF16)
            o_g = jnp.dot(pb, vcat, preferred_element_type=F32)
            dsum = jnp.sum(do_g * o_g, axis=-1, keepdims=True)
            dp = lax.dot_general(do_b, vcat, (((1,), (1,)), ((), ())), preferred_element_type=F32)
            ds = (probs * (dp - dsum) * AT_SCALE).astype(BF16)
            dsk = dsk + jnp.where(lane == g, -jnp.sum(psink * dsum, axis=0, keepdims=True), 0.0)
            dv_cat = dv_cat + lax.dot_general(pb, do_b, (((0,), (0,)), ((), ())), preferred_element_type=F32)
            dk_cat = dk_cat + lax.dot_general(ds, qg, (((0,), (0,)), ((), ())), preferred_element_type=F32)
            dq_ref[:, ls] = _rope_t(jnp.dot(ds, kcat, preferred_element_type=F32), cosc, sinc).astype(BF16)
        dsk_ref[0] += dsk
        cur = pl.ds(pl.multiple_of(i * AT_BLK, AT_BLK), AT_BLK)
        dk_ref[cur, :] = _rope_t(dk_cat[AT_BLK:], cosc, sinc)
        dv_ref[cur, :] = dv_cat[AT_BLK:]

        @pl.when(i > 0)
        def _():
            prv = pl.ds(pl.multiple_of((i - 1) * AT_BLK, AT_BLK), AT_BLK)
            dk_ref[prv, :] += _rope_t(dk_cat[:AT_BLK], cp_ref[...], sp_ref[...])
            dv_ref[prv, :] += dv_cat[:AT_BLK]

    kv_out = pl.BlockSpec((s_len, AT_HD), lambda h, i: (0, h))
    return pl.pallas_call(
        body, name="attention_bwd", grid=(AT_KVH, nb),
        in_specs=_att_specs() + [pl.BlockSpec((AT_BLK, AT_QW), lambda h, i: (i, h))],
        out_specs=[pl.BlockSpec((AT_BLK, AT_QW), lambda h, i: (i, h)), kv_out, kv_out,
                   pl.BlockSpec((1, 1, AT_HD), lambda h, i: (h, 0, 0))],
        out_shape=[jax.ShapeDtypeStruct((s_len, AT_KVH * AT_QW), BF16),
                   jax.ShapeDtypeStruct((s_len, AT_KVH * AT_HD), F32),
                   jax.ShapeDtypeStruct((s_len, AT_KVH * AT_HD), F32),
                   jax.ShapeDtypeStruct((AT_KVH, 1, AT_HD), F32)],
        compiler_params=_cparams(("arbitrary", "arbitrary")),
    )(qkv, qkv, qkv, qkv, qkv, bias, bias, bias, sinks_b, cos2, sin2, cos2, sin2, d_o)


ATT_QKV = 6144
RW_SHIFT = 13024
RW_PAD = 13056
D_GATE = 480
ROPE_THETA = 10000.0


def perm_cols(a):
    lead = a.shape[:-1]
    return jnp.swapaxes(a.reshape(lead + (RW_H, RW_N)), -1, -2).reshape(lead + (RW_C,))


def rw_reorder(a, pad_value=0):
    r, k, v = (perm_cols(a[..., i * RW_C:(i + 1) * RW_C]) for i in range(3))
    wd = a[..., 3 * RW_C:3 * RW_C + 128]
    ad = a[..., 3 * RW_C + 128:3 * RW_C + 256]
    gd = a[..., 3 * RW_C + 256:]
    pad = jnp.full(a.shape[:-1] + (512 - D_GATE,), pad_value, a.dtype)
    return jnp.concatenate([r, k, v, gd, pad, wd, ad], axis=-1)


def rw_restore(a):
    r, k, v = (perm_cols(a[..., i * RW_C:(i + 1) * RW_C]) for i in range(3))
    gd = a[..., 3 * RW_C:3 * RW_C + D_GATE]
    wd = a[..., 3 * RW_C + 512:3 * RW_C + 640]
    ad = a[..., 3 * RW_C + 640:3 * RW_C + 768]
    return jnp.concatenate([r, k, v, wd, ad, gd], axis=-1)


def to_tiles(a):
    t = a.reshape(a.shape[0], RW_N, RW_H)
    return jnp.concatenate([t, t], axis=-1)


def from_tiles(t):
    return t[:, :, :RW_H].reshape(t.shape[0], RW_C)


def rope_tables(s_len):
    pos = jnp.arange(s_len, dtype=F32)
    inv_freq = ROPE_THETA ** (-jnp.arange(0, AT_HD, 2, dtype=F32) / AT_HD)
    ang = pos[:, None] * inv_freq[None, :]
    cos, sin = jnp.cos(ang), jnp.sin(ang)
    return jnp.concatenate([cos, cos], axis=1), jnp.concatenate([-sin, sin], axis=1)


def local_step(x, tgt, small, big):
    s_len, d = x.shape
    w_in = big["w_in"]
    w_qkv = w_in[:, :ATT_QKV]
    w_rw = rw_reorder(w_in[:, ATT_QKV:ATT_QKV + RW_SHIFT])
    w_gate = w_in[:, ATT_QKV + RW_SHIFT:]
    w2 = perm_cols(big["w2"])
    a2 = perm_cols(big["a2"])
    g2 = jnp.pad(perm_cols(big["g2"]), ((0, 512 - D_GATE), (0, 0)))
    w_rb = big["w_rwkv_branch"].reshape(RW_H, RW_N, d).swapaxes(0, 1).reshape(RW_C, d)
    mu = rw_reorder(small["mu_shift"])
    w0, a0, k_k, k_a, ln_w, ln_b = (perm_cols(small[n]) for n in ("w0", "a0", "k_k", "k_a", "ln_x_w", "ln_x_b"))
    r_k = small["r_k"].reshape(RW_H, RW_N).T.reshape(1, RW_C)
    sinks_b = jnp.broadcast_to(small["att_sinks"].reshape(AT_KVH, AT_GROUP, 1), (AT_KVH, AT_GROUP, AT_HD))
    cos2, sin2 = rope_tables(s_len)
    bias = small["b_qkv"]

    h = rms_fwd(x, small["norm_mix_pre"])
    qkv = matmul(h, w_qkv, name="mm_qkv")
    prw = matmul(h, w_rw, name="mm_rw")
    gate = matmul(h, w_gate, name="mm_gate")
    o_att = attention_fwd(qkv, bias, sinks_b, cos2, sin2)
    pre_params = (mu, w0, a0, k_k, k_a, w2, a2, g2)
    r, dec, k, v, av, bv, g, tw, xa, sg = rwkv_pre_fwd(prw, *pre_params)
    v3 = to_tiles(v)
    y3, ck = rwkv_scan_fwd(r, dec, k, av, bv, v3)
    y = from_tiles(y3)
    o_rw = rwkv_post_fwd(y, r, k, v, g, ln_w, ln_b, r_k)
    ab = matmul(o_att, big["w_att_branch"], name="mm_ab")
    rb = matmul(o_rw, w_rb, name="mm_rb")
    merged = merge_fwd(gate, ab, rb)
    m2 = matmul(merged, big["w_out"], name="mm_out")
    x1, h2 = resid_norm_fwd(x, m2, small["norm_mix_post"], small["norm_ffn_pre"])
    gg = matmul(h2, big["w_ffn_gate"], name="mm_fg")
    uu = matmul(h2, big["w_ffn_up"], name="mm_fu")
    act = swiglu_fwd(gg, uu)
    ff = matmul(act, big["w_ffn_down"], name="mm_fd")
    loss, dy, dff, d_nfp = loss_head(x1, ff, tgt, small["norm_ffn_post"])

    dact = matmul(dff, big["w_ffn_down"], tb=True, name="mm_dact")
    g_fd = matmul(act, dff, ta=True, out_dtype=BF16, name="mm_gfd")
    dgg, duu = swiglu_bwd(gg, uu, dact)
    g_fg = matmul(h2, dgg, ta=True, out_dtype=BF16, name="mm_gfg")
    g_fu = matmul(h2, duu, ta=True, out_dtype=BF16, name="mm_gfu")
    dh2a = matmul(dgg, big["w_ffn_gate"], tb=True, name="mm_dh2a")
    dh2b = matmul(duu, big["w_ffn_up"], tb=True, name="mm_dh2b")
    dx1, dm2, d_nfpre, d_nmpost = resid_norm_bwd(x1, dh2a, dh2b, small["norm_ffn_pre"], m2, small["norm_mix_post"], dy)
    dmerged = matmul(dm2, big["w_out"], tb=True, name="mm_dmerged")
    g_out = matmul(merged, dm2, ta=True, out_dtype=BF16, name="mm_gout")
    dgate, dab, drb = merge_bwd(gate, ab, rb, dmerged)
    do_att = matmul(dab, big["w_att_branch"], tb=True, out_dtype=BF16, name="mm_doatt")
    g_ab = matmul(o_att, dab, ta=True, out_dtype=BF16, name="mm_gab")
    do_rw = matmul(drb, w_rb, tb=True, name="mm_dorw")
    g_rb = matmul(o_rw, drb, ta=True, out_dtype=BF16, name="mm_grb")
    dq, dk_att, dv_att, dsk = attention_bwd(qkv, bias, sinks_b, cos2, sin2, do_att)
    dqkv = jnp.concatenate([dq, dk_att.astype(BF16), dv_att.astype(BF16)], axis=1)
    dy_s, dr_p, dk_p, dv_p, dg, d_lnw, d_lnb, d_rk = rwkv_post_bwd(y, r, k, v, g, ln_w, ln_b, r_k, do_rw)
    dr_s, ddec, dk_s, dav, dbv, dv3 = rwkv_scan_bwd(r, dec, k, av, bv, v3, ck, to_tiles(dy_s))
    dprw, dzw, dza, dmu, dw0, da0, dkk, dka = rwkv_pre_bwd(
        prw, *pre_params, dr_p, ddec, dk_p, dv_p, dav, dbv, dg, dr_s, dk_s, from_tiles(dv3))
    g_w2 = matmul(tw, dzw, ta=True, out_dtype=BF16, name="mm_gw2")
    g_a2 = matmul(xa, dza, ta=True, out_dtype=BF16, name="mm_ga2")
    g_g2 = matmul(sg, dg.astype(BF16), ta=True, out_dtype=BF16, name="mm_gg2")
    g_qkv = matmul(h, dqkv, ta=True, out_dtype=BF16, name="mm_gqkv")
    g_rw = matmul(h, dprw, ta=True, out_dtype=BF16, name="mm_grw")
    g_gate = matmul(h, dgate, ta=True, out_dtype=BF16, name="mm_ggate")
    dh_a = matmul(dqkv, w_qkv, tb=True, name="mm_dha")
    dh_b = matmul(dprw, w_rw, tb=True, name="mm_dhb")
    dh_c = matmul(dgate, w_gate, tb=True, name="mm_dhc")
    grad_x, d_nmpre = rms_bwd(x, small["norm_mix_pre"], dh_a, dh_b, dh_c, dx1)
    d_bias = colsum(dqkv)

    gsmall = {
        "norm_mix_pre": d_nmpre, "norm_mix_post": d_nmpost, "norm_ffn_pre": d_nfpre, "norm_ffn_post": d_nfp,
        "b_qkv": d_bias, "att_sinks": dsk[:, 0, :AT_GROUP].reshape(1, AT_KVH * AT_GROUP),
        "mu_shift": rw_restore(dmu), "w0": perm_cols(dw0), "a0": perm_cols(da0), "k_k": perm_cols(dkk),
        "k_a": perm_cols(dka), "r_k": d_rk.reshape(RW_N, RW_H).T.reshape(1, RW_C),
        "ln_x_w": perm_cols(d_lnw), "ln_x_b": perm_cols(d_lnb),
    }
    gbig = {
        "w_in": jnp.concatenate([g_qkv, rw_restore(g_rw), g_gate], axis=1),
        "w2": perm_cols(g_w2), "a2": perm_cols(g_a2), "g2": perm_cols(g_g2)[:D_GATE],
        "w_att_branch": g_ab,
        "w_rwkv_branch": g_rb.reshape(RW_N, RW_H, d).swapaxes(0, 1).reshape(RW_C, d),
        "w_out": g_out, "w_ffn_gate": g_fg, "w_ffn_up": g_fu, "w_ffn_down": g_fd,
    }
    return loss, grad_x, gsmall, gbig


MESH = pl.DeviceIdType.MESH
ANY = pl.BlockSpec(memory_space=pl.ANY)


def _place():
    x, y, c = lax.axis_index("x"), lax.axis_index("y"), lax.axis_index("c")
    chips = [(1 - x, y), (x, 1 - y), (1 - x, 1 - y)]
    return x, y, c, chips


def _remote(src, dst, send_sems, recv_sems, k, dev):
    return pltpu.make_async_remote_copy(src_ref=src, dst_ref=dst, send_sem=send_sems.at[k], recv_sem=recv_sems.at[k],
                                        device_id=dev, device_id_type=MESH)


def gather_weights(shards):
    n = len(shards)

    def body(*refs):
        ins, outs = refs[:n], refs[n:2 * n]
        send_sems, recv_sems = refs[2 * n:]
        x, y, c, chips = _place()
        me = 2 * x + y
        sib = (x, y, 1 - c)

        def half(ref, which):
            hr = ref.shape[0] // 2
            return ref.at[pl.ds(which * hr, hr), :]

        sends = []
        for i in range(n):
            for j, chip in enumerate(chips):
                sends.append(_remote(half(ins[i], c), half(outs[i].at[me], c), send_sems, recv_sems, 6 * i + j,
                                     (*chip, c)))
        for cp in sends:
            cp.start()
        passed = []
        for i in range(n):
            for j, chip in enumerate(chips):
                got = half(outs[i].at[2 * chip[0] + chip[1]], c)
                _remote(got, got, send_sems, recv_sems, 6 * i + j, sib).wait_recv()
                cp = _remote(got, got, send_sems, recv_sems, 6 * i + 3 + j, sib)
                cp.start()
                passed.append(cp)
        for i in range(n):
            for j, chip in enumerate(chips):
                got = half(outs[i].at[2 * chip[0] + chip[1]], 1 - c)
                _remote(got, got, send_sems, recv_sems, 6 * i + 3 + j, sib).wait_recv()
        for cp in sends + passed:
            cp.wait_send()

    return pl.pallas_call(
        body, name="gather_weights",
        in_specs=[ANY] * n, out_specs=[ANY] * n,
        out_shape=[jax.ShapeDtypeStruct((4,) + s.shape, s.dtype) for s in shards],
        scratch_shapes=[pltpu.SemaphoreType.DMA((6 * n,)), pltpu.SemaphoreType.DMA((6 * n,))],
    )(*shards)


def swap_with_sibling(blocks, name):
    n = len(blocks)

    def body(*refs):
        ins, outs = refs[:n], refs[n:2 * n]
        send_sems, recv_sems = refs[2 * n:]
        x, y, c, _ = _place()
        cps = [_remote(ins[i], outs[i], send_sems, recv_sems, i, (x, y, 1 - c)) for i in range(n)]
        for cp in cps:
            cp.start()
        for cp in cps:
            cp.wait()

    return pl.pallas_call(
        body, name=name, in_specs=[ANY] * n, out_specs=[ANY] * n,
        out_shape=[jax.ShapeDtypeStruct(b.shape, b.dtype) for b in blocks],
        scratch_shapes=[pltpu.SemaphoreType.DMA((n,)), pltpu.SemaphoreType.DMA((n,))],
    )(*blocks)


def scatter_to_owners(parts):
    n = len(parts)

    def body(*refs):
        ins, outs = refs[:n], refs[n:2 * n]
        send_sems, recv_sems = refs[2 * n:]
        x, y, c, chips = _place()
        cps = []
        for i in range(n):
            for j, chip in enumerate(chips):
                cps.append(_remote(ins[i].at[2 * chip[0] + chip[1]], outs[i].at[j], send_sems, recv_sems, 3 * i + j,
                                   (*chip, c)))
        for cp in cps:
            cp.start()
        for cp in cps:
            cp.wait()

    return pl.pallas_call(
        body, name="scatter_to_owners", in_specs=[ANY] * n, out_specs=[ANY] * n,
        out_shape=[jax.ShapeDtypeStruct((3,) + p.shape[1:], p.dtype) for p in parts],
        scratch_shapes=[pltpu.SemaphoreType.DMA((3 * n,)), pltpu.SemaphoreType.DMA((3 * n,))],
    )(*parts)


def allreduce_small(v):
    rows = v.shape[0]

    def body(v_ref, o_ref, buf, send_sems, recv_sems):
        x, y, c, chips = _place()
        me, sib = (x, y, c), (x, y, 1 - c)

        def slot(px, py, pc):
            return buf.at[4 * px + 2 * py + pc]

        def copy(k, block, to, src=None):
            return _remote(slot(*block) if src is None else src, slot(*block), send_sems, recv_sems, k, to)

        buf[4 * x + 2 * y + c] = v_ref[...]
        first = [copy(0, me, sib, src=v_ref)]
        first += [copy(1 + j, me, (*chip, c), src=v_ref) for j, chip in enumerate(chips)]
        for cp in first:
            cp.start()
        passed = [copy(4 + j, (*chip, c), sib) for j, chip in enumerate(chips)]
        for j, chip in enumerate(chips):
            copy(1 + j, (*chip, c), me).wait_recv()
            passed[j].start()
        copy(0, sib, me).wait_recv()
        for j, chip in enumerate(chips):
            copy(4 + j, (*chip, 1 - c), me).wait_recv()
        for cp in first + passed:
            cp.wait_send()
        acc = buf[0]
        for k in range(1, 8):
            acc = acc + buf[k]
        o_ref[...] = acc

    vm = pl.BlockSpec(memory_space=pltpu.VMEM)
    return pl.pallas_call(
        body, name="allreduce_small", in_specs=[vm], out_specs=vm,
        out_shape=jax.ShapeDtypeStruct(v.shape, F32),
        scratch_shapes=[pltpu.VMEM((8, rows, LANES), F32), pltpu.SemaphoreType.DMA((7,)),
                        pltpu.SemaphoreType.DMA((7,))],
    )(v)


def _rows_tile(r):
    return _pick(r, (64, 32, 16, 8))


def add_pairs(a, b):
    _, r, c = a.shape
    tr = _rows_tile(r)
    spec = pl.BlockSpec((1, tr, c), lambda s, i: (s, i, 0))

    def body(a_ref, b_ref, o_ref):
        o_ref[...] = (a_ref[...].astype(F32) + b_ref[...].astype(F32)).astype(o_ref.dtype)

    return pl.pallas_call(
        body, name="add_pairs", grid=(4, r // tr), in_specs=[spec, spec], out_specs=spec,
        out_shape=jax.ShapeDtypeStruct(a.shape, a.dtype), compiler_params=_cparams(("parallel", "parallel")),
    )(a, b)


def add_four(mine, others):
    r, c = mine.shape
    tr = _rows_tile(r)

    def body(m_ref, o_ref, out_ref):
        acc = m_ref[...].astype(F32)
        for j in range(3):
            acc = acc + o_ref[j].astype(F32)
        out_ref[...] = acc

    return pl.pallas_call(
        body, name="add_four", grid=(r // tr,),
        in_specs=[pl.BlockSpec((tr, c), lambda i: (i, 0)), pl.BlockSpec((3, tr, c), lambda i: (0, i, 0))],
        out_specs=pl.BlockSpec((tr, c), lambda i: (i, 0)),
        out_shape=jax.ShapeDtypeStruct((r, c), F32), compiler_params=_cparams(("parallel",)),
    )(mine, others)


def reduce_sharded(blocks):
    cx, cy, cc = lax.axis_index("x"), lax.axis_index("y"), lax.axis_index("c")
    me = 2 * cx + cy
    mine, other = [], []
    for g4 in blocks:
        hr = g4.shape[1] // 2
        mine.append(lax.dynamic_slice_in_dim(g4, cc * hr, hr, axis=1))
        other.append(lax.dynamic_slice_in_dim(g4, (1 - cc) * hr, hr, axis=1))
    from_sibling = swap_with_sibling(other, "swap_halves")
    pair = [add_pairs(a, b) for a, b in zip(mine, from_sibling)]
    from_chips = scatter_to_owners(pair)
    sums = [add_four(lax.dynamic_index_in_dim(p, me, 0, keepdims=False), t) for p, t in zip(pair, from_chips)]
    got = swap_with_sibling(sums, "swap_sums")
    return [jnp.concatenate([jnp.where(cc == 0, s, g), jnp.where(cc == 0, g, s)], axis=0) for s, g in zip(sums, got)]


ADAM_LR = 0.001
ADAM_B1 = 0.9
ADAM_B2 = 0.999
ADAM_EPS = 1e-08
ADAM_WD = 0.01
ADAM_STEP = 10


def adamw(w, g, m, v):
    r, c = w.shape
    tr = _rows_tile(r)
    spec = pl.BlockSpec((tr, c), lambda i: (i, 0))

    def body(w_ref, g_ref, m_ref, v_ref, d_ref, nm_ref, nv_ref):
        gv = g_ref[...]
        nm = ADAM_B1 * m_ref[...] + (1.0 - ADAM_B1) * gv
        nv = ADAM_B2 * v_ref[...] + (1.0 - ADAM_B2) * jnp.square(gv)
        m_hat = nm / (1.0 - ADAM_B1 ** ADAM_STEP)
        v_hat = nv / (1.0 - ADAM_B2 ** ADAM_STEP)
        d_ref[...] = -ADAM_LR * (m_hat / (jnp.sqrt(v_hat) + ADAM_EPS) + ADAM_WD * w_ref[...])
        nm_ref[...] = nm
        nv_ref[...] = nv

    out = jax.ShapeDtypeStruct((r, c), F32)
    return pl.pallas_call(
        body, name="adamw", grid=(r // tr,), in_specs=[spec] * 4, out_specs=[spec] * 3, out_shape=[out] * 3,
        compiler_params=_cparams(("parallel",)),
    )(w, g, m, v)


WEIGHTS = ["norm_mix_pre", "norm_mix_post", "norm_ffn_pre", "norm_ffn_post", "w_in", "b_qkv", "att_sinks", "mu_shift",
           "w0", "w2", "a0", "a2", "g2", "k_k", "k_a", "r_k", "ln_x_w", "ln_x_b", "w_att_branch", "w_rwkv_branch",
           "w_out", "w_ffn_gate", "w_ffn_up", "w_ffn_down"]
BIG = {"w_in": 1, "w2": 1, "a2": 1, "g2": 1, "w_att_branch": 0, "w_rwkv_branch": 0, "w_out": 0, "w_ffn_gate": 1,
       "w_ffn_up": 1, "w_ffn_down": 0}
SMALL = [n for n in WEIGHTS if n not in BIG]
N_CHIPS = 4


def _whole(g4, axis):
    if axis == 0:
        return g4.reshape(g4.shape[0] * g4.shape[1], g4.shape[2])
    return jnp.swapaxes(g4, 0, 1).reshape(g4.shape[1], g4.shape[0] * g4.shape[2])


def _by_shard(w, axis):
    if axis == 0:
        return w.reshape(N_CHIPS, w.shape[0] // N_CHIPS, w.shape[1])
    return jnp.swapaxes(w.reshape(w.shape[0], N_CHIPS, w.shape[1] // N_CHIPS), 0, 1)


def _pack(parts):
    flat = jnp.concatenate([parts[n].reshape(-1) for n in SMALL])
    rows = -(-flat.shape[0] // (LANES * SUBLANES)) * SUBLANES
    return jnp.pad(flat, (0, rows * LANES - flat.shape[0])).reshape(rows, LANES)


def _unpack(packed, like):
    flat = packed.reshape(-1)
    out, off = {}, 0
    for n in SMALL:
        size = like[n].size
        out[n] = flat[off:off + size].reshape(like[n].shape)
        off += size
    return out


def kernel(x, norm_mix_pre, norm_mix_post, norm_ffn_pre, norm_ffn_post, w_in, b_qkv, att_sinks, mu_shift, w0, w2, a0, a2, g2, k_k, k_a, r_k, ln_x_w, ln_x_b, w_att_branch, w_rwkv_branch, w_out, w_ffn_gate, w_ffn_up, w_ffn_down, loss_target, m_norm_mix_pre, m_norm_mix_post, m_norm_ffn_pre, m_norm_ffn_post, m_w_in, m_b_qkv, m_att_sinks, m_mu_shift, m_w0, m_w2, m_a0, m_a2, m_g2, m_k_k, m_k_a, m_r_k, m_ln_x_w, m_ln_x_b, m_w_att_branch, m_w_rwkv_branch, m_w_out, m_w_ffn_gate, m_w_ffn_up, m_w_ffn_down, v_norm_mix_pre, v_norm_mix_post, v_norm_ffn_pre, v_norm_ffn_post, v_w_in, v_b_qkv, v_att_sinks, v_mu_shift, v_w0, v_w2, v_a0, v_a2, v_g2, v_k_k, v_k_a, v_r_k, v_ln_x_w, v_ln_x_b, v_w_att_branch, v_w_rwkv_branch, v_w_out, v_w_ffn_gate, v_w_ffn_up, v_w_ffn_down):
    given = dict(locals())
    wts = {n: given[n] for n in WEIGHTS}
    mom1 = {n: given["m_" + n] for n in WEIGHTS}
    mom2 = {n: given["v_" + n] for n in WEIGHTS}
    names = list(BIG)
    me = 2 * lax.axis_index("x") + lax.axis_index("y")
    own = [wts[n][0].astype(BF16) for n in names]
    gathered = [lax.dynamic_update_index_in_dim(g4, blk, me, 0) for g4, blk in zip(gather_weights(own), own)]
    big = {n: _whole(g4, BIG[n]) for n, g4 in zip(names, gathered)}
    small = {n: wts[n].reshape(1, -1) for n in SMALL}

    loss, grad_x, gsmall, gbig = local_step(x[0], loss_target[0], small, big)

    wholes = reduce_sharded([_by_shard(gbig[n], BIG[n]) for n in names])
    grads = {n: w2h.reshape(wts[n].shape[1:]) for n, w2h in zip(names, wholes)}

    gsum = _unpack(allreduce_small(_pack(gsmall)), small)

    outs_g, outs_d, outs_m, outs_v = {}, {}, {}, {}
    for n in names:
        d, nm, nv = adamw(wts[n][0], grads[n], mom1[n][0], mom2[n][0])
        outs_g[n], outs_d[n], outs_m[n], outs_v[n] = (t[None] for t in (grads[n], d, nm, nv))
    pk = lambda src: _pack({n: src[n] for n in SMALL})
    d, nm, nv = adamw(pk(wts), _pack(gsum), pk(mom1), pk(mom2))
    du, mu, vu = _unpack(d, small), _unpack(nm, small), _unpack(nv, small)
    for n in SMALL:
        outs_g[n], outs_d[n], outs_m[n], outs_v[n] = (t[n].reshape(wts[n].shape) for t in (gsum, du, mu, vu))

    total = lax.psum(loss[0, 0], ("x", "y", "c"))
    return (total, grad_x[None], *[outs_g[n] for n in WEIGHTS], *[outs_d[n] for n in WEIGHTS],
            *[outs_m[n] for n in WEIGHTS], *[outs_v[n] for n in WEIGHTS])
```

```python
import jax
import jax.numpy as jnp
from jax import lax
from jax.experimental import pallas as pl
from jax.experimental.pallas import tpu as pltpu

F32 = jnp.float32
BF16 = jnp.bfloat16

LANES = 128
SUBLANES = 8
VMEM_LIMIT = 56 * 1024 * 1024

RW_H = 64
RW_N = 64
RW_C = RW_H * RW_N
RW_NB = RW_C // LANES
SCAN_CHUNK = 8


MESH = pl.DeviceIdType.MESH
ANY = pl.BlockSpec(memory_space=pl.ANY)


def _cparams(sem=None):
    return pltpu.CompilerParams(dimension_semantics=sem, vmem_limit_bytes=VMEM_LIMIT)


def _fold(x):
    return x + pltpu.roll(x, 64, axis=x.ndim - 1)


def _scan_step_fwd(t, src_ref, dst_ref, r_ref, w_ref, k_ref, a_ref, b_ref, v_ref):
    vt = v_ref[t]
    acc = jnp.zeros((RW_N, LANES), F32)
    for j in range(RW_NB):
        ls = slice(j * LANES, (j + 1) * LANES)
        acc = acc + src_ref[j] * a_ref[t:t + 1, ls]
    sa = _fold(acc)
    yacc = jnp.zeros((RW_N, LANES), F32)
    for j in range(RW_NB):
        ls = slice(j * LANES, (j + 1) * LANES)
        s_new = src_ref[j] * w_ref[t:t + 1, ls] + sa * b_ref[t:t + 1, ls] + vt * k_ref[t:t + 1, ls]
        dst_ref[j] = s_new
        yacc = yacc + s_new * r_ref[t:t + 1, ls]
    return _fold(yacc), sa


def _rider_parts(rider):
    if rider is None:
        return [], [], []
    return list(rider["ins"]), list(rider["out_shapes"]), list(rider["scratch"])


def rwkv_scan_fwd(r, w, k, a, b, v3, rider=None):
    s_len = r.shape[0]
    nchunk = s_len // SCAN_CHUNK
    x_in, x_out, x_scr = _rider_parts(rider)
    ni, no = len(x_in), len(x_out)

    def body(*refs):
        r_ref, w_ref, k_ref, a_ref, b_ref, v_ref = refs[:6]
        y_ref, ck_ref = refs[6 + ni:8 + ni]
        st_ref = refs[8 + ni + no]
        ride = (refs[6:6 + ni], refs[8 + ni:8 + ni + no], *refs[9 + ni + no:])

        @pl.when(pl.program_id(0) == 0)
        def _():
            st_ref[...] = jnp.zeros_like(st_ref)
            if rider is not None:
                rider["start"](*ride)

        ck_ref[0] = st_ref[...]

        for t in range(SCAN_CHUNK):
            y, _ = _scan_step_fwd(t, st_ref, st_ref, r_ref, w_ref, k_ref, a_ref, b_ref, v_ref)
            y_ref[t] = y

        if rider is not None:
            @pl.when(pl.program_id(0) == nchunk - 1)
            def _():
                rider["finish"](*ride)

    row = pl.BlockSpec((SCAN_CHUNK, RW_C), lambda i: (i, 0))
    til = pl.BlockSpec((SCAN_CHUNK, RW_N, LANES), lambda i: (i, 0, 0))
    return pl.pallas_call(
        body,
        name="rwkv_scan_fwd",
        grid=(nchunk,),
        in_specs=[row, row, row, row, row, til] + [ANY] * ni,
        out_specs=[til, pl.BlockSpec((1, RW_NB, RW_N, LANES), lambda i: (i, 0, 0, 0))] + [ANY] * no,
        out_shape=[
            jax.ShapeDtypeStruct((s_len, RW_N, LANES), F32),
            jax.ShapeDtypeStruct((nchunk, RW_NB, RW_N, LANES), F32),
        ] + x_out,
        scratch_shapes=[pltpu.VMEM((RW_NB, RW_N, LANES), F32)] + x_scr,
        compiler_params=_cparams(("arbitrary",)),
    )(r, w, k, a, b, v3, *x_in)


def rwkv_scan_bwd(r, w, k, a, b, v3, ck, dy3, rider=None):
    s_len = r.shape[0]
    nchunk = s_len // SCAN_CHUNK
    x_in, x_out, x_scr = _rider_parts(rider)
    ni, no = len(x_in), len(x_out)

    def body(*refs):
        r_ref, w_ref, k_ref, a_ref, b_ref, v_ref, ck_ref, dy_ref = refs[:8]
        dr_ref, dw_ref, dk_ref, da_ref, db_ref, dv_ref = refs[8 + ni:14 + ni]
        hist_ref, sa_ref, ds_ref = refs[14 + ni + no:17 + ni + no]
        ride = (refs[8:8 + ni], refs[14 + ni:14 + ni + no], *refs[17 + ni + no:])

        @pl.when(pl.program_id(0) == 0)
        def _():
            ds_ref[...] = jnp.zeros_like(ds_ref)
            if rider is not None:
                rider["start"](*ride)

        hist_ref[0] = ck_ref[0]

        for t in range(SCAN_CHUNK):
            _, sa = _scan_step_fwd(t, hist_ref.at[t], hist_ref.at[t + 1], r_ref, w_ref, k_ref, a_ref, b_ref, v_ref)
            sa_ref[t] = sa

        for t in reversed(range(SCAN_CHUNK)):
            vt = v_ref[t]
            dyt = dy_ref[t]
            sat = sa_ref[t]
            dv_acc = jnp.zeros((RW_N, LANES), F32)
            dsa_acc = jnp.zeros((RW_N, LANES), F32)
            for j in range(RW_NB):
                ls = slice(j * LANES, (j + 1) * LANES)
                row = (slice(t, t + 1), ls)
                ds_j = ds_ref[j] + dyt * r_ref[row]
                ds_ref[j] = ds_j
                dr_ref[row] = jnp.sum(hist_ref[t + 1, j] * dyt, axis=0, keepdims=True)
                dv_acc = dv_acc + ds_j * k_ref[row]
                dk_ref[row] = jnp.sum(ds_j * vt, axis=0, keepdims=True)
                dsa_acc = dsa_acc + ds_j * b_ref[row]
                db_ref[row] = jnp.sum(ds_j * sat, axis=0, keepdims=True)
                dw_ref[row] = jnp.sum(ds_j * hist_ref[t, j], axis=0, keepdims=True)
            dv_ref[t] = _fold(dv_acc)
            dsa = _fold(dsa_acc)
            for j in range(RW_NB):
                ls = slice(j * LANES, (j + 1) * LANES)
                row = (slice(t, t + 1), ls)
                da_ref[row] = jnp.sum(hist_ref[t, j] * dsa, axis=0, keepdims=True)
                ds_ref[j] = ds_ref[j] * w_ref[row] + dsa * a_ref[row]

        if rider is not None:
            @pl.when(pl.program_id(0) == nchunk - 1)
            def _():
                rider["finish"](*ride)

    rev = lambda i: (nchunk - 1 - i, 0)
    rev3 = lambda i: (nchunk - 1 - i, 0, 0)
    row = pl.BlockSpec((SCAN_CHUNK, RW_C), rev)
    til = pl.BlockSpec((SCAN_CHUNK, RW_N, LANES), rev3)
    rows = jax.ShapeDtypeStruct((s_len, RW_C), F32)
    return pl.pallas_call(
        body,
        name="rwkv_scan_bwd",
        grid=(nchunk,),
        in_specs=[row, row, row, row, row, til,
                  pl.BlockSpec((1, RW_NB, RW_N, LANES), lambda i: (nchunk - 1 - i, 0, 0, 0)), til] + [ANY] * ni,
        out_specs=[row, row, row, row, row, til] + [ANY] * no,
        out_shape=[rows, rows, rows, rows, rows, jax.ShapeDtypeStruct((s_len, RW_N, LANES), F32)] + x_out,
        scratch_shapes=[
            pltpu.VMEM((SCAN_CHUNK + 1, RW_NB, RW_N, LANES), F32),
            pltpu.VMEM((SCAN_CHUNK, RW_N, LANES), F32),
            pltpu.VMEM((RW_NB, RW_N, LANES), F32),
        ] + x_scr,
        compiler_params=_cparams(("arbitrary",)),
    )(r, w, k, a, b, v3, ck, dy3, *x_in)


def _pick(n, cands):
    for c in cands:
        if n % c == 0:
            return c
    return n


MM_VMEM_BUDGET = 40 * 1024 * 1024
MM_FLOPS = 8.5e14
MM_HBM = 2.2e12
MM_STEP = 0.4e-6


def _mm_plan(m, n, k, out_bytes):
    divs = lambda d: [t for t in range(LANES, d + 1, LANES) if d % t == 0] or [d]
    best = None
    for tm in divs(m):
        for tn in divs(n):
            for tk in divs(k):
                nk = k // tk
                vmem = 4 * (tm * tk + tk * tn) + (4 * tm * tn if nk > 1 else 0) + 2 * tm * tn * out_bytes
                if vmem > MM_VMEM_BUDGET:
                    continue
                steps = (m // tm) * (n // tn) * nk
                for n_outer in (False, True):
                    if nk > 1:
                        traffic = steps * (tm * tk + tk * tn) * 2
                    elif n_outer:
                        traffic = (n // tn) * (k * tn + m * k) * 2
                    else:
                        traffic = (m // tm) * (tm * k + k * n) * 2
                    cost = max(2.0 * m * n * k / MM_FLOPS, (traffic + m * n * out_bytes) / MM_HBM) + steps * MM_STEP
                    if best is None or cost < best[0]:
                        best = (cost, tm, tn, tk, n_outer)
    return best[1:]


def matmul(a, b, *, ta=False, tb=False, out_dtype=F32, name="matmul", plan=None):
    m, kdim = (a.shape[1], a.shape[0]) if ta else a.shape
    n = b.shape[0] if tb else b.shape[1]
    assert (b.shape[1] if tb else b.shape[0]) == kdim
    tm, tn, tk, n_outer = plan or _mm_plan(m, n, kdim, jnp.dtype(out_dtype).itemsize)
    nk = kdim // tk
    dims = (((0 if ta else 1,), (1 if tb else 0,)), ((), ()))

    def body(a_ref, b_ref, o_ref, *acc):
        prod = lax.dot_general(a_ref[...].astype(BF16), b_ref[...].astype(BF16), dims, preferred_element_type=F32)
        if nk == 1:
            o_ref[...] = prod.astype(o_ref.dtype)
            return
        acc_ref, = acc
        kk = pl.program_id(2)

        @pl.when(kk == 0)
        def _():
            acc_ref[...] = prod

        @pl.when(kk > 0)
        def _():
            acc_ref[...] += prod

        @pl.when(kk == nk - 1)
        def _():
            o_ref[...] = acc_ref[...].astype(o_ref.dtype)

    ij = (lambda p, q: (q, p)) if n_outer else (lambda p, q: (p, q))
    a_map = (lambda p, q, k: (k, ij(p, q)[0])) if ta else (lambda p, q, k: (ij(p, q)[0], k))
    b_map = (lambda p, q, k: (ij(p, q)[1], k)) if tb else (lambda p, q, k: (k, ij(p, q)[1]))
    grid = (n // tn, m // tm, nk) if n_outer else (m // tm, n // tn, nk)
    return pl.pallas_call(
        body,
        name=name,
        grid=grid,
        in_specs=[pl.BlockSpec((tk, tm) if ta else (tm, tk), a_map), pl.BlockSpec((tn, tk) if tb else (tk, tn), b_map)],
        out_specs=pl.BlockSpec((tm, tn), lambda p, q, k: ij(p, q)),
        out_shape=jax.ShapeDtypeStruct((m, n), out_dtype),
        scratch_shapes=[pltpu.VMEM((tm, tn), F32)] if nk > 1 else [],
        compiler_params=_cparams(("parallel", "parallel", "arbitrary")),
    )(a, b)


@jax.custom_vjp
def hsum(x):
    acc = x[:, 0:LANES]
    for j in range(1, RW_NB):
        acc = acc + x[:, j * LANES:(j + 1) * LANES]
    return _fold(acc)


def _hsum_fwd(x):
    return hsum(x), None


def _hsum_bwd(_, ct):
    return (jnp.concatenate([_fold(ct)] * RW_NB, axis=1),)


hsum.defvjp(_hsum_fwd, _hsum_bwd)


@jax.custom_vjp
def hbcast(s):
    return jnp.concatenate([s] * RW_NB, axis=1)


def _hbcast_fwd(s):
    return hbcast(s), None


def _hbcast_bwd(_, ct):
    acc = ct[:, 0:LANES]
    for j in range(1, RW_NB):
        acc = acc + ct[:, j * LANES:(j + 1) * LANES]
    return (acc,)


hbcast.defvjp(_hbcast_fwd, _hbcast_bwd)


@jax.custom_vjp
def bdot(x, w):
    return jnp.dot(x.astype(BF16), w, preferred_element_type=F32)


def _bdot_fwd(x, w):
    return bdot(x, w), w


def _bdot_bwd(w, ct):
    dx = lax.dot_general(ct.astype(BF16), w, (((1,), (1,)), ((), ())), preferred_element_type=F32)
    return dx, jnp.zeros_like(w)


bdot.defvjp(_bdot_fwd, _bdot_bwd)

RMS_EPS = 1e-6
GN_EPS = 64e-5


def f_rms(x, g):
    return x * lax.rsqrt(jnp.mean(x * x, axis=-1, keepdims=True) + RMS_EPS) * g


def _softplus(z):
    return jnp.maximum(z, 0.0) + jnp.log1p(jnp.exp(-jnp.abs(z)))


def f_pre(xk, xg, xw, xa, ew, ea, w0, a0, k_k, k_a, w2, a2, g2):
    tw = jnp.tanh(xw)
    sg = jax.nn.sigmoid(xg)
    wlog = -_softplus(-(w0 + bdot(tw, w2) + ew)) - 0.5
    decay = jnp.exp(-jnp.exp(wlog))
    a = jax.nn.sigmoid(a0 + bdot(xa, a2) + ea)
    g = bdot(sg, g2)
    kk0 = xk * k_k
    nrm = jnp.sqrt(hbcast(hsum(kk0 * kk0)))
    kk = kk0 / jnp.maximum(nrm, 1e-12)
    k = xk * (1.0 + (a - 1.0) * k_a)
    return decay, k, -kk, kk * a, g, tw, sg


def f_post(y, r, k, v, g, ln_w, ln_b, r_k):
    mu = hbcast(hsum(y)) * (1.0 / RW_N)
    yc = y - mu
    var = hbcast(hsum(yc * yc)) * (1.0 / RW_N)
    yn = yc * lax.rsqrt(var + GN_EPS) * ln_w + ln_b
    bonus = hbcast(hsum(r * k * r_k)) * v
    return (yn + bonus) * g


def f_merge(ga, gr, ab, rb):
    return jax.nn.sigmoid(ga) * ab + jax.nn.sigmoid(gr) * rb


def f_swiglu(gg, uu):
    return gg * jax.nn.sigmoid(gg) * uu


def _row(tt, width, cb=0, rev_n=None):
    if rev_n is None:
        return pl.BlockSpec((tt, width), lambda i: (i, cb))
    return pl.BlockSpec((tt, width), lambda i: (rev_n - 1 - i, cb))


def _full(arr):
    nd = arr.ndim
    return pl.BlockSpec(arr.shape, lambda i: (0,) * nd)


def _acc_init(i_first, *refs):
    @pl.when(i_first)
    def _():
        for r in refs:
            r[...] = jnp.zeros_like(r)


def rms_fwd(x, g, *, tt=128):
    s_len, d = x.shape

    def body(x_ref, g_ref, o_ref):
        o_ref[...] = f_rms(x_ref[...], g_ref[...]).astype(BF16)

    return pl.pallas_call(
        body, name="rms_fwd", grid=(s_len // tt,),
        in_specs=[_row(tt, d), _full(g)], out_specs=_row(tt, d),
        out_shape=jax.ShapeDtypeStruct((s_len, d), BF16),
        compiler_params=_cparams(("parallel",)),
    )(x, g)


def rwkv_pre_fwd(proj, mu, w0, a0, k_k, k_a, w2, a2, g2, *, tt=64):
    s_len, c = proj.shape
    nt = s_len // tt
    sub = tt // SUBLANES

    def body(p_ref, pb_ref, mu_ref, w0_ref, a0_ref, kk_ref, ka_ref, w2_ref, a2_ref, g2_ref,
             r_ref, dec_ref, k_ref, v_ref, av_ref, bv_ref, g_ref, tw_ref, xa_ref, sg_ref):
        i = pl.program_id(0)
        cur = p_ref[...]
        edge = jnp.where(i > 0, pb_ref[SUBLANES - 1:SUBLANES, :], 0.0)
        rows = lax.broadcasted_iota(jnp.int32, cur.shape, 0)
        prev = jnp.where(rows == 0, edge, pltpu.roll(cur, 1, axis=0))
        xs = cur + (prev - cur) * mu_ref[...]
        xr, xk, xv = xs[:, 0:RW_C], xs[:, RW_C:2 * RW_C], xs[:, 2 * RW_C:3 * RW_C]
        xg = xs[:, 3 * RW_C:3 * RW_C + 512]
        xw = xs[:, 3 * RW_C + 512:3 * RW_C + 640]
        xa = xs[:, 3 * RW_C + 640:3 * RW_C + 768]
        zero = jnp.zeros((tt, RW_C), F32)
        dec, k, av, bv, g, tw, sg = f_pre(xk, xg, xw, xa, zero, zero, w0_ref[...], a0_ref[...], kk_ref[...],
                                          ka_ref[...], w2_ref[...], a2_ref[...], g2_ref[...])
        r_ref[...] = xr
        dec_ref[...] = dec
        k_ref[...] = k
        v_ref[...] = xv
        av_ref[...] = av
        bv_ref[...] = bv
        g_ref[...] = g
        tw_ref[...] = tw.astype(BF16)
        xa_ref[...] = xa.astype(BF16)
        sg_ref[...] = sg.astype(BF16)

    rows_f = jax.ShapeDtypeStruct((s_len, RW_C), F32)
    prev_spec = pl.BlockSpec((SUBLANES, c), lambda i: (jnp.maximum(i * sub - 1, 0), 0))
    params = [mu, w0, a0, k_k, k_a, w2, a2, g2]
    return pl.pallas_call(
        body, name="rwkv_pre_fwd", grid=(nt,),
        in_specs=[_row(tt, c), prev_spec] + [_full(p) for p in params],
        out_specs=[_row(tt, RW_C)] * 7 + [_row(tt, 128), _row(tt, 128), _row(tt, 512)],
        out_shape=[rows_f] * 7 + [jax.ShapeDtypeStruct((s_len, 128), BF16), jax.ShapeDtypeStruct((s_len, 128), BF16),
                                  jax.ShapeDtypeStruct((s_len, 512), BF16)],
        compiler_params=_cparams(("parallel",)),
    )(proj, proj, *params)


def rwkv_pre_bwd(proj, mu, w0, a0, k_k, k_a, w2, a2, g2, d_r, d_dec, d_k, d_v, d_av, d_bv, d_g, d_r2, d_k2, d_v2,
                 *, tt=32):
    s_len, c = proj.shape
    nt = s_len // tt
    sub = tt // SUBLANES

    def body(p_ref, pb_ref, mu_ref, w0_ref, a0_ref, kk_ref, ka_ref, w2_ref, a2_ref, g2_ref,
             dr_ref, ddec_ref, dk_ref, dv_ref, dav_ref, dbv_ref, dg_ref, dr2_ref, dk2_ref, dv2_ref,
             dp_ref, dzw_ref, dza_ref, dmu_ref, dw0_ref, da0_ref, dkk_ref, dka_ref, carry_ref):
        step = pl.program_id(0)
        i = nt - 1 - step
        _acc_init(step == 0, dmu_ref, dw0_ref, da0_ref, dkk_ref, dka_ref, carry_ref)
        cur = p_ref[...]
        edge = jnp.where(i > 0, pb_ref[SUBLANES - 1:SUBLANES, :], 0.0)
        rows = lax.broadcasted_iota(jnp.int32, cur.shape, 0)
        prev = jnp.where(rows == 0, edge, pltpu.roll(cur, 1, axis=0))
        mu_v = mu_ref[...]
        xs = cur + (prev - cur) * mu_v
        xk = xs[:, RW_C:2 * RW_C]
        xg = xs[:, 3 * RW_C:3 * RW_C + 512]
        xw = xs[:, 3 * RW_C + 512:3 * RW_C + 640]
        xa = xs[:, 3 * RW_C + 640:3 * RW_C + 768]
        zero = jnp.zeros((tt, RW_C), F32)
        w2_v, a2_v, g2_v = w2_ref[...], a2_ref[...], g2_ref[...]

        def core(xk, xg, xw, xa, ew, ea, w0, a0, k_k, k_a):
            return f_pre(xk, xg, xw, xa, ew, ea, w0, a0, k_k, k_a, w2_v, a2_v, g2_v)[:5]

        _, vjp = jax.vjp(core, xk, xg, xw, xa, zero, zero, w0_ref[...], a0_ref[...], kk_ref[...], ka_ref[...])
        dxk, dxg, dxw, dxa, dzw, dza, dw0, da0, dkk, dka = vjp(
            (ddec_ref[...], dk_ref[...] + dk2_ref[...], dav_ref[...], dbv_ref[...], dg_ref[...]))
        dzw_ref[...] = dzw.astype(BF16)
        dza_ref[...] = dza.astype(BF16)
        dw0_ref[...] += dw0
        da0_ref[...] += da0
        dkk_ref[...] += dkk
        dka_ref[...] += dka
        dxs = jnp.concatenate([dr_ref[...] + dr2_ref[...], dxk, dv_ref[...] + dv2_ref[...], dxg, dxw, dxa], axis=1)
        dmu_ref[...] += jnp.sum(dxs * (prev - cur), axis=0, keepdims=True)
        to_prev = dxs * mu_v
        nxt = jnp.where(rows == tt - 1, carry_ref[...], pltpu.roll(to_prev, tt - 1, axis=0))
        carry_ref[...] = to_prev[0:1, :]
        dp_ref[...] = (dxs * (1.0 - mu_v) + nxt).astype(BF16)

    prev_spec = pl.BlockSpec((SUBLANES, c), lambda s: (jnp.maximum((nt - 1 - s) * sub - 1, 0), 0))
    params = [mu, w0, a0, k_k, k_a, w2, a2, g2]
    cts = [d_r, d_dec, d_k, d_v, d_av, d_bv, d_g, d_r2, d_k2, d_v2]
    vec = jax.ShapeDtypeStruct((1, RW_C), F32)
    acc = pl.BlockSpec((1, RW_C), lambda s: (0, 0))
    return pl.pallas_call(
        body, name="rwkv_pre_bwd", grid=(nt,),
        in_specs=[_row(tt, c, rev_n=nt), prev_spec] + [_full(p) for p in params] + [_row(tt, RW_C, rev_n=nt)] * 10,
        out_specs=[_row(tt, c, rev_n=nt), _row(tt, RW_C, rev_n=nt), _row(tt, RW_C, rev_n=nt),
                   pl.BlockSpec((1, c), lambda s: (0, 0)), acc, acc, acc, acc],
        out_shape=[jax.ShapeDtypeStruct((s_len, c), BF16), jax.ShapeDtypeStruct((s_len, RW_C), BF16),
                   jax.ShapeDtypeStruct((s_len, RW_C), BF16), jax.ShapeDtypeStruct((1, c), F32), vec, vec, vec, vec],
        scratch_shapes=[pltpu.VMEM((1, c), F32)],
        compiler_params=_cparams(("arbitrary",)),
    )(proj, proj, *params, *cts)


def rwkv_post_fwd(y, r, k, v, g, ln_w, ln_b, r_k, *, tt=64):
    s_len = y.shape[0]

    def body(y_ref, r_ref, k_ref, v_ref, g_ref, lw_ref, lb_ref, rk_ref, o_ref):
        o_ref[...] = f_post(y_ref[...], r_ref[...], k_ref[...], v_ref[...], g_ref[...],
                            lw_ref[...], lb_ref[...], rk_ref[...]).astype(BF16)

    return pl.pallas_call(
        body, name="rwkv_post_fwd", grid=(s_len // tt,),
        in_specs=[_row(tt, RW_C)] * 5 + [_full(ln_w), _full(ln_b), _full(r_k)],
        out_specs=_row(tt, RW_C), out_shape=jax.ShapeDtypeStruct((s_len, RW_C), BF16),
        compiler_params=_cparams(("parallel",)),
    )(y, r, k, v, g, ln_w, ln_b, r_k)


def rwkv_post_bwd(y, r, k, v, g, ln_w, ln_b, r_k, d_o, *, tt=32):
    s_len = y.shape[0]

    def body(y_ref, r_ref, k_ref, v_ref, g_ref, lw_ref, lb_ref, rk_ref, do_ref,
             dy_ref, dr_ref, dk_ref, dv_ref, dg_ref, dlw_ref, dlb_ref, drk_ref):
        _acc_init(pl.program_id(0) == 0, dlw_ref, dlb_ref, drk_ref)
        _, vjp = jax.vjp(f_post, y_ref[...], r_ref[...], k_ref[...], v_ref[...], g_ref[...],
                         lw_ref[...], lb_ref[...], rk_ref[...])
        dy, dr, dk, dv, dg, dlw, dlb, drk = vjp(do_ref[...].astype(F32))
        dy_ref[...] = dy
        dr_ref[...] = dr
        dk_ref[...] = dk
        dv_ref[...] = dv
        dg_ref[...] = dg
        dlw_ref[...] += dlw
        dlb_ref[...] += dlb
        drk_ref[...] += drk

    rows_f = jax.ShapeDtypeStruct((s_len, RW_C), F32)
    vec = jax.ShapeDtypeStruct((1, RW_C), F32)
    acc = pl.BlockSpec((1, RW_C), lambda s: (0, 0))
    return pl.pallas_call(
        body, name="rwkv_post_bwd", grid=(s_len // tt,),
        in_specs=[_row(tt, RW_C)] * 5 + [_full(ln_w), _full(ln_b), _full(r_k), _row(tt, RW_C)],
        out_specs=[_row(tt, RW_C)] * 5 + [acc] * 3, out_shape=[rows_f] * 5 + [vec] * 3,
        compiler_params=_cparams(("arbitrary",)),
    )(y, r, k, v, g, ln_w, ln_b, r_k, d_o)


def merge_fwd(gate, ab, rb, *, tt=128):
    s_len, d = ab.shape

    def body(ga_ref, gr_ref, a_ref, r_ref, o_ref):
        o_ref[...] = f_merge(ga_ref[...], gr_ref[...], a_ref[...], r_ref[...]).astype(BF16)

    return pl.pallas_call(
        body, name="merge_fwd", grid=(s_len // tt,),
        in_specs=[_row(tt, d, 0), _row(tt, d, 1), _row(tt, d), _row(tt, d)],
        out_specs=_row(tt, d), out_shape=jax.ShapeDtypeStruct((s_len, d), BF16),
        compiler_params=_cparams(("parallel",)),
    )(gate, gate, ab, rb)


def merge_bwd(gate, ab, rb, d_m, *, tt=64):
    s_len, d = ab.shape

    def body(ga_ref, gr_ref, a_ref, r_ref, dm_ref, dgate_ref, da_ref, dr_ref):
        _, vjp = jax.vjp(f_merge, ga_ref[...], gr_ref[...], a_ref[...], r_ref[...])
        dga, dgr, da, dr = vjp(dm_ref[...].astype(F32))
        dgate_ref[:, 0:d] = dga.astype(BF16)
        dgate_ref[:, d:2 * d] = dgr.astype(BF16)
        da_ref[...] = da.astype(BF16)
        dr_ref[...] = dr.astype(BF16)

    return pl.pallas_call(
        body, name="merge_bwd", grid=(s_len // tt,),
        in_specs=[_row(tt, d, 0), _row(tt, d, 1), _row(tt, d), _row(tt, d), _row(tt, d)],
        out_specs=[_row(tt, 2 * d), _row(tt, d), _row(tt, d)],
        out_shape=[jax.ShapeDtypeStruct((s_len, 2 * d), BF16), jax.ShapeDtypeStruct((s_len, d), BF16),
                   jax.ShapeDtypeStruct((s_len, d), BF16)],
        compiler_params=_cparams(("parallel",)),
    )(gate, gate, ab, rb, d_m)


def swiglu_fwd(gg, uu, *, tt=64):
    s_len, f = gg.shape

    def body(g_ref, u_ref, o_ref):
        o_ref[...] = f_swiglu(g_ref[...], u_ref[...]).astype(BF16)

    return pl.pallas_call(
        body, name="swiglu_fwd", grid=(s_len // tt,),
        in_specs=[_row(tt, f), _row(tt, f)], out_specs=_row(tt, f),
        out_shape=jax.ShapeDtypeStruct((s_len, f), BF16),
        compiler_params=_cparams(("parallel",)),
    )(gg, uu)


def swiglu_bwd(gg, uu, d_act, *, tt=32):
    s_len, f = gg.shape

    def body(g_ref, u_ref, d_ref, dg_ref, du_ref):
        _, vjp = jax.vjp(f_swiglu, g_ref[...], u_ref[...])
        dg, du = vjp(d_ref[...].astype(F32))
        dg_ref[...] = dg.astype(BF16)
        du_ref[...] = du.astype(BF16)

    out = jax.ShapeDtypeStruct((s_len, f), BF16)
    return pl.pallas_call(
        body, name="swiglu_bwd", grid=(s_len // tt,),
        in_specs=[_row(tt, f)] * 3, out_specs=[_row(tt, f)] * 2, out_shape=[out, out],
        compiler_params=_cparams(("parallel",)),
    )(gg, uu, d_act)


def resid_norm_fwd(x, m2, g_post, g_pre, *, tt=128):
    s_len, d = x.shape

    def body(x_ref, m_ref, gp_ref, gn_ref, x1_ref, h_ref):
        x1 = x_ref[...] + f_rms(m_ref[...], gp_ref[...])
        x1_ref[...] = x1
        h_ref[...] = f_rms(x1, gn_ref[...]).astype(BF16)

    return pl.pallas_call(
        body, name="resid_norm_fwd", grid=(s_len // tt,),
        in_specs=[_row(tt, d), _row(tt, d), _full(g_post), _full(g_pre)],
        out_specs=[_row(tt, d), _row(tt, d)],
        out_shape=[jax.ShapeDtypeStruct((s_len, d), F32), jax.ShapeDtypeStruct((s_len, d), BF16)],
        compiler_params=_cparams(("parallel",)),
    )(x, m2, g_post, g_pre)


def loss_head(x1, ff, tgt, g_post, *, tt=64):
    s_len, d = x1.shape

    def body(x1_ref, f_ref, t_ref, g_ref, loss_ref, dy_ref, df_ref, dg_ref):
        _acc_init(pl.program_id(0) == 0, loss_ref, dg_ref)
        nrm, vjp = jax.vjp(f_rms, f_ref[...], g_ref[...])
        err = x1_ref[...] + nrm - t_ref[...]
        per_tok = jnp.mean(err * err, axis=-1, keepdims=True)
        loss_ref[...] += 0.5 * jnp.sum(per_tok, axis=0, keepdims=True)
        dy = err * (1.0 / d)
        dff, dg = vjp(dy)
        dy_ref[...] = dy
        df_ref[...] = dff.astype(BF16)
        dg_ref[...] += dg

    return pl.pallas_call(
        body, name="loss_head", grid=(s_len // tt,),
        in_specs=[_row(tt, d)] * 3 + [_full(g_post)],
        out_specs=[pl.BlockSpec((1, LANES), lambda s: (0, 0)), _row(tt, d), _row(tt, d),
                   pl.BlockSpec((1, d), lambda s: (0, 0))],
        out_shape=[jax.ShapeDtypeStruct((1, LANES), F32), jax.ShapeDtypeStruct((s_len, d), F32),
                   jax.ShapeDtypeStruct((s_len, d), BF16), jax.ShapeDtypeStruct((1, d), F32)],
        compiler_params=_cparams(("arbitrary",)),
    )(x1, ff, tgt, g_post)


def resid_norm_bwd(x1, dh_a, dh_b, g_pre, m2, g_post, dy, *, tt=64):
    s_len, d = x1.shape

    def body(x1_ref, da_ref, db_ref, gn_ref, m_ref, gp_ref, dy_ref, dx1_ref, dm_ref, dgn_ref, dgp_ref):
        _acc_init(pl.program_id(0) == 0, dgn_ref, dgp_ref)
        _, vjp_n = jax.vjp(f_rms, x1_ref[...], gn_ref[...])
        dx1_n, dgn = vjp_n(da_ref[...] + db_ref[...])
        dx1 = dy_ref[...] + dx1_n
        _, vjp_p = jax.vjp(f_rms, m_ref[...], gp_ref[...])
        dm, dgp = vjp_p(dx1)
        dx1_ref[...] = dx1
        dm_ref[...] = dm.astype(BF16)
        dgn_ref[...] += dgn
        dgp_ref[...] += dgp

    acc = pl.BlockSpec((1, d), lambda s: (0, 0))
    vec = jax.ShapeDtypeStruct((1, d), F32)
    return pl.pallas_call(
        body, name="resid_norm_bwd", grid=(s_len // tt,),
        in_specs=[_row(tt, d)] * 3 + [_full(g_pre), _row(tt, d), _full(g_post), _row(tt, d)],
        out_specs=[_row(tt, d), _row(tt, d), acc, acc],
        out_shape=[jax.ShapeDtypeStruct((s_len, d), F32), jax.ShapeDtypeStruct((s_len, d), BF16), vec, vec],
        compiler_params=_cparams(("arbitrary",)),
    )(x1, dh_a, dh_b, g_pre, m2, g_post, dy)


def rms_bwd(x, g, dh_a, dh_b, dh_c, dres, *, tt=64):
    s_len, d = x.shape

    def body(x_ref, g_ref, a_ref, b_ref, c_ref, r_ref, dx_ref, dg_ref):
        _acc_init(pl.program_id(0) == 0, dg_ref)
        _, vjp = jax.vjp(f_rms, x_ref[...], g_ref[...])
        dx, dg = vjp(a_ref[...] + b_ref[...] + c_ref[...])
        dx_ref[...] = r_ref[...] + dx
        dg_ref[...] += dg

    return pl.pallas_call(
        body, name="rms_bwd", grid=(s_len // tt,),
        in_specs=[_row(tt, d), _full(g)] + [_row(tt, d)] * 4,
        out_specs=[_row(tt, d), pl.BlockSpec((1, d), lambda s: (0, 0))],
        out_shape=[jax.ShapeDtypeStruct((s_len, d), F32), jax.ShapeDtypeStruct((1, d), F32)],
        compiler_params=_cparams(("arbitrary",)),
    )(x, g, dh_a, dh_b, dh_c, dres)


def colsum(a, *, tt=256):
    s_len, c = a.shape

    def body(a_ref, o_ref):
        _acc_init(pl.program_id(0) == 0, o_ref)
        o_ref[...] += jnp.sum(a_ref[...].astype(F32), axis=0, keepdims=True)

    return pl.pallas_call(
        body, name="colsum", grid=(s_len // tt,),
        in_specs=[_row(tt, c)], out_specs=pl.BlockSpec((1, c), lambda s: (0, 0)),
        out_shape=jax.ShapeDtypeStruct((1, c), F32),
        compiler_params=_cparams(("arbitrary",)),
    )(a)


AT_HD = 128
AT_GROUP = 4
AT_KVH = 8
AT_BLK = 128
AT_QW = AT_GROUP * AT_HD
AT_KCOL = AT_KVH * AT_GROUP
AT_VCOL = AT_KCOL + AT_KVH
NEG_INF = -1e30
AT_SCALE = AT_HD ** -0.5


def _rope(t, cos2, sin2):
    return t * cos2 + pltpu.roll(t, AT_HD // 2, axis=1) * sin2


def _rope_t(d, cos2, sin2):
    return d * cos2 + pltpu.roll(d * sin2, AT_HD // 2, axis=1)


def _att_specs():
    prev = lambda i: jnp.maximum(i - 1, 0)
    blk = (AT_BLK, AT_HD)
    return [
        pl.BlockSpec((AT_BLK, AT_QW), lambda h, i: (i, h)),
        pl.BlockSpec(blk, lambda h, i: (i, AT_KCOL + h)),
        pl.BlockSpec(blk, lambda h, i: (prev(i), AT_KCOL + h)),
        pl.BlockSpec(blk, lambda h, i: (i, AT_VCOL + h)),
        pl.BlockSpec(blk, lambda h, i: (prev(i), AT_VCOL + h)),
        pl.BlockSpec((1, AT_QW), lambda h, i: (0, h)),
        pl.BlockSpec((1, AT_HD), lambda h, i: (0, AT_KCOL + h)),
        pl.BlockSpec((1, AT_HD), lambda h, i: (0, AT_VCOL + h)),
        pl.BlockSpec((1, AT_GROUP, AT_HD), lambda h, i: (h, 0, 0)),
        pl.BlockSpec(blk, lambda h, i: (i, 0)),
        pl.BlockSpec(blk, lambda h, i: (i, 0)),
        pl.BlockSpec(blk, lambda h, i: (prev(i), 0)),
        pl.BlockSpec(blk, lambda h, i: (prev(i), 0)),
    ]


def _att_load(i, q_ref, kc_ref, kp_ref, vc_ref, vp_ref, bq_ref, bk_ref, bv_ref, cc_ref, sc_ref, cp_ref, sp_ref):
    cosc, sinc = cc_ref[...], sc_ref[...]
    q = q_ref[...] + bq_ref[...]
    kc = _rope(kc_ref[...] + bk_ref[...], cosc, sinc)
    kp = _rope(kp_ref[...] + bk_ref[...], cp_ref[...], sp_ref[...])
    kcat = jnp.concatenate([kp, kc], axis=0).astype(BF16)
    vcat = jnp.concatenate([vp_ref[...] + bv_ref[...], vc_ref[...] + bv_ref[...]], axis=0).astype(BF16)
    qi = lax.broadcasted_iota(jnp.int32, (AT_BLK, 2 * AT_BLK), 0)
    kj = lax.broadcasted_iota(jnp.int32, (AT_BLK, 2 * AT_BLK), 1)
    rel = qi + AT_BLK - kj
    mask = (rel >= 0) & (rel < AT_BLK) & ((kj >= AT_BLK) | (i > 0))
    return q, kcat, vcat, mask, cosc, sinc


def _att_probs(qg, kcat, mask, sink):
    s = lax.dot_general(qg.astype(BF16), kcat, (((1,), (1,)), ((), ())), preferred_element_type=F32) * AT_SCALE
    s = jnp.where(mask, s, NEG_INF)
    m = jnp.maximum(jnp.max(s, axis=-1, keepdims=True), sink)
    p = jnp.exp(s - m)
    es = jnp.exp(sink - m)
    inv = 1.0 / (jnp.sum(p, axis=-1, keepdims=True) + es)
    return p * inv, es * inv


def attention_fwd(qkv, bias, sinks_b, cos2, sin2):
    s_len = qkv.shape[0]
    nb = s_len // AT_BLK

    def body(q_ref, kc_ref, kp_ref, vc_ref, vp_ref, bq_ref, bk_ref, bv_ref, sk_ref, cc_ref, sc_ref, cp_ref, sp_ref,
             o_ref):
        i = pl.program_id(1)
        q, kcat, vcat, mask, cosc, sinc = _att_load(i, q_ref, kc_ref, kp_ref, vc_ref, vp_ref, bq_ref, bk_ref,
                                                    bv_ref, cc_ref, sc_ref, cp_ref, sp_ref)
        for g in range(AT_GROUP):
            ls = slice(g * AT_HD, (g + 1) * AT_HD)
            qg = _rope(q[:, ls], cosc, sinc)
            probs, _ = _att_probs(qg, kcat, mask, sk_ref[0, g:g + 1, 0:1])
            o_ref[:, ls] = jnp.dot(probs.astype(BF16), vcat, preferred_element_type=F32).astype(BF16)

    return pl.pallas_call(
        body, name="attention_fwd", grid=(AT_KVH, nb),
        in_specs=_att_specs(),
        out_specs=pl.BlockSpec((AT_BLK, AT_QW), lambda h, i: (i, h)),
        out_shape=jax.ShapeDtypeStruct((s_len, AT_KVH * AT_QW), BF16),
        compiler_params=_cparams(("parallel", "parallel")),
    )(qkv, qkv, qkv, qkv, qkv, bias, bias, bias, sinks_b, cos2, sin2, cos2, sin2)


def attention_bwd(qkv, bias, sinks_b, cos2, sin2, d_o):
    s_len = qkv.shape[0]
    nb = s_len // AT_BLK

    def body(q_ref, kc_ref, kp_ref, vc_ref, vp_ref, bq_ref, bk_ref, bv_ref, sk_ref, cc_ref, sc_ref, cp_ref, sp_ref,
             do_ref, dq_ref, dk_ref, dv_ref, dsk_ref):
        i = pl.program_id(1)
        _acc_init(i == 0, dsk_ref)
        q, kcat, vcat, mask, cosc, sinc = _att_load(i, q_ref, kc_ref, kp_ref, vc_ref, vp_ref, bq_ref, bk_ref,
                                                    bv_ref, cc_ref, sc_ref, cp_ref, sp_ref)
        dk_cat = jnp.zeros((2 * AT_BLK, AT_HD), F32)
        dv_cat = jnp.zeros((2 * AT_BLK, AT_HD), F32)
        lane = lax.broadcasted_iota(jnp.int32, (1, AT_HD), 1)
        dsk = jnp.zeros((1, AT_HD), F32)
        for g in range(AT_GROUP):
            ls = slice(g * AT_HD, (g + 1) * AT_HD)
            qg = _rope(q[:, ls], cosc, sinc).astype(BF16)
            probs, psink = _att_probs(qg, kcat, mask, sk_ref[0, g:g + 1, 0:1])
            pb = probs.astype(BF16)
            do_g = do_ref[:, ls].astype(F32)
            do_b = do_g.astype(BF16)
            o_g = jnp.dot(pb, vcat, preferred_element_type=F32)
            dsum = jnp.sum(do_g * o_g, axis=-1, keepdims=True)
            dp = lax.dot_general(do_b, vcat, (((1,), (1,)), ((), ())), preferred_element_type=F32)
            ds = (probs * (dp - dsum) * AT_SCALE).astype(BF16)
            dsk = dsk + jnp.where(lane == g, -jnp.sum(psink * dsum, axis=0, keepdims=True), 0.0)
            dv_cat = dv_cat + lax.dot_general(pb, do_b, (((0,), (0,)), ((), ())), preferred_element_type=F32)
            dk_cat = dk_cat + lax.dot_general(ds, qg, (((0,), (0,)), ((), ())), preferred_element_type=F32)
            dq_ref[:, ls] = _rope_t(jnp.dot(ds, kcat, preferred_element_type=F32), cosc, sinc).astype(BF16)
        dsk_ref[0] += dsk
        cur = pl.ds(pl.multiple_of(i * AT_BLK, AT_BLK), AT_BLK)
        dk_ref[cur, :] = _rope_t(dk_cat[AT_BLK:], cosc, sinc)
        dv_ref[cur, :] = dv_cat[AT_BLK:]

        @pl.when(i > 0)
        def _():
            prv = pl.ds(pl.multiple_of((i - 1) * AT_BLK, AT_BLK), AT_BLK)
            dk_ref[prv, :] += _rope_t(dk_cat[:AT_BLK], cp_ref[...], sp_ref[...])
            dv_ref[prv, :] += dv_cat[:AT_BLK]

    kv_out = pl.BlockSpec((s_len, AT_HD), lambda h, i: (0, h))
    return pl.pallas_call(
        body, name="attention_bwd", grid=(AT_KVH, nb),
        in_specs=_att_specs() + [pl.BlockSpec((AT_BLK, AT_QW), lambda h, i: (i, h))],
        out_specs=[pl.BlockSpec((AT_BLK, AT_QW), lambda h, i: (i, h)), kv_out, kv_out,
                   pl.BlockSpec((1, 1, AT_HD), lambda h, i: (h, 0, 0))],
        out_shape=[jax.ShapeDtypeStruct((s_len, AT_KVH * AT_QW), BF16),
                   jax.ShapeDtypeStruct((s_len, AT_KVH * AT_HD), F32),
                   jax.ShapeDtypeStruct((s_len, AT_KVH * AT_HD), F32),
                   jax.ShapeDtypeStruct((AT_KVH, 1, AT_HD), F32)],
        compiler_params=_cparams(("arbitrary", "arbitrary")),
    )(qkv, qkv, qkv, qkv, qkv, bias, bias, bias, sinks_b, cos2, sin2, cos2, sin2, d_o)


ATT_QKV = 6144
RW_SHIFT = 13024
RW_PAD = 13056
D_GATE = 480
ROPE_THETA = 10000.0


def perm_cols(a):
    lead = a.shape[:-1]
    return jnp.swapaxes(a.reshape(lead + (RW_H, RW_N)), -1, -2).reshape(lead + (RW_C,))


def rw_reorder(a, pad_value=0):
    r, k, v = (perm_cols(a[..., i * RW_C:(i + 1) * RW_C]) for i in range(3))
    wd = a[..., 3 * RW_C:3 * RW_C + 128]
    ad = a[..., 3 * RW_C + 128:3 * RW_C + 256]
    gd = a[..., 3 * RW_C + 256:]
    pad = jnp.full(a.shape[:-1] + (512 - D_GATE,), pad_value, a.dtype)
    return jnp.concatenate([r, k, v, gd, pad, wd, ad], axis=-1)


def rw_restore(a):
    r, k, v = (perm_cols(a[..., i * RW_C:(i + 1) * RW_C]) for i in range(3))
    gd = a[..., 3 * RW_C:3 * RW_C + D_GATE]
    wd = a[..., 3 * RW_C + 512:3 * RW_C + 640]
    ad = a[..., 3 * RW_C + 640:3 * RW_C + 768]
    return jnp.concatenate([r, k, v, wd, ad, gd], axis=-1)


def to_tiles(a):
    t = a.reshape(a.shape[0], RW_N, RW_H)
    return jnp.concatenate([t, t], axis=-1)


def from_tiles(t):
    return t[:, :, :RW_H].reshape(t.shape[0], RW_C)


def rope_tables(s_len):
    pos = jnp.arange(s_len, dtype=F32)
    inv_freq = ROPE_THETA ** (-jnp.arange(0, AT_HD, 2, dtype=F32) / AT_HD)
    ang = pos[:, None] * inv_freq[None, :]
    cos, sin = jnp.cos(ang), jnp.sin(ang)
    return jnp.concatenate([cos, cos], axis=1), jnp.concatenate([-sin, sin], axis=1)


def local_step(x, tgt, small, big, fwd_rider, got_early, bwd_rider):
    s_len, d = x.shape
    w_in = big["w_in"]
    w_qkv = w_in[:, :ATT_QKV]
    w_rw = rw_reorder(w_in[:, ATT_QKV:ATT_QKV + RW_SHIFT])
    w_gate = w_in[:, ATT_QKV + RW_SHIFT:]
    w2 = perm_cols(big["w2"])
    a2 = perm_cols(big["a2"])
    g2 = jnp.pad(perm_cols(big["g2"]), ((0, 512 - D_GATE), (0, 0)))
    mu = rw_reorder(small["mu_shift"])
    w0, a0, k_k, k_a, ln_w, ln_b = (perm_cols(small[n]) for n in ("w0", "a0", "k_k", "k_a", "ln_x_w", "ln_x_b"))
    r_k = small["r_k"].reshape(RW_H, RW_N).T.reshape(1, RW_C)
    sinks_b = jnp.broadcast_to(small["att_sinks"].reshape(AT_KVH, AT_GROUP, 1), (AT_KVH, AT_GROUP, AT_HD))
    cos2, sin2 = rope_tables(s_len)
    bias = small["b_qkv"]

    h = rms_fwd(x, small["norm_mix_pre"])
    qkv = matmul(h, w_qkv, name="mm_qkv")
    prw = matmul(h, w_rw, name="mm_rw")
    gate = matmul(h, w_gate, name="mm_gate")
    o_att = attention_fwd(qkv, bias, sinks_b, cos2, sin2)
    pre_params = (mu, w0, a0, k_k, k_a, w2, a2, g2)
    r, dec, k, v, av, bv, g, tw, xa, sg = rwkv_pre_fwd(prw, *pre_params)
    v3 = to_tiles(v)
    y3, ck, *arrived = rwkv_scan_fwd(r, dec, k, av, bv, v3, rider=fwd_rider)
    big = {**big, **got_early(arrived)}
    w_rb = big["w_rwkv_branch"].reshape(RW_H, RW_N, d).swapaxes(0, 1).reshape(RW_C, d)
    y = from_tiles(y3)
    o_rw = rwkv_post_fwd(y, r, k, v, g, ln_w, ln_b, r_k)
    ab = matmul(o_att, big["w_att_branch"], name="mm_ab")
    rb = matmul(o_rw, w_rb, name="mm_rb")
    merged = merge_fwd(gate, ab, rb)
    m2 = matmul(merged, big["w_out"], name="mm_out")
    x1, h2 = resid_norm_fwd(x, m2, small["norm_mix_post"], small["norm_ffn_pre"])
    gg = matmul(h2, big["w_ffn_gate"], name="mm_fg")
    uu = matmul(h2, big["w_ffn_up"], name="mm_fu")
    act = swiglu_fwd(gg, uu)
    ff = matmul(act, big["w_ffn_down"], name="mm_fd")
    loss, dy, dff, d_nfp = loss_head(x1, ff, tgt, small["norm_ffn_post"])

    dact = matmul(dff, big["w_ffn_down"], tb=True, name="mm_dact")
    g_fd = matmul(act, dff, ta=True, out_dtype=BF16, name="mm_gfd")
    dgg, duu = swiglu_bwd(gg, uu, dact)
    g_fg = matmul(h2, dgg, ta=True, out_dtype=BF16, name="mm_gfg")
    g_fu = matmul(h2, duu, ta=True, out_dtype=BF16, name="mm_gfu")
    dh2a = matmul(dgg, big["w_ffn_gate"], tb=True, name="mm_dh2a")
    dh2b = matmul(duu, big["w_ffn_up"], tb=True, name="mm_dh2b")
    dx1, dm2, d_nfpre, d_nmpost = resid_norm_bwd(x1, dh2a, dh2b, small["norm_ffn_pre"], m2, small["norm_mix_post"], dy)
    dmerged = matmul(dm2, big["w_out"], tb=True, name="mm_dmerged")
    g_out = matmul(merged, dm2, ta=True, out_dtype=BF16, name="mm_gout")
    dgate, dab, drb = merge_bwd(gate, ab, rb, dmerged)
    do_att = matmul(dab, big["w_att_branch"], tb=True, out_dtype=BF16, name="mm_doatt")
    g_ab = matmul(o_att, dab, ta=True, out_dtype=BF16, name="mm_gab")
    do_rw = matmul(drb, w_rb, tb=True, name="mm_dorw")
    g_rb = matmul(o_rw, drb, ta=True, out_dtype=BF16, name="mm_grb")
    dq, dk_att, dv_att, dsk = attention_bwd(qkv, bias, sinks_b, cos2, sin2, do_att)
    dqkv = jnp.concatenate([dq, dk_att.astype(BF16), dv_att.astype(BF16)], axis=1)
    dy_s, dr_p, dk_p, dv_p, dg, d_lnw, d_lnb, d_rk = rwkv_post_bwd(y, r, k, v, g, ln_w, ln_b, r_k, do_rw)
    early = {"w_att_branch": g_ab, "w_rwkv_branch": g_rb.reshape(RW_N, RW_H, d).swapaxes(0, 1).reshape(RW_C, d),
             "w_out": g_out, "w_ffn_gate": g_fg, "w_ffn_up": g_fu, "w_ffn_down": g_fd}
    dr_s, ddec, dk_s, dav, dbv, dv3, *from_chips = rwkv_scan_bwd(r, dec, k, av, bv, v3, ck, to_tiles(dy_s),
                                                                  rider=bwd_rider(early))
    dprw, dzw, dza, dmu, dw0, da0, dkk, dka = rwkv_pre_bwd(
        prw, *pre_params, dr_p, ddec, dk_p, dv_p, dav, dbv, dg, dr_s, dk_s, from_tiles(dv3))
    g_w2 = matmul(tw, dzw, ta=True, out_dtype=BF16, name="mm_gw2")
    g_a2 = matmul(xa, dza, ta=True, out_dtype=BF16, name="mm_ga2")
    g_g2 = matmul(sg, dg.astype(BF16), ta=True, out_dtype=BF16, name="mm_gg2")
    g_qkv = matmul(h, dqkv, ta=True, out_dtype=BF16, name="mm_gqkv")
    g_rw = matmul(h, dprw, ta=True, out_dtype=BF16, name="mm_grw")
    g_gate = matmul(h, dgate, ta=True, out_dtype=BF16, name="mm_ggate")
    dh_a = matmul(dqkv, w_qkv, tb=True, name="mm_dha")
    dh_b = matmul(dprw, w_rw, tb=True, name="mm_dhb")
    dh_c = matmul(dgate, w_gate, tb=True, name="mm_dhc")
    grad_x, d_nmpre = rms_bwd(x, small["norm_mix_pre"], dh_a, dh_b, dh_c, dx1)
    d_bias = colsum(dqkv)

    gsmall = {
        "norm_mix_pre": d_nmpre, "norm_mix_post": d_nmpost, "norm_ffn_pre": d_nfpre, "norm_ffn_post": d_nfp,
        "b_qkv": d_bias, "att_sinks": dsk[:, 0, :AT_GROUP].reshape(1, AT_KVH * AT_GROUP),
        "mu_shift": rw_restore(dmu), "w0": perm_cols(dw0), "a0": perm_cols(da0), "k_k": perm_cols(dkk),
        "k_a": perm_cols(dka), "r_k": d_rk.reshape(RW_N, RW_H).T.reshape(1, RW_C),
        "ln_x_w": perm_cols(d_lnw), "ln_x_b": perm_cols(d_lnb),
    }
    gbig = {
        "w_in": jnp.concatenate([g_qkv, rw_restore(g_rw), g_gate], axis=1),
        "w2": perm_cols(g_w2), "a2": perm_cols(g_a2), "g2": perm_cols(g_g2)[:D_GATE],
    }
    return loss, grad_x, gsmall, gbig, from_chips


def _place():
    x, y, c = lax.axis_index("x"), lax.axis_index("y"), lax.axis_index("c")
    chips = [(1 - x, y), (x, 1 - y), (1 - x, 1 - y)]
    return x, y, c, chips


def _remote(src, dst, send_sems, recv_sems, k, dev):
    return pltpu.make_async_remote_copy(src_ref=src, dst_ref=dst, send_sem=send_sems.at[k], recv_sem=recv_sems.at[k],
                                        device_id=dev, device_id_type=MESH)


def _gather_parts(n):
    def half(ref, which):
        hr = ref.shape[0] // 2
        return ref.at[pl.ds(which * hr, hr), :]

    def sends(ins, outs, send_sems, recv_sems):
        x, y, c, chips = _place()
        me = 2 * x + y
        return [_remote(half(ins[i], c), half(outs[i].at[me], c), send_sems, recv_sems, 6 * i + j, (*chip, c))
                for i in range(n) for j, chip in enumerate(chips)]

    def start(ins, outs, send_sems, recv_sems):
        for cp in sends(ins, outs, send_sems, recv_sems):
            cp.start()

    def finish(ins, outs, send_sems, recv_sems):
        x, y, c, chips = _place()
        sib = (x, y, 1 - c)
        passed = []
        for i in range(n):
            for j, chip in enumerate(chips):
                got = half(outs[i].at[2 * chip[0] + chip[1]], c)
                _remote(got, got, send_sems, recv_sems, 6 * i + j, sib).wait_recv()
                cp = _remote(got, got, send_sems, recv_sems, 6 * i + 3 + j, sib)
                cp.start()
                passed.append(cp)
        for i in range(n):
            for j, chip in enumerate(chips):
                got = half(outs[i].at[2 * chip[0] + chip[1]], 1 - c)
                _remote(got, got, send_sems, recv_sems, 6 * i + 3 + j, sib).wait_recv()
        for cp in sends(ins, outs, send_sems, recv_sems) + passed:
            cp.wait_send()

    return start, finish


def gather_rider(shards):
    n = len(shards)
    start, finish = _gather_parts(n)
    return {"ins": shards, "out_shapes": [jax.ShapeDtypeStruct((4,) + s.shape, s.dtype) for s in shards],
            "scratch": [pltpu.SemaphoreType.DMA((6 * n,)), pltpu.SemaphoreType.DMA((6 * n,))],
            "start": start, "finish": finish}


def gather_weights(shards):
    n = len(shards)
    start, finish = _gather_parts(n)

    def body(*refs):
        parts = (refs[:n], refs[n:2 * n], *refs[2 * n:])
        start(*parts)
        finish(*parts)

    return pl.pallas_call(
        body, name="gather_weights",
        in_specs=[ANY] * n, out_specs=[ANY] * n,
        out_shape=[jax.ShapeDtypeStruct((4,) + s.shape, s.dtype) for s in shards],
        scratch_shapes=[pltpu.SemaphoreType.DMA((6 * n,)), pltpu.SemaphoreType.DMA((6 * n,))],
    )(*shards)


def swap_with_sibling(blocks, name):
    n = len(blocks)

    def body(*refs):
        ins, outs = refs[:n], refs[n:2 * n]
        send_sems, recv_sems = refs[2 * n:]
        x, y, c, _ = _place()
        cps = [_remote(ins[i], outs[i], send_sems, recv_sems, i, (x, y, 1 - c)) for i in range(n)]
        for cp in cps:
            cp.start()
        for cp in cps:
            cp.wait()

    return pl.pallas_call(
        body, name=name, in_specs=[ANY] * n, out_specs=[ANY] * n,
        out_shape=[jax.ShapeDtypeStruct(b.shape, b.dtype) for b in blocks],
        scratch_shapes=[pltpu.SemaphoreType.DMA((n,)), pltpu.SemaphoreType.DMA((n,))],
    )(*blocks)


def _scatter_parts(n):
    def copies(ins, outs, send_sems, recv_sems):
        x, y, c, chips = _place()
        return [_remote(ins[i].at[2 * chip[0] + chip[1]], outs[i].at[j], send_sems, recv_sems, 3 * i + j, (*chip, c))
                for i in range(n) for j, chip in enumerate(chips)]

    def start(*refs):
        for cp in copies(*refs):
            cp.start()

    def finish(*refs):
        for cp in copies(*refs):
            cp.wait()

    return start, finish


def scatter_rider(parts):
    n = len(parts)
    start, finish = _scatter_parts(n)
    return {"ins": parts, "out_shapes": [jax.ShapeDtypeStruct((3,) + p.shape[1:], p.dtype) for p in parts],
            "scratch": [pltpu.SemaphoreType.DMA((3 * n,)), pltpu.SemaphoreType.DMA((3 * n,))],
            "start": start, "finish": finish}


def scatter_to_owners(parts):
    n = len(parts)
    start, finish = _scatter_parts(n)

    def body(*refs):
        parts_ = (refs[:n], refs[n:2 * n], *refs[2 * n:])
        start(*parts_)
        finish(*parts_)

    return pl.pallas_call(
        body, name="scatter_to_owners", in_specs=[ANY] * n, out_specs=[ANY] * n,
        out_shape=[jax.ShapeDtypeStruct((3,) + p.shape[1:], p.dtype) for p in parts],
        scratch_shapes=[pltpu.SemaphoreType.DMA((3 * n,)), pltpu.SemaphoreType.DMA((3 * n,))],
    )(*parts)


def allreduce_small(v):
    rows = v.shape[0]

    def body(v_ref, o_ref, buf, send_sems, recv_sems):
        x, y, c, chips = _place()
        me, sib = (x, y, c), (x, y, 1 - c)

        def slot(px, py, pc):
            return buf.at[4 * px + 2 * py + pc]

        def copy(k, block, to, src=None):
            return _remote(slot(*block) if src is None else src, slot(*block), send_sems, recv_sems, k, to)

        buf[4 * x + 2 * y + c] = v_ref[...]
        first = [copy(0, me, sib, src=v_ref)]
        first += [copy(1 + j, me, (*chip, c), src=v_ref) for j, chip in enumerate(chips)]
        for cp in first:
            cp.start()
        passed = [copy(4 + j, (*chip, c), sib) for j, chip in enumerate(chips)]
        for j, chip in enumerate(chips):
            copy(1 + j, (*chip, c), me).wait_recv()
            passed[j].start()
        copy(0, sib, me).wait_recv()
        for j, chip in enumerate(chips):
            copy(4 + j, (*chip, 1 - c), me).wait_recv()
        for cp in first + passed:
            cp.wait_send()
        acc = buf[0]
        for k in range(1, 8):
            acc = acc + buf[k]
        o_ref[...] = acc

    vm = pl.BlockSpec(memory_space=pltpu.VMEM)
    return pl.pallas_call(
        body, name="allreduce_small", in_specs=[vm], out_specs=vm,
        out_shape=jax.ShapeDtypeStruct(v.shape, F32),
        scratch_shapes=[pltpu.VMEM((8, rows, LANES), F32), pltpu.SemaphoreType.DMA((7,)),
                        pltpu.SemaphoreType.DMA((7,))],
    )(v)


def _rows_tile(r):
    return _pick(r, (64, 32, 16, 8))


def add_pairs(a, b):
    _, r, c = a.shape
    tr = _rows_tile(r)
    spec = pl.BlockSpec((1, tr, c), lambda s, i: (s, i, 0))

    def body(a_ref, b_ref, o_ref):
        o_ref[...] = (a_ref[...].astype(F32) + b_ref[...].astype(F32)).astype(o_ref.dtype)

    return pl.pallas_call(
        body, name="add_pairs", grid=(4, r // tr), in_specs=[spec, spec], out_specs=spec,
        out_shape=jax.ShapeDtypeStruct(a.shape, a.dtype), compiler_params=_cparams(("parallel", "parallel")),
    )(a, b)


def add_four(mine, others):
    r, c = mine.shape
    tr = _rows_tile(r)

    def body(m_ref, o_ref, out_ref):
        acc = m_ref[...].astype(F32)
        for j in range(3):
            acc = acc + o_ref[j].astype(F32)
        out_ref[...] = acc

    return pl.pallas_call(
        body, name="add_four", grid=(r // tr,),
        in_specs=[pl.BlockSpec((tr, c), lambda i: (i, 0)), pl.BlockSpec((3, tr, c), lambda i: (0, i, 0))],
        out_specs=pl.BlockSpec((tr, c), lambda i: (i, 0)),
        out_shape=jax.ShapeDtypeStruct((r, c), F32), compiler_params=_cparams(("parallel",)),
    )(mine, others)


def pair_sums(blocks, tag):
    cc = lax.axis_index("c")
    mine, other = [], []
    for g4 in blocks:
        hr = g4.shape[1] // 2
        mine.append(lax.dynamic_slice_in_dim(g4, cc * hr, hr, axis=1))
        other.append(lax.dynamic_slice_in_dim(g4, (1 - cc) * hr, hr, axis=1))
    from_sibling = swap_with_sibling(other, "swap_halves_" + tag)
    return [add_pairs(a, b) for a, b in zip(mine, from_sibling)]


def owner_sums(pair, from_chips):
    cx, cy, cc = lax.axis_index("x"), lax.axis_index("y"), lax.axis_index("c")
    me = 2 * cx + cy
    sums = [add_four(lax.dynamic_index_in_dim(p, me, 0, keepdims=False), t) for p, t in zip(pair, from_chips)]
    got = swap_with_sibling(sums, "swap_sums")
    return [jnp.concatenate([jnp.where(cc == 0, s, g), jnp.where(cc == 0, g, s)], axis=0) for s, g in zip(sums, got)]


ADAM_LR = 0.001
ADAM_B1 = 0.9
ADAM_B2 = 0.999
ADAM_EPS = 1e-08
ADAM_WD = 0.01
ADAM_STEP = 10


def adamw(w, g, m, v):
    r, c = w.shape
    tr = _rows_tile(r)
    spec = pl.BlockSpec((tr, c), lambda i: (i, 0))

    def body(w_ref, g_ref, m_ref, v_ref, d_ref, nm_ref, nv_ref):
        gv = g_ref[...]
        nm = ADAM_B1 * m_ref[...] + (1.0 - ADAM_B1) * gv
        nv = ADAM_B2 * v_ref[...] + (1.0 - ADAM_B2) * jnp.square(gv)
        m_hat = nm / (1.0 - ADAM_B1 ** ADAM_STEP)
        v_hat = nv / (1.0 - ADAM_B2 ** ADAM_STEP)
        d_ref[...] = -ADAM_LR * (m_hat / (jnp.sqrt(v_hat) + ADAM_EPS) + ADAM_WD * w_ref[...])
        nm_ref[...] = nm
        nv_ref[...] = nv

    out = jax.ShapeDtypeStruct((r, c), F32)
    return pl.pallas_call(
        body, name="adamw", grid=(r // tr,), in_specs=[spec] * 4, out_specs=[spec] * 3, out_shape=[out] * 3,
        compiler_params=_cparams(("parallel",)),
    )(w, g, m, v)


WEIGHTS = ["norm_mix_pre", "norm_mix_post", "norm_ffn_pre", "norm_ffn_post", "w_in", "b_qkv", "att_sinks", "mu_shift",
           "w0", "w2", "a0", "a2", "g2", "k_k", "k_a", "r_k", "ln_x_w", "ln_x_b", "w_att_branch", "w_rwkv_branch",
           "w_out", "w_ffn_gate", "w_ffn_up", "w_ffn_down"]
BIG = {"w_in": 1, "w2": 1, "a2": 1, "g2": 1, "w_att_branch": 0, "w_rwkv_branch": 0, "w_out": 0, "w_ffn_gate": 1,
       "w_ffn_up": 1, "w_ffn_down": 0}
SMALL = [n for n in WEIGHTS if n not in BIG]
LATE = ("w_in", "w2", "a2", "g2")
N_CHIPS = 4


def _whole(g4, axis):
    if axis == 0:
        return g4.reshape(g4.shape[0] * g4.shape[1], g4.shape[2])
    return jnp.swapaxes(g4, 0, 1).reshape(g4.shape[1], g4.shape[0] * g4.shape[2])


def _by_shard(w, axis):
    if axis == 0:
        return w.reshape(N_CHIPS, w.shape[0] // N_CHIPS, w.shape[1])
    return jnp.swapaxes(w.reshape(w.shape[0], N_CHIPS, w.shape[1] // N_CHIPS), 0, 1)


def _pack(parts):
    flat = jnp.concatenate([parts[n].reshape(-1) for n in SMALL])
    rows = -(-flat.shape[0] // (LANES * SUBLANES)) * SUBLANES
    return jnp.pad(flat, (0, rows * LANES - flat.shape[0])).reshape(rows, LANES)


def _unpack(packed, like):
    flat = packed.reshape(-1)
    out, off = {}, 0
    for n in SMALL:
        size = like[n].size
        out[n] = flat[off:off + size].reshape(like[n].shape)
        off += size
    return out


def kernel(x, norm_mix_pre, norm_mix_post, norm_ffn_pre, norm_ffn_post, w_in, b_qkv, att_sinks, mu_shift, w0, w2, a0, a2, g2, k_k, k_a, r_k, ln_x_w, ln_x_b, w_att_branch, w_rwkv_branch, w_out, w_ffn_gate, w_ffn_up, w_ffn_down, loss_target, m_norm_mix_pre, m_norm_mix_post, m_norm_ffn_pre, m_norm_ffn_post, m_w_in, m_b_qkv, m_att_sinks, m_mu_shift, m_w0, m_w2, m_a0, m_a2, m_g2, m_k_k, m_k_a, m_r_k, m_ln_x_w, m_ln_x_b, m_w_att_branch, m_w_rwkv_branch, m_w_out, m_w_ffn_gate, m_w_ffn_up, m_w_ffn_down, v_norm_mix_pre, v_norm_mix_post, v_norm_ffn_pre, v_norm_ffn_post, v_w_in, v_b_qkv, v_att_sinks, v_mu_shift, v_w0, v_w2, v_a0, v_a2, v_g2, v_k_k, v_k_a, v_r_k, v_ln_x_w, v_ln_x_b, v_w_att_branch, v_w_rwkv_branch, v_w_out, v_w_ffn_gate, v_w_ffn_up, v_w_ffn_down):
    given = dict(locals())
    wts = {n: given[n] for n in WEIGHTS}
    mom1 = {n: given["m_" + n] for n in WEIGHTS}
    mom2 = {n: given["v_" + n] for n in WEIGHTS}
    early = [n for n in BIG if n not in LATE]
    me = 2 * lax.axis_index("x") + lax.axis_index("y")
    own = {n: wts[n][0].astype(BF16) for n in BIG}

    def placed(names, gathered):
        return {n: _whole(lax.dynamic_update_index_in_dim(g4, own[n], me, 0), BIG[n]) for n, g4 in zip(names, gathered)}

    small = {n: wts[n].reshape(1, -1) for n in SMALL}
    pairs = {}

    def bwd_rider(grads_early):
        pairs["early"] = pair_sums([_by_shard(grads_early[n], BIG[n]) for n in early], "early")
        return scatter_rider(pairs["early"])

    loss, grad_x, gsmall, gbig, chips_early = local_step(
        x[0], loss_target[0], small, placed(LATE, gather_weights([own[n] for n in LATE])),
        gather_rider([own[n] for n in early]), lambda arrived: placed(early, arrived), bwd_rider)

    pair_late = pair_sums([_by_shard(gbig[n], BIG[n]) for n in LATE], "late")
    wholes = owner_sums(pairs["early"] + pair_late, list(chips_early) + list(scatter_to_owners(pair_late)))
    grads = dict(zip(early + list(LATE), wholes))
    names = list(BIG)

    gsum = _unpack(allreduce_small(_pack(gsmall)), small)

    outs_g, outs_d, outs_m, outs_v = {}, {}, {}, {}
    for n in names:
        d, nm, nv = adamw(wts[n][0], grads[n], mom1[n][0], mom2[n][0])
        outs_g[n], outs_d[n], outs_m[n], outs_v[n] = (t[None] for t in (grads[n], d, nm, nv))
    pk = lambda src: _pack({n: src[n] for n in SMALL})
    d, nm, nv = adamw(pk(wts), _pack(gsum), pk(mom1), pk(mom2))
    du, mu, vu = _unpack(d, small), _unpack(nm, small), _unpack(nv, small)
    for n in SMALL:
        outs_g[n], outs_d[n], outs_m[n], outs_v[n] = (t[n].reshape(wts[n].shape) for t in (gsum, du, mu, vu))

    total = lax.psum(loss[0, 0], ("x", "y", "c"))
    return (total, grad_x[None], *[outs_g[n] for n in WEIGHTS], *[outs_d[n] for n in WEIGHTS],
            *[outs_m[n] for n in WEIGHTS], *[outs_v[n] for n in WEIGHTS])
```

```python
import jax
import jax.numpy as jnp
from jax import lax
from jax.experimental import pallas as pl
from jax.experimental.pallas import tpu as pltpu

F32 = jnp.float32
BF16 = jnp.bfloat16

LANES = 128
SUBLANES = 8
VMEM_LIMIT = 56 * 1024 * 1024

RW_H = 64
RW_N = 64
RW_C = RW_H * RW_N
RW_NB = RW_C // LANES
SCAN_CHUNK = 8


MESH = pl.DeviceIdType.MESH
ANY = pl.BlockSpec(memory_space=pl.ANY)


def _cparams(sem=None):
    return pltpu.CompilerParams(dimension_semantics=sem, vmem_limit_bytes=VMEM_LIMIT)


def _fold(x):
    return x + pltpu.roll(x, 64, axis=x.ndim - 1)


def _scan_step_fwd(t, src_ref, dst_ref, r_ref, w_ref, k_ref, a_ref, b_ref, v_ref):
    vt = v_ref[t]
    acc = jnp.zeros((RW_N, LANES), F32)
    for j in range(RW_NB):
        ls = slice(j * LANES, (j + 1) * LANES)
        acc = acc + src_ref[j] * a_ref[t:t + 1, ls]
    sa = _fold(acc)
    yacc = jnp.zeros((RW_N, LANES), F32)
    for j in range(RW_NB):
        ls = slice(j * LANES, (j + 1) * LANES)
        s_new = src_ref[j] * w_ref[t:t + 1, ls] + sa * b_ref[t:t + 1, ls] + vt * k_ref[t:t + 1, ls]
        dst_ref[j] = s_new
        yacc = yacc + s_new * r_ref[t:t + 1, ls]
    return _fold(yacc), sa


def _rider_parts(rider):
    if rider is None:
        return [], [], []
    return list(rider["ins"]), list(rider["out_shapes"]), list(rider["scratch"])


def rwkv_scan_fwd(r, w, k, a, b, v3, rider=None):
    s_len = r.shape[0]
    nchunk = s_len // SCAN_CHUNK
    x_in, x_out, x_scr = _rider_parts(rider)
    ni, no = len(x_in), len(x_out)

    def body(*refs):
        r_ref, w_ref, k_ref, a_ref, b_ref, v_ref = refs[:6]
        y_ref, ck_ref = refs[6 + ni:8 + ni]
        st_ref = refs[8 + ni + no]
        ride = (refs[6:6 + ni], refs[8 + ni:8 + ni + no], *refs[9 + ni + no:])

        @pl.when(pl.program_id(0) == 0)
        def _():
            st_ref[...] = jnp.zeros_like(st_ref)
            if rider is not None:
                rider["start"](*ride)

        ck_ref[0] = st_ref[...]

        for t in range(SCAN_CHUNK):
            y, _ = _scan_step_fwd(t, st_ref, st_ref, r_ref, w_ref, k_ref, a_ref, b_ref, v_ref)
            y_ref[t] = y

        if rider is not None:
            @pl.when(pl.program_id(0) == nchunk - 1)
            def _():
                rider["finish"](*ride)

    row = pl.BlockSpec((SCAN_CHUNK, RW_C), lambda i: (i, 0))
    til = pl.BlockSpec((SCAN_CHUNK, RW_N, LANES), lambda i: (i, 0, 0))
    return pl.pallas_call(
        body,
        name="rwkv_scan_fwd",
        grid=(nchunk,),
        in_specs=[row, row, row, row, row, til] + [ANY] * ni,
        out_specs=[til, pl.BlockSpec((1, RW_NB, RW_N, LANES), lambda i: (i, 0, 0, 0))] + [ANY] * no,
        out_shape=[
            jax.ShapeDtypeStruct((s_len, RW_N, LANES), F32),
            jax.ShapeDtypeStruct((nchunk, RW_NB, RW_N, LANES), F32),
        ] + x_out,
        scratch_shapes=[pltpu.VMEM((RW_NB, RW_N, LANES), F32)] + x_scr,
        compiler_params=_cparams(("arbitrary",)),
    )(r, w, k, a, b, v3, *x_in)


def rwkv_scan_bwd(r, w, k, a, b, v3, ck, dy3, rider=None):
    s_len = r.shape[0]
    nchunk = s_len // SCAN_CHUNK
    x_in, x_out, x_scr = _rider_parts(rider)
    ni, no = len(x_in), len(x_out)

    def body(*refs):
        r_ref, w_ref, k_ref, a_ref, b_ref, v_ref, ck_ref, dy_ref = refs[:8]
        dr_ref, dw_ref, dk_ref, da_ref, db_ref, dv_ref = refs[8 + ni:14 + ni]
        hist_ref, sa_ref, ds_ref = refs[14 + ni + no:17 + ni + no]
        ride = (refs[8:8 + ni], refs[14 + ni:14 + ni + no], *refs[17 + ni + no:])

        @pl.when(pl.program_id(0) == 0)
        def _():
            ds_ref[...] = jnp.zeros_like(ds_ref)
            if rider is not None:
                rider["start"](*ride)

        hist_ref[0] = ck_ref[0]

        for t in range(SCAN_CHUNK):
            _, sa = _scan_step_fwd(t, hist_ref.at[t], hist_ref.at[t + 1], r_ref, w_ref, k_ref, a_ref, b_ref, v_ref)
            sa_ref[t] = sa

        for t in reversed(range(SCAN_CHUNK)):
            vt = v_ref[t]
            dyt = dy_ref[t]
            sat = sa_ref[t]
            dv_acc = jnp.zeros((RW_N, LANES), F32)
            dsa_acc = jnp.zeros((RW_N, LANES), F32)
            for j in range(RW_NB):
                ls = slice(j * LANES, (j + 1) * LANES)
                row = (slice(t, t + 1), ls)
                ds_j = ds_ref[j] + dyt * r_ref[row]
                ds_ref[j] = ds_j
                dr_ref[row] = jnp.sum(hist_ref[t + 1, j] * dyt, axis=0, keepdims=True)
                dv_acc = dv_acc + ds_j * k_ref[row]
                dk_ref[row] = jnp.sum(ds_j * vt, axis=0, keepdims=True)
                dsa_acc = dsa_acc + ds_j * b_ref[row]
                db_ref[row] = jnp.sum(ds_j * sat, axis=0, keepdims=True)
                dw_ref[row] = jnp.sum(ds_j * hist_ref[t, j], axis=0, keepdims=True)
            dv_ref[t] = _fold(dv_acc)
            dsa = _fold(dsa_acc)
            for j in range(RW_NB):
                ls = slice(j * LANES, (j + 1) * LANES)
                row = (slice(t, t + 1), ls)
                da_ref[row] = jnp.sum(hist_ref[t, j] * dsa, axis=0, keepdims=True)
                ds_ref[j] = ds_ref[j] * w_ref[row] + dsa * a_ref[row]

        if rider is not None:
            @pl.when(pl.program_id(0) == nchunk - 1)
            def _():
                rider["finish"](*ride)

    rev = lambda i: (nchunk - 1 - i, 0)
    rev3 = lambda i: (nchunk - 1 - i, 0, 0)
    row = pl.BlockSpec((SCAN_CHUNK, RW_C), rev)
    til = pl.BlockSpec((SCAN_CHUNK, RW_N, LANES), rev3)
    rows = jax.ShapeDtypeStruct((s_len, RW_C), F32)
    return pl.pallas_call(
        body,
        name="rwkv_scan_bwd",
        grid=(nchunk,),
        in_specs=[row, row, row, row, row, til,
                  pl.BlockSpec((1, RW_NB, RW_N, LANES), lambda i: (nchunk - 1 - i, 0, 0, 0)), til] + [ANY] * ni,
        out_specs=[row, row, row, row, row, til] + [ANY] * no,
        out_shape=[rows, rows, rows, rows, rows, jax.ShapeDtypeStruct((s_len, RW_N, LANES), F32)] + x_out,
        scratch_shapes=[
            pltpu.VMEM((SCAN_CHUNK + 1, RW_NB, RW_N, LANES), F32),
            pltpu.VMEM((SCAN_CHUNK, RW_N, LANES), F32),
            pltpu.VMEM((RW_NB, RW_N, LANES), F32),
        ] + x_scr,
        compiler_params=_cparams(("arbitrary",)),
    )(r, w, k, a, b, v3, ck, dy3, *x_in)


def _pick(n, cands):
    for c in cands:
        if n % c == 0:
            return c
    return n


MM_VMEM_BUDGET = 40 * 1024 * 1024
MM_FLOPS = 8.5e14
MM_HBM = 2.2e12
MM_STEP = 0.4e-6


def _mm_plan(m, n, k, out_bytes):
    divs = lambda d: [t for t in range(LANES, d + 1, LANES) if d % t == 0] or [d]
    best = None
    for tm in divs(m):
        for tn in divs(n):
            for tk in divs(k):
                nk = k // tk
                vmem = 4 * (tm * tk + tk * tn) + (4 * tm * tn if nk > 1 else 0) + 2 * tm * tn * out_bytes
                if vmem > MM_VMEM_BUDGET:
                    continue
                steps = (m // tm) * (n // tn) * nk
                for n_outer in (False, True):
                    if nk > 1:
                        traffic = steps * (tm * tk + tk * tn) * 2
                    elif n_outer:
                        traffic = (n // tn) * (k * tn + m * k) * 2
                    else:
                        traffic = (m // tm) * (tm * k + k * n) * 2
                    cost = max(2.0 * m * n * k / MM_FLOPS, (traffic + m * n * out_bytes) / MM_HBM) + steps * MM_STEP
                    if best is None or cost < best[0]:
                        best = (cost, tm, tn, tk, n_outer)
    return best[1:]


def matmul(a, b, *, ta=False, tb=False, out_dtype=F32, name="matmul", plan=None, rider=None):
    m, kdim = (a.shape[1], a.shape[0]) if ta else a.shape
    n = b.shape[0] if tb else b.shape[1]
    assert (b.shape[1] if tb else b.shape[0]) == kdim
    tm, tn, tk, n_outer = plan or _mm_plan(m, n, kdim, jnp.dtype(out_dtype).itemsize)
    nk = kdim // tk
    dims = (((0 if ta else 1,), (1 if tb else 0,)), ((), ()))
    grid = (n // tn, m // tm, nk) if n_outer else (m // tm, n // tn, nk)
    x_in, x_out, x_scr = _rider_parts(rider)
    ni, no = len(x_in), len(x_out)
    n_acc = 1 if nk > 1 else 0

    def body(*refs):
        a_ref, b_ref = refs[:2]
        o_ref = refs[2 + ni]
        ride = (refs[2:2 + ni], refs[3 + ni:3 + ni + no], *refs[3 + ni + no + n_acc:])
        ids = [pl.program_id(ax) for ax in range(3)]
        if rider is not None:
            @pl.when((ids[0] == 0) & (ids[1] == 0) & (ids[2] == 0))
            def _():
                rider["start"](*ride)

        prod = lax.dot_general(a_ref[...].astype(BF16), b_ref[...].astype(BF16), dims, preferred_element_type=F32)
        if nk == 1:
            o_ref[...] = prod.astype(o_ref.dtype)
        else:
            acc_ref = refs[3 + ni + no]
            kk = ids[2]

            @pl.when(kk == 0)
            def _():
                acc_ref[...] = prod

            @pl.when(kk > 0)
            def _():
                acc_ref[...] += prod

            @pl.when(kk == nk - 1)
            def _():
                o_ref[...] = acc_ref[...].astype(o_ref.dtype)

        if rider is not None:
            @pl.when((ids[0] == grid[0] - 1) & (ids[1] == grid[1] - 1) & (ids[2] == nk - 1))
            def _():
                rider["finish"](*ride)

    ij = (lambda p, q: (q, p)) if n_outer else (lambda p, q: (p, q))
    a_map = (lambda p, q, k: (k, ij(p, q)[0])) if ta else (lambda p, q, k: (ij(p, q)[0], k))
    b_map = (lambda p, q, k: (ij(p, q)[1], k)) if tb else (lambda p, q, k: (k, ij(p, q)[1]))
    out = pl.pallas_call(
        body,
        name=name,
        grid=grid,
        in_specs=[pl.BlockSpec((tk, tm) if ta else (tm, tk), a_map),
                  pl.BlockSpec((tn, tk) if tb else (tk, tn), b_map)] + [ANY] * ni,
        out_specs=[pl.BlockSpec((tm, tn), lambda p, q, k: ij(p, q))] + [ANY] * no,
        out_shape=[jax.ShapeDtypeStruct((m, n), out_dtype)] + x_out,
        scratch_shapes=([pltpu.VMEM((tm, tn), F32)] if nk > 1 else []) + x_scr,
        compiler_params=_cparams(("arbitrary",) * 3 if rider is not None else ("parallel", "parallel", "arbitrary")),
    )(a, b, *x_in)
    return out if rider is not None else out[0]


@jax.custom_vjp
def hsum(x):
    acc = x[:, 0:LANES]
    for j in range(1, RW_NB):
        acc = acc + x[:, j * LANES:(j + 1) * LANES]
    return _fold(acc)


def _hsum_fwd(x):
    return hsum(x), None


def _hsum_bwd(_, ct):
    return (jnp.concatenate([_fold(ct)] * RW_NB, axis=1),)


hsum.defvjp(_hsum_fwd, _hsum_bwd)


@jax.custom_vjp
def hbcast(s):
    return jnp.concatenate([s] * RW_NB, axis=1)


def _hbcast_fwd(s):
    return hbcast(s), None


def _hbcast_bwd(_, ct):
    acc = ct[:, 0:LANES]
    for j in range(1, RW_NB):
        acc = acc + ct[:, j * LANES:(j + 1) * LANES]
    return (acc,)


hbcast.defvjp(_hbcast_fwd, _hbcast_bwd)


@jax.custom_vjp
def bdot(x, w):
    return jnp.dot(x.astype(BF16), w, preferred_element_type=F32)


def _bdot_fwd(x, w):
    return bdot(x, w), w


def _bdot_bwd(w, ct):
    dx = lax.dot_general(ct.astype(BF16), w, (((1,), (1,)), ((), ())), preferred_element_type=F32)
    return dx, jnp.zeros_like(w)


bdot.defvjp(_bdot_fwd, _bdot_bwd)

RMS_EPS = 1e-6
GN_EPS = 64e-5


def f_rms(x, g):
    return x * lax.rsqrt(jnp.mean(x * x, axis=-1, keepdims=True) + RMS_EPS) * g


def _softplus(z):
    return jnp.maximum(z, 0.0) + jnp.log1p(jnp.exp(-jnp.abs(z)))


def f_pre(xk, xg, xw, xa, ew, ea, w0, a0, k_k, k_a, w2, a2, g2):
    tw = jnp.tanh(xw)
    sg = jax.nn.sigmoid(xg)
    wlog = -_softplus(-(w0 + bdot(tw, w2) + ew)) - 0.5
    decay = jnp.exp(-jnp.exp(wlog))
    a = jax.nn.sigmoid(a0 + bdot(xa, a2) + ea)
    g = bdot(sg, g2)
    kk0 = xk * k_k
    nrm = jnp.sqrt(hbcast(hsum(kk0 * kk0)))
    kk = kk0 / jnp.maximum(nrm, 1e-12)
    k = xk * (1.0 + (a - 1.0) * k_a)
    return decay, k, -kk, kk * a, g, tw, sg


def f_post(y, r, k, v, g, ln_w, ln_b, r_k):
    mu = hbcast(hsum(y)) * (1.0 / RW_N)
    yc = y - mu
    var = hbcast(hsum(yc * yc)) * (1.0 / RW_N)
    yn = yc * lax.rsqrt(var + GN_EPS) * ln_w + ln_b
    bonus = hbcast(hsum(r * k * r_k)) * v
    return (yn + bonus) * g


def f_merge(ga, gr, ab, rb):
    return jax.nn.sigmoid(ga) * ab + jax.nn.sigmoid(gr) * rb


def f_swiglu(gg, uu):
    return gg * jax.nn.sigmoid(gg) * uu


def _row(tt, width, cb=0, rev_n=None):
    if rev_n is None:
        return pl.BlockSpec((tt, width), lambda i: (i, cb))
    return pl.BlockSpec((tt, width), lambda i: (rev_n - 1 - i, cb))


def _full(arr):
    nd = arr.ndim
    return pl.BlockSpec(arr.shape, lambda i: (0,) * nd)


def _acc_init(i_first, *refs):
    @pl.when(i_first)
    def _():
        for r in refs:
            r[...] = jnp.zeros_like(r)


def rms_fwd(x, g, *, tt=128):
    s_len, d = x.shape

    def body(x_ref, g_ref, o_ref):
        o_ref[...] = f_rms(x_ref[...], g_ref[...]).astype(BF16)

    return pl.pallas_call(
        body, name="rms_fwd", grid=(s_len // tt,),
        in_specs=[_row(tt, d), _full(g)], out_specs=_row(tt, d),
        out_shape=jax.ShapeDtypeStruct((s_len, d), BF16),
        compiler_params=_cparams(("parallel",)),
    )(x, g)


def rwkv_pre_fwd(proj, mu, w0, a0, k_k, k_a, w2, a2, g2, *, tt=64):
    s_len, c = proj.shape
    nt = s_len // tt
    sub = tt // SUBLANES

    def body(p_ref, pb_ref, mu_ref, w0_ref, a0_ref, kk_ref, ka_ref, w2_ref, a2_ref, g2_ref,
             r_ref, dec_ref, k_ref, v_ref, av_ref, bv_ref, g_ref, tw_ref, xa_ref, sg_ref):
        i = pl.program_id(0)
        cur = p_ref[...]
        edge = jnp.where(i > 0, pb_ref[SUBLANES - 1:SUBLANES, :], 0.0)
        rows = lax.broadcasted_iota(jnp.int32, cur.shape, 0)
        prev = jnp.where(rows == 0, edge, pltpu.roll(cur, 1, axis=0))
        xs = cur + (prev - cur) * mu_ref[...]
        xr, xk, xv = xs[:, 0:RW_C], xs[:, RW_C:2 * RW_C], xs[:, 2 * RW_C:3 * RW_C]
        xg = xs[:, 3 * RW_C:3 * RW_C + 512]
        xw = xs[:, 3 * RW_C + 512:3 * RW_C + 640]
        xa = xs[:, 3 * RW_C + 640:3 * RW_C + 768]
        zero = jnp.zeros((tt, RW_C), F32)
        dec, k, av, bv, g, tw, sg = f_pre(xk, xg, xw, xa, zero, zero, w0_ref[...], a0_ref[...], kk_ref[...],
                                          ka_ref[...], w2_ref[...], a2_ref[...], g2_ref[...])
        r_ref[...] = xr
        dec_ref[...] = dec
        k_ref[...] = k
        v_ref[...] = xv
        av_ref[...] = av
        bv_ref[...] = bv
        g_ref[...] = g
        tw_ref[...] = tw.astype(BF16)
        xa_ref[...] = xa.astype(BF16)
        sg_ref[...] = sg.astype(BF16)

    rows_f = jax.ShapeDtypeStruct((s_len, RW_C), F32)
    prev_spec = pl.BlockSpec((SUBLANES, c), lambda i: (jnp.maximum(i * sub - 1, 0), 0))
    params = [mu, w0, a0, k_k, k_a, w2, a2, g2]
    return pl.pallas_call(
        body, name="rwkv_pre_fwd", grid=(nt,),
        in_specs=[_row(tt, c), prev_spec] + [_full(p) for p in params],
        out_specs=[_row(tt, RW_C)] * 7 + [_row(tt, 128), _row(tt, 128), _row(tt, 512)],
        out_shape=[rows_f] * 7 + [jax.ShapeDtypeStruct((s_len, 128), BF16), jax.ShapeDtypeStruct((s_len, 128), BF16),
                                  jax.ShapeDtypeStruct((s_len, 512), BF16)],
        compiler_params=_cparams(("parallel",)),
    )(proj, proj, *params)


def rwkv_pre_bwd(proj, mu, w0, a0, k_k, k_a, w2, a2, g2, d_r, d_dec, d_k, d_v, d_av, d_bv, d_g, d_r2, d_k2, d_v2,
                 *, tt=32):
    s_len, c = proj.shape
    nt = s_len // tt
    sub = tt // SUBLANES

    def body(p_ref, pb_ref, mu_ref, w0_ref, a0_ref, kk_ref, ka_ref, w2_ref, a2_ref, g2_ref,
             dr_ref, ddec_ref, dk_ref, dv_ref, dav_ref, dbv_ref, dg_ref, dr2_ref, dk2_ref, dv2_ref,
             dp_ref, dzw_ref, dza_ref, dmu_ref, dw0_ref, da0_ref, dkk_ref, dka_ref, carry_ref):
        step = pl.program_id(0)
        i = nt - 1 - step
        _acc_init(step == 0, dmu_ref, dw0_ref, da0_ref, dkk_ref, dka_ref, carry_ref)
        cur = p_ref[...]
        edge = jnp.where(i > 0, pb_ref[SUBLANES - 1:SUBLANES, :], 0.0)
        rows = lax.broadcasted_iota(jnp.int32, cur.shape, 0)
        prev = jnp.where(rows == 0, edge, pltpu.roll(cur, 1, axis=0))
        mu_v = mu_ref[...]
        xs = cur + (prev - cur) * mu_v
        xk = xs[:, RW_C:2 * RW_C]
        xg = xs[:, 3 * RW_C:3 * RW_C + 512]
        xw = xs[:, 3 * RW_C + 512:3 * RW_C + 640]
        xa = xs[:, 3 * RW_C + 640:3 * RW_C + 768]
        zero = jnp.zeros((tt, RW_C), F32)
        w2_v, a2_v, g2_v = w2_ref[...], a2_ref[...], g2_ref[...]

        def core(xk, xg, xw, xa, ew, ea, w0, a0, k_k, k_a):
            return f_pre(xk, xg, xw, xa, ew, ea, w0, a0, k_k, k_a, w2_v, a2_v, g2_v)[:5]

        _, vjp = jax.vjp(core, xk, xg, xw, xa, zero, zero, w0_ref[...], a0_ref[...], kk_ref[...], ka_ref[...])
        dxk, dxg, dxw, dxa, dzw, dza, dw0, da0, dkk, dka = vjp(
            (ddec_ref[...], dk_ref[...] + dk2_ref[...], dav_ref[...], dbv_ref[...], dg_ref[...]))
        dzw_ref[...] = dzw.astype(BF16)
        dza_ref[...] = dza.astype(BF16)
        dw0_ref[...] += dw0
        da0_ref[...] += da0
        dkk_ref[...] += dkk
        dka_ref[...] += dka
        dxs = jnp.concatenate([dr_ref[...] + dr2_ref[...], dxk, dv_ref[...] + dv2_ref[...], dxg, dxw, dxa], axis=1)
        dmu_ref[...] += jnp.sum(dxs * (prev - cur), axis=0, keepdims=True)
        to_prev = dxs * mu_v
        nxt = jnp.where(rows == tt - 1, carry_ref[...], pltpu.roll(to_prev, tt - 1, axis=0))
        carry_ref[...] = to_prev[0:1, :]
        dp_ref[...] = (dxs * (1.0 - mu_v) + nxt).astype(BF16)

    prev_spec = pl.BlockSpec((SUBLANES, c), lambda s: (jnp.maximum((nt - 1 - s) * sub - 1, 0), 0))
    params = [mu, w0, a0, k_k, k_a, w2, a2, g2]
    cts = [d_r, d_dec, d_k, d_v, d_av, d_bv, d_g, d_r2, d_k2, d_v2]
    vec = jax.ShapeDtypeStruct((1, RW_C), F32)
    acc = pl.BlockSpec((1, RW_C), lambda s: (0, 0))
    return pl.pallas_call(
        body, name="rwkv_pre_bwd", grid=(nt,),
        in_specs=[_row(tt, c, rev_n=nt), prev_spec] + [_full(p) for p in params] + [_row(tt, RW_C, rev_n=nt)] * 10,
        out_specs=[_row(tt, c, rev_n=nt), _row(tt, RW_C, rev_n=nt), _row(tt, RW_C, rev_n=nt),
                   pl.BlockSpec((1, c), lambda s: (0, 0)), acc, acc, acc, acc],
        out_shape=[jax.ShapeDtypeStruct((s_len, c), BF16), jax.ShapeDtypeStruct((s_len, RW_C), BF16),
                   jax.ShapeDtypeStruct((s_len, RW_C), BF16), jax.ShapeDtypeStruct((1, c), F32), vec, vec, vec, vec],
        scratch_shapes=[pltpu.VMEM((1, c), F32)],
        compiler_params=_cparams(("arbitrary",)),
    )(proj, proj, *params, *cts)


def rwkv_post_fwd(y, r, k, v, g, ln_w, ln_b, r_k, *, tt=64):
    s_len = y.shape[0]

    def body(y_ref, r_ref, k_ref, v_ref, g_ref, lw_ref, lb_ref, rk_ref, o_ref):
        o_ref[...] = f_post(y_ref[...], r_ref[...], k_ref[...], v_ref[...], g_ref[...],
                            lw_ref[...], lb_ref[...], rk_ref[...]).astype(BF16)

    return pl.pallas_call(
        body, name="rwkv_post_fwd", grid=(s_len // tt,),
        in_specs=[_row(tt, RW_C)] * 5 + [_full(ln_w), _full(ln_b), _full(r_k)],
        out_specs=_row(tt, RW_C), out_shape=jax.ShapeDtypeStruct((s_len, RW_C), BF16),
        compiler_params=_cparams(("parallel",)),
    )(y, r, k, v, g, ln_w, ln_b, r_k)


def rwkv_post_bwd(y, r, k, v, g, ln_w, ln_b, r_k, d_o, *, tt=32):
    s_len = y.shape[0]

    def body(y_ref, r_ref, k_ref, v_ref, g_ref, lw_ref, lb_ref, rk_ref, do_ref,
             dy_ref, dr_ref, dk_ref, dv_ref, dg_ref, dlw_ref, dlb_ref, drk_ref):
        _acc_init(pl.program_id(0) == 0, dlw_ref, dlb_ref, drk_ref)
        _, vjp = jax.vjp(f_post, y_ref[...], r_ref[...], k_ref[...], v_ref[...], g_ref[...],
                         lw_ref[...], lb_ref[...], rk_ref[...])
        dy, dr, dk, dv, dg, dlw, dlb, drk = vjp(do_ref[...].astype(F32))
        dy_ref[...] = dy
        dr_ref[...] = dr
        dk_ref[...] = dk
        dv_ref[...] = dv
        dg_ref[...] = dg
        dlw_ref[...] += dlw
        dlb_ref[...] += dlb
        drk_ref[...] += drk

    rows_f = jax.ShapeDtypeStruct((s_len, RW_C), F32)
    vec = jax.ShapeDtypeStruct((1, RW_C), F32)
    acc = pl.BlockSpec((1, RW_C), lambda s: (0, 0))
    return pl.pallas_call(
        body, name="rwkv_post_bwd", grid=(s_len // tt,),
        in_specs=[_row(tt, RW_C)] * 5 + [_full(ln_w), _full(ln_b), _full(r_k), _row(tt, RW_C)],
        out_specs=[_row(tt, RW_C)] * 5 + [acc] * 3, out_shape=[rows_f] * 5 + [vec] * 3,
        compiler_params=_cparams(("arbitrary",)),
    )(y, r, k, v, g, ln_w, ln_b, r_k, d_o)


def merge_fwd(gate, ab, rb, *, tt=128):
    s_len, d = ab.shape

    def body(ga_ref, gr_ref, a_ref, r_ref, o_ref):
        o_ref[...] = f_merge(ga_ref[...], gr_ref[...], a_ref[...], r_ref[...]).astype(BF16)

    return pl.pallas_call(
        body, name="merge_fwd", grid=(s_len // tt,),
        in_specs=[_row(tt, d, 0), _row(tt, d, 1), _row(tt, d), _row(tt, d)],
        out_specs=_row(tt, d), out_shape=jax.ShapeDtypeStruct((s_len, d), BF16),
        compiler_params=_cparams(("parallel",)),
    )(gate, gate, ab, rb)


def merge_bwd(gate, ab, rb, d_m, *, tt=64):
    s_len, d = ab.shape

    def body(ga_ref, gr_ref, a_ref, r_ref, dm_ref, dgate_ref, da_ref, dr_ref):
        _, vjp = jax.vjp(f_merge, ga_ref[...], gr_ref[...], a_ref[...], r_ref[...])
        dga, dgr, da, dr = vjp(dm_ref[...].astype(F32))
        dgate_ref[:, 0:d] = dga.astype(BF16)
        dgate_ref[:, d:2 * d] = dgr.astype(BF16)
        da_ref[...] = da.astype(BF16)
        dr_ref[...] = dr.astype(BF16)

    return pl.pallas_call(
        body, name="merge_bwd", grid=(s_len // tt,),
        in_specs=[_row(tt, d, 0), _row(tt, d, 1), _row(tt, d), _row(tt, d), _row(tt, d)],
        out_specs=[_row(tt, 2 * d), _row(tt, d), _row(tt, d)],
        out_shape=[jax.ShapeDtypeStruct((s_len, 2 * d), BF16), jax.ShapeDtypeStruct((s_len, d), BF16),
                   jax.ShapeDtypeStruct((s_len, d), BF16)],
        compiler_params=_cparams(("parallel",)),
    )(gate, gate, ab, rb, d_m)


def swiglu_fwd(gg, uu, *, tt=64):
    s_len, f = gg.shape

    def body(g_ref, u_ref, o_ref):
        o_ref[...] = f_swiglu(g_ref[...], u_ref[...]).astype(BF16)

    return pl.pallas_call(
        body, name="swiglu_fwd", grid=(s_len // tt,),
        in_specs=[_row(tt, f), _row(tt, f)], out_specs=_row(tt, f),
        out_shape=jax.ShapeDtypeStruct((s_len, f), BF16),
        compiler_params=_cparams(("parallel",)),
    )(gg, uu)


def swiglu_bwd(gg, uu, d_act, *, tt=32):
    s_len, f = gg.shape

    def body(g_ref, u_ref, d_ref, dg_ref, du_ref):
        _, vjp = jax.vjp(f_swiglu, g_ref[...], u_ref[...])
        dg, du = vjp(d_ref[...].astype(F32))
        dg_ref[...] = dg.astype(BF16)
        du_ref[...] = du.astype(BF16)

    out = jax.ShapeDtypeStruct((s_len, f), BF16)
    return pl.pallas_call(
        body, name="swiglu_bwd", grid=(s_len // tt,),
        in_specs=[_row(tt, f)] * 3, out_specs=[_row(tt, f)] * 2, out_shape=[out, out],
        compiler_params=_cparams(("parallel",)),
    )(gg, uu, d_act)


def resid_norm_fwd(x, m2, g_post, g_pre, *, tt=128):
    s_len, d = x.shape

    def body(x_ref, m_ref, gp_ref, gn_ref, x1_ref, h_ref):
        x1 = x_ref[...] + f_rms(m_ref[...], gp_ref[...])
        x1_ref[...] = x1
        h_ref[...] = f_rms(x1, gn_ref[...]).astype(BF16)

    return pl.pallas_call(
        body, name="resid_norm_fwd", grid=(s_len // tt,),
        in_specs=[_row(tt, d), _row(tt, d), _full(g_post), _full(g_pre)],
        out_specs=[_row(tt, d), _row(tt, d)],
        out_shape=[jax.ShapeDtypeStruct((s_len, d), F32), jax.ShapeDtypeStruct((s_len, d), BF16)],
        compiler_params=_cparams(("parallel",)),
    )(x, m2, g_post, g_pre)


def loss_head(x1, ff, tgt, g_post, *, tt=64):
    s_len, d = x1.shape

    def body(x1_ref, f_ref, t_ref, g_ref, loss_ref, dy_ref, df_ref, dg_ref):
        _acc_init(pl.program_id(0) == 0, loss_ref, dg_ref)
        nrm, vjp = jax.vjp(f_rms, f_ref[...], g_ref[...])
        err = x1_ref[...] + nrm - t_ref[...]
        per_tok = jnp.mean(err * err, axis=-1, keepdims=True)
        loss_ref[...] += 0.5 * jnp.sum(per_tok, axis=0, keepdims=True)
        dy = err * (1.0 / d)
        dff, dg = vjp(dy)
        dy_ref[...] = dy
        df_ref[...] = dff.astype(BF16)
        dg_ref[...] += dg

    return pl.pallas_call(
        body, name="loss_head", grid=(s_len // tt,),
        in_specs=[_row(tt, d)] * 3 + [_full(g_post)],
        out_specs=[pl.BlockSpec((1, LANES), lambda s: (0, 0)), _row(tt, d), _row(tt, d),
                   pl.BlockSpec((1, d), lambda s: (0, 0))],
        out_shape=[jax.ShapeDtypeStruct((1, LANES), F32), jax.ShapeDtypeStruct((s_len, d), F32),
                   jax.ShapeDtypeStruct((s_len, d), BF16), jax.ShapeDtypeStruct((1, d), F32)],
        compiler_params=_cparams(("arbitrary",)),
    )(x1, ff, tgt, g_post)


def resid_norm_bwd(x1, dh_a, dh_b, g_pre, m2, g_post, dy, *, tt=64):
    s_len, d = x1.shape

    def body(x1_ref, da_ref, db_ref, gn_ref, m_ref, gp_ref, dy_ref, dx1_ref, dm_ref, dgn_ref, dgp_ref):
        _acc_init(pl.program_id(0) == 0, dgn_ref, dgp_ref)
        _, vjp_n = jax.vjp(f_rms, x1_ref[...], gn_ref[...])
        dx1_n, dgn = vjp_n(da_ref[...] + db_ref[...])
        dx1 = dy_ref[...] + dx1_n
        _, vjp_p = jax.vjp(f_rms, m_ref[...], gp_ref[...])
        dm, dgp = vjp_p(dx1)
        dx1_ref[...] = dx1
        dm_ref[...] = dm.astype(BF16)
        dgn_ref[...] += dgn
        dgp_ref[...] += dgp

    acc = pl.BlockSpec((1, d), lambda s: (0, 0))
    vec = jax.ShapeDtypeStruct((1, d), F32)
    return pl.pallas_call(
        body, name="resid_norm_bwd", grid=(s_len // tt,),
        in_specs=[_row(tt, d)] * 3 + [_full(g_pre), _row(tt, d), _full(g_post), _row(tt, d)],
        out_specs=[_row(tt, d), _row(tt, d), acc, acc],
        out_shape=[jax.ShapeDtypeStruct((s_len, d), F32), jax.ShapeDtypeStruct((s_len, d), BF16), vec, vec],
        compiler_params=_cparams(("arbitrary",)),
    )(x1, dh_a, dh_b, g_pre, m2, g_post, dy)


def rms_bwd(x, g, dh_a, dh_b, dh_c, dres, *, tt=64):
    s_len, d = x.shape

    def body(x_ref, g_ref, a_ref, b_ref, c_ref, r_ref, dx_ref, dg_ref):
        _acc_init(pl.program_id(0) == 0, dg_ref)
        _, vjp = jax.vjp(f_rms, x_ref[...], g_ref[...])
        dx, dg = vjp(a_ref[...] + b_ref[...] + c_ref[...])
        dx_ref[...] = r_ref[...] + dx
        dg_ref[...] += dg

    return pl.pallas_call(
        body, name="rms_bwd", grid=(s_len // tt,),
        in_specs=[_row(tt, d), _full(g)] + [_row(tt, d)] * 4,
        out_specs=[_row(tt, d), pl.BlockSpec((1, d), lambda s: (0, 0))],
        out_shape=[jax.ShapeDtypeStruct((s_len, d), F32), jax.ShapeDtypeStruct((1, d), F32)],
        compiler_params=_cparams(("arbitrary",)),
    )(x, g, dh_a, dh_b, dh_c, dres)


def colsum(a, *, tt=256):
    s_len, c = a.shape

    def body(a_ref, o_ref):
        _acc_init(pl.program_id(0) == 0, o_ref)
        o_ref[...] += jnp.sum(a_ref[...].astype(F32), axis=0, keepdims=True)

    return pl.pallas_call(
        body, name="colsum", grid=(s_len // tt,),
        in_specs=[_row(tt, c)], out_specs=pl.BlockSpec((1, c), lambda s: (0, 0)),
        out_shape=jax.ShapeDtypeStruct((1, c), F32),
        compiler_params=_cparams(("arbitrary",)),
    )(a)


AT_HD = 128
AT_GROUP = 4
AT_KVH = 8
AT_BLK = 128
AT_QW = AT_GROUP * AT_HD
AT_KCOL = AT_KVH * AT_GROUP
AT_VCOL = AT_KCOL + AT_KVH
NEG_INF = -1e30
AT_SCALE = AT_HD ** -0.5


def _rope(t, cos2, sin2):
    return t * cos2 + pltpu.roll(t, AT_HD // 2, axis=1) * sin2


def _rope_t(d, cos2, sin2):
    return d * cos2 + pltpu.roll(d * sin2, AT_HD // 2, axis=1)


def _att_specs():
    prev = lambda i: jnp.maximum(i - 1, 0)
    blk = (AT_BLK, AT_HD)
    return [
        pl.BlockSpec((AT_BLK, AT_QW), lambda h, i: (i, h)),
        pl.BlockSpec(blk, lambda h, i: (i, AT_KCOL + h)),
        pl.BlockSpec(blk, lambda h, i: (prev(i), AT_KCOL + h)),
        pl.BlockSpec(blk, lambda h, i: (i, AT_VCOL + h)),
        pl.BlockSpec(blk, lambda h, i: (prev(i), AT_VCOL + h)),
        pl.BlockSpec((1, AT_QW), lambda h, i: (0, h)),
        pl.BlockSpec((1, AT_HD), lambda h, i: (0, AT_KCOL + h)),
        pl.BlockSpec((1, AT_HD), lambda h, i: (0, AT_VCOL + h)),
        pl.BlockSpec((1, AT_GROUP, AT_HD), lambda h, i: (h, 0, 0)),
        pl.BlockSpec(blk, lambda h, i: (i, 0)),
        pl.BlockSpec(blk, lambda h, i: (i, 0)),
        pl.BlockSpec(blk, lambda h, i: (prev(i), 0)),
        pl.BlockSpec(blk, lambda h, i: (prev(i), 0)),
    ]


def _att_load(i, q_ref, kc_ref, kp_ref, vc_ref, vp_ref, bq_ref, bk_ref, bv_ref, cc_ref, sc_ref, cp_ref, sp_ref):
    cosc, sinc = cc_ref[...], sc_ref[...]
    q = q_ref[...] + bq_ref[...]
    kc = _rope(kc_ref[...] + bk_ref[...], cosc, sinc)
    kp = _rope(kp_ref[...] + bk_ref[...], cp_ref[...], sp_ref[...])
    kcat = jnp.concatenate([kp, kc], axis=0).astype(BF16)
    vcat = jnp.concatenate([vp_ref[...] + bv_ref[...], vc_ref[...] + bv_ref[...]], axis=0).astype(BF16)
    qi = lax.broadcasted_iota(jnp.int32, (AT_GROUP * AT_BLK, 2 * AT_BLK), 0) & (AT_BLK - 1)
    kj = lax.broadcasted_iota(jnp.int32, (AT_GROUP * AT_BLK, 2 * AT_BLK), 1)
    rel = qi + AT_BLK - kj
    mask = (rel >= 0) & (rel < AT_BLK) & ((kj >= AT_BLK) | (i > 0))
    return q, kcat, vcat, mask, cosc, sinc


AT_ROWS = AT_GROUP * AT_BLK


def _att_stack(q, cosc, sinc):
    return jnp.concatenate([_rope(q[:, g * AT_HD:(g + 1) * AT_HD], cosc, sinc) for g in range(AT_GROUP)], axis=0)


def _att_cols(sk_ref):
    head = lax.broadcasted_iota(jnp.int32, (AT_ROWS, 1), 0) >> 7
    sink = jnp.zeros((AT_ROWS, 1), F32)
    for g in range(AT_GROUP):
        sink = jnp.where(head == g, sk_ref[0, g:g + 1, 0:1], sink)
    return sink, head


def _att_probs(qs, kcat, mask, sink):
    s = lax.dot_general(qs, kcat, (((1,), (1,)), ((), ())), preferred_element_type=F32) * AT_SCALE
    s = jnp.where(mask, s, NEG_INF)
    m = jnp.maximum(jnp.max(s, axis=-1, keepdims=True), sink)
    p = jnp.exp(s - m)
    es = jnp.exp(sink - m)
    inv = 1.0 / (jnp.sum(p, axis=-1, keepdims=True) + es)
    return p * inv, es * inv


def attention_fwd(qkv, bias, sinks_b, cos2, sin2):
    s_len = qkv.shape[0]
    nb = s_len // AT_BLK

    def body(q_ref, kc_ref, kp_ref, vc_ref, vp_ref, bq_ref, bk_ref, bv_ref, sk_ref, cc_ref, sc_ref, cp_ref, sp_ref,
             o_ref):
        i = pl.program_id(1)
        q, kcat, vcat, mask, cosc, sinc = _att_load(i, q_ref, kc_ref, kp_ref, vc_ref, vp_ref, bq_ref, bk_ref,
                                                    bv_ref, cc_ref, sc_ref, cp_ref, sp_ref)
        sink, _ = _att_cols(sk_ref)
        probs, _ = _att_probs(_att_stack(q, cosc, sinc).astype(BF16), kcat, mask, sink)
        o = jnp.dot(probs.astype(BF16), vcat, preferred_element_type=F32).astype(BF16)
        for g in range(AT_GROUP):
            o_ref[:, g * AT_HD:(g + 1) * AT_HD] = o[g * AT_BLK:(g + 1) * AT_BLK, :]

    return pl.pallas_call(
        body, name="attention_fwd", grid=(AT_KVH, nb),
        in_specs=_att_specs(),
        out_specs=pl.BlockSpec((AT_BLK, AT_QW), lambda h, i: (i, h)),
        out_shape=jax.ShapeDtypeStruct((s_len, AT_KVH * AT_QW), BF16),
        compiler_params=_cparams(("parallel", "parallel")),
    )(qkv, qkv, qkv, qkv, qkv, bias, bias, bias, sinks_b, cos2, sin2, cos2, sin2)


def attention_bwd(qkv, bias, sinks_b, cos2, sin2, d_o):
    s_len = qkv.shape[0]
    nb = s_len // AT_BLK

    def body(q_ref, kc_ref, kp_ref, vc_ref, vp_ref, bq_ref, bk_ref, bv_ref, sk_ref, cc_ref, sc_ref, cp_ref, sp_ref,
             do_ref, dq_ref, dk_ref, dv_ref, dsk_ref):
        i = pl.program_id(1)
        _acc_init(i == 0, dsk_ref)
        q, kcat, vcat, mask, cosc, sinc = _att_load(i, q_ref, kc_ref, kp_ref, vc_ref, vp_ref, bq_ref, bk_ref,
                                                    bv_ref, cc_ref, sc_ref, cp_ref, sp_ref)
        sink, head = _att_cols(sk_ref)
        qs = _att_stack(q, cosc, sinc).astype(BF16)
        probs, psink = _att_probs(qs, kcat, mask, sink)
        pb = probs.astype(BF16)
        do_s = jnp.concatenate([do_ref[:, g * AT_HD:(g + 1) * AT_HD] for g in range(AT_GROUP)], axis=0)
        do_f = do_s.astype(F32)
        do_b = do_s.astype(BF16)
        o_s = jnp.dot(pb, vcat, preferred_element_type=F32)
        dsum = jnp.sum(do_f * o_s, axis=-1, keepdims=True)
        dp = lax.dot_general(do_b, vcat, (((1,), (1,)), ((), ())), preferred_element_type=F32)
        ds = (probs * (dp - dsum) * AT_SCALE).astype(BF16)
        dv_cat = lax.dot_general(pb, do_b, (((0,), (0,)), ((), ())), preferred_element_type=F32)
        dk_cat = lax.dot_general(ds, qs, (((0,), (0,)), ((), ())), preferred_element_type=F32)
        dq_s = jnp.dot(ds, kcat, preferred_element_type=F32)
        lane = lax.broadcasted_iota(jnp.int32, (1, AT_HD), 1)
        dsk = jnp.zeros((1, AT_HD), F32)
        sink_term = psink * dsum
        for g in range(AT_GROUP):
            rows = slice(g * AT_BLK, (g + 1) * AT_BLK)
            dq_ref[:, g * AT_HD:(g + 1) * AT_HD] = _rope_t(dq_s[rows, :], cosc, sinc).astype(BF16)
            dsk = dsk + jnp.where(lane == g, -jnp.sum(sink_term[rows, :], axis=0, keepdims=True), 0.0)
        dsk_ref[0] += dsk
        cur = pl.ds(pl.multiple_of(i * AT_BLK, AT_BLK), AT_BLK)
        dk_ref[cur, :] = _rope_t(dk_cat[AT_BLK:], cosc, sinc)
        dv_ref[cur, :] = dv_cat[AT_BLK:]

        @pl.when(i > 0)
        def _():
            prv = pl.ds(pl.multiple_of((i - 1) * AT_BLK, AT_BLK), AT_BLK)
            dk_ref[prv, :] += _rope_t(dk_cat[:AT_BLK], cp_ref[...], sp_ref[...])
            dv_ref[prv, :] += dv_cat[:AT_BLK]

    kv_out = pl.BlockSpec((s_len, AT_HD), lambda h, i: (0, h))
    return pl.pallas_call(
        body, name="attention_bwd", grid=(AT_KVH, nb),
        in_specs=_att_specs() + [pl.BlockSpec((AT_BLK, AT_QW), lambda h, i: (i, h))],
        out_specs=[pl.BlockSpec((AT_BLK, AT_QW), lambda h, i: (i, h)), kv_out, kv_out,
                   pl.BlockSpec((1, 1, AT_HD), lambda h, i: (h, 0, 0))],
        out_shape=[jax.ShapeDtypeStruct((s_len, AT_KVH * AT_QW), BF16),
                   jax.ShapeDtypeStruct((s_len, AT_KVH * AT_HD), F32),
                   jax.ShapeDtypeStruct((s_len, AT_KVH * AT_HD), F32),
                   jax.ShapeDtypeStruct((AT_KVH, 1, AT_HD), F32)],
        compiler_params=_cparams(("arbitrary", "arbitrary")),
    )(qkv, qkv, qkv, qkv, qkv, bias, bias, bias, sinks_b, cos2, sin2, cos2, sin2, d_o)


ATT_QKV = 6144
RW_SHIFT = 13024
RW_PAD = 13056
D_GATE = 480
ROPE_THETA = 10000.0


def perm_cols(a):
    lead = a.shape[:-1]
    return jnp.swapaxes(a.reshape(lead + (RW_H, RW_N)), -1, -2).reshape(lead + (RW_C,))


def rw_reorder(a, pad_value=0):
    r, k, v = (perm_cols(a[..., i * RW_C:(i + 1) * RW_C]) for i in range(3))
    wd = a[..., 3 * RW_C:3 * RW_C + 128]
    ad = a[..., 3 * RW_C + 128:3 * RW_C + 256]
    gd = a[..., 3 * RW_C + 256:]
    pad = jnp.full(a.shape[:-1] + (512 - D_GATE,), pad_value, a.dtype)
    return jnp.concatenate([r, k, v, gd, pad, wd, ad], axis=-1)


def rw_restore(a):
    r, k, v = (perm_cols(a[..., i * RW_C:(i + 1) * RW_C]) for i in range(3))
    gd = a[..., 3 * RW_C:3 * RW_C + D_GATE]
    wd = a[..., 3 * RW_C + 512:3 * RW_C + 640]
    ad = a[..., 3 * RW_C + 640:3 * RW_C + 768]
    return jnp.concatenate([r, k, v, wd, ad, gd], axis=-1)


def to_tiles(a):
    t = a.reshape(a.shape[0], RW_N, RW_H)
    return jnp.concatenate([t, t], axis=-1)


def from_tiles(t):
    return t[:, :, :RW_H].reshape(t.shape[0], RW_C)


def rope_tables(s_len):
    pos = jnp.arange(s_len, dtype=F32)
    inv_freq = ROPE_THETA ** (-jnp.arange(0, AT_HD, 2, dtype=F32) / AT_HD)
    ang = pos[:, None] * inv_freq[None, :]
    cos, sin = jnp.cos(ang), jnp.sin(ang)
    return jnp.concatenate([cos, cos], axis=1), jnp.concatenate([-sin, sin], axis=1)


def local_step(x, tgt, small, big, fwd_rider, got_early, bwd_rider, late_riders):
    s_len, d = x.shape
    w_in = big["w_in"]
    w_qkv = w_in[:, :ATT_QKV]
    w_rw = rw_reorder(w_in[:, ATT_QKV:ATT_QKV + RW_SHIFT])
    w_gate = w_in[:, ATT_QKV + RW_SHIFT:]
    w2 = perm_cols(big["w2"])
    a2 = perm_cols(big["a2"])
    g2 = jnp.pad(perm_cols(big["g2"]), ((0, 512 - D_GATE), (0, 0)))
    mu = rw_reorder(small["mu_shift"])
    w0, a0, k_k, k_a, ln_w, ln_b = (perm_cols(small[n]) for n in ("w0", "a0", "k_k", "k_a", "ln_x_w", "ln_x_b"))
    r_k = small["r_k"].reshape(RW_H, RW_N).T.reshape(1, RW_C)
    sinks_b = jnp.broadcast_to(small["att_sinks"].reshape(AT_KVH, AT_GROUP, 1), (AT_KVH, AT_GROUP, AT_HD))
    cos2, sin2 = rope_tables(s_len)
    bias = small["b_qkv"]

    h = rms_fwd(x, small["norm_mix_pre"])
    qkv = matmul(h, w_qkv, name="mm_qkv")
    prw = matmul(h, w_rw, name="mm_rw")
    gate = matmul(h, w_gate, name="mm_gate")
    o_att = attention_fwd(qkv, bias, sinks_b, cos2, sin2)
    pre_params = (mu, w0, a0, k_k, k_a, w2, a2, g2)
    r, dec, k, v, av, bv, g, tw, xa, sg = rwkv_pre_fwd(prw, *pre_params)
    v3 = to_tiles(v)
    y3, ck, *arrived = rwkv_scan_fwd(r, dec, k, av, bv, v3, rider=fwd_rider)
    big = {**big, **got_early(arrived)}
    w_rb = big["w_rwkv_branch"].reshape(RW_H, RW_N, d).swapaxes(0, 1).reshape(RW_C, d)
    y = from_tiles(y3)
    o_rw = rwkv_post_fwd(y, r, k, v, g, ln_w, ln_b, r_k)
    ab = matmul(o_att, big["w_att_branch"], name="mm_ab")
    rb = matmul(o_rw, w_rb, name="mm_rb")
    merged = merge_fwd(gate, ab, rb)
    m2 = matmul(merged, big["w_out"], name="mm_out")
    x1, h2 = resid_norm_fwd(x, m2, small["norm_mix_post"], small["norm_ffn_pre"])
    gg = matmul(h2, big["w_ffn_gate"], name="mm_fg")
    uu = matmul(h2, big["w_ffn_up"], name="mm_fu")
    act = swiglu_fwd(gg, uu)
    ff = matmul(act, big["w_ffn_down"], name="mm_fd")
    loss, dy, dff, d_nfp = loss_head(x1, ff, tgt, small["norm_ffn_post"])

    dact = matmul(dff, big["w_ffn_down"], tb=True, name="mm_dact")
    g_fd = matmul(act, dff, ta=True, out_dtype=BF16, name="mm_gfd")
    dgg, duu = swiglu_bwd(gg, uu, dact)
    g_fg = matmul(h2, dgg, ta=True, out_dtype=BF16, name="mm_gfg")
    g_fu = matmul(h2, duu, ta=True, out_dtype=BF16, name="mm_gfu")
    dh2a = matmul(dgg, big["w_ffn_gate"], tb=True, name="mm_dh2a")
    dh2b = matmul(duu, big["w_ffn_up"], tb=True, name="mm_dh2b")
    dx1, dm2, d_nfpre, d_nmpost = resid_norm_bwd(x1, dh2a, dh2b, small["norm_ffn_pre"], m2, small["norm_mix_post"], dy)
    dmerged = matmul(dm2, big["w_out"], tb=True, name="mm_dmerged")
    g_out = matmul(merged, dm2, ta=True, out_dtype=BF16, name="mm_gout")
    dgate, dab, drb = merge_bwd(gate, ab, rb, dmerged)
    do_att = matmul(dab, big["w_att_branch"], tb=True, out_dtype=BF16, name="mm_doatt")
    g_ab = matmul(o_att, dab, ta=True, out_dtype=BF16, name="mm_gab")
    do_rw = matmul(drb, w_rb, tb=True, name="mm_dorw")
    g_rb = matmul(o_rw, drb, ta=True, out_dtype=BF16, name="mm_grb")
    dq, dk_att, dv_att, dsk = attention_bwd(qkv, bias, sinks_b, cos2, sin2, do_att)
    dqkv = jnp.concatenate([dq, dk_att.astype(BF16), dv_att.astype(BF16)], axis=1)
    dy_s, dr_p, dk_p, dv_p, dg, d_lnw, d_lnb, d_rk = rwkv_post_bwd(y, r, k, v, g, ln_w, ln_b, r_k, do_rw)
    early = {"w_att_branch": g_ab, "w_rwkv_branch": g_rb.reshape(RW_N, RW_H, d).swapaxes(0, 1).reshape(RW_C, d),
             "w_out": g_out, "w_ffn_gate": g_fg, "w_ffn_up": g_fu, "w_ffn_down": g_fd}
    dr_s, ddec, dk_s, dav, dbv, dv3, *from_chips = rwkv_scan_bwd(r, dec, k, av, bv, v3, ck, to_tiles(dy_s),
                                                                  rider=bwd_rider(early))
    dprw, dzw, dza, dmu, dw0, da0, dkk, dka = rwkv_pre_bwd(
        prw, *pre_params, dr_p, ddec, dk_p, dv_p, dav, dbv, dg, dr_s, dk_s, from_tiles(dv3))
    g_w2 = matmul(tw, dzw, ta=True, out_dtype=BF16, name="mm_gw2")
    g_a2 = matmul(xa, dza, ta=True, out_dtype=BF16, name="mm_ga2")
    g_g2 = matmul(sg, dg.astype(BF16), ta=True, out_dtype=BF16, name="mm_gg2")
    g_qkv = matmul(h, dqkv, ta=True, out_dtype=BF16, name="mm_gqkv")
    g_rw = matmul(h, dprw, ta=True, out_dtype=BF16, name="mm_grw")
    g_gate = matmul(h, dgate, ta=True, out_dtype=BF16, name="mm_ggate")
    gbig = {
        "w_in": jnp.concatenate([g_qkv, rw_restore(g_rw), g_gate], axis=1),
        "w2": perm_cols(g_w2), "a2": perm_cols(g_a2), "g2": perm_cols(g_g2)[:D_GATE],
    }
    ride_a, ride_b, ride_c = late_riders(gbig)
    dh_a, *late_a = matmul(dqkv, w_qkv, tb=True, name="mm_dha", rider=ride_a)
    dh_b, *late_b = matmul(dprw, w_rw, tb=True, name="mm_dhb", rider=ride_b)
    dh_c, *late_c = matmul(dgate, w_gate, tb=True, name="mm_dhc", rider=ride_c)
    grad_x, d_nmpre = rms_bwd(x, small["norm_mix_pre"], dh_a, dh_b, dh_c, dx1)
    d_bias = colsum(dqkv)

    gsmall = {
        "norm_mix_pre": d_nmpre, "norm_mix_post": d_nmpost, "norm_ffn_pre": d_nfpre, "norm_ffn_post": d_nfp,
        "b_qkv": d_bias, "att_sinks": dsk[:, 0, :AT_GROUP].reshape(1, AT_KVH * AT_GROUP),
        "mu_shift": rw_restore(dmu), "w0": perm_cols(dw0), "a0": perm_cols(da0), "k_k": perm_cols(dkk),
        "k_a": perm_cols(dka), "r_k": d_rk.reshape(RW_N, RW_H).T.reshape(1, RW_C),
        "ln_x_w": perm_cols(d_lnw), "ln_x_b": perm_cols(d_lnb),
    }
    return loss, grad_x, gsmall, from_chips, (late_a, late_b, late_c)


def _place():
    x, y, c = lax.axis_index("x"), lax.axis_index("y"), lax.axis_index("c")
    chips = [(1 - x, y), (x, 1 - y), (1 - x, 1 - y)]
    return x, y, c, chips


def _remote(src, dst, send_sems, recv_sems, k, dev):
    return pltpu.make_async_remote_copy(src_ref=src, dst_ref=dst, send_sem=send_sems.at[k], recv_sem=recv_sems.at[k],
                                        device_id=dev, device_id_type=MESH)


def _gather_parts(n):
    def half(ref, which):
        hr = ref.shape[0] // 2
        return ref.at[pl.ds(which * hr, hr), :]

    def sends(ins, outs, send_sems, recv_sems):
        x, y, c, chips = _place()
        me = 2 * x + y
        return [_remote(half(ins[i], c), half(outs[i].at[me], c), send_sems, recv_sems, 6 * i + j, (*chip, c))
                for i in range(n) for j, chip in enumerate(chips)]

    def start(ins, outs, send_sems, recv_sems):
        for cp in sends(ins, outs, send_sems, recv_sems):
            cp.start()

    def finish(ins, outs, send_sems, recv_sems):
        x, y, c, chips = _place()
        sib = (x, y, 1 - c)
        passed = []
        for i in range(n):
            for j, chip in enumerate(chips):
                got = half(outs[i].at[2 * chip[0] + chip[1]], c)
                _remote(got, got, send_sems, recv_sems, 6 * i + j, sib).wait_recv()
                cp = _remote(got, got, send_sems, recv_sems, 6 * i + 3 + j, sib)
                cp.start()
                passed.append(cp)
        for i in range(n):
            for j, chip in enumerate(chips):
                got = half(outs[i].at[2 * chip[0] + chip[1]], 1 - c)
                _remote(got, got, send_sems, recv_sems, 6 * i + 3 + j, sib).wait_recv()
        for cp in sends(ins, outs, send_sems, recv_sems) + passed:
            cp.wait_send()

    return start, finish


def gather_rider(shards):
    n = len(shards)
    start, finish = _gather_parts(n)
    return {"ins": shards, "out_shapes": [jax.ShapeDtypeStruct((4,) + s.shape, s.dtype) for s in shards],
            "scratch": [pltpu.SemaphoreType.DMA((6 * n,)), pltpu.SemaphoreType.DMA((6 * n,))],
            "start": start, "finish": finish}


def gather_weights(shards):
    n = len(shards)
    start, finish = _gather_parts(n)

    def body(*refs):
        parts = (refs[:n], refs[n:2 * n], *refs[2 * n:])
        start(*parts)
        finish(*parts)

    return pl.pallas_call(
        body, name="gather_weights",
        in_specs=[ANY] * n, out_specs=[ANY] * n,
        out_shape=[jax.ShapeDtypeStruct((4,) + s.shape, s.dtype) for s in shards],
        scratch_shapes=[pltpu.SemaphoreType.DMA((6 * n,)), pltpu.SemaphoreType.DMA((6 * n,))],
    )(*shards)


def swap_with_sibling(blocks, name):
    n = len(blocks)

    def body(*refs):
        ins, outs = refs[:n], refs[n:2 * n]
        send_sems, recv_sems = refs[2 * n:]
        x, y, c, _ = _place()
        cps = [_remote(ins[i], outs[i], send_sems, recv_sems, i, (x, y, 1 - c)) for i in range(n)]
        for cp in cps:
            cp.start()
        for cp in cps:
            cp.wait()

    return pl.pallas_call(
        body, name=name, in_specs=[ANY] * n, out_specs=[ANY] * n,
        out_shape=[jax.ShapeDtypeStruct(b.shape, b.dtype) for b in blocks],
        scratch_shapes=[pltpu.SemaphoreType.DMA((n,)), pltpu.SemaphoreType.DMA((n,))],
    )(*blocks)


def _scatter_parts(n, rows=None):
    def piece(ref, i):
        return ref if rows is None or rows[i] is None else ref.at[pl.ds(rows[i][0], rows[i][1]), :]

    def copies(ins, outs, send_sems, recv_sems):
        x, y, c, chips = _place()
        return [_remote(piece(ins[i].at[2 * chip[0] + chip[1]], i), outs[i].at[j], send_sems, recv_sems, 3 * i + j,
                        (*chip, c))
                for i in range(n) for j, chip in enumerate(chips)]

    def start(*refs):
        for cp in copies(*refs):
            cp.start()

    def finish(*refs):
        for cp in copies(*refs):
            cp.wait()

    return start, finish


def scatter_rider(parts, rows=None):
    n = len(parts)
    start, finish = _scatter_parts(n, rows)
    nrows = [p.shape[1] if rows is None or rows[i] is None else rows[i][1] for i, p in enumerate(parts)]
    return {"ins": parts, "out_shapes": [jax.ShapeDtypeStruct((3, r, p.shape[2]), p.dtype) for p, r in zip(parts, nrows)],
            "scratch": [pltpu.SemaphoreType.DMA((3 * n,)), pltpu.SemaphoreType.DMA((3 * n,))],
            "start": start, "finish": finish}


def allreduce_small(v):
    rows = v.shape[0]

    def body(v_ref, o_ref, buf, send_sems, recv_sems):
        x, y, c, chips = _place()
        me, sib = (x, y, c), (x, y, 1 - c)

        def slot(px, py, pc):
            return buf.at[4 * px + 2 * py + pc]

        def copy(k, block, to, src=None):
            return _remote(slot(*block) if src is None else src, slot(*block), send_sems, recv_sems, k, to)

        buf[4 * x + 2 * y + c] = v_ref[...]
        first = [copy(0, me, sib, src=v_ref)]
        first += [copy(1 + j, me, (*chip, c), src=v_ref) for j, chip in enumerate(chips)]
        for cp in first:
            cp.start()
        passed = [copy(4 + j, (*chip, c), sib) for j, chip in enumerate(chips)]
        for j, chip in enumerate(chips):
            copy(1 + j, (*chip, c), me).wait_recv()
            passed[j].start()
        copy(0, sib, me).wait_recv()
        for j, chip in enumerate(chips):
            copy(4 + j, (*chip, 1 - c), me).wait_recv()
        for cp in first + passed:
            cp.wait_send()
        acc = buf[0]
        for k in range(1, 8):
            acc = acc + buf[k]
        o_ref[...] = acc

    vm = pl.BlockSpec(memory_space=pltpu.VMEM)
    return pl.pallas_call(
        body, name="allreduce_small", in_specs=[vm], out_specs=vm,
        out_shape=jax.ShapeDtypeStruct(v.shape, F32),
        scratch_shapes=[pltpu.VMEM((8, rows, LANES), F32), pltpu.SemaphoreType.DMA((7,)),
                        pltpu.SemaphoreType.DMA((7,))],
    )(v)


def _rows_tile(r):
    return _pick(r, (64, 32, 16, 8))


def swap_halves(blocks, name):
    n = len(blocks)

    def body(*refs):
        ins, outs = refs[:n], refs[n:2 * n]
        send_sems, recv_sems = refs[2 * n:]
        x, y, c, _ = _place()
        cps = []
        for i in range(n):
            hr = ins[i].shape[1] // 2
            cps.append(_remote(ins[i].at[:, pl.ds((1 - c) * hr, hr), :], outs[i], send_sems, recv_sems, i,
                               (x, y, 1 - c)))
        for cp in cps:
            cp.start()
        for cp in cps:
            cp.wait()

    return pl.pallas_call(
        body, name=name, in_specs=[ANY] * n, out_specs=[ANY] * n,
        out_shape=[jax.ShapeDtypeStruct((4, b.shape[1] // 2, b.shape[2]), b.dtype) for b in blocks],
        scratch_shapes=[pltpu.SemaphoreType.DMA((n,)), pltpu.SemaphoreType.DMA((n,))],
    )(*blocks)


def add_pairs(g4, b, core):
    _, r, c = b.shape
    tr = _rows_tile(r)
    nrt = r // tr

    def body(core_ref, a_ref, b_ref, o_ref):
        o_ref[...] = (a_ref[...].astype(F32) + b_ref[...].astype(F32)).astype(o_ref.dtype)

    spec = pl.BlockSpec((1, tr, c), lambda s, i, core_ref: (s, i, 0))
    return pl.pallas_call(
        body, name="add_pairs",
        grid_spec=pltpu.PrefetchScalarGridSpec(
            num_scalar_prefetch=1, grid=(4, nrt),
            in_specs=[pl.BlockSpec((1, tr, c), lambda s, i, core_ref: (s, core_ref[0] * nrt + i, 0)), spec],
            out_specs=spec),
        out_shape=jax.ShapeDtypeStruct(b.shape, b.dtype), compiler_params=_cparams(("parallel", "parallel")),
    )(core, g4, b)


def add_four(mine, others):
    r, c = mine.shape
    tr = _rows_tile(r)

    def body(m_ref, o_ref, out_ref):
        acc = m_ref[...].astype(F32)
        for j in range(3):
            acc = acc + o_ref[j].astype(F32)
        out_ref[...] = acc

    return pl.pallas_call(
        body, name="add_four", grid=(r // tr,),
        in_specs=[pl.BlockSpec((tr, c), lambda i: (i, 0)), pl.BlockSpec((3, tr, c), lambda i: (0, i, 0))],
        out_specs=pl.BlockSpec((tr, c), lambda i: (i, 0)),
        out_shape=jax.ShapeDtypeStruct((r, c), F32), compiler_params=_cparams(("parallel",)),
    )(mine, others)


def pair_sums(blocks, tag):
    core = lax.axis_index("c").astype(jnp.int32).reshape(1)
    from_sibling = swap_halves(blocks, "swap_halves_" + tag)
    return [add_pairs(g4, b, core) for g4, b in zip(blocks, from_sibling)]


def owner_sums(pair, from_chips):
    cx, cy, cc = lax.axis_index("x"), lax.axis_index("y"), lax.axis_index("c")
    me = 2 * cx + cy
    sums = [add_four(lax.dynamic_index_in_dim(p, me, 0, keepdims=False), t) for p, t in zip(pair, from_chips)]
    got = swap_with_sibling(sums, "swap_sums")
    return [jnp.concatenate([jnp.where(cc == 0, s, g), jnp.where(cc == 0, g, s)], axis=0) for s, g in zip(sums, got)]


ADAM_LR = 0.001
ADAM_B1 = 0.9
ADAM_B2 = 0.999
ADAM_EPS = 1e-08
ADAM_WD = 0.01
ADAM_STEP = 10


def adamw(w, g, m, v):
    r, c = w.shape
    tr = _rows_tile(r)
    spec = pl.BlockSpec((tr, c), lambda i: (i, 0))

    def body(w_ref, g_ref, m_ref, v_ref, d_ref, nm_ref, nv_ref):
        gv = g_ref[...]
        nm = ADAM_B1 * m_ref[...] + (1.0 - ADAM_B1) * gv
        nv = ADAM_B2 * v_ref[...] + (1.0 - ADAM_B2) * jnp.square(gv)
        m_hat = nm / (1.0 - ADAM_B1 ** ADAM_STEP)
        v_hat = nv / (1.0 - ADAM_B2 ** ADAM_STEP)
        d_ref[...] = -ADAM_LR * (m_hat / (jnp.sqrt(v_hat) + ADAM_EPS) + ADAM_WD * w_ref[...])
        nm_ref[...] = nm
        nv_ref[...] = nv

    out = jax.ShapeDtypeStruct((r, c), F32)
    return pl.pallas_call(
        body, name="adamw", grid=(r // tr,), in_specs=[spec] * 4, out_specs=[spec] * 3, out_shape=[out] * 3,
        compiler_params=_cparams(("parallel",)),
    )(w, g, m, v)


WEIGHTS = ["norm_mix_pre", "norm_mix_post", "norm_ffn_pre", "norm_ffn_post", "w_in", "b_qkv", "att_sinks", "mu_shift",
           "w0", "w2", "a0", "a2", "g2", "k_k", "k_a", "r_k", "ln_x_w", "ln_x_b", "w_att_branch", "w_rwkv_branch",
           "w_out", "w_ffn_gate", "w_ffn_up", "w_ffn_down"]
BIG = {"w_in": 1, "w2": 1, "a2": 1, "g2": 1, "w_att_branch": 0, "w_rwkv_branch": 0, "w_out": 0, "w_ffn_gate": 1,
       "w_ffn_up": 1, "w_ffn_down": 0}
SMALL = [n for n in WEIGHTS if n not in BIG]
LATE_CUTS = (13, 45)
LATE = ("w_in", "w2", "a2", "g2")
N_CHIPS = 4


def _whole(g4, axis):
    if axis == 0:
        return g4.reshape(g4.shape[0] * g4.shape[1], g4.shape[2])
    return jnp.swapaxes(g4, 0, 1).reshape(g4.shape[1], g4.shape[0] * g4.shape[2])


def _by_shard(w, axis):
    if axis == 0:
        return w.reshape(N_CHIPS, w.shape[0] // N_CHIPS, w.shape[1])
    return jnp.swapaxes(w.reshape(w.shape[0], N_CHIPS, w.shape[1] // N_CHIPS), 0, 1)


def _pack(parts):
    flat = jnp.concatenate([parts[n].reshape(-1) for n in SMALL])
    rows = -(-flat.shape[0] // (LANES * SUBLANES)) * SUBLANES
    return jnp.pad(flat, (0, rows * LANES - flat.shape[0])).reshape(rows, LANES)


def _unpack(packed, like):
    flat = packed.reshape(-1)
    out, off = {}, 0
    for n in SMALL:
        size = like[n].size
        out[n] = flat[off:off + size].reshape(like[n].shape)
        off += size
    return out


def kernel(x, norm_mix_pre, norm_mix_post, norm_ffn_pre, norm_ffn_post, w_in, b_qkv, att_sinks, mu_shift, w0, w2, a0, a2, g2, k_k, k_a, r_k, ln_x_w, ln_x_b, w_att_branch, w_rwkv_branch, w_out, w_ffn_gate, w_ffn_up, w_ffn_down, loss_target, m_norm_mix_pre, m_norm_mix_post, m_norm_ffn_pre, m_norm_ffn_post, m_w_in, m_b_qkv, m_att_sinks, m_mu_shift, m_w0, m_w2, m_a0, m_a2, m_g2, m_k_k, m_k_a, m_r_k, m_ln_x_w, m_ln_x_b, m_w_att_branch, m_w_rwkv_branch, m_w_out, m_w_ffn_gate, m_w_ffn_up, m_w_ffn_down, v_norm_mix_pre, v_norm_mix_post, v_norm_ffn_pre, v_norm_ffn_post, v_w_in, v_b_qkv, v_att_sinks, v_mu_shift, v_w0, v_w2, v_a0, v_a2, v_g2, v_k_k, v_k_a, v_r_k, v_ln_x_w, v_ln_x_b, v_w_att_branch, v_w_rwkv_branch, v_w_out, v_w_ffn_gate, v_w_ffn_up, v_w_ffn_down):
    given = dict(locals())
    wts = {n: given[n] for n in WEIGHTS}
    mom1 = {n: given["m_" + n] for n in WEIGHTS}
    mom2 = {n: given["v_" + n] for n in WEIGHTS}
    early = [n for n in BIG if n not in LATE]
    me = 2 * lax.axis_index("x") + lax.axis_index("y")
    own = {n: wts[n][0].astype(BF16) for n in BIG}

    def placed(names, gathered):
        return {n: _whole(lax.dynamic_update_index_in_dim(g4, own[n], me, 0), BIG[n]) for n, g4 in zip(names, gathered)}

    small = {n: wts[n].reshape(1, -1) for n in SMALL}
    pairs = {}

    def bwd_rider(grads_early):
        pairs["early"] = pair_sums([_by_shard(grads_early[n], BIG[n]) for n in early], "early")
        return scatter_rider(pairs["early"])

    def late_riders(grads_late):
        pairs["late"] = pair_sums([_by_shard(grads_late[n], BIG[n]) for n in LATE], "late")
        half = pairs["late"][0].shape[1]
        cuts = [0] + [half * f // 64 // 16 * 16 for f in LATE_CUTS] + [half]
        spans = [(cuts[i], cuts[i + 1] - cuts[i]) for i in range(3)]
        return (scatter_rider(pairs["late"], [spans[0], None, None, None]),
                scatter_rider(pairs["late"][:1], [spans[1]]), scatter_rider(pairs["late"][:1], [spans[2]]))

    loss, grad_x, gsmall, chips_early, (late_a, late_b, late_c) = local_step(
        x[0], loss_target[0], small, placed(LATE, gather_weights([own[n] for n in LATE])),
        gather_rider([own[n] for n in early]), lambda arrived: placed(early, arrived), bwd_rider, late_riders)

    chips_late = [jnp.concatenate([late_a[0], late_b[0], late_c[0]], axis=1)] + list(late_a[1:])
    wholes = owner_sums(pairs["early"] + pairs["late"], list(chips_early) + chips_late)
    grads = dict(zip(early + list(LATE), wholes))
    names = list(BIG)

    gsum = _unpack(allreduce_small(_pack(gsmall)), small)

    outs_g, outs_d, outs_m, outs_v = {}, {}, {}, {}
    for n in names:
        d, nm, nv = adamw(wts[n][0], grads[n], mom1[n][0], mom2[n][0])
        outs_g[n], outs_d[n], outs_m[n], outs_v[n] = (t[None] for t in (grads[n], d, nm, nv))
    pk = lambda src: _pack({n: src[n] for n in SMALL})
    d, nm, nv = adamw(pk(wts), _pack(gsum), pk(mom1), pk(mom2))
    du, mu, vu = _unpack(d, small), _unpack(nm, small), _unpack(nv, small)
    for n in SMALL:
        outs_g[n], outs_d[n], outs_m[n], outs_v[n] = (t[n].reshape(wts[n].shape) for t in (gsum, du, mu, vu))

    total = lax.psum(loss[0, 0], ("x", "y", "c"))
    return (total, grad_x[None], *[outs_g[n] for n in WEIGHTS], *[outs_d[n] for n in WEIGHTS],
            *[outs_m[n] for n in WEIGHTS], *[outs_v[n] for n in WEIGHTS])
```

```python
import jax
import jax.numpy as jnp
from jax import lax
from jax.experimental import pallas as pl
from jax.experimental.pallas import tpu as pltpu

F32 = jnp.float32
BF16 = jnp.bfloat16

LANES = 128
SUBLANES = 8
VMEM_LIMIT = 56 * 1024 * 1024

RW_H = 64
RW_N = 64
RW_C = RW_H * RW_N
RW_NB = RW_C // LANES
SCAN_CHUNK = 8


MESH = pl.DeviceIdType.MESH
ANY = pl.BlockSpec(memory_space=pl.ANY)


def _cparams(sem=None):
    return pltpu.CompilerParams(dimension_semantics=sem, vmem_limit_bytes=VMEM_LIMIT)


def _fold(x):
    return x + pltpu.roll(x, 64, axis=x.ndim - 1)


def _scan_step_fwd(t, src_ref, dst_ref, r_ref, w_ref, k_ref, a_ref, b_ref, v_ref):
    vt = v_ref[t]
    acc = jnp.zeros((RW_N, LANES), F32)
    for j in range(RW_NB):
        ls = slice(j * LANES, (j + 1) * LANES)
        acc = acc + src_ref[j] * a_ref[t:t + 1, ls]
    sa = _fold(acc)
    yacc = jnp.zeros((RW_N, LANES), F32)
    for j in range(RW_NB):
        ls = slice(j * LANES, (j + 1) * LANES)
        s_new = src_ref[j] * w_ref[t:t + 1, ls] + sa * b_ref[t:t + 1, ls] + vt * k_ref[t:t + 1, ls]
        dst_ref[j] = s_new
        yacc = yacc + s_new * r_ref[t:t + 1, ls]
    return _fold(yacc), sa


def _rider_parts(rider):
    if rider is None:
        return [], [], []
    return list(rider["ins"]), list(rider["out_shapes"]), list(rider["scratch"])


def rwkv_scan_fwd(r, w, k, a, b, v3, rider=None):
    s_len = r.shape[0]
    nchunk = s_len // SCAN_CHUNK
    x_in, x_out, x_scr = _rider_parts(rider)
    ni, no = len(x_in), len(x_out)

    def body(*refs):
        r_ref, w_ref, k_ref, a_ref, b_ref, v_ref = refs[:6]
        y_ref, hist_ref, sa_ref = refs[6 + ni:9 + ni]
        st_ref = refs[9 + ni + no]
        ride = (refs[6:6 + ni], refs[9 + ni:9 + ni + no], *refs[10 + ni + no:])

        @pl.when(pl.program_id(0) == 0)
        def _():
            st_ref[...] = jnp.zeros_like(st_ref)
            if rider is not None:
                rider["start"](*ride)

        for t in range(SCAN_CHUNK):
            y, sa = _scan_step_fwd(t, st_ref if t == 0 else hist_ref.at[t - 1], hist_ref.at[t],
                                   r_ref, w_ref, k_ref, a_ref, b_ref, v_ref)
            y_ref[t] = y
            sa_ref[t] = sa
        st_ref[...] = hist_ref[SCAN_CHUNK - 1]

        if rider is not None:
            @pl.when(pl.program_id(0) == nchunk - 1)
            def _():
                rider["finish"](*ride)

    row = pl.BlockSpec((SCAN_CHUNK, RW_C), lambda i: (i, 0))
    til = pl.BlockSpec((SCAN_CHUNK, RW_N, LANES), lambda i: (i, 0, 0))
    return pl.pallas_call(
        body,
        name="rwkv_scan_fwd",
        grid=(nchunk,),
        in_specs=[row, row, row, row, row, til] + [ANY] * ni,
        out_specs=[til, pl.BlockSpec((SCAN_CHUNK, RW_NB, RW_N, LANES), lambda i: (i, 0, 0, 0)), til] + [ANY] * no,
        out_shape=[
            jax.ShapeDtypeStruct((s_len, RW_N, LANES), F32),
            jax.ShapeDtypeStruct((s_len, RW_NB, RW_N, LANES), F32),
            jax.ShapeDtypeStruct((s_len, RW_N, LANES), F32),
        ] + x_out,
        scratch_shapes=[pltpu.VMEM((RW_NB, RW_N, LANES), F32)] + x_scr,
        compiler_params=_cparams(("arbitrary",)),
    )(r, w, k, a, b, v3, *x_in)


def rwkv_scan_bwd(r, w, k, a, b, v3, hist, sa3, dy3, rider=None):
    s_len = r.shape[0]
    nchunk = s_len // SCAN_CHUNK
    x_in, x_out, x_scr = _rider_parts(rider)
    ni, no = len(x_in), len(x_out)

    def body(*refs):
        r_ref, w_ref, k_ref, a_ref, b_ref, v_ref, hist_ref, prev_ref, sa_ref, dy_ref = refs[:10]
        dr_ref, dw_ref, dk_ref, da_ref, db_ref, dv_ref = refs[10 + ni:16 + ni]
        ds_ref = refs[16 + ni + no]
        ride = (refs[10:10 + ni], refs[16 + ni:16 + ni + no], *refs[17 + ni + no:])

        @pl.when(pl.program_id(0) == 0)
        def _():
            ds_ref[...] = jnp.zeros_like(ds_ref)
            if rider is not None:
                rider["start"](*ride)

        not_first = (pl.program_id(0) < nchunk - 1).astype(F32)

        def before(t, j):
            return prev_ref[0, j] * not_first if t == 0 else hist_ref[t - 1, j]

        for t in reversed(range(SCAN_CHUNK)):
            vt = v_ref[t]
            dyt = dy_ref[t]
            sat = sa_ref[t]
            dv_acc = jnp.zeros((RW_N, LANES), F32)
            dsa_acc = jnp.zeros((RW_N, LANES), F32)
            for j in range(RW_NB):
                ls = slice(j * LANES, (j + 1) * LANES)
                row = (slice(t, t + 1), ls)
                ds_j = ds_ref[j] + dyt * r_ref[row]
                ds_ref[j] = ds_j
                dr_ref[row] = jnp.sum(hist_ref[t, j] * dyt, axis=0, keepdims=True)
                dv_acc = dv_acc + ds_j * k_ref[row]
                dk_ref[row] = jnp.sum(ds_j * vt, axis=0, keepdims=True)
                dsa_acc = dsa_acc + ds_j * b_ref[row]
                db_ref[row] = jnp.sum(ds_j * sat, axis=0, keepdims=True)
                dw_ref[row] = jnp.sum(ds_j * before(t, j), axis=0, keepdims=True)
            dv_ref[t] = _fold(dv_acc)
            dsa = _fold(dsa_acc)
            for j in range(RW_NB):
                ls = slice(j * LANES, (j + 1) * LANES)
                row = (slice(t, t + 1), ls)
                da_ref[row] = jnp.sum(before(t, j) * dsa, axis=0, keepdims=True)
                ds_ref[j] = ds_ref[j] * w_ref[row] + dsa * a_ref[row]

        if rider is not None:
            @pl.when(pl.program_id(0) == nchunk - 1)
            def _():
                rider["finish"](*ride)

    rev = lambda i: (nchunk - 1 - i, 0)
    rev3 = lambda i: (nchunk - 1 - i, 0, 0)
    row = pl.BlockSpec((SCAN_CHUNK, RW_C), rev)
    til = pl.BlockSpec((SCAN_CHUNK, RW_N, LANES), rev3)
    rows = jax.ShapeDtypeStruct((s_len, RW_C), F32)
    return pl.pallas_call(
        body,
        name="rwkv_scan_bwd",
        grid=(nchunk,),
        in_specs=[row, row, row, row, row, til,
                  pl.BlockSpec((SCAN_CHUNK, RW_NB, RW_N, LANES), lambda i: (nchunk - 1 - i, 0, 0, 0)),
                  pl.BlockSpec((1, RW_NB, RW_N, LANES),
                               lambda i: (jnp.maximum((nchunk - 1 - i) * SCAN_CHUNK - 1, 0), 0, 0, 0)),
                  til, til] + [ANY] * ni,
        out_specs=[row, row, row, row, row, til] + [ANY] * no,
        out_shape=[rows, rows, rows, rows, rows, jax.ShapeDtypeStruct((s_len, RW_N, LANES), F32)] + x_out,
        scratch_shapes=[pltpu.VMEM((RW_NB, RW_N, LANES), F32)] + x_scr,
        compiler_params=_cparams(("arbitrary",)),
    )(r, w, k, a, b, v3, hist, hist, sa3, dy3, *x_in)


def _pick(n, cands):
    for c in cands:
        if n % c == 0:
            return c
    return n


MM_VMEM_BUDGET = 40 * 1024 * 1024
MM_FLOPS = 8.5e14
MM_HBM = 2.2e12
MM_STEP = 0.4e-6


def _mm_plan(m, n, k, out_bytes):
    divs = lambda d: [t for t in range(LANES, d + 1, LANES) if d % t == 0] or [d]
    best = None
    for tm in divs(m):
        for tn in divs(n):
            for tk in divs(k):
                nk = k // tk
                vmem = 4 * (tm * tk + tk * tn) + (4 * tm * tn if nk > 1 else 0) + 2 * tm * tn * out_bytes
                if vmem > MM_VMEM_BUDGET:
                    continue
                steps = (m // tm) * (n // tn) * nk
                for n_outer in (False, True):
                    if nk > 1:
                        traffic = steps * (tm * tk + tk * tn) * 2
                    elif n_outer:
                        traffic = (n // tn) * (k * tn + m * k) * 2
                    else:
                        traffic = (m // tm) * (tm * k + k * n) * 2
                    cost = max(2.0 * m * n * k / MM_FLOPS, (traffic + m * n * out_bytes) / MM_HBM) + steps * MM_STEP
                    if best is None or cost < best[0]:
                        best = (cost, tm, tn, tk, n_outer)
    return best[1:]


def matmul(a, b, *, ta=False, tb=False, out_dtype=F32, name="matmul", plan=None, rider=None):
    m, kdim = (a.shape[1], a.shape[0]) if ta else a.shape
    n = b.shape[0] if tb else b.shape[1]
    assert (b.shape[1] if tb else b.shape[0]) == kdim
    tm, tn, tk, n_outer = plan or _mm_plan(m, n, kdim, jnp.dtype(out_dtype).itemsize)
    nk = kdim // tk
    dims = (((0 if ta else 1,), (1 if tb else 0,)), ((), ()))
    grid = (n // tn, m // tm, nk) if n_outer else (m // tm, n // tn, nk)
    x_in, x_out, x_scr = _rider_parts(rider)
    ni, no = len(x_in), len(x_out)
    n_acc = 1 if nk > 1 else 0

    def body(*refs):
        a_ref, b_ref = refs[:2]
        o_ref = refs[2 + ni]
        ride = (refs[2:2 + ni], refs[3 + ni:3 + ni + no], *refs[3 + ni + no + n_acc:])
        ids = [pl.program_id(ax) for ax in range(3)]
        if rider is not None:
            @pl.when((ids[0] == 0) & (ids[1] == 0) & (ids[2] == 0))
            def _():
                rider["start"](*ride)

        prod = lax.dot_general(a_ref[...].astype(BF16), b_ref[...].astype(BF16), dims, preferred_element_type=F32)
        if nk == 1:
            o_ref[...] = prod.astype(o_ref.dtype)
        else:
            acc_ref = refs[3 + ni + no]
            kk = ids[2]

            @pl.when(kk == 0)
            def _():
                acc_ref[...] = prod

            @pl.when(kk > 0)
            def _():
                acc_ref[...] += prod

            @pl.when(kk == nk - 1)
            def _():
                o_ref[...] = acc_ref[...].astype(o_ref.dtype)

        if rider is not None:
            @pl.when((ids[0] == grid[0] - 1) & (ids[1] == grid[1] - 1) & (ids[2] == nk - 1))
            def _():
                rider["finish"](*ride)

    ij = (lambda p, q: (q, p)) if n_outer else (lambda p, q: (p, q))
    a_map = (lambda p, q, k: (k, ij(p, q)[0])) if ta else (lambda p, q, k: (ij(p, q)[0], k))
    b_map = (lambda p, q, k: (ij(p, q)[1], k)) if tb else (lambda p, q, k: (k, ij(p, q)[1]))
    out = pl.pallas_call(
        body,
        name=name,
        grid=grid,
        in_specs=[pl.BlockSpec((tk, tm) if ta else (tm, tk), a_map),
                  pl.BlockSpec((tn, tk) if tb else (tk, tn), b_map)] + [ANY] * ni,
        out_specs=[pl.BlockSpec((tm, tn), lambda p, q, k: ij(p, q))] + [ANY] * no,
        out_shape=[jax.ShapeDtypeStruct((m, n), out_dtype)] + x_out,
        scratch_shapes=([pltpu.VMEM((tm, tn), F32)] if nk > 1 else []) + x_scr,
        compiler_params=_cparams(("arbitrary",) * 3 if rider is not None else ("parallel", "parallel", "arbitrary")),
    )(a, b, *x_in)
    return out if rider is not None else out[0]


@jax.custom_vjp
def hsum(x):
    acc = x[:, 0:LANES]
    for j in range(1, RW_NB):
        acc = acc + x[:, j * LANES:(j + 1) * LANES]
    return _fold(acc)


def _hsum_fwd(x):
    return hsum(x), None


def _hsum_bwd(_, ct):
    return (jnp.concatenate([_fold(ct)] * RW_NB, axis=1),)


hsum.defvjp(_hsum_fwd, _hsum_bwd)


@jax.custom_vjp
def hbcast(s):
    return jnp.concatenate([s] * RW_NB, axis=1)


def _hbcast_fwd(s):
    return hbcast(s), None


def _hbcast_bwd(_, ct):
    acc = ct[:, 0:LANES]
    for j in range(1, RW_NB):
        acc = acc + ct[:, j * LANES:(j + 1) * LANES]
    return (acc,)


hbcast.defvjp(_hbcast_fwd, _hbcast_bwd)


@jax.custom_vjp
def bdot(x, w):
    return jnp.dot(x.astype(BF16), w, preferred_element_type=F32)


def _bdot_fwd(x, w):
    return bdot(x, w), w


def _bdot_bwd(w, ct):
    dx = lax.dot_general(ct.astype(BF16), w, (((1,), (1,)), ((), ())), preferred_element_type=F32)
    return dx, jnp.zeros_like(w)


bdot.defvjp(_bdot_fwd, _bdot_bwd)

RMS_EPS = 1e-6
GN_EPS = 64e-5


def f_rms(x, g):
    return x * lax.rsqrt(jnp.mean(x * x, axis=-1, keepdims=True) + RMS_EPS) * g


def _softplus(z):
    return jnp.maximum(z, 0.0) + jnp.log1p(jnp.exp(-jnp.abs(z)))


def f_pre(xk, xg, xw, xa, ew, ea, w0, a0, k_k, k_a, w2, a2, g2):
    tw = jnp.tanh(xw)
    sg = jax.nn.sigmoid(xg)
    wlog = -_softplus(-(w0 + bdot(tw, w2) + ew)) - 0.5
    decay = jnp.exp(-jnp.exp(wlog))
    a = jax.nn.sigmoid(a0 + bdot(xa, a2) + ea)
    g = bdot(sg, g2)
    kk0 = xk * k_k
    nrm = jnp.sqrt(hbcast(hsum(kk0 * kk0)))
    kk = kk0 / jnp.maximum(nrm, 1e-12)
    k = xk * (1.0 + (a - 1.0) * k_a)
    return decay, k, -kk, kk * a, g, tw, sg


def f_post(y, r, k, v, g, ln_w, ln_b, r_k):
    mu = hbcast(hsum(y)) * (1.0 / RW_N)
    yc = y - mu
    var = hbcast(hsum(yc * yc)) * (1.0 / RW_N)
    yn = yc * lax.rsqrt(var + GN_EPS) * ln_w + ln_b
    bonus = hbcast(hsum(r * k * r_k)) * v
    return (yn + bonus) * g


def f_merge(ga, gr, ab, rb):
    return jax.nn.sigmoid(ga) * ab + jax.nn.sigmoid(gr) * rb


def f_swiglu(gg, uu):
    return gg * jax.nn.sigmoid(gg) * uu


def _row(tt, width, cb=0, rev_n=None):
    if rev_n is None:
        return pl.BlockSpec((tt, width), lambda i: (i, cb))
    return pl.BlockSpec((tt, width), lambda i: (rev_n - 1 - i, cb))


def _full(arr):
    nd = arr.ndim
    return pl.BlockSpec(arr.shape, lambda i: (0,) * nd)


def _acc_init(i_first, *refs):
    @pl.when(i_first)
    def _():
        for r in refs:
            r[...] = jnp.zeros_like(r)


def rms_fwd(x, g, *, tt=128):
    s_len, d = x.shape

    def body(x_ref, g_ref, o_ref):
        o_ref[...] = f_rms(x_ref[...], g_ref[...]).astype(BF16)

    return pl.pallas_call(
        body, name="rms_fwd", grid=(s_len // tt,),
        in_specs=[_row(tt, d), _full(g)], out_specs=_row(tt, d),
        out_shape=jax.ShapeDtypeStruct((s_len, d), BF16),
        compiler_params=_cparams(("parallel",)),
    )(x, g)


def rwkv_pre_fwd(proj, mu, w0, a0, k_k, k_a, w2, a2, g2, *, tt=64):
    s_len, c = proj.shape
    nt = s_len // tt
    sub = tt // SUBLANES

    def body(p_ref, pb_ref, mu_ref, w0_ref, a0_ref, kk_ref, ka_ref, w2_ref, a2_ref, g2_ref,
             r_ref, dec_ref, k_ref, v_ref, av_ref, bv_ref, g_ref, tw_ref, xa_ref, sg_ref):
        i = pl.program_id(0)
        cur = p_ref[...]
        edge = jnp.where(i > 0, pb_ref[SUBLANES - 1:SUBLANES, :], 0.0)
        rows = lax.broadcasted_iota(jnp.int32, cur.shape, 0)
        prev = jnp.where(rows == 0, edge, pltpu.roll(cur, 1, axis=0))
        xs = cur + (prev - cur) * mu_ref[...]
        xr, xk, xv = xs[:, 0:RW_C], xs[:, RW_C:2 * RW_C], xs[:, 2 * RW_C:3 * RW_C]
        xg = xs[:, 3 * RW_C:3 * RW_C + 512]
        xw = xs[:, 3 * RW_C + 512:3 * RW_C + 640]
        xa = xs[:, 3 * RW_C + 640:3 * RW_C + 768]
        zero = jnp.zeros((tt, RW_C), F32)
        dec, k, av, bv, g, tw, sg = f_pre(xk, xg, xw, xa, zero, zero, w0_ref[...], a0_ref[...], kk_ref[...],
                                          ka_ref[...], w2_ref[...], a2_ref[...], g2_ref[...])
        r_ref[...] = xr
        dec_ref[...] = dec
        k_ref[...] = k
        v_ref[...] = xv
        av_ref[...] = av
        bv_ref[...] = bv
        g_ref[...] = g
        tw_ref[...] = tw.astype(BF16)
        xa_ref[...] = xa.astype(BF16)
        sg_ref[...] = sg.astype(BF16)

    rows_f = jax.ShapeDtypeStruct((s_len, RW_C), F32)
    prev_spec = pl.BlockSpec((SUBLANES, c), lambda i: (jnp.maximum(i * sub - 1, 0), 0))
    params = [mu, w0, a0, k_k, k_a, w2, a2, g2]
    return pl.pallas_call(
        body, name="rwkv_pre_fwd", grid=(nt,),
        in_specs=[_row(tt, c), prev_spec] + [_full(p) for p in params],
        out_specs=[_row(tt, RW_C)] * 7 + [_row(tt, 128), _row(tt, 128), _row(tt, 512)],
        out_shape=[rows_f] * 7 + [jax.ShapeDtypeStruct((s_len, 128), BF16), jax.ShapeDtypeStruct((s_len, 128), BF16),
                                  jax.ShapeDtypeStruct((s_len, 512), BF16)],
        compiler_params=_cparams(("parallel",)),
    )(proj, proj, *params)


def rwkv_pre_bwd(proj, mu, w0, a0, k_k, k_a, w2, a2, g2, d_r, d_dec, d_k, d_v, d_av, d_bv, d_g, d_r2, d_k2, d_v2,
                 *, tt=32):
    s_len, c = proj.shape
    nt = s_len // tt
    sub = tt // SUBLANES

    def body(p_ref, pb_ref, mu_ref, w0_ref, a0_ref, kk_ref, ka_ref, w2_ref, a2_ref, g2_ref,
             dr_ref, ddec_ref, dk_ref, dv_ref, dav_ref, dbv_ref, dg_ref, dr2_ref, dk2_ref, dv2_ref,
             dp_ref, dzw_ref, dza_ref, dmu_ref, dw0_ref, da0_ref, dkk_ref, dka_ref, carry_ref):
        step = pl.program_id(0)
        i = nt - 1 - step
        _acc_init(step == 0, dmu_ref, dw0_ref, da0_ref, dkk_ref, dka_ref, carry_ref)
        cur = p_ref[...]
        edge = jnp.where(i > 0, pb_ref[SUBLANES - 1:SUBLANES, :], 0.0)
        rows = lax.broadcasted_iota(jnp.int32, cur.shape, 0)
        prev = jnp.where(rows == 0, edge, pltpu.roll(cur, 1, axis=0))
        mu_v = mu_ref[...]
        xs = cur + (prev - cur) * mu_v
        xk = xs[:, RW_C:2 * RW_C]
        xg = xs[:, 3 * RW_C:3 * RW_C + 512]
        xw = xs[:, 3 * RW_C + 512:3 * RW_C + 640]
        xa = xs[:, 3 * RW_C + 640:3 * RW_C + 768]
        zero = jnp.zeros((tt, RW_C), F32)
        w2_v, a2_v, g2_v = w2_ref[...], a2_ref[...], g2_ref[...]

        def core(xk, xg, xw, xa, ew, ea, w0, a0, k_k, k_a):
            return f_pre(xk, xg, xw, xa, ew, ea, w0, a0, k_k, k_a, w2_v, a2_v, g2_v)[:5]

        _, vjp = jax.vjp(core, xk, xg, xw, xa, zero, zero, w0_ref[...], a0_ref[...], kk_ref[...], ka_ref[...])
        dxk, dxg, dxw, dxa, dzw, dza, dw0, da0, dkk, dka = vjp(
            (ddec_ref[...], dk_ref[...] + dk2_ref[...], dav_ref[...], dbv_ref[...], dg_ref[...]))
        dzw_ref[...] = dzw.astype(BF16)
        dza_ref[...] = dza.astype(BF16)
        dw0_ref[...] += dw0
        da0_ref[...] += da0
        dkk_ref[...] += dkk
        dka_ref[...] += dka
        dxs = jnp.concatenate([dr_ref[...] + dr2_ref[...], dxk, dv_ref[...] + dv2_ref[...], dxg, dxw, dxa], axis=1)
        dmu_ref[...] += jnp.sum(dxs * (prev - cur), axis=0, keepdims=True)
        to_prev = dxs * mu_v
        nxt = jnp.where(rows == tt - 1, carry_ref[...], pltpu.roll(to_prev, tt - 1, axis=0))
        carry_ref[...] = to_prev[0:1, :]
        dp_ref[...] = (dxs * (1.0 - mu_v) + nxt).astype(BF16)

    prev_spec = pl.BlockSpec((SUBLANES, c), lambda s: (jnp.maximum((nt - 1 - s) * sub - 1, 0), 0))
    params = [mu, w0, a0, k_k, k_a, w2, a2, g2]
    cts = [d_r, d_dec, d_k, d_v, d_av, d_bv, d_g, d_r2, d_k2, d_v2]
    vec = jax.ShapeDtypeStruct((1, RW_C), F32)
    acc = pl.BlockSpec((1, RW_C), lambda s: (0, 0))
    return pl.pallas_call(
        body, name="rwkv_pre_bwd", grid=(nt,),
        in_specs=[_row(tt, c, rev_n=nt), prev_spec] + [_full(p) for p in params] + [_row(tt, RW_C, rev_n=nt)] * 10,
        out_specs=[_row(tt, c, rev_n=nt), _row(tt, RW_C, rev_n=nt), _row(tt, RW_C, rev_n=nt),
                   pl.BlockSpec((1, c), lambda s: (0, 0)), acc, acc, acc, acc],
        out_shape=[jax.ShapeDtypeStruct((s_len, c), BF16), jax.ShapeDtypeStruct((s_len, RW_C), BF16),
                   jax.ShapeDtypeStruct((s_len, RW_C), BF16), jax.ShapeDtypeStruct((1, c), F32), vec, vec, vec, vec],
        scratch_shapes=[pltpu.VMEM((1, c), F32)],
        compiler_params=_cparams(("arbitrary",)),
    )(proj, proj, *params, *cts)


def rwkv_post_fwd(y, r, k, v, g, ln_w, ln_b, r_k, *, tt=64):
    s_len = y.shape[0]

    def body(y_ref, r_ref, k_ref, v_ref, g_ref, lw_ref, lb_ref, rk_ref, o_ref):
        o_ref[...] = f_post(y_ref[...], r_ref[...], k_ref[...], v_ref[...], g_ref[...],
                            lw_ref[...], lb_ref[...], rk_ref[...]).astype(BF16)

    return pl.pallas_call(
        body, name="rwkv_post_fwd", grid=(s_len // tt,),
        in_specs=[_row(tt, RW_C)] * 5 + [_full(ln_w), _full(ln_b), _full(r_k)],
        out_specs=_row(tt, RW_C), out_shape=jax.ShapeDtypeStruct((s_len, RW_C), BF16),
        compiler_params=_cparams(("parallel",)),
    )(y, r, k, v, g, ln_w, ln_b, r_k)


def rwkv_post_bwd(y, r, k, v, g, ln_w, ln_b, r_k, d_o, *, tt=32):
    s_len = y.shape[0]

    def body(y_ref, r_ref, k_ref, v_ref, g_ref, lw_ref, lb_ref, rk_ref, do_ref,
             dy_ref, dr_ref, dk_ref, dv_ref, dg_ref, dlw_ref, dlb_ref, drk_ref):
        _acc_init(pl.program_id(0) == 0, dlw_ref, dlb_ref, drk_ref)
        _, vjp = jax.vjp(f_post, y_ref[...], r_ref[...], k_ref[...], v_ref[...], g_ref[...],
                         lw_ref[...], lb_ref[...], rk_ref[...])
        dy, dr, dk, dv, dg, dlw, dlb, drk = vjp(do_ref[...].astype(F32))
        dy_ref[...] = dy
        dr_ref[...] = dr
        dk_ref[...] = dk
        dv_ref[...] = dv
        dg_ref[...] = dg
        dlw_ref[...] += dlw
        dlb_ref[...] += dlb
        drk_ref[...] += drk

    rows_f = jax.ShapeDtypeStruct((s_len, RW_C), F32)
    vec = jax.ShapeDtypeStruct((1, RW_C), F32)
    acc = pl.BlockSpec((1, RW_C), lambda s: (0, 0))
    return pl.pallas_call(
        body, name="rwkv_post_bwd", grid=(s_len // tt,),
        in_specs=[_row(tt, RW_C)] * 5 + [_full(ln_w), _full(ln_b), _full(r_k), _row(tt, RW_C)],
        out_specs=[_row(tt, RW_C)] * 5 + [acc] * 3, out_shape=[rows_f] * 5 + [vec] * 3,
        compiler_params=_cparams(("arbitrary",)),
    )(y, r, k, v, g, ln_w, ln_b, r_k, d_o)


def merge_fwd(gate, ab, rb, *, tt=128):
    s_len, d = ab.shape

    def body(ga_ref, gr_ref, a_ref, r_ref, o_ref):
        o_ref[...] = f_merge(ga_ref[...], gr_ref[...], a_ref[...], r_ref[...]).astype(BF16)

    return pl.pallas_call(
        body, name="merge_fwd", grid=(s_len // tt,),
        in_specs=[_row(tt, d, 0), _row(tt, d, 1), _row(tt, d), _row(tt, d)],
        out_specs=_row(tt, d), out_shape=jax.ShapeDtypeStruct((s_len, d), BF16),
        compiler_params=_cparams(("parallel",)),
    )(gate, gate, ab, rb)


def merge_bwd(gate, ab, rb, d_m, *, tt=64):
    s_len, d = ab.shape

    def body(ga_ref, gr_ref, a_ref, r_ref, dm_ref, dgate_ref, da_ref, dr_ref):
        _, vjp = jax.vjp(f_merge, ga_ref[...], gr_ref[...], a_ref[...], r_ref[...])
        dga, dgr, da, dr = vjp(dm_ref[...].astype(F32))
        dgate_ref[:, 0:d] = dga.astype(BF16)
        dgate_ref[:, d:2 * d] = dgr.astype(BF16)
        da_ref[...] = da.astype(BF16)
        dr_ref[...] = dr.astype(BF16)

    return pl.pallas_call(
        body, name="merge_bwd", grid=(s_len // tt,),
        in_specs=[_row(tt, d, 0), _row(tt, d, 1), _row(tt, d), _row(tt, d), _row(tt, d)],
        out_specs=[_row(tt, 2 * d), _row(tt, d), _row(tt, d)],
        out_shape=[jax.ShapeDtypeStruct((s_len, 2 * d), BF16), jax.ShapeDtypeStruct((s_len, d), BF16),
                   jax.ShapeDtypeStruct((s_len, d), BF16)],
        compiler_params=_cparams(("parallel",)),
    )(gate, gate, ab, rb, d_m)


def swiglu_fwd(gg, uu, *, tt=64):
    s_len, f = gg.shape

    def body(g_ref, u_ref, o_ref):
        o_ref[...] = f_swiglu(g_ref[...], u_ref[...]).astype(BF16)

    return pl.pallas_call(
        body, name="swiglu_fwd", grid=(s_len // tt,),
        in_specs=[_row(tt, f), _row(tt, f)], out_specs=_row(tt, f),
        out_shape=jax.ShapeDtypeStruct((s_len, f), BF16),
        compiler_params=_cparams(("parallel",)),
    )(gg, uu)


def swiglu_bwd(gg, uu, d_act, *, tt=32):
    s_len, f = gg.shape

    def body(g_ref, u_ref, d_ref, dg_ref, du_ref):
        _, vjp = jax.vjp(f_swiglu, g_ref[...], u_ref[...])
        dg, du = vjp(d_ref[...].astype(F32))
        dg_ref[...] = dg.astype(BF16)
        du_ref[...] = du.astype(BF16)

    out = jax.ShapeDtypeStruct((s_len, f), BF16)
    return pl.pallas_call(
        body, name="swiglu_bwd", grid=(s_len // tt,),
        in_specs=[_row(tt, f)] * 3, out_specs=[_row(tt, f)] * 2, out_shape=[out, out],
        compiler_params=_cparams(("parallel",)),
    )(gg, uu, d_act)


def resid_norm_fwd(x, m2, g_post, g_pre, *, tt=128):
    s_len, d = x.shape

    def body(x_ref, m_ref, gp_ref, gn_ref, x1_ref, h_ref):
        x1 = x_ref[...] + f_rms(m_ref[...], gp_ref[...])
        x1_ref[...] = x1
        h_ref[...] = f_rms(x1, gn_ref[...]).astype(BF16)

    return pl.pallas_call(
        body, name="resid_norm_fwd", grid=(s_len // tt,),
        in_specs=[_row(tt, d), _row(tt, d), _full(g_post), _full(g_pre)],
        out_specs=[_row(tt, d), _row(tt, d)],
        out_shape=[jax.ShapeDtypeStruct((s_len, d), F32), jax.ShapeDtypeStruct((s_len, d), BF16)],
        compiler_params=_cparams(("parallel",)),
    )(x, m2, g_post, g_pre)


def loss_head(x1, ff, tgt, g_post, *, tt=64):
    s_len, d = x1.shape

    def body(x1_ref, f_ref, t_ref, g_ref, loss_ref, dy_ref, df_ref, dg_ref):
        _acc_init(pl.program_id(0) == 0, loss_ref, dg_ref)
        nrm, vjp = jax.vjp(f_rms, f_ref[...], g_ref[...])
        err = x1_ref[...] + nrm - t_ref[...]
        per_tok = jnp.mean(err * err, axis=-1, keepdims=True)
        loss_ref[...] += 0.5 * jnp.sum(per_tok, axis=0, keepdims=True)
        dy = err * (1.0 / d)
        dff, dg = vjp(dy)
        dy_ref[...] = dy
        df_ref[...] = dff.astype(BF16)
        dg_ref[...] += dg

    return pl.pallas_call(
        body, name="loss_head", grid=(s_len // tt,),
        in_specs=[_row(tt, d)] * 3 + [_full(g_post)],
        out_specs=[pl.BlockSpec((1, LANES), lambda s: (0, 0)), _row(tt, d), _row(tt, d),
                   pl.BlockSpec((1, d), lambda s: (0, 0))],
        out_shape=[jax.ShapeDtypeStruct((1, LANES), F32), jax.ShapeDtypeStruct((s_len, d), F32),
                   jax.ShapeDtypeStruct((s_len, d), BF16), jax.ShapeDtypeStruct((1, d), F32)],
        compiler_params=_cparams(("arbitrary",)),
    )(x1, ff, tgt, g_post)


def resid_norm_bwd(x1, dh_a, dh_b, g_pre, m2, g_post, dy, *, tt=64):
    s_len, d = x1.shape

    def body(x1_ref, da_ref, db_ref, gn_ref, m_ref, gp_ref, dy_ref, dx1_ref, dm_ref, dgn_ref, dgp_ref):
        _acc_init(pl.program_id(0) == 0, dgn_ref, dgp_ref)
        _, vjp_n = jax.vjp(f_rms, x1_ref[...], gn_ref[...])
        dx1_n, dgn = vjp_n(da_ref[...] + db_ref[...])
        dx1 = dy_ref[...] + dx1_n
        _, vjp_p = jax.vjp(f_rms, m_ref[...], gp_ref[...])
        dm, dgp = vjp_p(dx1)
        dx1_ref[...] = dx1
        dm_ref[...] = dm.astype(BF16)
        dgn_ref[...] += dgn
        dgp_ref[...] += dgp

    acc = pl.BlockSpec((1, d), lambda s: (0, 0))
    vec = jax.ShapeDtypeStruct((1, d), F32)
    return pl.pallas_call(
        body, name="resid_norm_bwd", grid=(s_len // tt,),
        in_specs=[_row(tt, d)] * 3 + [_full(g_pre), _row(tt, d), _full(g_post), _row(tt, d)],
        out_specs=[_row(tt, d), _row(tt, d), acc, acc],
        out_shape=[jax.ShapeDtypeStruct((s_len, d), F32), jax.ShapeDtypeStruct((s_len, d), BF16), vec, vec],
        compiler_params=_cparams(("arbitrary",)),
    )(x1, dh_a, dh_b, g_pre, m2, g_post, dy)


def rms_bwd(x, g, dh_a, dh_b, dh_c, dres, *, tt=64):
    s_len, d = x.shape

    def body(x_ref, g_ref, a_ref, b_ref, c_ref, r_ref, dx_ref, dg_ref):
        _acc_init(pl.program_id(0) == 0, dg_ref)
        _, vjp = jax.vjp(f_rms, x_ref[...], g_ref[...])
        dx, dg = vjp(a_ref[...] + b_ref[...] + c_ref[...])
        dx_ref[...] = r_ref[...] + dx
        dg_ref[...] += dg

    return pl.pallas_call(
        body, name="rms_bwd", grid=(s_len // tt,),
        in_specs=[_row(tt, d), _full(g)] + [_row(tt, d)] * 4,
        out_specs=[_row(tt, d), pl.BlockSpec((1, d), lambda s: (0, 0))],
        out_shape=[jax.ShapeDtypeStruct((s_len, d), F32), jax.ShapeDtypeStruct((1, d), F32)],
        compiler_params=_cparams(("arbitrary",)),
    )(x, g, dh_a, dh_b, dh_c, dres)


def colsum(a, *, tt=256):
    s_len, c = a.shape

    def body(a_ref, o_ref):
        _acc_init(pl.program_id(0) == 0, o_ref)
        o_ref[...] += jnp.sum(a_ref[...].astype(F32), axis=0, keepdims=True)

    return pl.pallas_call(
        body, name="colsum", grid=(s_len // tt,),
        in_specs=[_row(tt, c)], out_specs=pl.BlockSpec((1, c), lambda s: (0, 0)),
        out_shape=jax.ShapeDtypeStruct((1, c), F32),
        compiler_params=_cparams(("arbitrary",)),
    )(a)


AT_HD = 128
AT_GROUP = 4
AT_KVH = 8
AT_BLK = 128
AT_QW = AT_GROUP * AT_HD
AT_KCOL = AT_KVH * AT_GROUP
AT_VCOL = AT_KCOL + AT_KVH
NEG_INF = -1e30
AT_SCALE = AT_HD ** -0.5


def _rope(t, cos2, sin2):
    return t * cos2 + pltpu.roll(t, AT_HD // 2, axis=1) * sin2


def _rope_t(d, cos2, sin2):
    return d * cos2 + pltpu.roll(d * sin2, AT_HD // 2, axis=1)


def _att_specs():
    prev = lambda i: jnp.maximum(i - 1, 0)
    blk = (AT_BLK, AT_HD)
    return [
        pl.BlockSpec((AT_BLK, AT_QW), lambda h, i: (i, h)),
        pl.BlockSpec(blk, lambda h, i: (i, AT_KCOL + h)),
        pl.BlockSpec(blk, lambda h, i: (prev(i), AT_KCOL + h)),
        pl.BlockSpec(blk, lambda h, i: (i, AT_VCOL + h)),
        pl.BlockSpec(blk, lambda h, i: (prev(i), AT_VCOL + h)),
        pl.BlockSpec((1, AT_QW), lambda h, i: (0, h)),
        pl.BlockSpec((1, AT_HD), lambda h, i: (0, AT_KCOL + h)),
        pl.BlockSpec((1, AT_HD), lambda h, i: (0, AT_VCOL + h)),
        pl.BlockSpec((1, AT_GROUP, AT_HD), lambda h, i: (h, 0, 0)),
        pl.BlockSpec(blk, lambda h, i: (i, 0)),
        pl.BlockSpec(blk, lambda h, i: (i, 0)),
        pl.BlockSpec(blk, lambda h, i: (prev(i), 0)),
        pl.BlockSpec(blk, lambda h, i: (prev(i), 0)),
    ]


def _att_load(i, q_ref, kc_ref, kp_ref, vc_ref, vp_ref, bq_ref, bk_ref, bv_ref, cc_ref, sc_ref, cp_ref, sp_ref):
    cosc, sinc = cc_ref[...], sc_ref[...]
    q = q_ref[...] + bq_ref[...]
    kc = _rope(kc_ref[...] + bk_ref[...], cosc, sinc)
    kp = _rope(kp_ref[...] + bk_ref[...], cp_ref[...], sp_ref[...])
    kcat = jnp.concatenate([kp, kc], axis=0).astype(BF16)
    vcat = jnp.concatenate([vp_ref[...] + bv_ref[...], vc_ref[...] + bv_ref[...]], axis=0).astype(BF16)
    qi = lax.broadcasted_iota(jnp.int32, (AT_GROUP * AT_BLK, 2 * AT_BLK), 0) & (AT_BLK - 1)
    kj = lax.broadcasted_iota(jnp.int32, (AT_GROUP * AT_BLK, 2 * AT_BLK), 1)
    rel = qi + AT_BLK - kj
    mask = (rel >= 0) & (rel < AT_BLK) & ((kj >= AT_BLK) | (i > 0))
    return q, kcat, vcat, mask, cosc, sinc


AT_ROWS = AT_GROUP * AT_BLK


def _att_stack(q, cosc, sinc):
    return jnp.concatenate([_rope(q[:, g * AT_HD:(g + 1) * AT_HD], cosc, sinc) for g in range(AT_GROUP)], axis=0)


def _att_cols(sk_ref):
    head = lax.broadcasted_iota(jnp.int32, (AT_ROWS, 1), 0) >> 7
    sink = jnp.zeros((AT_ROWS, 1), F32)
    for g in range(AT_GROUP):
        sink = jnp.where(head == g, sk_ref[0, g:g + 1, 0:1], sink)
    return sink, head


def _att_probs(qs, kcat, mask, sink):
    s = lax.dot_general(qs, kcat, (((1,), (1,)), ((), ())), preferred_element_type=F32) * AT_SCALE
    s = jnp.where(mask, s, NEG_INF)
    m = jnp.maximum(jnp.max(s, axis=-1, keepdims=True), sink)
    p = jnp.exp(s - m)
    es = jnp.exp(sink - m)
    inv = 1.0 / (jnp.sum(p, axis=-1, keepdims=True) + es)
    return p * inv, es * inv


def attention_fwd(qkv, bias, sinks_b, cos2, sin2):
    s_len = qkv.shape[0]
    nb = s_len // AT_BLK

    def body(q_ref, kc_ref, kp_ref, vc_ref, vp_ref, bq_ref, bk_ref, bv_ref, sk_ref, cc_ref, sc_ref, cp_ref, sp_ref,
             o_ref):
        i = pl.program_id(1)
        q, kcat, vcat, mask, cosc, sinc = _att_load(i, q_ref, kc_ref, kp_ref, vc_ref, vp_ref, bq_ref, bk_ref,
                                                    bv_ref, cc_ref, sc_ref, cp_ref, sp_ref)
        sink, _ = _att_cols(sk_ref)
        probs, _ = _att_probs(_att_stack(q, cosc, sinc).astype(BF16), kcat, mask, sink)
        o = jnp.dot(probs.astype(BF16), vcat, preferred_element_type=F32).astype(BF16)
        for g in range(AT_GROUP):
            o_ref[:, g * AT_HD:(g + 1) * AT_HD] = o[g * AT_BLK:(g + 1) * AT_BLK, :]

    return pl.pallas_call(
        body, name="attention_fwd", grid=(AT_KVH, nb),
        in_specs=_att_specs(),
        out_specs=pl.BlockSpec((AT_BLK, AT_QW), lambda h, i: (i, h)),
        out_shape=jax.ShapeDtypeStruct((s_len, AT_KVH * AT_QW), BF16),
        compiler_params=_cparams(("parallel", "parallel")),
    )(qkv, qkv, qkv, qkv, qkv, bias, bias, bias, sinks_b, cos2, sin2, cos2, sin2)


def attention_bwd(qkv, bias, sinks_b, cos2, sin2, d_o):
    s_len = qkv.shape[0]
    nb = s_len // AT_BLK

    def body(q_ref, kc_ref, kp_ref, vc_ref, vp_ref, bq_ref, bk_ref, bv_ref, sk_ref, cc_ref, sc_ref, cp_ref, sp_ref,
             do_ref, dq_ref, dk_ref, dv_ref, dsk_ref):
        i = pl.program_id(1)
        _acc_init(i == 0, dsk_ref)
        q, kcat, vcat, mask, cosc, sinc = _att_load(i, q_ref, kc_ref, kp_ref, vc_ref, vp_ref, bq_ref, bk_ref,
                                                    bv_ref, cc_ref, sc_ref, cp_ref, sp_ref)
        sink, head = _att_cols(sk_ref)
        qs = _att_stack(q, cosc, sinc).astype(BF16)
        probs, psink = _att_probs(qs, kcat, mask, sink)
        pb = probs.astype(BF16)
        do_s = jnp.concatenate([do_ref[:, g * AT_HD:(g + 1) * AT_HD] for g in range(AT_GROUP)], axis=0)
        do_f = do_s.astype(F32)
        do_b = do_s.astype(BF16)
        o_s = jnp.dot(pb, vcat, preferred_element_type=F32)
        dsum = jnp.sum(do_f * o_s, axis=-1, keepdims=True)
        dp = lax.dot_general(do_b, vcat, (((1,), (1,)), ((), ())), preferred_element_type=F32)
        ds = (probs * (dp - dsum) * AT_SCALE).astype(BF16)
        dv_cat = lax.dot_general(pb, do_b, (((0,), (0,)), ((), ())), preferred_element_type=F32)
        dk_cat = lax.dot_general(ds, qs, (((0,), (0,)), ((), ())), preferred_element_type=F32)
        dq_s = jnp.dot(ds, kcat, preferred_element_type=F32)
        lane = lax.broadcasted_iota(jnp.int32, (1, AT_HD), 1)
        dsk = jnp.zeros((1, AT_HD), F32)
        sink_term = psink * dsum
        for g in range(AT_GROUP):
            rows = slice(g * AT_BLK, (g + 1) * AT_BLK)
            dq_ref[:, g * AT_HD:(g + 1) * AT_HD] = _rope_t(dq_s[rows, :], cosc, sinc).astype(BF16)
            dsk = dsk + jnp.where(lane == g, -jnp.sum(sink_term[rows, :], axis=0, keepdims=True), 0.0)
        dsk_ref[0] += dsk
        cur = pl.ds(pl.multiple_of(i * AT_BLK, AT_BLK), AT_BLK)
        dk_ref[cur, :] = _rope_t(dk_cat[AT_BLK:], cosc, sinc)
        dv_ref[cur, :] = dv_cat[AT_BLK:]

        @pl.when(i > 0)
        def _():
            prv = pl.ds(pl.multiple_of((i - 1) * AT_BLK, AT_BLK), AT_BLK)
            dk_ref[prv, :] += _rope_t(dk_cat[:AT_BLK], cp_ref[...], sp_ref[...])
            dv_ref[prv, :] += dv_cat[:AT_BLK]

    kv_out = pl.BlockSpec((s_len, AT_HD), lambda h, i: (0, h))
    return pl.pallas_call(
        body, name="attention_bwd", grid=(AT_KVH, nb),
        in_specs=_att_specs() + [pl.BlockSpec((AT_BLK, AT_QW), lambda h, i: (i, h))],
        out_specs=[pl.BlockSpec((AT_BLK, AT_QW), lambda h, i: (i, h)), kv_out, kv_out,
                   pl.BlockSpec((1, 1, AT_HD), lambda h, i: (h, 0, 0))],
        out_shape=[jax.ShapeDtypeStruct((s_len, AT_KVH * AT_QW), BF16),
                   jax.ShapeDtypeStruct((s_len, AT_KVH * AT_HD), F32),
                   jax.ShapeDtypeStruct((s_len, AT_KVH * AT_HD), F32),
                   jax.ShapeDtypeStruct((AT_KVH, 1, AT_HD), F32)],
        compiler_params=_cparams(("arbitrary", "arbitrary")),
    )(qkv, qkv, qkv, qkv, qkv, bias, bias, bias, sinks_b, cos2, sin2, cos2, sin2, d_o)


ATT_QKV = 6144
RW_SHIFT = 13024
RW_PAD = 13056
D_GATE = 480
ROPE_THETA = 10000.0


def perm_cols(a):
    lead = a.shape[:-1]
    return jnp.swapaxes(a.reshape(lead + (RW_H, RW_N)), -1, -2).reshape(lead + (RW_C,))


def rw_reorder(a, pad_value=0):
    r, k, v = (perm_cols(a[..., i * RW_C:(i + 1) * RW_C]) for i in range(3))
    wd = a[..., 3 * RW_C:3 * RW_C + 128]
    ad = a[..., 3 * RW_C + 128:3 * RW_C + 256]
    gd = a[..., 3 * RW_C + 256:]
    pad = jnp.full(a.shape[:-1] + (512 - D_GATE,), pad_value, a.dtype)
    return jnp.concatenate([r, k, v, gd, pad, wd, ad], axis=-1)


def rw_restore(a):
    r, k, v = (perm_cols(a[..., i * RW_C:(i + 1) * RW_C]) for i in range(3))
    gd = a[..., 3 * RW_C:3 * RW_C + D_GATE]
    wd = a[..., 3 * RW_C + 512:3 * RW_C + 640]
    ad = a[..., 3 * RW_C + 640:3 * RW_C + 768]
    return jnp.concatenate([r, k, v, wd, ad, gd], axis=-1)


def to_tiles(a):
    t = a.reshape(a.shape[0], RW_N, RW_H)
    return jnp.concatenate([t, t], axis=-1)


def from_tiles(t):
    return t[:, :, :RW_H].reshape(t.shape[0], RW_C)


def rope_tables(s_len):
    pos = jnp.arange(s_len, dtype=F32)
    inv_freq = ROPE_THETA ** (-jnp.arange(0, AT_HD, 2, dtype=F32) / AT_HD)
    ang = pos[:, None] * inv_freq[None, :]
    cos, sin = jnp.cos(ang), jnp.sin(ang)
    return jnp.concatenate([cos, cos], axis=1), jnp.concatenate([-sin, sin], axis=1)


def local_step(x, tgt, small, big, fwd_rider, got_early, bwd_rider, late_riders):
    s_len, d = x.shape
    w_in = big["w_in"]
    w_qkv = w_in[:, :ATT_QKV]
    w_rw = rw_reorder(w_in[:, ATT_QKV:ATT_QKV + RW_SHIFT])
    w_gate = w_in[:, ATT_QKV + RW_SHIFT:]
    w2 = perm_cols(big["w2"])
    a2 = perm_cols(big["a2"])
    g2 = jnp.pad(perm_cols(big["g2"]), ((0, 512 - D_GATE), (0, 0)))
    mu = rw_reorder(small["mu_shift"])
    w0, a0, k_k, k_a, ln_w, ln_b = (perm_cols(small[n]) for n in ("w0", "a0", "k_k", "k_a", "ln_x_w", "ln_x_b"))
    r_k = small["r_k"].reshape(RW_H, RW_N).T.reshape(1, RW_C)
    sinks_b = jnp.broadcast_to(small["att_sinks"].reshape(AT_KVH, AT_GROUP, 1), (AT_KVH, AT_GROUP, AT_HD))
    cos2, sin2 = rope_tables(s_len)
    bias = small["b_qkv"]

    h = rms_fwd(x, small["norm_mix_pre"])
    qkv = matmul(h, w_qkv, name="mm_qkv")
    prw = matmul(h, w_rw, name="mm_rw")
    gate = matmul(h, w_gate, name="mm_gate")
    o_att = attention_fwd(qkv, bias, sinks_b, cos2, sin2)
    pre_params = (mu, w0, a0, k_k, k_a, w2, a2, g2)
    r, dec, k, v, av, bv, g, tw, xa, sg = rwkv_pre_fwd(prw, *pre_params)
    v3 = to_tiles(v)
    y3, hist, sa3, *arrived = rwkv_scan_fwd(r, dec, k, av, bv, v3, rider=fwd_rider)
    big = {**big, **got_early(arrived)}
    w_rb = big["w_rwkv_branch"].reshape(RW_H, RW_N, d).swapaxes(0, 1).reshape(RW_C, d)
    y = from_tiles(y3)
    o_rw = rwkv_post_fwd(y, r, k, v, g, ln_w, ln_b, r_k)
    ab = matmul(o_att, big["w_att_branch"], name="mm_ab")
    rb = matmul(o_rw, w_rb, name="mm_rb")
    merged = merge_fwd(gate, ab, rb)
    m2 = matmul(merged, big["w_out"], name="mm_out")
    x1, h2 = resid_norm_fwd(x, m2, small["norm_mix_post"], small["norm_ffn_pre"])
    gg = matmul(h2, big["w_ffn_gate"], name="mm_fg")
    uu = matmul(h2, big["w_ffn_up"], name="mm_fu")
    act = swiglu_fwd(gg, uu)
    ff = matmul(act, big["w_ffn_down"], name="mm_fd")
    loss, dy, dff, d_nfp = loss_head(x1, ff, tgt, small["norm_ffn_post"])

    dact = matmul(dff, big["w_ffn_down"], tb=True, name="mm_dact")
    g_fd = matmul(act, dff, ta=True, out_dtype=BF16, name="mm_gfd")
    dgg, duu = swiglu_bwd(gg, uu, dact)
    g_fg = matmul(h2, dgg, ta=True, out_dtype=BF16, name="mm_gfg")
    g_fu = matmul(h2, duu, ta=True, out_dtype=BF16, name="mm_gfu")
    dh2a = matmul(dgg, big["w_ffn_gate"], tb=True, name="mm_dh2a")
    dh2b = matmul(duu, big["w_ffn_up"], tb=True, name="mm_dh2b")
    dx1, dm2, d_nfpre, d_nmpost = resid_norm_bwd(x1, dh2a, dh2b, small["norm_ffn_pre"], m2, small["norm_mix_post"], dy)
    dmerged = matmul(dm2, big["w_out"], tb=True, name="mm_dmerged")
    g_out = matmul(merged, dm2, ta=True, out_dtype=BF16, name="mm_gout")
    dgate, dab, drb = merge_bwd(gate, ab, rb, dmerged)
    do_att = matmul(dab, big["w_att_branch"], tb=True, out_dtype=BF16, name="mm_doatt")
    g_ab = matmul(o_att, dab, ta=True, out_dtype=BF16, name="mm_gab")
    do_rw = matmul(drb, w_rb, tb=True, name="mm_dorw")
    g_rb = matmul(o_rw, drb, ta=True, out_dtype=BF16, name="mm_grb")
    dq, dk_att, dv_att, dsk = attention_bwd(qkv, bias, sinks_b, cos2, sin2, do_att)
    dqkv = jnp.concatenate([dq, dk_att.astype(BF16), dv_att.astype(BF16)], axis=1)
    dy_s, dr_p, dk_p, dv_p, dg, d_lnw, d_lnb, d_rk = rwkv_post_bwd(y, r, k, v, g, ln_w, ln_b, r_k, do_rw)
    early = {"w_att_branch": g_ab, "w_rwkv_branch": g_rb.reshape(RW_N, RW_H, d).swapaxes(0, 1).reshape(RW_C, d),
             "w_out": g_out, "w_ffn_gate": g_fg, "w_ffn_up": g_fu, "w_ffn_down": g_fd}
    dr_s, ddec, dk_s, dav, dbv, dv3, *from_chips = rwkv_scan_bwd(r, dec, k, av, bv, v3, hist, sa3, to_tiles(dy_s),
                                                                  rider=bwd_rider(early))
    dprw, dzw, dza, dmu, dw0, da0, dkk, dka = rwkv_pre_bwd(
        prw, *pre_params, dr_p, ddec, dk_p, dv_p, dav, dbv, dg, dr_s, dk_s, from_tiles(dv3))
    g_w2 = matmul(tw, dzw, ta=True, out_dtype=BF16, name="mm_gw2")
    g_a2 = matmul(xa, dza, ta=True, out_dtype=BF16, name="mm_ga2")
    g_g2 = matmul(sg, dg.astype(BF16), ta=True, out_dtype=BF16, name="mm_gg2")
    g_qkv = matmul(h, dqkv, ta=True, out_dtype=BF16, name="mm_gqkv")
    g_rw = matmul(h, dprw, ta=True, out_dtype=BF16, name="mm_grw")
    g_gate = matmul(h, dgate, ta=True, out_dtype=BF16, name="mm_ggate")
    gbig = {
        "w_in": jnp.concatenate([g_qkv, rw_restore(g_rw), g_gate], axis=1),
        "w2": perm_cols(g_w2), "a2": perm_cols(g_a2), "g2": perm_cols(g_g2)[:D_GATE],
    }
    ride_a, ride_b, ride_c = late_riders(gbig)
    dh_a, *late_a = matmul(dqkv, w_qkv, tb=True, name="mm_dha", rider=ride_a)
    dh_b, *late_b = matmul(dprw, w_rw, tb=True, name="mm_dhb", rider=ride_b)
    dh_c, *late_c = matmul(dgate, w_gate, tb=True, name="mm_dhc", rider=ride_c)
    grad_x, d_nmpre = rms_bwd(x, small["norm_mix_pre"], dh_a, dh_b, dh_c, dx1)
    d_bias = colsum(dqkv)

    gsmall = {
        "norm_mix_pre": d_nmpre, "norm_mix_post": d_nmpost, "norm_ffn_pre": d_nfpre, "norm_ffn_post": d_nfp,
        "b_qkv": d_bias, "att_sinks": dsk[:, 0, :AT_GROUP].reshape(1, AT_KVH * AT_GROUP),
        "mu_shift": rw_restore(dmu), "w0": perm_cols(dw0), "a0": perm_cols(da0), "k_k": perm_cols(dkk),
        "k_a": perm_cols(dka), "r_k": d_rk.reshape(RW_N, RW_H).T.reshape(1, RW_C),
        "ln_x_w": perm_cols(d_lnw), "ln_x_b": perm_cols(d_lnb),
    }
    return loss, grad_x, gsmall, from_chips, (late_a, late_b, late_c)


def _place():
    x, y, c = lax.axis_index("x"), lax.axis_index("y"), lax.axis_index("c")
    chips = [(1 - x, y), (x, 1 - y), (1 - x, 1 - y)]
    return x, y, c, chips


def _remote(src, dst, send_sems, recv_sems, k, dev):
    return pltpu.make_async_remote_copy(src_ref=src, dst_ref=dst, send_sem=send_sems.at[k], recv_sem=recv_sems.at[k],
                                        device_id=dev, device_id_type=MESH)


def _gather_parts(n):
    def half(ref, which):
        hr = ref.shape[0] // 2
        return ref.at[pl.ds(which * hr, hr), :]

    def sends(ins, outs, send_sems, recv_sems):
        x, y, c, chips = _place()
        me = 2 * x + y
        return [_remote(half(ins[i], c), half(outs[i].at[me], c), send_sems, recv_sems, 6 * i + j, (*chip, c))
                for i in range(n) for j, chip in enumerate(chips)]

    def start(ins, outs, send_sems, recv_sems):
        for cp in sends(ins, outs, send_sems, recv_sems):
            cp.start()

    def finish(ins, outs, send_sems, recv_sems):
        x, y, c, chips = _place()
        sib = (x, y, 1 - c)
        passed = []
        for i in range(n):
            for j, chip in enumerate(chips):
                got = half(outs[i].at[2 * chip[0] + chip[1]], c)
                _remote(got, got, send_sems, recv_sems, 6 * i + j, sib).wait_recv()
                cp = _remote(got, got, send_sems, recv_sems, 6 * i + 3 + j, sib)
                cp.start()
                passed.append(cp)
        for i in range(n):
            for j, chip in enumerate(chips):
                got = half(outs[i].at[2 * chip[0] + chip[1]], 1 - c)
                _remote(got, got, send_sems, recv_sems, 6 * i + 3 + j, sib).wait_recv()
        for cp in sends(ins, outs, send_sems, recv_sems) + passed:
            cp.wait_send()

    return start, finish


def gather_rider(shards):
    n = len(shards)
    start, finish = _gather_parts(n)
    return {"ins": shards, "out_shapes": [jax.ShapeDtypeStruct((4,) + s.shape, s.dtype) for s in shards],
            "scratch": [pltpu.SemaphoreType.DMA((6 * n,)), pltpu.SemaphoreType.DMA((6 * n,))],
            "start": start, "finish": finish}


def gather_weights(shards):
    n = len(shards)
    start, finish = _gather_parts(n)

    def body(*refs):
        parts = (refs[:n], refs[n:2 * n], *refs[2 * n:])
        start(*parts)
        finish(*parts)

    return pl.pallas_call(
        body, name="gather_weights",
        in_specs=[ANY] * n, out_specs=[ANY] * n,
        out_shape=[jax.ShapeDtypeStruct((4,) + s.shape, s.dtype) for s in shards],
        scratch_shapes=[pltpu.SemaphoreType.DMA((6 * n,)), pltpu.SemaphoreType.DMA((6 * n,))],
    )(*shards)


def swap_with_sibling(blocks, name):
    n = len(blocks)

    def body(*refs):
        ins, outs = refs[:n], refs[n:2 * n]
        send_sems, recv_sems = refs[2 * n:]
        x, y, c, _ = _place()
        cps = [_remote(ins[i], outs[i], send_sems, recv_sems, i, (x, y, 1 - c)) for i in range(n)]
        for cp in cps:
            cp.start()
        for cp in cps:
            cp.wait()

    return pl.pallas_call(
        body, name=name, in_specs=[ANY] * n, out_specs=[ANY] * n,
        out_shape=[jax.ShapeDtypeStruct(b.shape, b.dtype) for b in blocks],
        scratch_shapes=[pltpu.SemaphoreType.DMA((n,)), pltpu.SemaphoreType.DMA((n,))],
    )(*blocks)


def _scatter_parts(n, rows=None):
    def piece(ref, i):
        return ref if rows is None or rows[i] is None else ref.at[pl.ds(rows[i][0], rows[i][1]), :]

    def copies(ins, outs, send_sems, recv_sems):
        x, y, c, chips = _place()
        return [_remote(piece(ins[i].at[2 * chip[0] + chip[1]], i), outs[i].at[j], send_sems, recv_sems, 3 * i + j,
                        (*chip, c))
                for i in range(n) for j, chip in enumerate(chips)]

    def start(*refs):
        for cp in copies(*refs):
            cp.start()

    def finish(*refs):
        for cp in copies(*refs):
            cp.wait()

    return start, finish


def scatter_rider(parts, rows=None):
    n = len(parts)
    start, finish = _scatter_parts(n, rows)
    nrows = [p.shape[1] if rows is None or rows[i] is None else rows[i][1] for i, p in enumerate(parts)]
    return {"ins": parts, "out_shapes": [jax.ShapeDtypeStruct((3, r, p.shape[2]), p.dtype) for p, r in zip(parts, nrows)],
            "scratch": [pltpu.SemaphoreType.DMA((3 * n,)), pltpu.SemaphoreType.DMA((3 * n,))],
            "start": start, "finish": finish}


def allreduce_small(v):
    rows = v.shape[0]

    def body(v_ref, o_ref, buf, send_sems, recv_sems):
        x, y, c, chips = _place()
        me, sib = (x, y, c), (x, y, 1 - c)

        def slot(px, py, pc):
            return buf.at[4 * px + 2 * py + pc]

        def copy(k, block, to, src=None):
            return _remote(slot(*block) if src is None else src, slot(*block), send_sems, recv_sems, k, to)

        buf[4 * x + 2 * y + c] = v_ref[...]
        first = [copy(0, me, sib, src=v_ref)]
        first += [copy(1 + j, me, (*chip, c), src=v_ref) for j, chip in enumerate(chips)]
        for cp in first:
            cp.start()
        passed = [copy(4 + j, (*chip, c), sib) for j, chip in enumerate(chips)]
        for j, chip in enumerate(chips):
            copy(1 + j, (*chip, c), me).wait_recv()
            passed[j].start()
        copy(0, sib, me).wait_recv()
        for j, chip in enumerate(chips):
            copy(4 + j, (*chip, 1 - c), me).wait_recv()
        for cp in first + passed:
            cp.wait_send()
        acc = buf[0]
        for k in range(1, 8):
            acc = acc + buf[k]
        o_ref[...] = acc

    vm = pl.BlockSpec(memory_space=pltpu.VMEM)
    return pl.pallas_call(
        body, name="allreduce_small", in_specs=[vm], out_specs=vm,
        out_shape=jax.ShapeDtypeStruct(v.shape, F32),
        scratch_shapes=[pltpu.VMEM((8, rows, LANES), F32), pltpu.SemaphoreType.DMA((7,)),
                        pltpu.SemaphoreType.DMA((7,))],
    )(v)


def _rows_tile(r):
    return _pick(r, (64, 32, 16, 8))


def swap_halves(blocks, name):
    n = len(blocks)

    def body(*refs):
        ins, outs = refs[:n], refs[n:2 * n]
        send_sems, recv_sems = refs[2 * n:]
        x, y, c, _ = _place()
        cps = []
        for i in range(n):
            hr = ins[i].shape[1] // 2
            cps.append(_remote(ins[i].at[:, pl.ds((1 - c) * hr, hr), :], outs[i], send_sems, recv_sems, i,
                               (x, y, 1 - c)))
        for cp in cps:
            cp.start()
        for cp in cps:
            cp.wait()

    return pl.pallas_call(
        body, name=name, in_specs=[ANY] * n, out_specs=[ANY] * n,
        out_shape=[jax.ShapeDtypeStruct((4, b.shape[1] // 2, b.shape[2]), b.dtype) for b in blocks],
        scratch_shapes=[pltpu.SemaphoreType.DMA((n,)), pltpu.SemaphoreType.DMA((n,))],
    )(*blocks)


def add_pairs(g4, b, core):
    _, r, c = b.shape
    tr = _rows_tile(r)
    nrt = r // tr

    def body(core_ref, a_ref, b_ref, o_ref):
        o_ref[...] = (a_ref[...].astype(F32) + b_ref[...].astype(F32)).astype(o_ref.dtype)

    spec = pl.BlockSpec((1, tr, c), lambda s, i, core_ref: (s, i, 0))
    return pl.pallas_call(
        body, name="add_pairs",
        grid_spec=pltpu.PrefetchScalarGridSpec(
            num_scalar_prefetch=1, grid=(4, nrt),
            in_specs=[pl.BlockSpec((1, tr, c), lambda s, i, core_ref: (s, core_ref[0] * nrt + i, 0)), spec],
            out_specs=spec),
        out_shape=jax.ShapeDtypeStruct(b.shape, b.dtype), compiler_params=_cparams(("parallel", "parallel")),
    )(core, g4, b)


def add_four(mine, others):
    r, c = mine.shape
    tr = _rows_tile(r)

    def body(m_ref, o_ref, out_ref):
        acc = m_ref[...].astype(F32)
        for j in range(3):
            acc = acc + o_ref[j].astype(F32)
        out_ref[...] = acc

    return pl.pallas_call(
        body, name="add_four", grid=(r // tr,),
        in_specs=[pl.BlockSpec((tr, c), lambda i: (i, 0)), pl.BlockSpec((3, tr, c), lambda i: (0, i, 0))],
        out_specs=pl.BlockSpec((tr, c), lambda i: (i, 0)),
        out_shape=jax.ShapeDtypeStruct((r, c), F32), compiler_params=_cparams(("parallel",)),
    )(mine, others)


def pair_sums(blocks, tag):
    core = lax.axis_index("c").astype(jnp.int32).reshape(1)
    from_sibling = swap_halves(blocks, "swap_halves_" + tag)
    return [add_pairs(g4, b, core) for g4, b in zip(blocks, from_sibling)]


def owner_sums(pair, from_chips):
    cx, cy, cc = lax.axis_index("x"), lax.axis_index("y"), lax.axis_index("c")
    me = 2 * cx + cy
    sums = [add_four(lax.dynamic_index_in_dim(p, me, 0, keepdims=False), t) for p, t in zip(pair, from_chips)]
    got = swap_with_sibling(sums, "swap_sums")
    return [jnp.concatenate([jnp.where(cc == 0, s, g), jnp.where(cc == 0, g, s)], axis=0) for s, g in zip(sums, got)]


ADAM_LR = 0.001
ADAM_B1 = 0.9
ADAM_B2 = 0.999
ADAM_EPS = 1e-08
ADAM_WD = 0.01
ADAM_STEP = 10


def adamw(w, g, m, v):
    r, c = w.shape
    tr = _rows_tile(r)
    spec = pl.BlockSpec((tr, c), lambda i: (i, 0))

    def body(w_ref, g_ref, m_ref, v_ref, d_ref, nm_ref, nv_ref):
        gv = g_ref[...]
        nm = ADAM_B1 * m_ref[...] + (1.0 - ADAM_B1) * gv
        nv = ADAM_B2 * v_ref[...] + (1.0 - ADAM_B2) * jnp.square(gv)
        m_hat = nm / (1.0 - ADAM_B1 ** ADAM_STEP)
        v_hat = nv / (1.0 - ADAM_B2 ** ADAM_STEP)
        d_ref[...] = -ADAM_LR * (m_hat / (jnp.sqrt(v_hat) + ADAM_EPS) + ADAM_WD * w_ref[...])
        nm_ref[...] = nm
        nv_ref[...] = nv

    out = jax.ShapeDtypeStruct((r, c), F32)
    return pl.pallas_call(
        body, name="adamw", grid=(r // tr,), in_specs=[spec] * 4, out_specs=[spec] * 3, out_shape=[out] * 3,
        compiler_params=_cparams(("parallel",)),
    )(w, g, m, v)


WEIGHTS = ["norm_mix_pre", "norm_mix_post", "norm_ffn_pre", "norm_ffn_post", "w_in", "b_qkv", "att_sinks", "mu_shift",
           "w0", "w2", "a0", "a2", "g2", "k_k", "k_a", "r_k", "ln_x_w", "ln_x_b", "w_att_branch", "w_rwkv_branch",
           "w_out", "w_ffn_gate", "w_ffn_up", "w_ffn_down"]
BIG = {"w_in": 1, "w2": 1, "a2": 1, "g2": 1, "w_att_branch": 0, "w_rwkv_branch": 0, "w_out": 0, "w_ffn_gate": 1,
       "w_ffn_up": 1, "w_ffn_down": 0}
SMALL = [n for n in WEIGHTS if n not in BIG]
LATE_CUTS = (13, 45)
LATE = ("w_in", "w2", "a2", "g2")
N_CHIPS = 4


def _whole(g4, axis):
    if axis == 0:
        return g4.reshape(g4.shape[0] * g4.shape[1], g4.shape[2])
    return jnp.swapaxes(g4, 0, 1).reshape(g4.shape[1], g4.shape[0] * g4.shape[2])


def _by_shard(w, axis):
    if axis == 0:
        return w.reshape(N_CHIPS, w.shape[0] // N_CHIPS, w.shape[1])
    return jnp.swapaxes(w.reshape(w.shape[0], N_CHIPS, w.shape[1] // N_CHIPS), 0, 1)


def _pack(parts):
    flat = jnp.concatenate([parts[n].reshape(-1) for n in SMALL])
    rows = -(-flat.shape[0] // (LANES * SUBLANES)) * SUBLANES
    return jnp.pad(flat, (0, rows * LANES - flat.shape[0])).reshape(rows, LANES)


def _unpack(packed, like):
    flat = packed.reshape(-1)
    out, off = {}, 0
    for n in SMALL:
        size = like[n].size
        out[n] = flat[off:off + size].reshape(like[n].shape)
        off += size
    return out


def kernel(x, norm_mix_pre, norm_mix_post, norm_ffn_pre, norm_ffn_post, w_in, b_qkv, att_sinks, mu_shift, w0, w2, a0, a2, g2, k_k, k_a, r_k, ln_x_w, ln_x_b, w_att_branch, w_rwkv_branch, w_out, w_ffn_gate, w_ffn_up, w_ffn_down, loss_target, m_norm_mix_pre, m_norm_mix_post, m_norm_ffn_pre, m_norm_ffn_post, m_w_in, m_b_qkv, m_att_sinks, m_mu_shift, m_w0, m_w2, m_a0, m_a2, m_g2, m_k_k, m_k_a, m_r_k, m_ln_x_w, m_ln_x_b, m_w_att_branch, m_w_rwkv_branch, m_w_out, m_w_ffn_gate, m_w_ffn_up, m_w_ffn_down, v_norm_mix_pre, v_norm_mix_post, v_norm_ffn_pre, v_norm_ffn_post, v_w_in, v_b_qkv, v_att_sinks, v_mu_shift, v_w0, v_w2, v_a0, v_a2, v_g2, v_k_k, v_k_a, v_r_k, v_ln_x_w, v_ln_x_b, v_w_att_branch, v_w_rwkv_branch, v_w_out, v_w_ffn_gate, v_w_ffn_up, v_w_ffn_down):
    given = dict(locals())
    wts = {n: given[n] for n in WEIGHTS}
    mom1 = {n: given["m_" + n] for n in WEIGHTS}
    mom2 = {n: given["v_" + n] for n in WEIGHTS}
    early = [n for n in BIG if n not in LATE]
    me = 2 * lax.axis_index("x") + lax.axis_index("y")
    own = {n: wts[n][0].astype(BF16) for n in BIG}

    def placed(names, gathered):
        return {n: _whole(lax.dynamic_update_index_in_dim(g4, own[n], me, 0), BIG[n]) for n, g4 in zip(names, gathered)}

    small = {n: wts[n].reshape(1, -1) for n in SMALL}
    pairs = {}

    def bwd_rider(grads_early):
        pairs["early"] = pair_sums([_by_shard(grads_early[n], BIG[n]) for n in early], "early")
        return scatter_rider(pairs["early"])

    def late_riders(grads_late):
        pairs["late"] = pair_sums([_by_shard(grads_late[n], BIG[n]) for n in LATE], "late")
        half = pairs["late"][0].shape[1]
        cuts = [0] + [half * f // 64 // 16 * 16 for f in LATE_CUTS] + [half]
        spans = [(cuts[i], cuts[i + 1] - cuts[i]) for i in range(3)]
        return (scatter_rider(pairs["late"], [spans[0], None, None, None]),
                scatter_rider(pairs["late"][:1], [spans[1]]), scatter_rider(pairs["late"][:1], [spans[2]]))

    loss, grad_x, gsmall, chips_early, (late_a, late_b, late_c) = local_step(
        x[0], loss_target[0], small, placed(LATE, gather_weights([own[n] for n in LATE])),
        gather_rider([own[n] for n in early]), lambda arrived: placed(early, arrived), bwd_rider, late_riders)

    chips_late = [jnp.concatenate([late_a[0], late_b[0], late_c[0]], axis=1)] + list(late_a[1:])
    wholes = owner_sums(pairs["early"] + pairs["late"], list(chips_early) + chips_late)
    grads = dict(zip(early + list(LATE), wholes))
    names = list(BIG)

    gsum = _unpack(allreduce_small(_pack(gsmall)), small)

    outs_g, outs_d, outs_m, outs_v = {}, {}, {}, {}
    for n in names:
        d, nm, nv = adamw(wts[n][0], grads[n], mom1[n][0], mom2[n][0])
        outs_g[n], outs_d[n], outs_m[n], outs_v[n] = (t[None] for t in (grads[n], d, nm, nv))
    pk = lambda src: _pack({n: src[n] for n in SMALL})
    d, nm, nv = adamw(pk(wts), _pack(gsum), pk(mom1), pk(mom2))
    du, mu, vu = _unpack(d, small), _unpack(nm, small), _unpack(nv, small)
    for n in SMALL:
        outs_g[n], outs_d[n], outs_m[n], outs_v[n] = (t[n].reshape(wts[n].shape) for t in (gsum, du, mu, vu))

    total = lax.psum(loss[0, 0], ("x", "y", "c"))
    return (total, grad_x[None], *[outs_g[n] for n in WEIGHTS], *[outs_d[n] for n in WEIGHTS],
            *[outs_m[n] for n in WEIGHTS], *[outs_v[n] for n in WEIGHTS])
```

```python
import jax
import jax.numpy as jnp
from jax import lax
from jax.experimental import pallas as pl
from jax.experimental.pallas import tpu as pltpu

F32 = jnp.float32
BF16 = jnp.bfloat16

LANES = 128
SUBLANES = 8
VMEM_LIMIT = 56 * 1024 * 1024

RW_H = 64
RW_N = 64
RW_C = RW_H * RW_N
RW_NB = RW_C // LANES
SCAN_CHUNK = 8


MESH = pl.DeviceIdType.MESH
ANY = pl.BlockSpec(memory_space=pl.ANY)


def _cparams(sem=None):
    return pltpu.CompilerParams(dimension_semantics=sem, vmem_limit_bytes=VMEM_LIMIT)


def _fold(x):
    return x + pltpu.roll(x, 64, axis=x.ndim - 1)


def _scan_step_fwd(t, src_ref, dst_ref, r_ref, w_ref, k_ref, a_ref, b_ref, v_ref):
    vt = v_ref[t]
    acc = jnp.zeros((RW_N, LANES), F32)
    for j in range(RW_NB):
        ls = slice(j * LANES, (j + 1) * LANES)
        acc = acc + src_ref[j] * a_ref[t:t + 1, ls]
    sa = _fold(acc)
    yacc = jnp.zeros((RW_N, LANES), F32)
    for j in range(RW_NB):
        ls = slice(j * LANES, (j + 1) * LANES)
        s_new = src_ref[j] * w_ref[t:t + 1, ls] + sa * b_ref[t:t + 1, ls] + vt * k_ref[t:t + 1, ls]
        dst_ref[j] = s_new
        yacc = yacc + s_new * r_ref[t:t + 1, ls]
    return _fold(yacc), sa


def _rider_parts(rider):
    if rider is None:
        return [], [], []
    return list(rider["ins"]), list(rider["out_shapes"]), list(rider["scratch"])


def rwkv_scan_fwd(r, w, k, a, b, v3, rider=None):
    s_len = r.shape[0]
    nchunk = s_len // SCAN_CHUNK
    x_in, x_out, x_scr = _rider_parts(rider)
    ni, no = len(x_in), len(x_out)

    def body(*refs):
        r_ref, w_ref, k_ref, a_ref, b_ref, v_ref = refs[:6]
        y_ref, hist_ref, sa_ref = refs[6 + ni:9 + ni]
        st_ref = refs[9 + ni + no]
        ride = (refs[6:6 + ni], refs[9 + ni:9 + ni + no], *refs[10 + ni + no:])

        @pl.when(pl.program_id(0) == 0)
        def _():
            st_ref[...] = jnp.zeros_like(st_ref)
            if rider is not None:
                rider["start"](*ride)

        for t in range(SCAN_CHUNK):
            y, sa = _scan_step_fwd(t, st_ref if t == 0 else hist_ref.at[t - 1], hist_ref.at[t],
                                   r_ref, w_ref, k_ref, a_ref, b_ref, v_ref)
            y_ref[t] = y
            sa_ref[t] = sa
        st_ref[...] = hist_ref[SCAN_CHUNK - 1]

        if rider is not None:
            @pl.when(pl.program_id(0) == nchunk - 1)
            def _():
                rider["finish"](*ride)

    row = pl.BlockSpec((SCAN_CHUNK, RW_C), lambda i: (i, 0))
    til = pl.BlockSpec((SCAN_CHUNK, RW_N, LANES), lambda i: (i, 0, 0))
    return pl.pallas_call(
        body,
        name="rwkv_scan_fwd",
        grid=(nchunk,),
        in_specs=[row, row, row, row, row, til] + [ANY] * ni,
        out_specs=[til, pl.BlockSpec((SCAN_CHUNK, RW_NB, RW_N, LANES), lambda i: (i, 0, 0, 0)), til] + [ANY] * no,
        out_shape=[
            jax.ShapeDtypeStruct((s_len, RW_N, LANES), F32),
            jax.ShapeDtypeStruct((s_len, RW_NB, RW_N, LANES), F32),
            jax.ShapeDtypeStruct((s_len, RW_N, LANES), F32),
        ] + x_out,
        scratch_shapes=[pltpu.VMEM((RW_NB, RW_N, LANES), F32)] + x_scr,
        compiler_params=_cparams(("arbitrary",)),
    )(r, w, k, a, b, v3, *x_in)


def rwkv_scan_bwd(r, w, k, a, b, v3, hist, sa3, dy3, rider=None):
    s_len = r.shape[0]
    nchunk = s_len // SCAN_CHUNK
    x_in, x_out, x_scr = _rider_parts(rider)
    ni, no = len(x_in), len(x_out)

    def body(*refs):
        r_ref, w_ref, k_ref, a_ref, b_ref, v_ref, hist_ref, prev_ref, sa_ref, dy_ref = refs[:10]
        dr_ref, dw_ref, dk_ref, da_ref, db_ref, dv_ref = refs[10 + ni:16 + ni]
        ds_ref = refs[16 + ni + no]
        ride = (refs[10:10 + ni], refs[16 + ni:16 + ni + no], *refs[17 + ni + no:])

        @pl.when(pl.program_id(0) == 0)
        def _():
            ds_ref[...] = jnp.zeros_like(ds_ref)
            if rider is not None:
                rider["start"](*ride)

        not_first = (pl.program_id(0) < nchunk - 1).astype(F32)

        def before(t, j):
            return prev_ref[0, j] * not_first if t == 0 else hist_ref[t - 1, j]

        for t in reversed(range(SCAN_CHUNK)):
            vt = v_ref[t]
            dyt = dy_ref[t]
            sat = sa_ref[t]
            dv_acc = jnp.zeros((RW_N, LANES), F32)
            dsa_acc = jnp.zeros((RW_N, LANES), F32)
            for j in range(RW_NB):
                ls = slice(j * LANES, (j + 1) * LANES)
                row = (slice(t, t + 1), ls)
                ds_j = ds_ref[j] + dyt * r_ref[row]
                ds_ref[j] = ds_j
                dr_ref[row] = jnp.sum(hist_ref[t, j] * dyt, axis=0, keepdims=True)
                dv_acc = dv_acc + ds_j * k_ref[row]
                dk_ref[row] = jnp.sum(ds_j * vt, axis=0, keepdims=True)
                dsa_acc = dsa_acc + ds_j * b_ref[row]
                db_ref[row] = jnp.sum(ds_j * sat, axis=0, keepdims=True)
                dw_ref[row] = jnp.sum(ds_j * before(t, j), axis=0, keepdims=True)
            dv_ref[t] = _fold(dv_acc)
            dsa = _fold(dsa_acc)
            for j in range(RW_NB):
                ls = slice(j * LANES, (j + 1) * LANES)
                row = (slice(t, t + 1), ls)
                da_ref[row] = jnp.sum(before(t, j) * dsa, axis=0, keepdims=True)
                ds_ref[j] = ds_ref[j] * w_ref[row] + dsa * a_ref[row]

        if rider is not None:
            @pl.when(pl.program_id(0) == nchunk - 1)
            def _():
                rider["finish"](*ride)

    rev = lambda i: (nchunk - 1 - i, 0)
    rev3 = lambda i: (nchunk - 1 - i, 0, 0)
    row = pl.BlockSpec((SCAN_CHUNK, RW_C), rev)
    til = pl.BlockSpec((SCAN_CHUNK, RW_N, LANES), rev3)
    rows = jax.ShapeDtypeStruct((s_len, RW_C), F32)
    return pl.pallas_call(
        body,
        name="rwkv_scan_bwd",
        grid=(nchunk,),
        in_specs=[row, row, row, row, row, til,
                  pl.BlockSpec((SCAN_CHUNK, RW_NB, RW_N, LANES), lambda i: (nchunk - 1 - i, 0, 0, 0)),
                  pl.BlockSpec((1, RW_NB, RW_N, LANES),
                               lambda i: (jnp.maximum((nchunk - 1 - i) * SCAN_CHUNK - 1, 0), 0, 0, 0)),
                  til, til] + [ANY] * ni,
        out_specs=[row, row, row, row, row, til] + [ANY] * no,
        out_shape=[rows, rows, rows, rows, rows, jax.ShapeDtypeStruct((s_len, RW_N, LANES), F32)] + x_out,
        scratch_shapes=[pltpu.VMEM((RW_NB, RW_N, LANES), F32)] + x_scr,
        compiler_params=_cparams(("arbitrary",)),
    )(r, w, k, a, b, v3, hist, hist, sa3, dy3, *x_in)


def _pick(n, cands):
    for c in cands:
        if n % c == 0:
            return c
    return n


MM_VMEM_BUDGET = 40 * 1024 * 1024
MM_FLOPS = 8.5e14
MM_HBM = 2.2e12
MM_STEP = 0.4e-6


def _mm_plan(m, n, k, out_bytes):
    divs = lambda d: [t for t in range(LANES, d + 1, LANES) if d % t == 0] or [d]
    best = None
    for tm in divs(m):
        for tn in divs(n):
            for tk in divs(k):
                nk = k // tk
                vmem = 4 * (tm * tk + tk * tn) + (4 * tm * tn if nk > 1 else 0) + 2 * tm * tn * out_bytes
                if vmem > MM_VMEM_BUDGET:
                    continue
                steps = (m // tm) * (n // tn) * nk
                for n_outer in (False, True):
                    if nk > 1:
                        traffic = steps * (tm * tk + tk * tn) * 2
                    elif n_outer:
                        traffic = (n // tn) * (k * tn + m * k) * 2
                    else:
                        traffic = (m // tm) * (tm * k + k * n) * 2
                    cost = max(2.0 * m * n * k / MM_FLOPS, (traffic + m * n * out_bytes) / MM_HBM) + steps * MM_STEP
                    if best is None or cost < best[0]:
                        best = (cost, tm, tn, tk, n_outer)
    return best[1:]


def matmul(a, b, *, ta=False, tb=False, out_dtype=F32, name="matmul", plan=None, rider=None):
    m, kdim = (a.shape[1], a.shape[0]) if ta else a.shape
    n = b.shape[0] if tb else b.shape[1]
    assert (b.shape[1] if tb else b.shape[0]) == kdim
    tm, tn, tk, n_outer = plan or _mm_plan(m, n, kdim, jnp.dtype(out_dtype).itemsize)
    nk = kdim // tk
    dims = (((0 if ta else 1,), (1 if tb else 0,)), ((), ()))
    grid = (n // tn, m // tm, nk) if n_outer else (m // tm, n // tn, nk)
    x_in, x_out, x_scr = _rider_parts(rider)
    ni, no = len(x_in), len(x_out)
    n_acc = 1 if nk > 1 else 0

    def body(*refs):
        a_ref, b_ref = refs[:2]
        o_ref = refs[2 + ni]
        ride = (refs[2:2 + ni], refs[3 + ni:3 + ni + no], *refs[3 + ni + no + n_acc:])
        ids = [pl.program_id(ax) for ax in range(3)]
        if rider is not None:
            @pl.when((ids[0] == 0) & (ids[1] == 0) & (ids[2] == 0))
            def _():
                rider["start"](*ride)

        prod = lax.dot_general(a_ref[...].astype(BF16), b_ref[...].astype(BF16), dims, preferred_element_type=F32)
        if nk == 1:
            o_ref[...] = prod.astype(o_ref.dtype)
        else:
            acc_ref = refs[3 + ni + no]
            kk = ids[2]

            @pl.when(kk == 0)
            def _():
                acc_ref[...] = prod

            @pl.when(kk > 0)
            def _():
                acc_ref[...] += prod

            @pl.when(kk == nk - 1)
            def _():
                o_ref[...] = acc_ref[...].astype(o_ref.dtype)

        if rider is not None:
            @pl.when((ids[0] == grid[0] - 1) & (ids[1] == grid[1] - 1) & (ids[2] == nk - 1))
            def _():
                rider["finish"](*ride)

    ij = (lambda p, q: (q, p)) if n_outer else (lambda p, q: (p, q))
    a_map = (lambda p, q, k: (k, ij(p, q)[0])) if ta else (lambda p, q, k: (ij(p, q)[0], k))
    b_map = (lambda p, q, k: (ij(p, q)[1], k)) if tb else (lambda p, q, k: (k, ij(p, q)[1]))
    out = pl.pallas_call(
        body,
        name=name,
        grid=grid,
        in_specs=[pl.BlockSpec((tk, tm) if ta else (tm, tk), a_map),
                  pl.BlockSpec((tn, tk) if tb else (tk, tn), b_map)] + [ANY] * ni,
        out_specs=[pl.BlockSpec((tm, tn), lambda p, q, k: ij(p, q))] + [ANY] * no,
        out_shape=[jax.ShapeDtypeStruct((m, n), out_dtype)] + x_out,
        scratch_shapes=([pltpu.VMEM((tm, tn), F32)] if nk > 1 else []) + x_scr,
        compiler_params=_cparams(("arbitrary",) * 3 if rider is not None else ("parallel", "parallel", "arbitrary")),
    )(a, b, *x_in)
    return out if rider is not None else out[0]


@jax.custom_vjp
def hsum(x):
    acc = x[:, 0:LANES]
    for j in range(1, RW_NB):
        acc = acc + x[:, j * LANES:(j + 1) * LANES]
    return _fold(acc)


def _hsum_fwd(x):
    return hsum(x), None


def _hsum_bwd(_, ct):
    return (jnp.concatenate([_fold(ct)] * RW_NB, axis=1),)


hsum.defvjp(_hsum_fwd, _hsum_bwd)


@jax.custom_vjp
def hbcast(s):
    return jnp.concatenate([s] * RW_NB, axis=1)


def _hbcast_fwd(s):
    return hbcast(s), None


def _hbcast_bwd(_, ct):
    acc = ct[:, 0:LANES]
    for j in range(1, RW_NB):
        acc = acc + ct[:, j * LANES:(j + 1) * LANES]
    return (acc,)


hbcast.defvjp(_hbcast_fwd, _hbcast_bwd)


@jax.custom_vjp
def bdot(x, w):
    return jnp.dot(x.astype(BF16), w, preferred_element_type=F32)


def _bdot_fwd(x, w):
    return bdot(x, w), w


def _bdot_bwd(w, ct):
    dx = lax.dot_general(ct.astype(BF16), w, (((1,), (1,)), ((), ())), preferred_element_type=F32)
    return dx, jnp.zeros_like(w)


bdot.defvjp(_bdot_fwd, _bdot_bwd)

RMS_EPS = 1e-6
GN_EPS = 64e-5


def f_rms(x, g):
    return x * lax.rsqrt(jnp.mean(x * x, axis=-1, keepdims=True) + RMS_EPS) * g


def _softplus(z):
    return jnp.maximum(z, 0.0) + jnp.log1p(jnp.exp(-jnp.abs(z)))


def f_pre(xk, xg, xw, xa, ew, ea, w0, a0, k_k, k_a, w2, a2, g2):
    tw = jnp.tanh(xw)
    sg = jax.nn.sigmoid(xg)
    wlog = -_softplus(-(w0 + bdot(tw, w2) + ew)) - 0.5
    decay = jnp.exp(-jnp.exp(wlog))
    a = jax.nn.sigmoid(a0 + bdot(xa, a2) + ea)
    g = bdot(sg, g2)
    kk0 = xk * k_k
    nrm = jnp.sqrt(hbcast(hsum(kk0 * kk0)))
    kk = kk0 / jnp.maximum(nrm, 1e-12)
    k = xk * (1.0 + (a - 1.0) * k_a)
    return decay, k, -kk, kk * a, g, tw, sg


def f_post(y, r, k, v, g, ln_w, ln_b, r_k):
    mu = hbcast(hsum(y)) * (1.0 / RW_N)
    yc = y - mu
    var = hbcast(hsum(yc * yc)) * (1.0 / RW_N)
    yn = yc * lax.rsqrt(var + GN_EPS) * ln_w + ln_b
    bonus = hbcast(hsum(r * k * r_k)) * v
    return (yn + bonus) * g


def f_merge(ga, gr, ab, rb):
    return jax.nn.sigmoid(ga) * ab + jax.nn.sigmoid(gr) * rb


def f_swiglu(gg, uu):
    return gg * jax.nn.sigmoid(gg) * uu


def _row(tt, width, cb=0, rev_n=None):
    if rev_n is None:
        return pl.BlockSpec((tt, width), lambda i: (i, cb))
    return pl.BlockSpec((tt, width), lambda i: (rev_n - 1 - i, cb))


def _full(arr):
    nd = arr.ndim
    return pl.BlockSpec(arr.shape, lambda i: (0,) * nd)


def _acc_init(i_first, *refs):
    @pl.when(i_first)
    def _():
        for r in refs:
            r[...] = jnp.zeros_like(r)


def rms_fwd(x, g, *, tt=128):
    s_len, d = x.shape

    def body(x_ref, g_ref, o_ref):
        o_ref[...] = f_rms(x_ref[...], g_ref[...]).astype(BF16)

    return pl.pallas_call(
        body, name="rms_fwd", grid=(s_len // tt,),
        in_specs=[_row(tt, d), _full(g)], out_specs=_row(tt, d),
        out_shape=jax.ShapeDtypeStruct((s_len, d), BF16),
        compiler_params=_cparams(("parallel",)),
    )(x, g)


def rwkv_pre_fwd(proj, mu, w0, a0, k_k, k_a, w2, a2, g2, *, tt=64):
    s_len, c = proj.shape
    nt = s_len // tt
    sub = tt // SUBLANES

    def body(p_ref, pb_ref, mu_ref, w0_ref, a0_ref, kk_ref, ka_ref, w2_ref, a2_ref, g2_ref,
             r_ref, dec_ref, k_ref, v_ref, av_ref, bv_ref, g_ref, tw_ref, xa_ref, sg_ref):
        i = pl.program_id(0)
        cur = p_ref[...]
        edge = jnp.where(i > 0, pb_ref[SUBLANES - 1:SUBLANES, :], 0.0)
        rows = lax.broadcasted_iota(jnp.int32, cur.shape, 0)
        prev = jnp.where(rows == 0, edge, pltpu.roll(cur, 1, axis=0))
        xs = cur + (prev - cur) * mu_ref[...]
        xr, xk, xv = xs[:, 0:RW_C], xs[:, RW_C:2 * RW_C], xs[:, 2 * RW_C:3 * RW_C]
        xg = xs[:, 3 * RW_C:3 * RW_C + 512]
        xw = xs[:, 3 * RW_C + 512:3 * RW_C + 640]
        xa = xs[:, 3 * RW_C + 640:3 * RW_C + 768]
        zero = jnp.zeros((tt, RW_C), F32)
        dec, k, av, bv, g, tw, sg = f_pre(xk, xg, xw, xa, zero, zero, w0_ref[...], a0_ref[...], kk_ref[...],
                                          ka_ref[...], w2_ref[...], a2_ref[...], g2_ref[...])
        r_ref[...] = xr
        dec_ref[...] = dec
        k_ref[...] = k
        v_ref[...] = xv
        av_ref[...] = av
        bv_ref[...] = bv
        g_ref[...] = g
        tw_ref[...] = tw.astype(BF16)
        xa_ref[...] = xa.astype(BF16)
        sg_ref[...] = sg.astype(BF16)

    rows_f = jax.ShapeDtypeStruct((s_len, RW_C), F32)
    prev_spec = pl.BlockSpec((SUBLANES, c), lambda i: (jnp.maximum(i * sub - 1, 0), 0))
    params = [mu, w0, a0, k_k, k_a, w2, a2, g2]
    return pl.pallas_call(
        body, name="rwkv_pre_fwd", grid=(nt,),
        in_specs=[_row(tt, c), prev_spec] + [_full(p) for p in params],
        out_specs=[_row(tt, RW_C)] * 7 + [_row(tt, 128), _row(tt, 128), _row(tt, 512)],
        out_shape=[rows_f] * 7 + [jax.ShapeDtypeStruct((s_len, 128), BF16), jax.ShapeDtypeStruct((s_len, 128), BF16),
                                  jax.ShapeDtypeStruct((s_len, 512), BF16)],
        compiler_params=_cparams(("parallel",)),
    )(proj, proj, *params)


def rwkv_pre_bwd(proj, mu, w0, a0, k_k, k_a, w2, a2, g2, d_r, d_dec, d_k, d_v, d_av, d_bv, d_g, d_r2, d_k2, d_v2,
                 *, tt=32):
    s_len, c = proj.shape
    nt = s_len // tt
    sub = tt // SUBLANES

    def body(p_ref, pb_ref, mu_ref, w0_ref, a0_ref, kk_ref, ka_ref, w2_ref, a2_ref, g2_ref,
             dr_ref, ddec_ref, dk_ref, dv_ref, dav_ref, dbv_ref, dg_ref, dr2_ref, dk2_ref, dv2_ref,
             dp_ref, dzw_ref, dza_ref, dmu_ref, dw0_ref, da0_ref, dkk_ref, dka_ref, carry_ref):
        step = pl.program_id(0)
        i = nt - 1 - step
        _acc_init(step == 0, dmu_ref, dw0_ref, da0_ref, dkk_ref, dka_ref, carry_ref)
        cur = p_ref[...]
        edge = jnp.where(i > 0, pb_ref[SUBLANES - 1:SUBLANES, :], 0.0)
        rows = lax.broadcasted_iota(jnp.int32, cur.shape, 0)
        prev = jnp.where(rows == 0, edge, pltpu.roll(cur, 1, axis=0))
        mu_v = mu_ref[...]
        xs = cur + (prev - cur) * mu_v
        xk = xs[:, RW_C:2 * RW_C]
        xg = xs[:, 3 * RW_C:3 * RW_C + 512]
        xw = xs[:, 3 * RW_C + 512:3 * RW_C + 640]
        xa = xs[:, 3 * RW_C + 640:3 * RW_C + 768]
        zero = jnp.zeros((tt, RW_C), F32)
        w2_v, a2_v, g2_v = w2_ref[...], a2_ref[...], g2_ref[...]

        def core(xk, xg, xw, xa, ew, ea, w0, a0, k_k, k_a):
            return f_pre(xk, xg, xw, xa, ew, ea, w0, a0, k_k, k_a, w2_v, a2_v, g2_v)[:5]

        _, vjp = jax.vjp(core, xk, xg, xw, xa, zero, zero, w0_ref[...], a0_ref[...], kk_ref[...], ka_ref[...])
        dxk, dxg, dxw, dxa, dzw, dza, dw0, da0, dkk, dka = vjp(
            (ddec_ref[...], dk_ref[...] + dk2_ref[...], dav_ref[...], dbv_ref[...], dg_ref[...]))
        dzw_ref[...] = dzw.astype(BF16)
        dza_ref[...] = dza.astype(BF16)
        dw0_ref[...] += dw0
        da0_ref[...] += da0
        dkk_ref[...] += dkk
        dka_ref[...] += dka
        dxs = jnp.concatenate([dr_ref[...] + dr2_ref[...], dxk, dv_ref[...] + dv2_ref[...], dxg, dxw, dxa], axis=1)
        dmu_ref[...] += jnp.sum(dxs * (prev - cur), axis=0, keepdims=True)
        to_prev = dxs * mu_v
        nxt = jnp.where(rows == tt - 1, carry_ref[...], pltpu.roll(to_prev, tt - 1, axis=0))
        carry_ref[...] = to_prev[0:1, :]
        dp_ref[...] = (dxs * (1.0 - mu_v) + nxt).astype(BF16)

    prev_spec = pl.BlockSpec((SUBLANES, c), lambda s: (jnp.maximum((nt - 1 - s) * sub - 1, 0), 0))
    params = [mu, w0, a0, k_k, k_a, w2, a2, g2]
    cts = [d_r, d_dec, d_k, d_v, d_av, d_bv, d_g, d_r2, d_k2, d_v2]
    vec = jax.ShapeDtypeStruct((1, RW_C), F32)
    acc = pl.BlockSpec((1, RW_C), lambda s: (0, 0))
    return pl.pallas_call(
        body, name="rwkv_pre_bwd", grid=(nt,),
        in_specs=[_row(tt, c, rev_n=nt), prev_spec] + [_full(p) for p in params] + [_row(tt, RW_C, rev_n=nt)] * 10,
        out_specs=[_row(tt, c, rev_n=nt), _row(tt, RW_C, rev_n=nt), _row(tt, RW_C, rev_n=nt),
                   pl.BlockSpec((1, c), lambda s: (0, 0)), acc, acc, acc, acc],
        out_shape=[jax.ShapeDtypeStruct((s_len, c), BF16), jax.ShapeDtypeStruct((s_len, RW_C), BF16),
                   jax.ShapeDtypeStruct((s_len, RW_C), BF16), jax.ShapeDtypeStruct((1, c), F32), vec, vec, vec, vec],
        scratch_shapes=[pltpu.VMEM((1, c), F32)],
        compiler_params=_cparams(("arbitrary",)),
    )(proj, proj, *params, *cts)


def rwkv_post_fwd(y, r, k, v, g, ln_w, ln_b, r_k, *, tt=64):
    s_len = y.shape[0]

    def body(y_ref, r_ref, k_ref, v_ref, g_ref, lw_ref, lb_ref, rk_ref, o_ref):
        o_ref[...] = f_post(y_ref[...], r_ref[...], k_ref[...], v_ref[...], g_ref[...],
                            lw_ref[...], lb_ref[...], rk_ref[...]).astype(BF16)

    return pl.pallas_call(
        body, name="rwkv_post_fwd", grid=(s_len // tt,),
        in_specs=[_row(tt, RW_C)] * 5 + [_full(ln_w), _full(ln_b), _full(r_k)],
        out_specs=_row(tt, RW_C), out_shape=jax.ShapeDtypeStruct((s_len, RW_C), BF16),
        compiler_params=_cparams(("parallel",)),
    )(y, r, k, v, g, ln_w, ln_b, r_k)


def rwkv_post_bwd(y, r, k, v, g, ln_w, ln_b, r_k, d_o, *, tt=32):
    s_len = y.shape[0]

    def body(y_ref, r_ref, k_ref, v_ref, g_ref, lw_ref, lb_ref, rk_ref, do_ref,
             dy_ref, dr_ref, dk_ref, dv_ref, dg_ref, dlw_ref, dlb_ref, drk_ref):
        _acc_init(pl.program_id(0) == 0, dlw_ref, dlb_ref, drk_ref)
        _, vjp = jax.vjp(f_post, y_ref[...], r_ref[...], k_ref[...], v_ref[...], g_ref[...],
                         lw_ref[...], lb_ref[...], rk_ref[...])
        dy, dr, dk, dv, dg, dlw, dlb, drk = vjp(do_ref[...].astype(F32))
        dy_ref[...] = dy
        dr_ref[...] = dr
        dk_ref[...] = dk
        dv_ref[...] = dv
        dg_ref[...] = dg
        dlw_ref[...] += dlw
        dlb_ref[...] += dlb
        drk_ref[...] += drk

    rows_f = jax.ShapeDtypeStruct((s_len, RW_C), F32)
    vec = jax.ShapeDtypeStruct((1, RW_C), F32)
    acc = pl.BlockSpec((1, RW_C), lambda s: (0, 0))
    return pl.pallas_call(
        body, name="rwkv_post_bwd", grid=(s_len // tt,),
        in_specs=[_row(tt, RW_C)] * 5 + [_full(ln_w), _full(ln_b), _full(r_k), _row(tt, RW_C)],
        out_specs=[_row(tt, RW_C)] * 5 + [acc] * 3, out_shape=[rows_f] * 5 + [vec] * 3,
        compiler_params=_cparams(("arbitrary",)),
    )(y, r, k, v, g, ln_w, ln_b, r_k, d_o)


def merge_fwd(gate, ab, rb, *, tt=128):
    s_len, d = ab.shape

    def body(ga_ref, gr_ref, a_ref, r_ref, o_ref):
        o_ref[...] = f_merge(ga_ref[...], gr_ref[...], a_ref[...], r_ref[...]).astype(BF16)

    return pl.pallas_call(
        body, name="merge_fwd", grid=(s_len // tt,),
        in_specs=[_row(tt, d, 0), _row(tt, d, 1), _row(tt, d), _row(tt, d)],
        out_specs=_row(tt, d), out_shape=jax.ShapeDtypeStruct((s_len, d), BF16),
        compiler_params=_cparams(("parallel",)),
    )(gate, gate, ab, rb)


def merge_bwd(gate, ab, rb, d_m, *, tt=64):
    s_len, d = ab.shape

    def body(ga_ref, gr_ref, a_ref, r_ref, dm_ref, dgate_ref, da_ref, dr_ref):
        _, vjp = jax.vjp(f_merge, ga_ref[...], gr_ref[...], a_ref[...], r_ref[...])
        dga, dgr, da, dr = vjp(dm_ref[...].astype(F32))
        dgate_ref[:, 0:d] = dga.astype(BF16)
        dgate_ref[:, d:2 * d] = dgr.astype(BF16)
        da_ref[...] = da.astype(BF16)
        dr_ref[...] = dr.astype(BF16)

    return pl.pallas_call(
        body, name="merge_bwd", grid=(s_len // tt,),
        in_specs=[_row(tt, d, 0), _row(tt, d, 1), _row(tt, d), _row(tt, d), _row(tt, d)],
        out_specs=[_row(tt, 2 * d), _row(tt, d), _row(tt, d)],
        out_shape=[jax.ShapeDtypeStruct((s_len, 2 * d), BF16), jax.ShapeDtypeStruct((s_len, d), BF16),
                   jax.ShapeDtypeStruct((s_len, d), BF16)],
        compiler_params=_cparams(("parallel",)),
    )(gate, gate, ab, rb, d_m)


def swiglu_fwd(gg, uu, *, tt=64):
    s_len, f = gg.shape

    def body(g_ref, u_ref, o_ref):
        o_ref[...] = f_swiglu(g_ref[...], u_ref[...]).astype(BF16)

    return pl.pallas_call(
        body, name="swiglu_fwd", grid=(s_len // tt,),
        in_specs=[_row(tt, f), _row(tt, f)], out_specs=_row(tt, f),
        out_shape=jax.ShapeDtypeStruct((s_len, f), BF16),
        compiler_params=_cparams(("parallel",)),
    )(gg, uu)


def swiglu_bwd(gg, uu, d_act, *, tt=32):
    s_len, f = gg.shape

    def body(g_ref, u_ref, d_ref, dg_ref, du_ref):
        _, vjp = jax.vjp(f_swiglu, g_ref[...], u_ref[...])
        dg, du = vjp(d_ref[...].astype(F32))
        dg_ref[...] = dg.astype(BF16)
        du_ref[...] = du.astype(BF16)

    out = jax.ShapeDtypeStruct((s_len, f), BF16)
    return pl.pallas_call(
        body, name="swiglu_bwd", grid=(s_len // tt,),
        in_specs=[_row(tt, f)] * 3, out_specs=[_row(tt, f)] * 2, out_shape=[out, out],
        compiler_params=_cparams(("parallel",)),
    )(gg, uu, d_act)


def resid_norm_fwd(x, m2, g_post, g_pre, *, tt=128):
    s_len, d = x.shape

    def body(x_ref, m_ref, gp_ref, gn_ref, x1_ref, h_ref):
        x1 = x_ref[...] + f_rms(m_ref[...], gp_ref[...])
        x1_ref[...] = x1
        h_ref[...] = f_rms(x1, gn_ref[...]).astype(BF16)

    return pl.pallas_call(
        body, name="resid_norm_fwd", grid=(s_len // tt,),
        in_specs=[_row(tt, d), _row(tt, d), _full(g_post), _full(g_pre)],
        out_specs=[_row(tt, d), _row(tt, d)],
        out_shape=[jax.ShapeDtypeStruct((s_len, d), F32), jax.ShapeDtypeStruct((s_len, d), BF16)],
        compiler_params=_cparams(("parallel",)),
    )(x, m2, g_post, g_pre)


def loss_head(x1, ff, tgt, g_post, *, tt=64):
    s_len, d = x1.shape

    def body(x1_ref, f_ref, t_ref, g_ref, loss_ref, dy_ref, df_ref, dg_ref):
        _acc_init(pl.program_id(0) == 0, loss_ref, dg_ref)
        nrm, vjp = jax.vjp(f_rms, f_ref[...], g_ref[...])
        err = x1_ref[...] + nrm - t_ref[...]
        per_tok = jnp.mean(err * err, axis=-1, keepdims=True)
        loss_ref[...] += 0.5 * jnp.sum(per_tok, axis=0, keepdims=True)
        dy = err * (1.0 / d)
        dff, dg = vjp(dy)
        dy_ref[...] = dy
        df_ref[...] = dff.astype(BF16)
        dg_ref[...] += dg

    return pl.pallas_call(
        body, name="loss_head", grid=(s_len // tt,),
        in_specs=[_row(tt, d)] * 3 + [_full(g_post)],
        out_specs=[pl.BlockSpec((1, LANES), lambda s: (0, 0)), _row(tt, d), _row(tt, d),
                   pl.BlockSpec((1, d), lambda s: (0, 0))],
        out_shape=[jax.ShapeDtypeStruct((1, LANES), F32), jax.ShapeDtypeStruct((s_len, d), F32),
                   jax.ShapeDtypeStruct((s_len, d), BF16), jax.ShapeDtypeStruct((1, d), F32)],
        compiler_params=_cparams(("arbitrary",)),
    )(x1, ff, tgt, g_post)


def resid_norm_bwd(x1, dh_a, dh_b, g_pre, m2, g_post, dy, *, tt=64):
    s_len, d = x1.shape

    def body(x1_ref, da_ref, db_ref, gn_ref, m_ref, gp_ref, dy_ref, dx1_ref, dm_ref, dgn_ref, dgp_ref):
        _acc_init(pl.program_id(0) == 0, dgn_ref, dgp_ref)
        _, vjp_n = jax.vjp(f_rms, x1_ref[...], gn_ref[...])
        dx1_n, dgn = vjp_n(da_ref[...] + db_ref[...])
        dx1 = dy_ref[...] + dx1_n
        _, vjp_p = jax.vjp(f_rms, m_ref[...], gp_ref[...])
        dm, dgp = vjp_p(dx1)
        dx1_ref[...] = dx1
        dm_ref[...] = dm.astype(BF16)
        dgn_ref[...] += dgn
        dgp_ref[...] += dgp

    acc = pl.BlockSpec((1, d), lambda s: (0, 0))
    vec = jax.ShapeDtypeStruct((1, d), F32)
    return pl.pallas_call(
        body, name="resid_norm_bwd", grid=(s_len // tt,),
        in_specs=[_row(tt, d)] * 3 + [_full(g_pre), _row(tt, d), _full(g_post), _row(tt, d)],
        out_specs=[_row(tt, d), _row(tt, d), acc, acc],
        out_shape=[jax.ShapeDtypeStruct((s_len, d), F32), jax.ShapeDtypeStruct((s_len, d), BF16), vec, vec],
        compiler_params=_cparams(("arbitrary",)),
    )(x1, dh_a, dh_b, g_pre, m2, g_post, dy)


def rms_bwd(x, g, dh_a, dh_b, dh_c, dres, *, tt=64):
    s_len, d = x.shape

    def body(x_ref, g_ref, a_ref, b_ref, c_ref, r_ref, dx_ref, dg_ref):
        _acc_init(pl.program_id(0) == 0, dg_ref)
        _, vjp = jax.vjp(f_rms, x_ref[...], g_ref[...])
        dx, dg = vjp(a_ref[...] + b_ref[...] + c_ref[...])
        dx_ref[...] = r_ref[...] + dx
        dg_ref[...] += dg

    return pl.pallas_call(
        body, name="rms_bwd", grid=(s_len // tt,),
        in_specs=[_row(tt, d), _full(g)] + [_row(tt, d)] * 4,
        out_specs=[_row(tt, d), pl.BlockSpec((1, d), lambda s: (0, 0))],
        out_shape=[jax.ShapeDtypeStruct((s_len, d), F32), jax.ShapeDtypeStruct((1, d), F32)],
        compiler_params=_cparams(("arbitrary",)),
    )(x, g, dh_a, dh_b, dh_c, dres)


def colsum(a, *, tt=256):
    s_len, c = a.shape

    def body(a_ref, o_ref):
        _acc_init(pl.program_id(0) == 0, o_ref)
        o_ref[...] += jnp.sum(a_ref[...].astype(F32), axis=0, keepdims=True)

    return pl.pallas_call(
        body, name="colsum", grid=(s_len // tt,),
        in_specs=[_row(tt, c)], out_specs=pl.BlockSpec((1, c), lambda s: (0, 0)),
        out_shape=jax.ShapeDtypeStruct((1, c), F32),
        compiler_params=_cparams(("arbitrary",)),
    )(a)


AT_HD = 128
AT_GROUP = 4
AT_KVH = 8
AT_BLK = 128
AT_QW = AT_GROUP * AT_HD
AT_KCOL = AT_KVH * AT_GROUP
AT_VCOL = AT_KCOL + AT_KVH
NEG_INF = -1e30
AT_SCALE = AT_HD ** -0.5


def _rope(t, cos2, sin2):
    return t * cos2 + pltpu.roll(t, AT_HD // 2, axis=1) * sin2


def _rope_t(d, cos2, sin2):
    return d * cos2 + pltpu.roll(d * sin2, AT_HD // 2, axis=1)


def _att_specs():
    prev = lambda i: jnp.maximum(i - 1, 0)
    blk = (AT_BLK, AT_HD)
    return [
        pl.BlockSpec((AT_BLK, AT_QW), lambda h, i: (i, h)),
        pl.BlockSpec(blk, lambda h, i: (i, AT_KCOL + h)),
        pl.BlockSpec(blk, lambda h, i: (prev(i), AT_KCOL + h)),
        pl.BlockSpec(blk, lambda h, i: (i, AT_VCOL + h)),
        pl.BlockSpec(blk, lambda h, i: (prev(i), AT_VCOL + h)),
        pl.BlockSpec((1, AT_QW), lambda h, i: (0, h)),
        pl.BlockSpec((1, AT_HD), lambda h, i: (0, AT_KCOL + h)),
        pl.BlockSpec((1, AT_HD), lambda h, i: (0, AT_VCOL + h)),
        pl.BlockSpec((1, AT_GROUP, AT_HD), lambda h, i: (h, 0, 0)),
        pl.BlockSpec(blk, lambda h, i: (i, 0)),
        pl.BlockSpec(blk, lambda h, i: (i, 0)),
        pl.BlockSpec(blk, lambda h, i: (prev(i), 0)),
        pl.BlockSpec(blk, lambda h, i: (prev(i), 0)),
    ]


def _att_load(i, q_ref, kc_ref, kp_ref, vc_ref, vp_ref, bq_ref, bk_ref, bv_ref, cc_ref, sc_ref, cp_ref, sp_ref):
    cosc, sinc = cc_ref[...], sc_ref[...]
    q = q_ref[...] + bq_ref[...]
    kc = _rope(kc_ref[...] + bk_ref[...], cosc, sinc)
    kp = _rope(kp_ref[...] + bk_ref[...], cp_ref[...], sp_ref[...])
    kcat = jnp.concatenate([kp, kc], axis=0).astype(BF16)
    vcat = jnp.concatenate([vp_ref[...] + bv_ref[...], vc_ref[...] + bv_ref[...]], axis=0).astype(BF16)
    qi = lax.broadcasted_iota(jnp.int32, (AT_GROUP * AT_BLK, 2 * AT_BLK), 0) & (AT_BLK - 1)
    kj = lax.broadcasted_iota(jnp.int32, (AT_GROUP * AT_BLK, 2 * AT_BLK), 1)
    rel = qi + AT_BLK - kj
    mask = (rel >= 0) & (rel < AT_BLK) & ((kj >= AT_BLK) | (i > 0))
    return q, kcat, vcat, mask, cosc, sinc


AT_ROWS = AT_GROUP * AT_BLK


def _att_stack(q, cosc, sinc):
    return jnp.concatenate([_rope(q[:, g * AT_HD:(g + 1) * AT_HD], cosc, sinc) for g in range(AT_GROUP)], axis=0)


def _att_cols(sk_ref):
    head = lax.broadcasted_iota(jnp.int32, (AT_ROWS, 1), 0) >> 7
    sink = jnp.zeros((AT_ROWS, 1), F32)
    for g in range(AT_GROUP):
        sink = jnp.where(head == g, sk_ref[0, g:g + 1, 0:1], sink)
    return sink, head


def _att_probs(qs, kcat, mask, sink):
    s = lax.dot_general(qs, kcat, (((1,), (1,)), ((), ())), preferred_element_type=F32) * AT_SCALE
    s = jnp.where(mask, s, NEG_INF)
    m = jnp.maximum(jnp.max(s, axis=-1, keepdims=True), sink)
    p = jnp.exp(s - m)
    es = jnp.exp(sink - m)
    inv = 1.0 / (jnp.sum(p, axis=-1, keepdims=True) + es)
    return p * inv, es * inv


def attention_fwd(qkv, bias, sinks_b, cos2, sin2):
    s_len = qkv.shape[0]
    nb = s_len // AT_BLK

    def body(q_ref, kc_ref, kp_ref, vc_ref, vp_ref, bq_ref, bk_ref, bv_ref, sk_ref, cc_ref, sc_ref, cp_ref, sp_ref,
             o_ref):
        i = pl.program_id(1)
        q, kcat, vcat, mask, cosc, sinc = _att_load(i, q_ref, kc_ref, kp_ref, vc_ref, vp_ref, bq_ref, bk_ref,
                                                    bv_ref, cc_ref, sc_ref, cp_ref, sp_ref)
        sink, _ = _att_cols(sk_ref)
        probs, _ = _att_probs(_att_stack(q, cosc, sinc).astype(BF16), kcat, mask, sink)
        o = jnp.dot(probs.astype(BF16), vcat, preferred_element_type=F32).astype(BF16)
        for g in range(AT_GROUP):
            o_ref[:, g * AT_HD:(g + 1) * AT_HD] = o[g * AT_BLK:(g + 1) * AT_BLK, :]

    return pl.pallas_call(
        body, name="attention_fwd", grid=(AT_KVH, nb),
        in_specs=_att_specs(),
        out_specs=pl.BlockSpec((AT_BLK, AT_QW), lambda h, i: (i, h)),
        out_shape=jax.ShapeDtypeStruct((s_len, AT_KVH * AT_QW), BF16),
        compiler_params=_cparams(("parallel", "parallel")),
    )(qkv, qkv, qkv, qkv, qkv, bias, bias, bias, sinks_b, cos2, sin2, cos2, sin2)


def attention_bwd(qkv, bias, sinks_b, cos2, sin2, d_o, rider=None):
    s_len = qkv.shape[0]
    nb = s_len // AT_BLK
    x_in, x_out, x_scr = _rider_parts(rider)
    ni, no = len(x_in), len(x_out)

    def body(*refs):
        (q_ref, kc_ref, kp_ref, vc_ref, vp_ref, bq_ref, bk_ref, bv_ref, sk_ref, cc_ref, sc_ref, cp_ref, sp_ref,
         do_ref) = refs[:14]
        dq_ref, dk_ref, dv_ref, dsk_ref = refs[14 + ni:18 + ni]
        ride = (refs[14:14 + ni], refs[18 + ni:18 + ni + no], *refs[18 + ni + no:])
        i = pl.program_id(1)
        if rider is not None:
            @pl.when((pl.program_id(0) == 0) & (i == 0))
            def _():
                rider["start"](*ride)

        _acc_init(i == 0, dsk_ref)
        q, kcat, vcat, mask, cosc, sinc = _att_load(i, q_ref, kc_ref, kp_ref, vc_ref, vp_ref, bq_ref, bk_ref,
                                                    bv_ref, cc_ref, sc_ref, cp_ref, sp_ref)
        sink, head = _att_cols(sk_ref)
        qs = _att_stack(q, cosc, sinc).astype(BF16)
        probs, psink = _att_probs(qs, kcat, mask, sink)
        pb = probs.astype(BF16)
        do_s = jnp.concatenate([do_ref[:, g * AT_HD:(g + 1) * AT_HD] for g in range(AT_GROUP)], axis=0)
        do_f = do_s.astype(F32)
        do_b = do_s.astype(BF16)
        o_s = jnp.dot(pb, vcat, preferred_element_type=F32)
        dsum = jnp.sum(do_f * o_s, axis=-1, keepdims=True)
        dp = lax.dot_general(do_b, vcat, (((1,), (1,)), ((), ())), preferred_element_type=F32)
        ds = (probs * (dp - dsum) * AT_SCALE).astype(BF16)
        dv_cat = lax.dot_general(pb, do_b, (((0,), (0,)), ((), ())), preferred_element_type=F32)
        dk_cat = lax.dot_general(ds, qs, (((0,), (0,)), ((), ())), preferred_element_type=F32)
        dq_s = jnp.dot(ds, kcat, preferred_element_type=F32)
        lane = lax.broadcasted_iota(jnp.int32, (1, AT_HD), 1)
        dsk = jnp.zeros((1, AT_HD), F32)
        sink_term = psink * dsum
        for g in range(AT_GROUP):
            rows = slice(g * AT_BLK, (g + 1) * AT_BLK)
            dq_ref[:, g * AT_HD:(g + 1) * AT_HD] = _rope_t(dq_s[rows, :], cosc, sinc).astype(BF16)
            dsk = dsk + jnp.where(lane == g, -jnp.sum(sink_term[rows, :], axis=0, keepdims=True), 0.0)
        dsk_ref[0] += dsk
        cur = pl.ds(pl.multiple_of(i * AT_BLK, AT_BLK), AT_BLK)
        dk_ref[cur, :] = _rope_t(dk_cat[AT_BLK:], cosc, sinc)
        dv_ref[cur, :] = dv_cat[AT_BLK:]

        @pl.when(i > 0)
        def _():
            prv = pl.ds(pl.multiple_of((i - 1) * AT_BLK, AT_BLK), AT_BLK)
            dk_ref[prv, :] += _rope_t(dk_cat[:AT_BLK], cp_ref[...], sp_ref[...])
            dv_ref[prv, :] += dv_cat[:AT_BLK]

        if rider is not None:
            @pl.when((pl.program_id(0) == AT_KVH - 1) & (i == nb - 1))
            def _():
                rider["finish"](*ride)

    kv_out = pl.BlockSpec((s_len, AT_HD), lambda h, i: (0, h))
    return pl.pallas_call(
        body, name="attention_bwd", grid=(AT_KVH, nb),
        in_specs=_att_specs() + [pl.BlockSpec((AT_BLK, AT_QW), lambda h, i: (i, h))] + [ANY] * ni,
        out_specs=[pl.BlockSpec((AT_BLK, AT_QW), lambda h, i: (i, h)), kv_out, kv_out,
                   pl.BlockSpec((1, 1, AT_HD), lambda h, i: (h, 0, 0))] + [ANY] * no,
        out_shape=[jax.ShapeDtypeStruct((s_len, AT_KVH * AT_QW), BF16),
                   jax.ShapeDtypeStruct((s_len, AT_KVH * AT_HD), F32),
                   jax.ShapeDtypeStruct((s_len, AT_KVH * AT_HD), F32),
                   jax.ShapeDtypeStruct((AT_KVH, 1, AT_HD), F32)] + x_out,
        scratch_shapes=x_scr,
        compiler_params=_cparams(("arbitrary", "arbitrary")),
    )(qkv, qkv, qkv, qkv, qkv, bias, bias, bias, sinks_b, cos2, sin2, cos2, sin2, d_o, *x_in)


ATT_QKV = 6144
RW_SHIFT = 13024
RW_PAD = 13056
D_GATE = 480
ROPE_THETA = 10000.0


def perm_cols(a):
    lead = a.shape[:-1]
    return jnp.swapaxes(a.reshape(lead + (RW_H, RW_N)), -1, -2).reshape(lead + (RW_C,))


def rw_reorder(a, pad_value=0):
    r, k, v = (perm_cols(a[..., i * RW_C:(i + 1) * RW_C]) for i in range(3))
    wd = a[..., 3 * RW_C:3 * RW_C + 128]
    ad = a[..., 3 * RW_C + 128:3 * RW_C + 256]
    gd = a[..., 3 * RW_C + 256:]
    pad = jnp.full(a.shape[:-1] + (512 - D_GATE,), pad_value, a.dtype)
    return jnp.concatenate([r, k, v, gd, pad, wd, ad], axis=-1)


def rw_restore(a):
    r, k, v = (perm_cols(a[..., i * RW_C:(i + 1) * RW_C]) for i in range(3))
    gd = a[..., 3 * RW_C:3 * RW_C + D_GATE]
    wd = a[..., 3 * RW_C + 512:3 * RW_C + 640]
    ad = a[..., 3 * RW_C + 640:3 * RW_C + 768]
    return jnp.concatenate([r, k, v, wd, ad, gd], axis=-1)


def to_tiles(a):
    t = a.reshape(a.shape[0], RW_N, RW_H)
    return jnp.concatenate([t, t], axis=-1)


def from_tiles(t):
    return t[:, :, :RW_H].reshape(t.shape[0], RW_C)


def rope_tables(s_len):
    pos = jnp.arange(s_len, dtype=F32)
    inv_freq = ROPE_THETA ** (-jnp.arange(0, AT_HD, 2, dtype=F32) / AT_HD)
    ang = pos[:, None] * inv_freq[None, :]
    cos, sin = jnp.cos(ang), jnp.sin(ang)
    return jnp.concatenate([cos, cos], axis=1), jnp.concatenate([-sin, sin], axis=1)


def local_step(x, tgt, small, big, fwd_rider, got_early, att_rider, bwd_rider, late_riders):
    s_len, d = x.shape
    w_in = big["w_in"]
    w_qkv = w_in[:, :ATT_QKV]
    w_rw = rw_reorder(w_in[:, ATT_QKV:ATT_QKV + RW_SHIFT])
    w_gate = w_in[:, ATT_QKV + RW_SHIFT:]
    w2 = perm_cols(big["w2"])
    a2 = perm_cols(big["a2"])
    g2 = jnp.pad(perm_cols(big["g2"]), ((0, 512 - D_GATE), (0, 0)))
    mu = rw_reorder(small["mu_shift"])
    w0, a0, k_k, k_a, ln_w, ln_b = (perm_cols(small[n]) for n in ("w0", "a0", "k_k", "k_a", "ln_x_w", "ln_x_b"))
    r_k = small["r_k"].reshape(RW_H, RW_N).T.reshape(1, RW_C)
    sinks_b = jnp.broadcast_to(small["att_sinks"].reshape(AT_KVH, AT_GROUP, 1), (AT_KVH, AT_GROUP, AT_HD))
    cos2, sin2 = rope_tables(s_len)
    bias = small["b_qkv"]

    h = rms_fwd(x, small["norm_mix_pre"])
    qkv = matmul(h, w_qkv, name="mm_qkv")
    prw = matmul(h, w_rw, name="mm_rw")
    gate = matmul(h, w_gate, name="mm_gate")
    o_att = attention_fwd(qkv, bias, sinks_b, cos2, sin2)
    pre_params = (mu, w0, a0, k_k, k_a, w2, a2, g2)
    r, dec, k, v, av, bv, g, tw, xa, sg = rwkv_pre_fwd(prw, *pre_params)
    v3 = to_tiles(v)
    y3, hist, sa3, *arrived = rwkv_scan_fwd(r, dec, k, av, bv, v3, rider=fwd_rider)
    big = {**big, **got_early(arrived)}
    w_rb = big["w_rwkv_branch"].reshape(RW_H, RW_N, d).swapaxes(0, 1).reshape(RW_C, d)
    y = from_tiles(y3)
    o_rw = rwkv_post_fwd(y, r, k, v, g, ln_w, ln_b, r_k)
    ab = matmul(o_att, big["w_att_branch"], name="mm_ab")
    rb = matmul(o_rw, w_rb, name="mm_rb")
    merged = merge_fwd(gate, ab, rb)
    m2 = matmul(merged, big["w_out"], name="mm_out")
    x1, h2 = resid_norm_fwd(x, m2, small["norm_mix_post"], small["norm_ffn_pre"])
    gg = matmul(h2, big["w_ffn_gate"], name="mm_fg")
    uu = matmul(h2, big["w_ffn_up"], name="mm_fu")
    act = swiglu_fwd(gg, uu)
    ff = matmul(act, big["w_ffn_down"], name="mm_fd")
    loss, dy, dff, d_nfp = loss_head(x1, ff, tgt, small["norm_ffn_post"])

    dact = matmul(dff, big["w_ffn_down"], tb=True, name="mm_dact")
    g_fd = matmul(act, dff, ta=True, out_dtype=BF16, name="mm_gfd")
    dgg, duu = swiglu_bwd(gg, uu, dact)
    g_fg = matmul(h2, dgg, ta=True, out_dtype=BF16, name="mm_gfg")
    g_fu = matmul(h2, duu, ta=True, out_dtype=BF16, name="mm_gfu")
    dh2a = matmul(dgg, big["w_ffn_gate"], tb=True, name="mm_dh2a")
    dh2b = matmul(duu, big["w_ffn_up"], tb=True, name="mm_dh2b")
    dx1, dm2, d_nfpre, d_nmpost = resid_norm_bwd(x1, dh2a, dh2b, small["norm_ffn_pre"], m2, small["norm_mix_post"], dy)
    dmerged = matmul(dm2, big["w_out"], tb=True, name="mm_dmerged")
    g_out = matmul(merged, dm2, ta=True, out_dtype=BF16, name="mm_gout")
    dgate, dab, drb = merge_bwd(gate, ab, rb, dmerged)
    do_att = matmul(dab, big["w_att_branch"], tb=True, out_dtype=BF16, name="mm_doatt")
    g_ab = matmul(o_att, dab, ta=True, out_dtype=BF16, name="mm_gab")
    do_rw = matmul(drb, w_rb, tb=True, name="mm_dorw")
    g_rb = matmul(o_rw, drb, ta=True, out_dtype=BF16, name="mm_grb")
    early = {"w_att_branch": g_ab, "w_rwkv_branch": g_rb.reshape(RW_N, RW_H, d).swapaxes(0, 1).reshape(RW_C, d),
             "w_out": g_out, "w_ffn_gate": g_fg, "w_ffn_up": g_fu, "w_ffn_down": g_fd}
    dq, dk_att, dv_att, dsk, *from_sibling = attention_bwd(qkv, bias, sinks_b, cos2, sin2, do_att,
                                                           rider=att_rider(early))
    dqkv = jnp.concatenate([dq, dk_att.astype(BF16), dv_att.astype(BF16)], axis=1)
    dy_s, dr_p, dk_p, dv_p, dg, d_lnw, d_lnb, d_rk = rwkv_post_bwd(y, r, k, v, g, ln_w, ln_b, r_k, do_rw)
    dr_s, ddec, dk_s, dav, dbv, dv3, *from_chips = rwkv_scan_bwd(r, dec, k, av, bv, v3, hist, sa3, to_tiles(dy_s),
                                                                  rider=bwd_rider(from_sibling))
    dprw, dzw, dza, dmu, dw0, da0, dkk, dka = rwkv_pre_bwd(
        prw, *pre_params, dr_p, ddec, dk_p, dv_p, dav, dbv, dg, dr_s, dk_s, from_tiles(dv3))
    g_w2 = matmul(tw, dzw, ta=True, out_dtype=BF16, name="mm_gw2")
    g_a2 = matmul(xa, dza, ta=True, out_dtype=BF16, name="mm_ga2")
    g_g2 = matmul(sg, dg.astype(BF16), ta=True, out_dtype=BF16, name="mm_gg2")
    g_qkv = matmul(h, dqkv, ta=True, out_dtype=BF16, name="mm_gqkv")
    g_rw = matmul(h, dprw, ta=True, out_dtype=BF16, name="mm_grw")
    g_gate = matmul(h, dgate, ta=True, out_dtype=BF16, name="mm_ggate")
    gbig = {
        "w_in": jnp.concatenate([g_qkv, rw_restore(g_rw), g_gate], axis=1),
        "w2": perm_cols(g_w2), "a2": perm_cols(g_a2), "g2": perm_cols(g_g2)[:D_GATE],
    }
    ride_a, ride_b, ride_c = late_riders(gbig)
    dh_a, *late_a = matmul(dqkv, w_qkv, tb=True, name="mm_dha", rider=ride_a)
    dh_b, *late_b = matmul(dprw, w_rw, tb=True, name="mm_dhb", rider=ride_b)
    dh_c, *late_c = matmul(dgate, w_gate, tb=True, name="mm_dhc", rider=ride_c)
    grad_x, d_nmpre = rms_bwd(x, small["norm_mix_pre"], dh_a, dh_b, dh_c, dx1)
    d_bias = colsum(dqkv)

    gsmall = {
        "norm_mix_pre": d_nmpre, "norm_mix_post": d_nmpost, "norm_ffn_pre": d_nfpre, "norm_ffn_post": d_nfp,
        "b_qkv": d_bias, "att_sinks": dsk[:, 0, :AT_GROUP].reshape(1, AT_KVH * AT_GROUP),
        "mu_shift": rw_restore(dmu), "w0": perm_cols(dw0), "a0": perm_cols(da0), "k_k": perm_cols(dkk),
        "k_a": perm_cols(dka), "r_k": d_rk.reshape(RW_N, RW_H).T.reshape(1, RW_C),
        "ln_x_w": perm_cols(d_lnw), "ln_x_b": perm_cols(d_lnb),
    }
    return loss, grad_x, gsmall, from_chips, (late_a, late_b, late_c)


def _place():
    x, y, c = lax.axis_index("x"), lax.axis_index("y"), lax.axis_index("c")
    chips = [(1 - x, y), (x, 1 - y), (1 - x, 1 - y)]
    return x, y, c, chips


def _remote(src, dst, send_sems, recv_sems, k, dev):
    return pltpu.make_async_remote_copy(src_ref=src, dst_ref=dst, send_sem=send_sems.at[k], recv_sem=recv_sems.at[k],
                                        device_id=dev, device_id_type=MESH)


def _gather_parts(n):
    def half(ref, which):
        hr = ref.shape[0] // 2
        return ref.at[pl.ds(which * hr, hr), :]

    def sends(ins, outs, send_sems, recv_sems):
        x, y, c, chips = _place()
        me = 2 * x + y
        return [_remote(half(ins[i], c), half(outs[i].at[me], c), send_sems, recv_sems, 6 * i + j, (*chip, c))
                for i in range(n) for j, chip in enumerate(chips)]

    def start(ins, outs, send_sems, recv_sems):
        for cp in sends(ins, outs, send_sems, recv_sems):
            cp.start()

    def finish(ins, outs, send_sems, recv_sems):
        x, y, c, chips = _place()
        sib = (x, y, 1 - c)
        passed = []
        for i in range(n):
            for j, chip in enumerate(chips):
                got = half(outs[i].at[2 * chip[0] + chip[1]], c)
                _remote(got, got, send_sems, recv_sems, 6 * i + j, sib).wait_recv()
                cp = _remote(got, got, send_sems, recv_sems, 6 * i + 3 + j, sib)
                cp.start()
                passed.append(cp)
        for i in range(n):
            for j, chip in enumerate(chips):
                got = half(outs[i].at[2 * chip[0] + chip[1]], 1 - c)
                _remote(got, got, send_sems, recv_sems, 6 * i + 3 + j, sib).wait_recv()
        for cp in sends(ins, outs, send_sems, recv_sems) + passed:
            cp.wait_send()

    return start, finish


def gather_rider(shards):
    n = len(shards)
    start, finish = _gather_parts(n)
    return {"ins": shards, "out_shapes": [jax.ShapeDtypeStruct((4,) + s.shape, s.dtype) for s in shards],
            "scratch": [pltpu.SemaphoreType.DMA((6 * n,)), pltpu.SemaphoreType.DMA((6 * n,))],
            "start": start, "finish": finish}


def gather_weights(shards):
    n = len(shards)
    start, finish = _gather_parts(n)

    def body(*refs):
        parts = (refs[:n], refs[n:2 * n], *refs[2 * n:])
        start(*parts)
        finish(*parts)

    return pl.pallas_call(
        body, name="gather_weights",
        in_specs=[ANY] * n, out_specs=[ANY] * n,
        out_shape=[jax.ShapeDtypeStruct((4,) + s.shape, s.dtype) for s in shards],
        scratch_shapes=[pltpu.SemaphoreType.DMA((6 * n,)), pltpu.SemaphoreType.DMA((6 * n,))],
    )(*shards)


def swap_with_sibling(blocks, name):
    n = len(blocks)

    def body(*refs):
        ins, outs = refs[:n], refs[n:2 * n]
        send_sems, recv_sems = refs[2 * n:]
        x, y, c, _ = _place()
        cps = [_remote(ins[i], outs[i], send_sems, recv_sems, i, (x, y, 1 - c)) for i in range(n)]
        for cp in cps:
            cp.start()
        for cp in cps:
            cp.wait()

    return pl.pallas_call(
        body, name=name, in_specs=[ANY] * n, out_specs=[ANY] * n,
        out_shape=[jax.ShapeDtypeStruct(b.shape, b.dtype) for b in blocks],
        scratch_shapes=[pltpu.SemaphoreType.DMA((n,)), pltpu.SemaphoreType.DMA((n,))],
    )(*blocks)


def _scatter_parts(n, rows=None):
    def piece(ref, i):
        return ref if rows is None or rows[i] is None else ref.at[pl.ds(rows[i][0], rows[i][1]), :]

    def copies(ins, outs, send_sems, recv_sems):
        x, y, c, chips = _place()
        return [_remote(piece(ins[i].at[2 * chip[0] + chip[1]], i), outs[i].at[j], send_sems, recv_sems, 3 * i + j,
                        (*chip, c))
                for i in range(n) for j, chip in enumerate(chips)]

    def start(*refs):
        for cp in copies(*refs):
            cp.start()

    def finish(*refs):
        for cp in copies(*refs):
            cp.wait()

    return start, finish


def scatter_rider(parts, rows=None):
    n = len(parts)
    start, finish = _scatter_parts(n, rows)
    nrows = [p.shape[1] if rows is None or rows[i] is None else rows[i][1] for i, p in enumerate(parts)]
    return {"ins": parts, "out_shapes": [jax.ShapeDtypeStruct((3, r, p.shape[2]), p.dtype) for p, r in zip(parts, nrows)],
            "scratch": [pltpu.SemaphoreType.DMA((3 * n,)), pltpu.SemaphoreType.DMA((3 * n,))],
            "start": start, "finish": finish}


def allreduce_small(v):
    rows = v.shape[0]

    def body(v_ref, o_ref, buf, send_sems, recv_sems):
        x, y, c, chips = _place()
        me, sib = (x, y, c), (x, y, 1 - c)

        def slot(px, py, pc):
            return buf.at[4 * px + 2 * py + pc]

        def copy(k, block, to, src=None):
            return _remote(slot(*block) if src is None else src, slot(*block), send_sems, recv_sems, k, to)

        buf[4 * x + 2 * y + c] = v_ref[...]
        first = [copy(0, me, sib, src=v_ref)]
        first += [copy(1 + j, me, (*chip, c), src=v_ref) for j, chip in enumerate(chips)]
        for cp in first:
            cp.start()
        passed = [copy(4 + j, (*chip, c), sib) for j, chip in enumerate(chips)]
        for j, chip in enumerate(chips):
            copy(1 + j, (*chip, c), me).wait_recv()
            passed[j].start()
        copy(0, sib, me).wait_recv()
        for j, chip in enumerate(chips):
            copy(4 + j, (*chip, 1 - c), me).wait_recv()
        for cp in first + passed:
            cp.wait_send()
        acc = buf[0]
        for k in range(1, 8):
            acc = acc + buf[k]
        o_ref[...] = acc

    vm = pl.BlockSpec(memory_space=pltpu.VMEM)
    return pl.pallas_call(
        body, name="allreduce_small", in_specs=[vm], out_specs=vm,
        out_shape=jax.ShapeDtypeStruct(v.shape, F32),
        scratch_shapes=[pltpu.VMEM((8, rows, LANES), F32), pltpu.SemaphoreType.DMA((7,)),
                        pltpu.SemaphoreType.DMA((7,))],
    )(v)


def _rows_tile(r, most=64):
    return _pick(r, tuple(t for t in (256, 128, 64, 32, 16, 8) if t <= most))


def _swap_parts(n):
    def copies(ins, outs, send_sems, recv_sems):
        x, y, c, _ = _place()
        return [_remote(ins[i].at[:, pl.ds((1 - c) * (ins[i].shape[1] // 2), ins[i].shape[1] // 2), :], outs[i],
                        send_sems, recv_sems, i, (x, y, 1 - c)) for i in range(n)]

    def start(*refs):
        for cp in copies(*refs):
            cp.start()

    def finish(*refs):
        for cp in copies(*refs):
            cp.wait()

    return start, finish


def swap_rider(blocks):
    n = len(blocks)
    start, finish = _swap_parts(n)
    return {"ins": blocks, "out_shapes": [jax.ShapeDtypeStruct((4, b.shape[1] // 2, b.shape[2]), b.dtype) for b in blocks],
            "scratch": [pltpu.SemaphoreType.DMA((n,)), pltpu.SemaphoreType.DMA((n,))], "start": start, "finish": finish}


def swap_halves(blocks, name):
    n = len(blocks)
    start, finish = _swap_parts(n)

    def body(*refs):
        parts = (refs[:n], refs[n:2 * n], *refs[2 * n:])
        start(*parts)
        finish(*parts)

    return pl.pallas_call(
        body, name=name, in_specs=[ANY] * n, out_specs=[ANY] * n,
        out_shape=[jax.ShapeDtypeStruct((4, b.shape[1] // 2, b.shape[2]), b.dtype) for b in blocks],
        scratch_shapes=[pltpu.SemaphoreType.DMA((n,)), pltpu.SemaphoreType.DMA((n,))],
    )(*blocks)


def add_pairs(g4, b, core):
    _, r, c = b.shape
    tr = _rows_tile(r, 256)
    nrt = r // tr

    def body(core_ref, a_ref, b_ref, o_ref):
        o_ref[...] = (a_ref[...].astype(F32) + b_ref[...].astype(F32)).astype(o_ref.dtype)

    spec = pl.BlockSpec((1, tr, c), lambda s, i, core_ref: (s, i, 0))
    return pl.pallas_call(
        body, name="add_pairs",
        grid_spec=pltpu.PrefetchScalarGridSpec(
            num_scalar_prefetch=1, grid=(4, nrt),
            in_specs=[pl.BlockSpec((1, tr, c), lambda s, i, core_ref: (s, core_ref[0] * nrt + i, 0)), spec],
            out_specs=spec),
        out_shape=jax.ShapeDtypeStruct(b.shape, b.dtype), compiler_params=_cparams(("parallel", "parallel")),
    )(core, g4, b)


def add_four(mine, others):
    r, c = mine.shape
    tr = _rows_tile(r, 128)

    def body(m_ref, o_ref, out_ref):
        acc = m_ref[...].astype(F32)
        for j in range(3):
            acc = acc + o_ref[j].astype(F32)
        out_ref[...] = acc

    return pl.pallas_call(
        body, name="add_four", grid=(r // tr,),
        in_specs=[pl.BlockSpec((tr, c), lambda i: (i, 0)), pl.BlockSpec((3, tr, c), lambda i: (0, i, 0))],
        out_specs=pl.BlockSpec((tr, c), lambda i: (i, 0)),
        out_shape=jax.ShapeDtypeStruct((r, c), F32), compiler_params=_cparams(("parallel",)),
    )(mine, others)


def pair_sums(blocks, from_sibling=None):
    core = lax.axis_index("c").astype(jnp.int32).reshape(1)
    if from_sibling is None:
        from_sibling = swap_halves(blocks, "swap_halves_late")
    return [add_pairs(g4, b, core) for g4, b in zip(blocks, from_sibling)]


def owner_sums(pair, from_chips):
    cx, cy, cc = lax.axis_index("x"), lax.axis_index("y"), lax.axis_index("c")
    me = 2 * cx + cy
    sums = [add_four(lax.dynamic_index_in_dim(p, me, 0, keepdims=False), t) for p, t in zip(pair, from_chips)]
    got = swap_with_sibling(sums, "swap_sums")
    return [jnp.concatenate([jnp.where(cc == 0, s, g), jnp.where(cc == 0, g, s)], axis=0) for s, g in zip(sums, got)]


ADAM_LR = 0.001
ADAM_B1 = 0.9
ADAM_B2 = 0.999
ADAM_EPS = 1e-08
ADAM_WD = 0.01
ADAM_STEP = 10


def adamw(w, g, m, v):
    r, c = w.shape
    tr = _rows_tile(r)
    spec = pl.BlockSpec((tr, c), lambda i: (i, 0))

    def body(w_ref, g_ref, m_ref, v_ref, d_ref, nm_ref, nv_ref):
        gv = g_ref[...]
        nm = ADAM_B1 * m_ref[...] + (1.0 - ADAM_B1) * gv
        nv = ADAM_B2 * v_ref[...] + (1.0 - ADAM_B2) * jnp.square(gv)
        m_hat = nm / (1.0 - ADAM_B1 ** ADAM_STEP)
        v_hat = nv / (1.0 - ADAM_B2 ** ADAM_STEP)
        d_ref[...] = -ADAM_LR * (m_hat / (jnp.sqrt(v_hat) + ADAM_EPS) + ADAM_WD * w_ref[...])
        nm_ref[...] = nm
        nv_ref[...] = nv

    out = jax.ShapeDtypeStruct((r, c), F32)
    return pl.pallas_call(
        body, name="adamw", grid=(r // tr,), in_specs=[spec] * 4, out_specs=[spec] * 3, out_shape=[out] * 3,
        compiler_params=_cparams(("parallel",)),
    )(w, g, m, v)


WEIGHTS = ["norm_mix_pre", "norm_mix_post", "norm_ffn_pre", "norm_ffn_post", "w_in", "b_qkv", "att_sinks", "mu_shift",
           "w0", "w2", "a0", "a2", "g2", "k_k", "k_a", "r_k", "ln_x_w", "ln_x_b", "w_att_branch", "w_rwkv_branch",
           "w_out", "w_ffn_gate", "w_ffn_up", "w_ffn_down"]
BIG = {"w_in": 1, "w2": 1, "a2": 1, "g2": 1, "w_att_branch": 0, "w_rwkv_branch": 0, "w_out": 0, "w_ffn_gate": 1,
       "w_ffn_up": 1, "w_ffn_down": 0}
SMALL = [n for n in WEIGHTS if n not in BIG]
LATE_CUTS = (13, 45)
LATE = ("w_in", "w2", "a2", "g2")
N_CHIPS = 4


def _whole(g4, axis):
    if axis == 0:
        return g4.reshape(g4.shape[0] * g4.shape[1], g4.shape[2])
    return jnp.swapaxes(g4, 0, 1).reshape(g4.shape[1], g4.shape[0] * g4.shape[2])


def _by_shard(w, axis):
    if axis == 0:
        return w.reshape(N_CHIPS, w.shape[0] // N_CHIPS, w.shape[1])
    return jnp.swapaxes(w.reshape(w.shape[0], N_CHIPS, w.shape[1] // N_CHIPS), 0, 1)


def _pack(parts):
    flat = jnp.concatenate([parts[n].reshape(-1) for n in SMALL])
    rows = -(-flat.shape[0] // (LANES * SUBLANES)) * SUBLANES
    return jnp.pad(flat, (0, rows * LANES - flat.shape[0])).reshape(rows, LANES)


def _unpack(packed, like):
    flat = packed.reshape(-1)
    out, off = {}, 0
    for n in SMALL:
        size = like[n].size
        out[n] = flat[off:off + size].reshape(like[n].shape)
        off += size
    return out


def kernel(x, norm_mix_pre, norm_mix_post, norm_ffn_pre, norm_ffn_post, w_in, b_qkv, att_sinks, mu_shift, w0, w2, a0, a2, g2, k_k, k_a, r_k, ln_x_w, ln_x_b, w_att_branch, w_rwkv_branch, w_out, w_ffn_gate, w_ffn_up, w_ffn_down, loss_target, m_norm_mix_pre, m_norm_mix_post, m_norm_ffn_pre, m_norm_ffn_post, m_w_in, m_b_qkv, m_att_sinks, m_mu_shift, m_w0, m_w2, m_a0, m_a2, m_g2, m_k_k, m_k_a, m_r_k, m_ln_x_w, m_ln_x_b, m_w_att_branch, m_w_rwkv_branch, m_w_out, m_w_ffn_gate, m_w_ffn_up, m_w_ffn_down, v_norm_mix_pre, v_norm_mix_post, v_norm_ffn_pre, v_norm_ffn_post, v_w_in, v_b_qkv, v_att_sinks, v_mu_shift, v_w0, v_w2, v_a0, v_a2, v_g2, v_k_k, v_k_a, v_r_k, v_ln_x_w, v_ln_x_b, v_w_att_branch, v_w_rwkv_branch, v_w_out, v_w_ffn_gate, v_w_ffn_up, v_w_ffn_down):
    given = dict(locals())
    wts = {n: given[n] for n in WEIGHTS}
    mom1 = {n: given["m_" + n] for n in WEIGHTS}
    mom2 = {n: given["v_" + n] for n in WEIGHTS}
    early = [n for n in BIG if n not in LATE]
    me = 2 * lax.axis_index("x") + lax.axis_index("y")
    own = {n: wts[n][0].astype(BF16) for n in BIG}

    def placed(names, gathered):
        return {n: _whole(lax.dynamic_update_index_in_dim(g4, own[n], me, 0), BIG[n]) for n, g4 in zip(names, gathered)}

    small = {n: wts[n].reshape(1, -1) for n in SMALL}
    pairs = {}

    def att_rider(grads_early):
        pairs["blocks"] = [_by_shard(grads_early[n], BIG[n]) for n in early]
        return swap_rider(pairs["blocks"])

    def bwd_rider(from_sibling):
        pairs["early"] = pair_sums(pairs["blocks"], from_sibling)
        return scatter_rider(pairs["early"])

    def late_riders(grads_late):
        pairs["late"] = pair_sums([_by_shard(grads_late[n], BIG[n]) for n in LATE])
        half = pairs["late"][0].shape[1]
        cuts = [0] + [half * f // 64 // 16 * 16 for f in LATE_CUTS] + [half]
        spans = [(cuts[i], cuts[i + 1] - cuts[i]) for i in range(3)]
        return (scatter_rider(pairs["late"], [spans[0], None, None, None]),
                scatter_rider(pairs["late"][:1], [spans[1]]), scatter_rider(pairs["late"][:1], [spans[2]]))

    loss, grad_x, gsmall, chips_early, (late_a, late_b, late_c) = local_step(
        x[0], loss_target[0], small, placed(LATE, gather_weights([own[n] for n in LATE])),
        gather_rider([own[n] for n in early]), lambda arrived: placed(early, arrived), att_rider, bwd_rider,
        late_riders)

    chips_late = [jnp.concatenate([late_a[0], late_b[0], late_c[0]], axis=1)] + list(late_a[1:])
    wholes = owner_sums(pairs["early"] + pairs["late"], list(chips_early) + chips_late)
    grads = dict(zip(early + list(LATE), wholes))
    names = list(BIG)

    gsum = _unpack(allreduce_small(_pack(gsmall)), small)

    outs_g, outs_d, outs_m, outs_v = {}, {}, {}, {}
    for n in names:
        d, nm, nv = adamw(wts[n][0], grads[n], mom1[n][0], mom2[n][0])
        outs_g[n], outs_d[n], outs_m[n], outs_v[n] = (t[None] for t in (grads[n], d, nm, nv))
    pk = lambda src: _pack({n: src[n] for n in SMALL})
    d, nm, nv = adamw(pk(wts), _pack(gsum), pk(mom1), pk(mom2))
    du, mu, vu = _unpack(d, small), _unpack(nm, small), _unpack(nv, small)
    for n in SMALL:
        outs_g[n], outs_d[n], outs_m[n], outs_v[n] = (t[n].reshape(wts[n].shape) for t in (gsum, du, mu, vu))

    total = lax.psum(loss[0, 0], ("x", "y", "c"))
    return (total, grad_x[None], *[outs_g[n] for n in WEIGHTS], *[outs_d[n] for n in WEIGHTS],
            *[outs_m[n] for n in WEIGHTS], *[outs_v[n] for n in WEIGHTS])
```

```python
import jax
import jax.numpy as jnp
from jax import lax
from jax.experimental import pallas as pl
from jax.experimental.pallas import tpu as pltpu

F32 = jnp.float32
BF16 = jnp.bfloat16

LANES = 128
SUBLANES = 8
VMEM_LIMIT = 56 * 1024 * 1024

RW_H = 64
RW_N = 64
RW_C = RW_H * RW_N
RW_NB = RW_C // LANES
SCAN_CHUNK = 8


MESH = pl.DeviceIdType.MESH
ANY = pl.BlockSpec(memory_space=pl.ANY)


def _cparams(sem=None):
    return pltpu.CompilerParams(dimension_semantics=sem, vmem_limit_bytes=VMEM_LIMIT)


def _fold(x):
    return x + pltpu.roll(x, 64, axis=x.ndim - 1)


def _scan_step_fwd(t, src_ref, dst_ref, r_ref, w_ref, k_ref, a_ref, b_ref, v_ref):
    vt = v_ref[t]
    acc = jnp.zeros((RW_N, LANES), F32)
    for j in range(RW_NB):
        ls = slice(j * LANES, (j + 1) * LANES)
        acc = acc + src_ref[j] * a_ref[t:t + 1, ls]
    sa = _fold(acc)
    yacc = jnp.zeros((RW_N, LANES), F32)
    for j in range(RW_NB):
        ls = slice(j * LANES, (j + 1) * LANES)
        s_new = src_ref[j] * w_ref[t:t + 1, ls] + sa * b_ref[t:t + 1, ls] + vt * k_ref[t:t + 1, ls]
        dst_ref[j] = s_new
        yacc = yacc + s_new * r_ref[t:t + 1, ls]
    return _fold(yacc), sa


def _rider_parts(rider):
    if rider is None:
        return [], [], []
    return list(rider["ins"]), list(rider["out_shapes"]), list(rider["scratch"])


def rwkv_scan_fwd(r, w, k, a, b, v3, rider=None):
    s_len = r.shape[0]
    nchunk = s_len // SCAN_CHUNK
    x_in, x_out, x_scr = _rider_parts(rider)
    ni, no = len(x_in), len(x_out)

    def body(*refs):
        r_ref, w_ref, k_ref, a_ref, b_ref, v_ref = refs[:6]
        y_ref, hist_ref, sa_ref = refs[6 + ni:9 + ni]
        st_ref = refs[9 + ni + no]
        ride = (refs[6:6 + ni], refs[9 + ni:9 + ni + no], *refs[10 + ni + no:])

        @pl.when(pl.program_id(0) == 0)
        def _():
            st_ref[...] = jnp.zeros_like(st_ref)
            if rider is not None:
                rider["start"](*ride)

        for t in range(SCAN_CHUNK):
            y, sa = _scan_step_fwd(t, st_ref if t == 0 else hist_ref.at[t - 1], hist_ref.at[t],
                                   r_ref, w_ref, k_ref, a_ref, b_ref, v_ref)
            y_ref[t] = y
            sa_ref[t] = sa
        st_ref[...] = hist_ref[SCAN_CHUNK - 1]

        if rider is not None:
            @pl.when(pl.program_id(0) == nchunk - 1)
            def _():
                rider["finish"](*ride)

    row = pl.BlockSpec((SCAN_CHUNK, RW_C), lambda i: (i, 0))
    til = pl.BlockSpec((SCAN_CHUNK, RW_N, LANES), lambda i: (i, 0, 0))
    return pl.pallas_call(
        body,
        name="rwkv_scan_fwd",
        grid=(nchunk,),
        in_specs=[row, row, row, row, row, til] + [ANY] * ni,
        out_specs=[til, pl.BlockSpec((SCAN_CHUNK, RW_NB, RW_N, LANES), lambda i: (i, 0, 0, 0)), til] + [ANY] * no,
        out_shape=[
            jax.ShapeDtypeStruct((s_len, RW_N, LANES), F32),
            jax.ShapeDtypeStruct((s_len, RW_NB, RW_N, LANES), F32),
            jax.ShapeDtypeStruct((s_len, RW_N, LANES), F32),
        ] + x_out,
        scratch_shapes=[pltpu.VMEM((RW_NB, RW_N, LANES), F32)] + x_scr,
        compiler_params=_cparams(("arbitrary",)),
    )(r, w, k, a, b, v3, *x_in)


def rwkv_scan_bwd(r, w, k, a, b, v3, hist, sa3, dy3, rider=None):
    s_len = r.shape[0]
    nchunk = s_len // SCAN_CHUNK
    x_in, x_out, x_scr = _rider_parts(rider)
    ni, no = len(x_in), len(x_out)

    def body(*refs):
        r_ref, w_ref, k_ref, a_ref, b_ref, v_ref, hist_ref, prev_ref, sa_ref, dy_ref = refs[:10]
        dr_ref, dw_ref, dk_ref, da_ref, db_ref, dv_ref = refs[10 + ni:16 + ni]
        ds_ref = refs[16 + ni + no]
        ride = (refs[10:10 + ni], refs[16 + ni:16 + ni + no], *refs[17 + ni + no:])

        @pl.when(pl.program_id(0) == 0)
        def _():
            ds_ref[...] = jnp.zeros_like(ds_ref)
            if rider is not None:
                rider["start"](*ride)

        not_first = (pl.program_id(0) < nchunk - 1).astype(F32)

        def before(t, j):
            return prev_ref[0, j] * not_first if t == 0 else hist_ref[t - 1, j]

        for t in reversed(range(SCAN_CHUNK)):
            vt = v_ref[t]
            dyt = dy_ref[t]
            sat = sa_ref[t]
            dv_acc = jnp.zeros((RW_N, LANES), F32)
            dsa_acc = jnp.zeros((RW_N, LANES), F32)
            for j in range(RW_NB):
                ls = slice(j * LANES, (j + 1) * LANES)
                row = (slice(t, t + 1), ls)
                ds_j = ds_ref[j] + dyt * r_ref[row]
                ds_ref[j] = ds_j
                dr_ref[row] = jnp.sum(hist_ref[t, j] * dyt, axis=0, keepdims=True)
                dv_acc = dv_acc + ds_j * k_ref[row]
                dk_ref[row] = jnp.sum(ds_j * vt, axis=0, keepdims=True)
                dsa_acc = dsa_acc + ds_j * b_ref[row]
                db_ref[row] = jnp.sum(ds_j * sat, axis=0, keepdims=True)
                dw_ref[row] = jnp.sum(ds_j * before(t, j), axis=0, keepdims=True)
            dv_ref[t] = _fold(dv_acc)
            dsa = _fold(dsa_acc)
            for j in range(RW_NB):
                ls = slice(j * LANES, (j + 1) * LANES)
                row = (slice(t, t + 1), ls)
                da_ref[row] = jnp.sum(before(t, j) * dsa, axis=0, keepdims=True)
                ds_ref[j] = ds_ref[j] * w_ref[row] + dsa * a_ref[row]

        if rider is not None:
            @pl.when(pl.program_id(0) == nchunk - 1)
            def _():
                rider["finish"](*ride)

    rev = lambda i: (nchunk - 1 - i, 0)
    rev3 = lambda i: (nchunk - 1 - i, 0, 0)
    row = pl.BlockSpec((SCAN_CHUNK, RW_C), rev)
    til = pl.BlockSpec((SCAN_CHUNK, RW_N, LANES), rev3)
    rows = jax.ShapeDtypeStruct((s_len, RW_C), F32)
    return pl.pallas_call(
        body,
        name="rwkv_scan_bwd",
        grid=(nchunk,),
        in_specs=[row, row, row, row, row, til,
                  pl.BlockSpec((SCAN_CHUNK, RW_NB, RW_N, LANES), lambda i: (nchunk - 1 - i, 0, 0, 0)),
                  pl.BlockSpec((1, RW_NB, RW_N, LANES),
                               lambda i: (jnp.maximum((nchunk - 1 - i) * SCAN_CHUNK - 1, 0), 0, 0, 0)),
                  til, til] + [ANY] * ni,
        out_specs=[row, row, row, row, row, til] + [ANY] * no,
        out_shape=[rows, rows, rows, rows, rows, jax.ShapeDtypeStruct((s_len, RW_N, LANES), F32)] + x_out,
        scratch_shapes=[pltpu.VMEM((RW_NB, RW_N, LANES), F32)] + x_scr,
        compiler_params=_cparams(("arbitrary",)),
    )(r, w, k, a, b, v3, hist, hist, sa3, dy3, *x_in)


def _pick(n, cands):
    for c in cands:
        if n % c == 0:
            return c
    return n


MM_VMEM_BUDGET = 40 * 1024 * 1024
MM_FLOPS = 8.5e14
MM_HBM = 2.2e12
MM_STEP = 0.4e-6


def _mm_plan(m, n, k, out_bytes):
    divs = lambda d: [t for t in range(LANES, d + 1, LANES) if d % t == 0] or [d]
    best = None
    for tm in divs(m):
        for tn in divs(n):
            for tk in divs(k):
                nk = k // tk
                vmem = 4 * (tm * tk + tk * tn) + (4 * tm * tn if nk > 1 else 0) + 2 * tm * tn * out_bytes
                if vmem > MM_VMEM_BUDGET:
                    continue
                steps = (m // tm) * (n // tn) * nk
                for n_outer in (False, True):
                    if nk > 1:
                        traffic = steps * (tm * tk + tk * tn) * 2
                    elif n_outer:
                        traffic = (n // tn) * (k * tn + m * k) * 2
                    else:
                        traffic = (m // tm) * (tm * k + k * n) * 2
                    cost = max(2.0 * m * n * k / MM_FLOPS, (traffic + m * n * out_bytes) / MM_HBM) + steps * MM_STEP
                    if best is None or cost < best[0]:
                        best = (cost, tm, tn, tk, n_outer)
    return best[1:]


def matmul(a, b, *, ta=False, tb=False, out_dtype=F32, name="matmul", plan=None, rider=None):
    m, kdim = (a.shape[1], a.shape[0]) if ta else a.shape
    n = b.shape[0] if tb else b.shape[1]
    assert (b.shape[1] if tb else b.shape[0]) == kdim
    tm, tn, tk, n_outer = plan or _mm_plan(m, n, kdim, jnp.dtype(out_dtype).itemsize)
    nk = kdim // tk
    dims = (((0 if ta else 1,), (1 if tb else 0,)), ((), ()))
    grid = (n // tn, m // tm, nk) if n_outer else (m // tm, n // tn, nk)
    x_in, x_out, x_scr = _rider_parts(rider)
    ni, no = len(x_in), len(x_out)
    n_acc = 1 if nk > 1 else 0

    def body(*refs):
        a_ref, b_ref = refs[:2]
        o_ref = refs[2 + ni]
        ride = (refs[2:2 + ni], refs[3 + ni:3 + ni + no], *refs[3 + ni + no + n_acc:])
        ids = [pl.program_id(ax) for ax in range(3)]
        if rider is not None:
            @pl.when((ids[0] == 0) & (ids[1] == 0) & (ids[2] == 0))
            def _():
                rider["start"](*ride)

        prod = lax.dot_general(a_ref[...].astype(BF16), b_ref[...].astype(BF16), dims, preferred_element_type=F32)
        if nk == 1:
            o_ref[...] = prod.astype(o_ref.dtype)
        else:
            acc_ref = refs[3 + ni + no]
            kk = ids[2]

            @pl.when(kk == 0)
            def _():
                acc_ref[...] = prod

            @pl.when(kk > 0)
            def _():
                acc_ref[...] += prod

            @pl.when(kk == nk - 1)
            def _():
                o_ref[...] = acc_ref[...].astype(o_ref.dtype)

        if rider is not None:
            @pl.when((ids[0] == grid[0] - 1) & (ids[1] == grid[1] - 1) & (ids[2] == nk - 1))
            def _():
                rider["finish"](*ride)

    ij = (lambda p, q: (q, p)) if n_outer else (lambda p, q: (p, q))
    a_map = (lambda p, q, k: (k, ij(p, q)[0])) if ta else (lambda p, q, k: (ij(p, q)[0], k))
    b_map = (lambda p, q, k: (ij(p, q)[1], k)) if tb else (lambda p, q, k: (k, ij(p, q)[1]))
    out = pl.pallas_call(
        body,
        name=name,
        grid=grid,
        in_specs=[pl.BlockSpec((tk, tm) if ta else (tm, tk), a_map),
                  pl.BlockSpec((tn, tk) if tb else (tk, tn), b_map)] + [ANY] * ni,
        out_specs=[pl.BlockSpec((tm, tn), lambda p, q, k: ij(p, q))] + [ANY] * no,
        out_shape=[jax.ShapeDtypeStruct((m, n), out_dtype)] + x_out,
        scratch_shapes=([pltpu.VMEM((tm, tn), F32)] if nk > 1 else []) + x_scr,
        compiler_params=_cparams(("arbitrary",) * 3 if rider is not None else ("parallel", "parallel", "arbitrary")),
    )(a, b, *x_in)
    return out if rider is not None else out[0]


@jax.custom_vjp
def hsum(x):
    acc = x[:, 0:LANES]
    for j in range(1, RW_NB):
        acc = acc + x[:, j * LANES:(j + 1) * LANES]
    return _fold(acc)


def _hsum_fwd(x):
    return hsum(x), None


def _hsum_bwd(_, ct):
    return (jnp.concatenate([_fold(ct)] * RW_NB, axis=1),)


hsum.defvjp(_hsum_fwd, _hsum_bwd)


@jax.custom_vjp
def hbcast(s):
    return jnp.concatenate([s] * RW_NB, axis=1)


def _hbcast_fwd(s):
    return hbcast(s), None


def _hbcast_bwd(_, ct):
    acc = ct[:, 0:LANES]
    for j in range(1, RW_NB):
        acc = acc + ct[:, j * LANES:(j + 1) * LANES]
    return (acc,)


hbcast.defvjp(_hbcast_fwd, _hbcast_bwd)


@jax.custom_vjp
def bdot(x, w):
    return jnp.dot(x.astype(BF16), w, preferred_element_type=F32)


def _bdot_fwd(x, w):
    return bdot(x, w), w


def _bdot_bwd(w, ct):
    dx = lax.dot_general(ct.astype(BF16), w, (((1,), (1,)), ((), ())), preferred_element_type=F32)
    return dx, jnp.zeros_like(w)


bdot.defvjp(_bdot_fwd, _bdot_bwd)

RMS_EPS = 1e-6
GN_EPS = 64e-5


def f_rms(x, g):
    return x * lax.rsqrt(jnp.mean(x * x, axis=-1, keepdims=True) + RMS_EPS) * g


def _softplus(z):
    return jnp.maximum(z, 0.0) + jnp.log1p(jnp.exp(-jnp.abs(z)))


def f_pre(xk, xg, xw, xa, ew, ea, w0, a0, k_k, k_a, w2, a2, g2):
    tw = jnp.tanh(xw)
    sg = jax.nn.sigmoid(xg)
    wlog = -_softplus(-(w0 + bdot(tw, w2) + ew)) - 0.5
    decay = jnp.exp(-jnp.exp(wlog))
    a = jax.nn.sigmoid(a0 + bdot(xa, a2) + ea)
    g = bdot(sg, g2)
    kk0 = xk * k_k
    nrm = jnp.sqrt(hbcast(hsum(kk0 * kk0)))
    kk = kk0 / jnp.maximum(nrm, 1e-12)
    k = xk * (1.0 + (a - 1.0) * k_a)
    return decay, k, -kk, kk * a, g, tw, sg


def f_post(y, r, k, v, g, ln_w, ln_b, r_k):
    mu = hbcast(hsum(y)) * (1.0 / RW_N)
    yc = y - mu
    var = hbcast(hsum(yc * yc)) * (1.0 / RW_N)
    yn = yc * lax.rsqrt(var + GN_EPS) * ln_w + ln_b
    bonus = hbcast(hsum(r * k * r_k)) * v
    return (yn + bonus) * g


def f_merge(ga, gr, ab, rb):
    return jax.nn.sigmoid(ga) * ab + jax.nn.sigmoid(gr) * rb


def f_swiglu(gg, uu):
    return gg * jax.nn.sigmoid(gg) * uu


def _row(tt, width, cb=0, rev_n=None):
    if rev_n is None:
        return pl.BlockSpec((tt, width), lambda i: (i, cb))
    return pl.BlockSpec((tt, width), lambda i: (rev_n - 1 - i, cb))


def _full(arr):
    nd = arr.ndim
    return pl.BlockSpec(arr.shape, lambda i: (0,) * nd)


def _acc_init(i_first, *refs):
    @pl.when(i_first)
    def _():
        for r in refs:
            r[...] = jnp.zeros_like(r)


def rms_fwd(x, g, *, tt=128):
    s_len, d = x.shape

    def body(x_ref, g_ref, o_ref):
        o_ref[...] = f_rms(x_ref[...], g_ref[...]).astype(BF16)

    return pl.pallas_call(
        body, name="rms_fwd", grid=(s_len // tt,),
        in_specs=[_row(tt, d), _full(g)], out_specs=_row(tt, d),
        out_shape=jax.ShapeDtypeStruct((s_len, d), BF16),
        compiler_params=_cparams(("parallel",)),
    )(x, g)


def rwkv_pre_fwd(proj, mu, w0, a0, k_k, k_a, w2, a2, g2, *, tt=64):
    s_len, c = proj.shape
    nt = s_len // tt
    sub = tt // SUBLANES

    def body(p_ref, pb_ref, mu_ref, w0_ref, a0_ref, kk_ref, ka_ref, w2_ref, a2_ref, g2_ref,
             r_ref, dec_ref, k_ref, v_ref, av_ref, bv_ref, g_ref, tw_ref, xa_ref, sg_ref):
        i = pl.program_id(0)
        cur = p_ref[...]
        edge = jnp.where(i > 0, pb_ref[SUBLANES - 1:SUBLANES, :], 0.0)
        rows = lax.broadcasted_iota(jnp.int32, cur.shape, 0)
        prev = jnp.where(rows == 0, edge, pltpu.roll(cur, 1, axis=0))
        xs = cur + (prev - cur) * mu_ref[...]
        xr, xk, xv = xs[:, 0:RW_C], xs[:, RW_C:2 * RW_C], xs[:, 2 * RW_C:3 * RW_C]
        xg = xs[:, 3 * RW_C:3 * RW_C + 512]
        xw = xs[:, 3 * RW_C + 512:3 * RW_C + 640]
        xa = xs[:, 3 * RW_C + 640:3 * RW_C + 768]
        zero = jnp.zeros((tt, RW_C), F32)
        dec, k, av, bv, g, tw, sg = f_pre(xk, xg, xw, xa, zero, zero, w0_ref[...], a0_ref[...], kk_ref[...],
                                          ka_ref[...], w2_ref[...], a2_ref[...], g2_ref[...])
        r_ref[...] = xr
        dec_ref[...] = dec
        k_ref[...] = k
        v_ref[...] = xv
        av_ref[...] = av
        bv_ref[...] = bv
        g_ref[...] = g
        tw_ref[...] = tw.astype(BF16)
        xa_ref[...] = xa.astype(BF16)
        sg_ref[...] = sg.astype(BF16)

    rows_f = jax.ShapeDtypeStruct((s_len, RW_C), F32)
    prev_spec = pl.BlockSpec((SUBLANES, c), lambda i: (jnp.maximum(i * sub - 1, 0), 0))
    params = [mu, w0, a0, k_k, k_a, w2, a2, g2]
    return pl.pallas_call(
        body, name="rwkv_pre_fwd", grid=(nt,),
        in_specs=[_row(tt, c), prev_spec] + [_full(p) for p in params],
        out_specs=[_row(tt, RW_C)] * 7 + [_row(tt, 128), _row(tt, 128), _row(tt, 512)],
        out_shape=[rows_f] * 7 + [jax.ShapeDtypeStruct((s_len, 128), BF16), jax.ShapeDtypeStruct((s_len, 128), BF16),
                                  jax.ShapeDtypeStruct((s_len, 512), BF16)],
        compiler_params=_cparams(("parallel",)),
    )(proj, proj, *params)


def rwkv_pre_bwd(proj, mu, w0, a0, k_k, k_a, w2, a2, g2, d_r, d_dec, d_k, d_v, d_av, d_bv, d_g, d_r2, d_k2, d_v2,
                 *, tt=32):
    s_len, c = proj.shape
    nt = s_len // tt
    sub = tt // SUBLANES

    def body(p_ref, pb_ref, mu_ref, w0_ref, a0_ref, kk_ref, ka_ref, w2_ref, a2_ref, g2_ref,
             dr_ref, ddec_ref, dk_ref, dv_ref, dav_ref, dbv_ref, dg_ref, dr2_ref, dk2_ref, dv2_ref,
             dp_ref, dzw_ref, dza_ref, dmu_ref, dw0_ref, da0_ref, dkk_ref, dka_ref, carry_ref):
        step = pl.program_id(0)
        i = nt - 1 - step
        _acc_init(step == 0, dmu_ref, dw0_ref, da0_ref, dkk_ref, dka_ref, carry_ref)
        cur = p_ref[...]
        edge = jnp.where(i > 0, pb_ref[SUBLANES - 1:SUBLANES, :], 0.0)
        rows = lax.broadcasted_iota(jnp.int32, cur.shape, 0)
        prev = jnp.where(rows == 0, edge, pltpu.roll(cur, 1, axis=0))
        mu_v = mu_ref[...]
        xs = cur + (prev - cur) * mu_v
        xk = xs[:, RW_C:2 * RW_C]
        xg = xs[:, 3 * RW_C:3 * RW_C + 512]
        xw = xs[:, 3 * RW_C + 512:3 * RW_C + 640]
        xa = xs[:, 3 * RW_C + 640:3 * RW_C + 768]
        zero = jnp.zeros((tt, RW_C), F32)
        w2_v, a2_v, g2_v = w2_ref[...], a2_ref[...], g2_ref[...]

        def core(xk, xg, xw, xa, ew, ea, w0, a0, k_k, k_a):
            return f_pre(xk, xg, xw, xa, ew, ea, w0, a0, k_k, k_a, w2_v, a2_v, g2_v)[:5]

        _, vjp = jax.vjp(core, xk, xg, xw, xa, zero, zero, w0_ref[...], a0_ref[...], kk_ref[...], ka_ref[...])
        dxk, dxg, dxw, dxa, dzw, dza, dw0, da0, dkk, dka = vjp(
            (ddec_ref[...], dk_ref[...] + dk2_ref[...], dav_ref[...], dbv_ref[...], dg_ref[...]))
        dzw_ref[...] = dzw.astype(BF16)
        dza_ref[...] = dza.astype(BF16)
        dw0_ref[...] += dw0
        da0_ref[...] += da0
        dkk_ref[...] += dkk
        dka_ref[...] += dka
        dxs = jnp.concatenate([dr_ref[...] + dr2_ref[...], dxk, dv_ref[...] + dv2_ref[...], dxg, dxw, dxa], axis=1)
        dmu_ref[...] += jnp.sum(dxs * (prev - cur), axis=0, keepdims=True)
        to_prev = dxs * mu_v
        nxt = jnp.where(rows == tt - 1, carry_ref[...], pltpu.roll(to_prev, tt - 1, axis=0))
        carry_ref[...] = to_prev[0:1, :]
        dp_ref[...] = (dxs * (1.0 - mu_v) + nxt).astype(BF16)

    prev_spec = pl.BlockSpec((SUBLANES, c), lambda s: (jnp.maximum((nt - 1 - s) * sub - 1, 0), 0))
    params = [mu, w0, a0, k_k, k_a, w2, a2, g2]
    cts = [d_r, d_dec, d_k, d_v, d_av, d_bv, d_g, d_r2, d_k2, d_v2]
    vec = jax.ShapeDtypeStruct((1, RW_C), F32)
    acc = pl.BlockSpec((1, RW_C), lambda s: (0, 0))
    return pl.pallas_call(
        body, name="rwkv_pre_bwd", grid=(nt,),
        in_specs=[_row(tt, c, rev_n=nt), prev_spec] + [_full(p) for p in params] + [_row(tt, RW_C, rev_n=nt)] * 10,
        out_specs=[_row(tt, c, rev_n=nt), _row(tt, RW_C, rev_n=nt), _row(tt, RW_C, rev_n=nt),
                   pl.BlockSpec((1, c), lambda s: (0, 0)), acc, acc, acc, acc],
        out_shape=[jax.ShapeDtypeStruct((s_len, c), BF16), jax.ShapeDtypeStruct((s_len, RW_C), BF16),
                   jax.ShapeDtypeStruct((s_len, RW_C), BF16), jax.ShapeDtypeStruct((1, c), F32), vec, vec, vec, vec],
        scratch_shapes=[pltpu.VMEM((1, c), F32)],
        compiler_params=_cparams(("arbitrary",)),
    )(proj, proj, *params, *cts)


def rwkv_post_fwd(y, r, k, v, g, ln_w, ln_b, r_k, *, tt=64):
    s_len = y.shape[0]

    def body(y_ref, r_ref, k_ref, v_ref, g_ref, lw_ref, lb_ref, rk_ref, o_ref):
        o_ref[...] = f_post(y_ref[...], r_ref[...], k_ref[...], v_ref[...], g_ref[...],
                            lw_ref[...], lb_ref[...], rk_ref[...]).astype(BF16)

    return pl.pallas_call(
        body, name="rwkv_post_fwd", grid=(s_len // tt,),
        in_specs=[_row(tt, RW_C)] * 5 + [_full(ln_w), _full(ln_b), _full(r_k)],
        out_specs=_row(tt, RW_C), out_shape=jax.ShapeDtypeStruct((s_len, RW_C), BF16),
        compiler_params=_cparams(("parallel",)),
    )(y, r, k, v, g, ln_w, ln_b, r_k)


def rwkv_post_bwd(y, r, k, v, g, ln_w, ln_b, r_k, d_o, *, tt=32):
    s_len = y.shape[0]

    def body(y_ref, r_ref, k_ref, v_ref, g_ref, lw_ref, lb_ref, rk_ref, do_ref,
             dy_ref, dr_ref, dk_ref, dv_ref, dg_ref, dlw_ref, dlb_ref, drk_ref):
        _acc_init(pl.program_id(0) == 0, dlw_ref, dlb_ref, drk_ref)
        _, vjp = jax.vjp(f_post, y_ref[...], r_ref[...], k_ref[...], v_ref[...], g_ref[...],
                         lw_ref[...], lb_ref[...], rk_ref[...])
        dy, dr, dk, dv, dg, dlw, dlb, drk = vjp(do_ref[...].astype(F32))
        dy_ref[...] = dy
        dr_ref[...] = dr
        dk_ref[...] = dk
        dv_ref[...] = dv
        dg_ref[...] = dg
        dlw_ref[...] += dlw
        dlb_ref[...] += dlb
        drk_ref[...] += drk

    rows_f = jax.ShapeDtypeStruct((s_len, RW_C), F32)
    vec = jax.ShapeDtypeStruct((1, RW_C), F32)
    acc = pl.BlockSpec((1, RW_C), lambda s: (0, 0))
    return pl.pallas_call(
        body, name="rwkv_post_bwd", grid=(s_len // tt,),
        in_specs=[_row(tt, RW_C)] * 5 + [_full(ln_w), _full(ln_b), _full(r_k), _row(tt, RW_C)],
        out_specs=[_row(tt, RW_C)] * 5 + [acc] * 3, out_shape=[rows_f] * 5 + [vec] * 3,
        compiler_params=_cparams(("arbitrary",)),
    )(y, r, k, v, g, ln_w, ln_b, r_k, d_o)


def merge_fwd(gate, ab, rb, *, tt=128):
    s_len, d = ab.shape

    def body(ga_ref, gr_ref, a_ref, r_ref, o_ref):
        o_ref[...] = f_merge(ga_ref[...], gr_ref[...], a_ref[...], r_ref[...]).astype(BF16)

    return pl.pallas_call(
        body, name="merge_fwd", grid=(s_len // tt,),
        in_specs=[_row(tt, d, 0), _row(tt, d, 1), _row(tt, d), _row(tt, d)],
        out_specs=_row(tt, d), out_shape=jax.ShapeDtypeStruct((s_len, d), BF16),
        compiler_params=_cparams(("parallel",)),
    )(gate, gate, ab, rb)


def merge_bwd(gate, ab, rb, d_m, *, tt=64):
    s_len, d = ab.shape

    def body(ga_ref, gr_ref, a_ref, r_ref, dm_ref, dgate_ref, da_ref, dr_ref):
        _, vjp = jax.vjp(f_merge, ga_ref[...], gr_ref[...], a_ref[...], r_ref[...])
        dga, dgr, da, dr = vjp(dm_ref[...].astype(F32))
        dgate_ref[:, 0:d] = dga.astype(BF16)
        dgate_ref[:, d:2 * d] = dgr.astype(BF16)
        da_ref[...] = da.astype(BF16)
        dr_ref[...] = dr.astype(BF16)

    return pl.pallas_call(
        body, name="merge_bwd", grid=(s_len // tt,),
        in_specs=[_row(tt, d, 0), _row(tt, d, 1), _row(tt, d), _row(tt, d), _row(tt, d)],
        out_specs=[_row(tt, 2 * d), _row(tt, d), _row(tt, d)],
        out_shape=[jax.ShapeDtypeStruct((s_len, 2 * d), BF16), jax.ShapeDtypeStruct((s_len, d), BF16),
                   jax.ShapeDtypeStruct((s_len, d), BF16)],
        compiler_params=_cparams(("parallel",)),
    )(gate, gate, ab, rb, d_m)


def swiglu_fwd(gg, uu, *, tt=64):
    s_len, f = gg.shape

    def body(g_ref, u_ref, o_ref):
        o_ref[...] = f_swiglu(g_ref[...], u_ref[...]).astype(BF16)

    return pl.pallas_call(
        body, name="swiglu_fwd", grid=(s_len // tt,),
        in_specs=[_row(tt, f), _row(tt, f)], out_specs=_row(tt, f),
        out_shape=jax.ShapeDtypeStruct((s_len, f), BF16),
        compiler_params=_cparams(("parallel",)),
    )(gg, uu)


def swiglu_bwd(gg, uu, d_act, *, tt=32):
    s_len, f = gg.shape

    def body(g_ref, u_ref, d_ref, dg_ref, du_ref):
        _, vjp = jax.vjp(f_swiglu, g_ref[...], u_ref[...])
        dg, du = vjp(d_ref[...].astype(F32))
        dg_ref[...] = dg.astype(BF16)
        du_ref[...] = du.astype(BF16)

    out = jax.ShapeDtypeStruct((s_len, f), BF16)
    return pl.pallas_call(
        body, name="swiglu_bwd", grid=(s_len // tt,),
        in_specs=[_row(tt, f)] * 3, out_specs=[_row(tt, f)] * 2, out_shape=[out, out],
        compiler_params=_cparams(("parallel",)),
    )(gg, uu, d_act)


def resid_norm_fwd(x, m2, g_post, g_pre, *, tt=128):
    s_len, d = x.shape

    def body(x_ref, m_ref, gp_ref, gn_ref, x1_ref, h_ref):
        x1 = x_ref[...] + f_rms(m_ref[...], gp_ref[...])
        x1_ref[...] = x1
        h_ref[...] = f_rms(x1, gn_ref[...]).astype(BF16)

    return pl.pallas_call(
        body, name="resid_norm_fwd", grid=(s_len // tt,),
        in_specs=[_row(tt, d), _row(tt, d), _full(g_post), _full(g_pre)],
        out_specs=[_row(tt, d), _row(tt, d)],
        out_shape=[jax.ShapeDtypeStruct((s_len, d), F32), jax.ShapeDtypeStruct((s_len, d), BF16)],
        compiler_params=_cparams(("parallel",)),
    )(x, m2, g_post, g_pre)


def loss_head(x1, ff, tgt, g_post, *, tt=64):
    s_len, d = x1.shape

    def body(x1_ref, f_ref, t_ref, g_ref, loss_ref, dy_ref, df_ref, dg_ref):
        _acc_init(pl.program_id(0) == 0, loss_ref, dg_ref)
        nrm, vjp = jax.vjp(f_rms, f_ref[...], g_ref[...])
        err = x1_ref[...] + nrm - t_ref[...]
        per_tok = jnp.mean(err * err, axis=-1, keepdims=True)
        loss_ref[...] += 0.5 * jnp.sum(per_tok, axis=0, keepdims=True)
        dy = err * (1.0 / d)
        dff, dg = vjp(dy)
        dy_ref[...] = dy
        df_ref[...] = dff.astype(BF16)
        dg_ref[...] += dg

    return pl.pallas_call(
        body, name="loss_head", grid=(s_len // tt,),
        in_specs=[_row(tt, d)] * 3 + [_full(g_post)],
        out_specs=[pl.BlockSpec((1, LANES), lambda s: (0, 0)), _row(tt, d), _row(tt, d),
                   pl.BlockSpec((1, d), lambda s: (0, 0))],
        out_shape=[jax.ShapeDtypeStruct((1, LANES), F32), jax.ShapeDtypeStruct((s_len, d), F32),
                   jax.ShapeDtypeStruct((s_len, d), BF16), jax.ShapeDtypeStruct((1, d), F32)],
        compiler_params=_cparams(("arbitrary",)),
    )(x1, ff, tgt, g_post)


def resid_norm_bwd(x1, dh_a, dh_b, g_pre, m2, g_post, dy, *, tt=64):
    s_len, d = x1.shape

    def body(x1_ref, da_ref, db_ref, gn_ref, m_ref, gp_ref, dy_ref, dx1_ref, dm_ref, dgn_ref, dgp_ref):
        _acc_init(pl.program_id(0) == 0, dgn_ref, dgp_ref)
        _, vjp_n = jax.vjp(f_rms, x1_ref[...], gn_ref[...])
        dx1_n, dgn = vjp_n(da_ref[...] + db_ref[...])
        dx1 = dy_ref[...] + dx1_n
        _, vjp_p = jax.vjp(f_rms, m_ref[...], gp_ref[...])
        dm, dgp = vjp_p(dx1)
        dx1_ref[...] = dx1
        dm_ref[...] = dm.astype(BF16)
        dgn_ref[...] += dgn
        dgp_ref[...] += dgp

    acc = pl.BlockSpec((1, d), lambda s: (0, 0))
    vec = jax.ShapeDtypeStruct((1, d), F32)
    return pl.pallas_call(
        body, name="resid_norm_bwd", grid=(s_len // tt,),
        in_specs=[_row(tt, d)] * 3 + [_full(g_pre), _row(tt, d), _full(g_post), _row(tt, d)],
        out_specs=[_row(tt, d), _row(tt, d), acc, acc],
        out_shape=[jax.ShapeDtypeStruct((s_len, d), F32), jax.ShapeDtypeStruct((s_len, d), BF16), vec, vec],
        compiler_params=_cparams(("arbitrary",)),
    )(x1, dh_a, dh_b, g_pre, m2, g_post, dy)


def rms_bwd(x, g, dh_a, dh_b, dh_c, dres, *, tt=64):
    s_len, d = x.shape

    def body(x_ref, g_ref, a_ref, b_ref, c_ref, r_ref, dx_ref, dg_ref):
        _acc_init(pl.program_id(0) == 0, dg_ref)
        _, vjp = jax.vjp(f_rms, x_ref[...], g_ref[...])
        dx, dg = vjp(a_ref[...] + b_ref[...] + c_ref[...])
        dx_ref[...] = r_ref[...] + dx
        dg_ref[...] += dg

    return pl.pallas_call(
        body, name="rms_bwd", grid=(s_len // tt,),
        in_specs=[_row(tt, d), _full(g)] + [_row(tt, d)] * 4,
        out_specs=[_row(tt, d), pl.BlockSpec((1, d), lambda s: (0, 0))],
        out_shape=[jax.ShapeDtypeStruct((s_len, d), F32), jax.ShapeDtypeStruct((1, d), F32)],
        compiler_params=_cparams(("arbitrary",)),
    )(x, g, dh_a, dh_b, dh_c, dres)


def colsum(a, *, tt=256):
    s_len, c = a.shape

    def body(a_ref, o_ref):
        _acc_init(pl.program_id(0) == 0, o_ref)
        o_ref[...] += jnp.sum(a_ref[...].astype(F32), axis=0, keepdims=True)

    return pl.pallas_call(
        body, name="colsum", grid=(s_len // tt,),
        in_specs=[_row(tt, c)], out_specs=pl.BlockSpec((1, c), lambda s: (0, 0)),
        out_shape=jax.ShapeDtypeStruct((1, c), F32),
        compiler_params=_cparams(("arbitrary",)),
    )(a)


AT_HD = 128
AT_GROUP = 4
AT_KVH = 8
AT_BLK = 128
AT_QW = AT_GROUP * AT_HD
AT_KCOL = AT_KVH * AT_GROUP
AT_VCOL = AT_KCOL + AT_KVH
NEG_INF = -1e30
AT_SCALE = AT_HD ** -0.5


def _rope(t, cos2, sin2):
    return t * cos2 + pltpu.roll(t, AT_HD // 2, axis=1) * sin2


def _rope_t(d, cos2, sin2):
    return d * cos2 + pltpu.roll(d * sin2, AT_HD // 2, axis=1)


def _att_specs():
    prev = lambda i: jnp.maximum(i - 1, 0)
    blk = (AT_BLK, AT_HD)
    return [
        pl.BlockSpec((AT_BLK, AT_QW), lambda h, i: (i, h)),
        pl.BlockSpec(blk, lambda h, i: (i, AT_KCOL + h)),
        pl.BlockSpec(blk, lambda h, i: (prev(i), AT_KCOL + h)),
        pl.BlockSpec(blk, lambda h, i: (i, AT_VCOL + h)),
        pl.BlockSpec(blk, lambda h, i: (prev(i), AT_VCOL + h)),
        pl.BlockSpec((1, AT_QW), lambda h, i: (0, h)),
        pl.BlockSpec((1, AT_HD), lambda h, i: (0, AT_KCOL + h)),
        pl.BlockSpec((1, AT_HD), lambda h, i: (0, AT_VCOL + h)),
        pl.BlockSpec((1, AT_GROUP, AT_HD), lambda h, i: (h, 0, 0)),
        pl.BlockSpec(blk, lambda h, i: (i, 0)),
        pl.BlockSpec(blk, lambda h, i: (i, 0)),
        pl.BlockSpec(blk, lambda h, i: (prev(i), 0)),
        pl.BlockSpec(blk, lambda h, i: (prev(i), 0)),
    ]


def _att_load(i, q_ref, kc_ref, kp_ref, vc_ref, vp_ref, bq_ref, bk_ref, bv_ref, cc_ref, sc_ref, cp_ref, sp_ref):
    cosc, sinc = cc_ref[...], sc_ref[...]
    q = q_ref[...] + bq_ref[...]
    kc = _rope(kc_ref[...] + bk_ref[...], cosc, sinc)
    kp = _rope(kp_ref[...] + bk_ref[...], cp_ref[...], sp_ref[...])
    kcat = jnp.concatenate([kp, kc], axis=0).astype(BF16)
    vcat = jnp.concatenate([vp_ref[...] + bv_ref[...], vc_ref[...] + bv_ref[...]], axis=0).astype(BF16)
    qi = lax.broadcasted_iota(jnp.int32, (AT_GROUP * AT_BLK, 2 * AT_BLK), 0) & (AT_BLK - 1)
    kj = lax.broadcasted_iota(jnp.int32, (AT_GROUP * AT_BLK, 2 * AT_BLK), 1)
    rel = qi + AT_BLK - kj
    mask = (rel >= 0) & (rel < AT_BLK) & ((kj >= AT_BLK) | (i > 0))
    return q, kcat, vcat, mask, cosc, sinc


AT_ROWS = AT_GROUP * AT_BLK


def _att_stack(q, cosc, sinc):
    return jnp.concatenate([_rope(q[:, g * AT_HD:(g + 1) * AT_HD], cosc, sinc) for g in range(AT_GROUP)], axis=0)


def _att_cols(sk_ref):
    head = lax.broadcasted_iota(jnp.int32, (AT_ROWS, 1), 0) >> 7
    sink = jnp.zeros((AT_ROWS, 1), F32)
    for g in range(AT_GROUP):
        sink = jnp.where(head == g, sk_ref[0, g:g + 1, 0:1], sink)
    return sink, head


def _att_probs(qs, kcat, mask, sink):
    s = lax.dot_general(qs, kcat, (((1,), (1,)), ((), ())), preferred_element_type=F32) * AT_SCALE
    s = jnp.where(mask, s, NEG_INF)
    m = jnp.maximum(jnp.max(s, axis=-1, keepdims=True), sink)
    p = jnp.exp(s - m)
    es = jnp.exp(sink - m)
    inv = 1.0 / (jnp.sum(p, axis=-1, keepdims=True) + es)
    return p * inv, es * inv


def attention_fwd(qkv, bias, sinks_b, cos2, sin2):
    s_len = qkv.shape[0]
    nb = s_len // AT_BLK

    def body(q_ref, kc_ref, kp_ref, vc_ref, vp_ref, bq_ref, bk_ref, bv_ref, sk_ref, cc_ref, sc_ref, cp_ref, sp_ref,
             o_ref):
        i = pl.program_id(1)
        q, kcat, vcat, mask, cosc, sinc = _att_load(i, q_ref, kc_ref, kp_ref, vc_ref, vp_ref, bq_ref, bk_ref,
                                                    bv_ref, cc_ref, sc_ref, cp_ref, sp_ref)
        sink, _ = _att_cols(sk_ref)
        probs, _ = _att_probs(_att_stack(q, cosc, sinc).astype(BF16), kcat, mask, sink)
        o = jnp.dot(probs.astype(BF16), vcat, preferred_element_type=F32).astype(BF16)
        for g in range(AT_GROUP):
            o_ref[:, g * AT_HD:(g + 1) * AT_HD] = o[g * AT_BLK:(g + 1) * AT_BLK, :]

    return pl.pallas_call(
        body, name="attention_fwd", grid=(AT_KVH, nb),
        in_specs=_att_specs(),
        out_specs=pl.BlockSpec((AT_BLK, AT_QW), lambda h, i: (i, h)),
        out_shape=jax.ShapeDtypeStruct((s_len, AT_KVH * AT_QW), BF16),
        compiler_params=_cparams(("parallel", "parallel")),
    )(qkv, qkv, qkv, qkv, qkv, bias, bias, bias, sinks_b, cos2, sin2, cos2, sin2)


def attention_bwd(qkv, bias, sinks_b, cos2, sin2, d_o, rider=None):
    s_len = qkv.shape[0]
    nb = s_len // AT_BLK
    x_in, x_out, x_scr = _rider_parts(rider)
    ni, no = len(x_in), len(x_out)

    def body(*refs):
        (q_ref, kc_ref, kp_ref, vc_ref, vp_ref, bq_ref, bk_ref, bv_ref, sk_ref, cc_ref, sc_ref, cp_ref, sp_ref,
         do_ref) = refs[:14]
        dq_ref, dk_ref, dv_ref, dsk_ref = refs[14 + ni:18 + ni]
        ride = (refs[14:14 + ni], refs[18 + ni:18 + ni + no], *refs[18 + ni + no:])
        i = pl.program_id(1)
        if rider is not None:
            @pl.when((pl.program_id(0) == 0) & (i == 0))
            def _():
                rider["start"](*ride)

        _acc_init(i == 0, dsk_ref)
        q, kcat, vcat, mask, cosc, sinc = _att_load(i, q_ref, kc_ref, kp_ref, vc_ref, vp_ref, bq_ref, bk_ref,
                                                    bv_ref, cc_ref, sc_ref, cp_ref, sp_ref)
        sink, head = _att_cols(sk_ref)
        qs = _att_stack(q, cosc, sinc).astype(BF16)
        probs, psink = _att_probs(qs, kcat, mask, sink)
        pb = probs.astype(BF16)
        do_s = jnp.concatenate([do_ref[:, g * AT_HD:(g + 1) * AT_HD] for g in range(AT_GROUP)], axis=0)
        do_f = do_s.astype(F32)
        do_b = do_s.astype(BF16)
        o_s = jnp.dot(pb, vcat, preferred_element_type=F32)
        dsum = jnp.sum(do_f * o_s, axis=-1, keepdims=True)
        dp = lax.dot_general(do_b, vcat, (((1,), (1,)), ((), ())), preferred_element_type=F32)
        ds = (probs * (dp - dsum) * AT_SCALE).astype(BF16)
        dv_cat = lax.dot_general(pb, do_b, (((0,), (0,)), ((), ())), preferred_element_type=F32)
        dk_cat = lax.dot_general(ds, qs, (((0,), (0,)), ((), ())), preferred_element_type=F32)
        dq_s = jnp.dot(ds, kcat, preferred_element_type=F32)
        lane = lax.broadcasted_iota(jnp.int32, (1, AT_HD), 1)
        dsk = jnp.zeros((1, AT_HD), F32)
        sink_term = psink * dsum
        for g in range(AT_GROUP):
            rows = slice(g * AT_BLK, (g + 1) * AT_BLK)
            dq_ref[:, g * AT_HD:(g + 1) * AT_HD] = _rope_t(dq_s[rows, :], cosc, sinc).astype(BF16)
            dsk = dsk + jnp.where(lane == g, -jnp.sum(sink_term[rows, :], axis=0, keepdims=True), 0.0)
        dsk_ref[0] += dsk
        cur = pl.ds(pl.multiple_of(i * AT_BLK, AT_BLK), AT_BLK)
        dk_ref[cur, :] = _rope_t(dk_cat[AT_BLK:], cosc, sinc)
        dv_ref[cur, :] = dv_cat[AT_BLK:]

        @pl.when(i > 0)
        def _():
            prv = pl.ds(pl.multiple_of((i - 1) * AT_BLK, AT_BLK), AT_BLK)
            dk_ref[prv, :] += _rope_t(dk_cat[:AT_BLK], cp_ref[...], sp_ref[...])
            dv_ref[prv, :] += dv_cat[:AT_BLK]

        if rider is not None:
            @pl.when((pl.program_id(0) == AT_KVH - 1) & (i == nb - 1))
            def _():
                rider["finish"](*ride)

    kv_out = pl.BlockSpec((s_len, AT_HD), lambda h, i: (0, h))
    return pl.pallas_call(
        body, name="attention_bwd", grid=(AT_KVH, nb),
        in_specs=_att_specs() + [pl.BlockSpec((AT_BLK, AT_QW), lambda h, i: (i, h))] + [ANY] * ni,
        out_specs=[pl.BlockSpec((AT_BLK, AT_QW), lambda h, i: (i, h)), kv_out, kv_out,
                   pl.BlockSpec((1, 1, AT_HD), lambda h, i: (h, 0, 0))] + [ANY] * no,
        out_shape=[jax.ShapeDtypeStruct((s_len, AT_KVH * AT_QW), BF16),
                   jax.ShapeDtypeStruct((s_len, AT_KVH * AT_HD), F32),
                   jax.ShapeDtypeStruct((s_len, AT_KVH * AT_HD), F32),
                   jax.ShapeDtypeStruct((AT_KVH, 1, AT_HD), F32)] + x_out,
        scratch_shapes=x_scr,
        compiler_params=_cparams(("arbitrary", "arbitrary")),
    )(qkv, qkv, qkv, qkv, qkv, bias, bias, bias, sinks_b, cos2, sin2, cos2, sin2, d_o, *x_in)


ATT_QKV = 6144
RW_SHIFT = 13024
RW_PAD = 13056
N_CHIPS = 4
D_GATE = 480
ROPE_THETA = 10000.0


def perm_cols(a):
    lead = a.shape[:-1]
    return jnp.swapaxes(a.reshape(lead + (RW_H, RW_N)), -1, -2).reshape(lead + (RW_C,))


def rw_reorder(a, pad_value=0):
    r, k, v = (perm_cols(a[..., i * RW_C:(i + 1) * RW_C]) for i in range(3))
    wd = a[..., 3 * RW_C:3 * RW_C + 128]
    ad = a[..., 3 * RW_C + 128:3 * RW_C + 256]
    gd = a[..., 3 * RW_C + 256:]
    pad = jnp.full(a.shape[:-1] + (512 - D_GATE,), pad_value, a.dtype)
    return jnp.concatenate([r, k, v, gd, pad, wd, ad], axis=-1)


def rw_restore(a):
    r, k, v = (perm_cols(a[..., i * RW_C:(i + 1) * RW_C]) for i in range(3))
    gd = a[..., 3 * RW_C:3 * RW_C + D_GATE]
    wd = a[..., 3 * RW_C + 512:3 * RW_C + 640]
    ad = a[..., 3 * RW_C + 640:3 * RW_C + 768]
    return jnp.concatenate([r, k, v, wd, ad, gd], axis=-1)


W_IN_PIECES = (ATT_QKV, RW_C, RW_C, RW_C, 128, 128, D_GATE, 2 * RW_C)


def pieces_to_blocks(pieces, n_blocks):
    bw = sum(p.shape[1] for p in pieces) // n_blocks
    blocks = []
    for s in range(n_blocks):
        parts, off = [], 0
        for p in pieces:
            lo, hi = max(s * bw, off), min((s + 1) * bw, off + p.shape[1])
            if lo < hi:
                parts.append(p[:, lo - off:hi - off])
            off += p.shape[1]
        blocks.append(jnp.concatenate(parts, axis=1))
    return jnp.stack(blocks)


def blocks_to_pieces(g4, widths):
    bw = g4.shape[2]
    pieces, off = [], 0
    for w in widths:
        parts = []
        for s in range(g4.shape[0]):
            lo, hi = max(off, s * bw), min(off + w, (s + 1) * bw)
            if lo < hi:
                parts.append(g4[s][:, lo - s * bw:hi - s * bw])
        pieces.append(parts[0] if len(parts) == 1 else jnp.concatenate(parts, axis=1))
        off += w
    return pieces


def to_tiles(a):
    t = a.reshape(a.shape[0], RW_N, RW_H)
    return jnp.concatenate([t, t], axis=-1)


def from_tiles(t):
    return t[:, :, :RW_H].reshape(t.shape[0], RW_C)


def rope_tables(s_len):
    pos = jnp.arange(s_len, dtype=F32)
    inv_freq = ROPE_THETA ** (-jnp.arange(0, AT_HD, 2, dtype=F32) / AT_HD)
    ang = pos[:, None] * inv_freq[None, :]
    cos, sin = jnp.cos(ang), jnp.sin(ang)
    return jnp.concatenate([cos, cos], axis=1), jnp.concatenate([-sin, sin], axis=1)


def local_step(x, tgt, small, big, fwd_rider, got_early, att_rider, bwd_rider, late_riders):
    s_len, d = x.shape
    w_qkv, p_r, p_k, p_v, p_wd, p_ad, p_gd, w_gate = blocks_to_pieces(big["w_in"], W_IN_PIECES)
    w_rw = jnp.concatenate([perm_cols(p_r), perm_cols(p_k), perm_cols(p_v), p_gd,
                            jnp.zeros((d, 512 - D_GATE), BF16), p_wd, p_ad], axis=1)
    w2 = perm_cols(big["w2"])
    a2 = perm_cols(big["a2"])
    g2 = jnp.pad(perm_cols(big["g2"]), ((0, 512 - D_GATE), (0, 0)))
    mu = rw_reorder(small["mu_shift"])
    w0, a0, k_k, k_a, ln_w, ln_b = (perm_cols(small[n]) for n in ("w0", "a0", "k_k", "k_a", "ln_x_w", "ln_x_b"))
    r_k = small["r_k"].reshape(RW_H, RW_N).T.reshape(1, RW_C)
    sinks_b = jnp.broadcast_to(small["att_sinks"].reshape(AT_KVH, AT_GROUP, 1), (AT_KVH, AT_GROUP, AT_HD))
    cos2, sin2 = rope_tables(s_len)
    bias = small["b_qkv"]

    h = rms_fwd(x, small["norm_mix_pre"])
    qkv = matmul(h, w_qkv, name="mm_qkv")
    prw = matmul(h, w_rw, name="mm_rw")
    gate = matmul(h, w_gate, name="mm_gate")
    o_att = attention_fwd(qkv, bias, sinks_b, cos2, sin2)
    pre_params = (mu, w0, a0, k_k, k_a, w2, a2, g2)
    r, dec, k, v, av, bv, g, tw, xa, sg = rwkv_pre_fwd(prw, *pre_params)
    v3 = to_tiles(v)
    y3, hist, sa3, *arrived = rwkv_scan_fwd(r, dec, k, av, bv, v3, rider=fwd_rider)
    big = {**big, **got_early(arrived)}
    w_rb = big["w_rwkv_branch"].reshape(RW_H, RW_N, d).swapaxes(0, 1).reshape(RW_C, d)
    y = from_tiles(y3)
    o_rw = rwkv_post_fwd(y, r, k, v, g, ln_w, ln_b, r_k)
    ab = matmul(o_att, big["w_att_branch"], name="mm_ab")
    rb = matmul(o_rw, w_rb, name="mm_rb")
    merged = merge_fwd(gate, ab, rb)
    m2 = matmul(merged, big["w_out"], name="mm_out")
    x1, h2 = resid_norm_fwd(x, m2, small["norm_mix_post"], small["norm_ffn_pre"])
    gg = matmul(h2, big["w_ffn_gate"], name="mm_fg")
    uu = matmul(h2, big["w_ffn_up"], name="mm_fu")
    act = swiglu_fwd(gg, uu)
    ff = matmul(act, big["w_ffn_down"], name="mm_fd")
    loss, dy, dff, d_nfp = loss_head(x1, ff, tgt, small["norm_ffn_post"])

    dact = matmul(dff, big["w_ffn_down"], tb=True, name="mm_dact")
    g_fd = matmul(act, dff, ta=True, out_dtype=BF16, name="mm_gfd")
    dgg, duu = swiglu_bwd(gg, uu, dact)
    g_fg = matmul(h2, dgg, ta=True, out_dtype=BF16, name="mm_gfg")
    g_fu = matmul(h2, duu, ta=True, out_dtype=BF16, name="mm_gfu")
    dh2a = matmul(dgg, big["w_ffn_gate"], tb=True, name="mm_dh2a")
    dh2b = matmul(duu, big["w_ffn_up"], tb=True, name="mm_dh2b")
    dx1, dm2, d_nfpre, d_nmpost = resid_norm_bwd(x1, dh2a, dh2b, small["norm_ffn_pre"], m2, small["norm_mix_post"], dy)
    dmerged = matmul(dm2, big["w_out"], tb=True, name="mm_dmerged")
    g_out = matmul(merged, dm2, ta=True, out_dtype=BF16, name="mm_gout")
    dgate, dab, drb = merge_bwd(gate, ab, rb, dmerged)
    do_att = matmul(dab, big["w_att_branch"], tb=True, out_dtype=BF16, name="mm_doatt")
    g_ab = matmul(o_att, dab, ta=True, out_dtype=BF16, name="mm_gab")
    do_rw = matmul(drb, w_rb, tb=True, name="mm_dorw")
    g_rb = matmul(o_rw, drb, ta=True, out_dtype=BF16, name="mm_grb")
    early = {"w_att_branch": g_ab, "w_rwkv_branch": g_rb.reshape(RW_N, RW_H, d).swapaxes(0, 1).reshape(RW_C, d),
             "w_out": g_out, "w_ffn_gate": g_fg, "w_ffn_up": g_fu, "w_ffn_down": g_fd}
    dq, dk_att, dv_att, dsk, *from_sibling = attention_bwd(qkv, bias, sinks_b, cos2, sin2, do_att,
                                                           rider=att_rider(early))
    dqkv = jnp.concatenate([dq, dk_att.astype(BF16), dv_att.astype(BF16)], axis=1)
    dy_s, dr_p, dk_p, dv_p, dg, d_lnw, d_lnb, d_rk = rwkv_post_bwd(y, r, k, v, g, ln_w, ln_b, r_k, do_rw)
    dr_s, ddec, dk_s, dav, dbv, dv3, *from_chips = rwkv_scan_bwd(r, dec, k, av, bv, v3, hist, sa3, to_tiles(dy_s),
                                                                  rider=bwd_rider(from_sibling))
    dprw, dzw, dza, dmu, dw0, da0, dkk, dka = rwkv_pre_bwd(
        prw, *pre_params, dr_p, ddec, dk_p, dv_p, dav, dbv, dg, dr_s, dk_s, from_tiles(dv3))
    g_w2 = matmul(tw, dzw, ta=True, out_dtype=BF16, name="mm_gw2")
    g_a2 = matmul(xa, dza, ta=True, out_dtype=BF16, name="mm_ga2")
    g_g2 = matmul(sg, dg.astype(BF16), ta=True, out_dtype=BF16, name="mm_gg2")
    g_qkv = matmul(h, dqkv, ta=True, out_dtype=BF16, name="mm_gqkv")
    g_rw = matmul(h, dprw, ta=True, out_dtype=BF16, name="mm_grw")
    g_gate = matmul(h, dgate, ta=True, out_dtype=BF16, name="mm_ggate")
    g_r, g_k, g_v = (perm_cols(g_rw[:, i * RW_C:(i + 1) * RW_C]) for i in range(3))
    gbig = {
        "w_in": pieces_to_blocks([g_qkv, g_r, g_k, g_v, g_rw[:, 3 * RW_C + 512:3 * RW_C + 640],
                                  g_rw[:, 3 * RW_C + 640:3 * RW_C + 768], g_rw[:, 3 * RW_C:3 * RW_C + D_GATE], g_gate],
                                 N_CHIPS),
        "w2": perm_cols(g_w2), "a2": perm_cols(g_a2), "g2": perm_cols(g_g2)[:D_GATE],
    }
    ride_a, ride_b, ride_c = late_riders(gbig)
    dh_a, *late_a = matmul(dqkv, w_qkv, tb=True, name="mm_dha", rider=ride_a)
    dh_b, *late_b = matmul(dprw, w_rw, tb=True, name="mm_dhb", rider=ride_b)
    dh_c, *late_c = matmul(dgate, w_gate, tb=True, name="mm_dhc", rider=ride_c)
    grad_x, d_nmpre = rms_bwd(x, small["norm_mix_pre"], dh_a, dh_b, dh_c, dx1)
    d_bias = colsum(dqkv)

    gsmall = {
        "norm_mix_pre": d_nmpre, "norm_mix_post": d_nmpost, "norm_ffn_pre": d_nfpre, "norm_ffn_post": d_nfp,
        "b_qkv": d_bias, "att_sinks": dsk[:, 0, :AT_GROUP].reshape(1, AT_KVH * AT_GROUP),
        "mu_shift": rw_restore(dmu), "w0": perm_cols(dw0), "a0": perm_cols(da0), "k_k": perm_cols(dkk),
        "k_a": perm_cols(dka), "r_k": d_rk.reshape(RW_N, RW_H).T.reshape(1, RW_C),
        "ln_x_w": perm_cols(d_lnw), "ln_x_b": perm_cols(d_lnb),
    }
    return loss, grad_x, gsmall, from_chips, (late_a, late_b, late_c)


def _place():
    x, y, c = lax.axis_index("x"), lax.axis_index("y"), lax.axis_index("c")
    chips = [(1 - x, y), (x, 1 - y), (1 - x, 1 - y)]
    return x, y, c, chips


def _remote(src, dst, send_sems, recv_sems, k, dev):
    return pltpu.make_async_remote_copy(src_ref=src, dst_ref=dst, send_sem=send_sems.at[k], recv_sem=recv_sems.at[k],
                                        device_id=dev, device_id_type=MESH)


def _gather_parts(n):
    def half(ref, which):
        hr = ref.shape[0] // 2
        return ref.at[pl.ds(which * hr, hr), :]

    def sends(ins, outs, send_sems, recv_sems):
        x, y, c, chips = _place()
        me = 2 * x + y
        return [_remote(half(ins[i], c), half(outs[i].at[me], c), send_sems, recv_sems, 6 * i + j, (*chip, c))
                for i in range(n) for j, chip in enumerate(chips)]

    def start(ins, outs, send_sems, recv_sems):
        for cp in sends(ins, outs, send_sems, recv_sems):
            cp.start()

    def finish(ins, outs, send_sems, recv_sems):
        x, y, c, chips = _place()
        sib = (x, y, 1 - c)
        passed = []
        for i in range(n):
            for j, chip in enumerate(chips):
                got = half(outs[i].at[2 * chip[0] + chip[1]], c)
                _remote(got, got, send_sems, recv_sems, 6 * i + j, sib).wait_recv()
                cp = _remote(got, got, send_sems, recv_sems, 6 * i + 3 + j, sib)
                cp.start()
                passed.append(cp)
        for i in range(n):
            for j, chip in enumerate(chips):
                got = half(outs[i].at[2 * chip[0] + chip[1]], 1 - c)
                _remote(got, got, send_sems, recv_sems, 6 * i + 3 + j, sib).wait_recv()
        for cp in sends(ins, outs, send_sems, recv_sems) + passed:
            cp.wait_send()

    return start, finish


def gather_rider(shards):
    n = len(shards)
    start, finish = _gather_parts(n)
    return {"ins": shards, "out_shapes": [jax.ShapeDtypeStruct((4,) + s.shape, s.dtype) for s in shards],
            "scratch": [pltpu.SemaphoreType.DMA((6 * n,)), pltpu.SemaphoreType.DMA((6 * n,))],
            "start": start, "finish": finish}


def gather_weights(shards):
    n = len(shards)
    start, finish = _gather_parts(n)

    def body(*refs):
        parts = (refs[:n], refs[n:2 * n], *refs[2 * n:])
        start(*parts)
        finish(*parts)

    return pl.pallas_call(
        body, name="gather_weights",
        in_specs=[ANY] * n, out_specs=[ANY] * n,
        out_shape=[jax.ShapeDtypeStruct((4,) + s.shape, s.dtype) for s in shards],
        scratch_shapes=[pltpu.SemaphoreType.DMA((6 * n,)), pltpu.SemaphoreType.DMA((6 * n,))],
    )(*shards)


def swap_with_sibling(blocks, name):
    n = len(blocks)

    def body(*refs):
        ins, outs = refs[:n], refs[n:2 * n]
        send_sems, recv_sems = refs[2 * n:]
        x, y, c, _ = _place()
        cps = [_remote(ins[i], outs[i], send_sems, recv_sems, i, (x, y, 1 - c)) for i in range(n)]
        for cp in cps:
            cp.start()
        for cp in cps:
            cp.wait()

    return pl.pallas_call(
        body, name=name, in_specs=[ANY] * n, out_specs=[ANY] * n,
        out_shape=[jax.ShapeDtypeStruct(b.shape, b.dtype) for b in blocks],
        scratch_shapes=[pltpu.SemaphoreType.DMA((n,)), pltpu.SemaphoreType.DMA((n,))],
    )(*blocks)


def _scatter_parts(n, rows=None):
    def piece(ref, i):
        return ref if rows is None or rows[i] is None else ref.at[pl.ds(rows[i][0], rows[i][1]), :]

    def copies(ins, outs, send_sems, recv_sems):
        x, y, c, chips = _place()
        return [_remote(piece(ins[i].at[2 * chip[0] + chip[1]], i), outs[i].at[j], send_sems, recv_sems, 3 * i + j,
                        (*chip, c))
                for i in range(n) for j, chip in enumerate(chips)]

    def start(*refs):
        for cp in copies(*refs):
            cp.start()

    def finish(*refs):
        for cp in copies(*refs):
            cp.wait()

    return start, finish


def scatter_rider(parts, rows=None):
    n = len(parts)
    start, finish = _scatter_parts(n, rows)
    nrows = [p.shape[1] if rows is None or rows[i] is None else rows[i][1] for i, p in enumerate(parts)]
    return {"ins": parts, "out_shapes": [jax.ShapeDtypeStruct((3, r, p.shape[2]), p.dtype) for p, r in zip(parts, nrows)],
            "scratch": [pltpu.SemaphoreType.DMA((3 * n,)), pltpu.SemaphoreType.DMA((3 * n,))],
            "start": start, "finish": finish}


def allreduce_small(v):
    rows = v.shape[0]

    def body(v_ref, o_ref, buf, send_sems, recv_sems):
        x, y, c, chips = _place()
        me, sib = (x, y, c), (x, y, 1 - c)

        def slot(px, py, pc):
            return buf.at[4 * px + 2 * py + pc]

        def copy(k, block, to, src=None):
            return _remote(slot(*block) if src is None else src, slot(*block), send_sems, recv_sems, k, to)

        buf[4 * x + 2 * y + c] = v_ref[...]
        first = [copy(0, me, sib, src=v_ref)]
        first += [copy(1 + j, me, (*chip, c), src=v_ref) for j, chip in enumerate(chips)]
        for cp in first:
            cp.start()
        passed = [copy(4 + j, (*chip, c), sib) for j, chip in enumerate(chips)]
        for j, chip in enumerate(chips):
            copy(1 + j, (*chip, c), me).wait_recv()
            passed[j].start()
        copy(0, sib, me).wait_recv()
        for j, chip in enumerate(chips):
            copy(4 + j, (*chip, 1 - c), me).wait_recv()
        for cp in first + passed:
            cp.wait_send()
        acc = buf[0]
        for k in range(1, 8):
            acc = acc + buf[k]
        o_ref[...] = acc

    vm = pl.BlockSpec(memory_space=pltpu.VMEM)
    return pl.pallas_call(
        body, name="allreduce_small", in_specs=[vm], out_specs=vm,
        out_shape=jax.ShapeDtypeStruct(v.shape, F32),
        scratch_shapes=[pltpu.VMEM((8, rows, LANES), F32), pltpu.SemaphoreType.DMA((7,)),
                        pltpu.SemaphoreType.DMA((7,))],
    )(v)


def _rows_tile(r, most=64):
    return _pick(r, tuple(t for t in (256, 128, 64, 32, 16, 8) if t <= most))


def _swap_parts(n):
    def copies(ins, outs, send_sems, recv_sems):
        x, y, c, _ = _place()
        return [_remote(ins[i].at[:, pl.ds((1 - c) * (ins[i].shape[1] // 2), ins[i].shape[1] // 2), :], outs[i],
                        send_sems, recv_sems, i, (x, y, 1 - c)) for i in range(n)]

    def start(*refs):
        for cp in copies(*refs):
            cp.start()

    def finish(*refs):
        for cp in copies(*refs):
            cp.wait()

    return start, finish


def swap_rider(blocks):
    n = len(blocks)
    start, finish = _swap_parts(n)
    return {"ins": blocks, "out_shapes": [jax.ShapeDtypeStruct((4, b.shape[1] // 2, b.shape[2]), b.dtype) for b in blocks],
            "scratch": [pltpu.SemaphoreType.DMA((n,)), pltpu.SemaphoreType.DMA((n,))], "start": start, "finish": finish}


def swap_halves(blocks, name):
    n = len(blocks)
    start, finish = _swap_parts(n)

    def body(*refs):
        parts = (refs[:n], refs[n:2 * n], *refs[2 * n:])
        start(*parts)
        finish(*parts)

    return pl.pallas_call(
        body, name=name, in_specs=[ANY] * n, out_specs=[ANY] * n,
        out_shape=[jax.ShapeDtypeStruct((4, b.shape[1] // 2, b.shape[2]), b.dtype) for b in blocks],
        scratch_shapes=[pltpu.SemaphoreType.DMA((n,)), pltpu.SemaphoreType.DMA((n,))],
    )(*blocks)


def add_pairs(g4, b, core):
    _, r, c = b.shape
    tr = _rows_tile(r, 256)
    nrt = r // tr

    def body(core_ref, a_ref, b_ref, o_ref):
        o_ref[...] = (a_ref[...].astype(F32) + b_ref[...].astype(F32)).astype(o_ref.dtype)

    spec = pl.BlockSpec((1, tr, c), lambda s, i, core_ref: (s, i, 0))
    return pl.pallas_call(
        body, name="add_pairs",
        grid_spec=pltpu.PrefetchScalarGridSpec(
            num_scalar_prefetch=1, grid=(4, nrt),
            in_specs=[pl.BlockSpec((1, tr, c), lambda s, i, core_ref: (s, core_ref[0] * nrt + i, 0)), spec],
            out_specs=spec),
        out_shape=jax.ShapeDtypeStruct(b.shape, b.dtype), compiler_params=_cparams(("parallel", "parallel")),
    )(core, g4, b)


def add_four(mine, others):
    r, c = mine.shape
    tr = _rows_tile(r, 128)

    def body(m_ref, o_ref, out_ref):
        acc = m_ref[...].astype(F32)
        for j in range(3):
            acc = acc + o_ref[j].astype(F32)
        out_ref[...] = acc

    return pl.pallas_call(
        body, name="add_four", grid=(r // tr,),
        in_specs=[pl.BlockSpec((tr, c), lambda i: (i, 0)), pl.BlockSpec((3, tr, c), lambda i: (0, i, 0))],
        out_specs=pl.BlockSpec((tr, c), lambda i: (i, 0)),
        out_shape=jax.ShapeDtypeStruct((r, c), F32), compiler_params=_cparams(("parallel",)),
    )(mine, others)


def pair_sums(blocks, from_sibling=None):
    core = lax.axis_index("c").astype(jnp.int32).reshape(1)
    if from_sibling is None:
        from_sibling = swap_halves(blocks, "swap_halves_late")
    return [add_pairs(g4, b, core) for g4, b in zip(blocks, from_sibling)]


def owner_sums(pair, from_chips):
    cx, cy, cc = lax.axis_index("x"), lax.axis_index("y"), lax.axis_index("c")
    me = 2 * cx + cy
    sums = [add_four(lax.dynamic_index_in_dim(p, me, 0, keepdims=False), t) for p, t in zip(pair, from_chips)]
    got = swap_with_sibling(sums, "swap_sums")
    return [jnp.concatenate([jnp.where(cc == 0, s, g), jnp.where(cc == 0, g, s)], axis=0) for s, g in zip(sums, got)]


ADAM_LR = 0.001
ADAM_B1 = 0.9
ADAM_B2 = 0.999
ADAM_EPS = 1e-08
ADAM_WD = 0.01
ADAM_STEP = 10


def adamw(w, g, m, v):
    r, c = w.shape
    tr = _rows_tile(r)
    spec = pl.BlockSpec((tr, c), lambda i: (i, 0))

    def body(w_ref, g_ref, m_ref, v_ref, d_ref, nm_ref, nv_ref):
        gv = g_ref[...]
        nm = ADAM_B1 * m_ref[...] + (1.0 - ADAM_B1) * gv
        nv = ADAM_B2 * v_ref[...] + (1.0 - ADAM_B2) * jnp.square(gv)
        m_hat = nm / (1.0 - ADAM_B1 ** ADAM_STEP)
        v_hat = nv / (1.0 - ADAM_B2 ** ADAM_STEP)
        d_ref[...] = -ADAM_LR * (m_hat / (jnp.sqrt(v_hat) + ADAM_EPS) + ADAM_WD * w_ref[...])
        nm_ref[...] = nm
        nv_ref[...] = nv

    out = jax.ShapeDtypeStruct((r, c), F32)
    return pl.pallas_call(
        body, name="adamw", grid=(r // tr,), in_specs=[spec] * 4, out_specs=[spec] * 3, out_shape=[out] * 3,
        compiler_params=_cparams(("parallel",)),
    )(w, g, m, v)


WEIGHTS = ["norm_mix_pre", "norm_mix_post", "norm_ffn_pre", "norm_ffn_post", "w_in", "b_qkv", "att_sinks", "mu_shift",
           "w0", "w2", "a0", "a2", "g2", "k_k", "k_a", "r_k", "ln_x_w", "ln_x_b", "w_att_branch", "w_rwkv_branch",
           "w_out", "w_ffn_gate", "w_ffn_up", "w_ffn_down"]
BIG = {"w_in": 1, "w2": 1, "a2": 1, "g2": 1, "w_att_branch": 0, "w_rwkv_branch": 0, "w_out": 0, "w_ffn_gate": 1,
       "w_ffn_up": 1, "w_ffn_down": 0}
SMALL = [n for n in WEIGHTS if n not in BIG]
LATE_CUTS = (13, 45)
LATE = ("w_in", "w2", "a2", "g2")


def _whole(g4, axis):
    if axis == 0:
        return g4.reshape(g4.shape[0] * g4.shape[1], g4.shape[2])
    return jnp.swapaxes(g4, 0, 1).reshape(g4.shape[1], g4.shape[0] * g4.shape[2])


def _by_shard(w, axis):
    if axis == 0:
        return w.reshape(N_CHIPS, w.shape[0] // N_CHIPS, w.shape[1])
    return jnp.swapaxes(w.reshape(w.shape[0], N_CHIPS, w.shape[1] // N_CHIPS), 0, 1)


def _pack(parts):
    flat = jnp.concatenate([parts[n].reshape(-1) for n in SMALL])
    rows = -(-flat.shape[0] // (LANES * SUBLANES)) * SUBLANES
    return jnp.pad(flat, (0, rows * LANES - flat.shape[0])).reshape(rows, LANES)


def _unpack(packed, like):
    flat = packed.reshape(-1)
    out, off = {}, 0
    for n in SMALL:
        size = like[n].size
        out[n] = flat[off:off + size].reshape(like[n].shape)
        off += size
    return out


def kernel(x, norm_mix_pre, norm_mix_post, norm_ffn_pre, norm_ffn_post, w_in, b_qkv, att_sinks, mu_shift, w0, w2, a0, a2, g2, k_k, k_a, r_k, ln_x_w, ln_x_b, w_att_branch, w_rwkv_branch, w_out, w_ffn_gate, w_ffn_up, w_ffn_down, loss_target, m_norm_mix_pre, m_norm_mix_post, m_norm_ffn_pre, m_norm_ffn_post, m_w_in, m_b_qkv, m_att_sinks, m_mu_shift, m_w0, m_w2, m_a0, m_a2, m_g2, m_k_k, m_k_a, m_r_k, m_ln_x_w, m_ln_x_b, m_w_att_branch, m_w_rwkv_branch, m_w_out, m_w_ffn_gate, m_w_ffn_up, m_w_ffn_down, v_norm_mix_pre, v_norm_mix_post, v_norm_ffn_pre, v_norm_ffn_post, v_w_in, v_b_qkv, v_att_sinks, v_mu_shift, v_w0, v_w2, v_a0, v_a2, v_g2, v_k_k, v_k_a, v_r_k, v_ln_x_w, v_ln_x_b, v_w_att_branch, v_w_rwkv_branch, v_w_out, v_w_ffn_gate, v_w_ffn_up, v_w_ffn_down):
    given = dict(locals())
    wts = {n: given[n] for n in WEIGHTS}
    mom1 = {n: given["m_" + n] for n in WEIGHTS}
    mom2 = {n: given["v_" + n] for n in WEIGHTS}
    early = [n for n in BIG if n not in LATE]
    me = 2 * lax.axis_index("x") + lax.axis_index("y")
    own = {n: wts[n][0].astype(BF16) for n in BIG}

    def placed(names, gathered):
        return {n: (lambda g4: g4 if n == "w_in" else _whole(g4, BIG[n]))(
            lax.dynamic_update_index_in_dim(g4, own[n], me, 0)) for n, g4 in zip(names, gathered)}

    small = {n: wts[n].reshape(1, -1) for n in SMALL}
    pairs = {}

    def att_rider(grads_early):
        pairs["blocks"] = [_by_shard(grads_early[n], BIG[n]) for n in early]
        return swap_rider(pairs["blocks"])

    def bwd_rider(from_sibling):
        pairs["early"] = pair_sums(pairs["blocks"], from_sibling)
        return scatter_rider(pairs["early"])

    def late_riders(grads_late):
        pairs["late"] = pair_sums([grads_late[n] if n == "w_in" else _by_shard(grads_late[n], BIG[n]) for n in LATE])
        half = pairs["late"][0].shape[1]
        cuts = [0] + [half * f // 64 // 16 * 16 for f in LATE_CUTS] + [half]
        spans = [(cuts[i], cuts[i + 1] - cuts[i]) for i in range(3)]
        return (scatter_rider(pairs["late"], [spans[0], None, None, None]),
                scatter_rider(pairs["late"][:1], [spans[1]]), scatter_rider(pairs["late"][:1], [spans[2]]))

    loss, grad_x, gsmall, chips_early, (late_a, late_b, late_c) = local_step(
        x[0], loss_target[0], small, placed(LATE, gather_weights([own[n] for n in LATE])),
        gather_rider([own[n] for n in early]), lambda arrived: placed(early, arrived), att_rider, bwd_rider,
        late_riders)

    chips_late = [jnp.concatenate([late_a[0], late_b[0], late_c[0]], axis=1)] + list(late_a[1:])
    wholes = owner_sums(pairs["early"] + pairs["late"], list(chips_early) + chips_late)
    grads = dict(zip(early + list(LATE), wholes))
    names = list(BIG)

    gsum = _unpack(allreduce_small(_pack(gsmall)), small)

    outs_g, outs_d, outs_m, outs_v = {}, {}, {}, {}
    for n in names:
        d, nm, nv = adamw(wts[n][0], grads[n], mom1[n][0], mom2[n][0])
        outs_g[n], outs_d[n], outs_m[n], outs_v[n] = (t[None] for t in (grads[n], d, nm, nv))
    pk = lambda src: _pack({n: src[n] for n in SMALL})
    d, nm, nv = adamw(pk(wts), _pack(gsum), pk(mom1), pk(mom2))
    du, mu, vu = _unpack(d, small), _unpack(nm, small), _unpack(nv, small)
    for n in SMALL:
        outs_g[n], outs_d[n], outs_m[n], outs_v[n] = (t[n].reshape(wts[n].shape) for t in (gsum, du, mu, vu))

    total = lax.psum(loss[0, 0], ("x", "y", "c"))
    return (total, grad_x[None], *[outs_g[n] for n in WEIGHTS], *[outs_d[n] for n in WEIGHTS],
            *[outs_m[n] for n in WEIGHTS], *[outs_v[n] for n in WEIGHTS])
```

```python
import jax
import jax.numpy as jnp
from jax import lax
from jax.experimental import pallas as pl
from jax.experimental.pallas import tpu as pltpu

F32 = jnp.float32
BF16 = jnp.bfloat16

LANES = 128
SUBLANES = 8
VMEM_LIMIT = 56 * 1024 * 1024

RW_H = 64
RW_N = 64
RW_C = RW_H * RW_N
RW_NB = RW_C // LANES
SCAN_CHUNK = 8


MESH = pl.DeviceIdType.MESH
ANY = pl.BlockSpec(memory_space=pl.ANY)


def _cparams(sem=None):
    return pltpu.CompilerParams(dimension_semantics=sem, vmem_limit_bytes=VMEM_LIMIT)


def _fold(x):
    return x + pltpu.roll(x, 64, axis=x.ndim - 1)


def _scan_step_fwd(t, src_ref, dst_ref, r_ref, w_ref, k_ref, a_ref, b_ref, v_ref):
    vt = v_ref[t]
    acc = jnp.zeros((RW_N, LANES), F32)
    for j in range(RW_NB):
        ls = slice(j * LANES, (j + 1) * LANES)
        acc = acc + src_ref[j] * a_ref[t:t + 1, ls]
    sa = _fold(acc)
    yacc = jnp.zeros((RW_N, LANES), F32)
    for j in range(RW_NB):
        ls = slice(j * LANES, (j + 1) * LANES)
        s_new = src_ref[j] * w_ref[t:t + 1, ls] + sa * b_ref[t:t + 1, ls] + vt * k_ref[t:t + 1, ls]
        dst_ref[j] = s_new
        yacc = yacc + s_new * r_ref[t:t + 1, ls]
    return _fold(yacc), sa


def _rider_parts(rider):
    if rider is None:
        return [], [], []
    return list(rider["ins"]), list(rider["out_shapes"]), list(rider["scratch"])


def rwkv_scan_fwd(r, w, k, a, b, v3, rider=None):
    s_len = r.shape[0]
    nchunk = s_len // SCAN_CHUNK
    x_in, x_out, x_scr = _rider_parts(rider)
    ni, no = len(x_in), len(x_out)

    def body(*refs):
        r_ref, w_ref, k_ref, a_ref, b_ref, v_ref = refs[:6]
        y_ref, hist_ref, sa_ref = refs[6 + ni:9 + ni]
        st_ref = refs[9 + ni + no]
        ride = (refs[6:6 + ni], refs[9 + ni:9 + ni + no], *refs[10 + ni + no:])

        @pl.when(pl.program_id(0) == 0)
        def _():
            st_ref[...] = jnp.zeros_like(st_ref)
            if rider is not None:
                rider["start"](*ride)

        for t in range(SCAN_CHUNK):
            y, sa = _scan_step_fwd(t, st_ref if t == 0 else hist_ref.at[t - 1], hist_ref.at[t],
                                   r_ref, w_ref, k_ref, a_ref, b_ref, v_ref)
            y_ref[t] = y
            sa_ref[t] = sa
        st_ref[...] = hist_ref[SCAN_CHUNK - 1]

        if rider is not None:
            @pl.when(pl.program_id(0) == nchunk - 1)
            def _():
                rider["finish"](*ride)

    row = pl.BlockSpec((SCAN_CHUNK, RW_C), lambda i: (i, 0))
    til = pl.BlockSpec((SCAN_CHUNK, RW_N, LANES), lambda i: (i, 0, 0))
    return pl.pallas_call(
        body,
        name="rwkv_scan_fwd",
        grid=(nchunk,),
        in_specs=[row, row, row, row, row, til] + [ANY] * ni,
        out_specs=[til, pl.BlockSpec((SCAN_CHUNK, RW_NB, RW_N, LANES), lambda i: (i, 0, 0, 0)), til] + [ANY] * no,
        out_shape=[
            jax.ShapeDtypeStruct((s_len, RW_N, LANES), F32),
            jax.ShapeDtypeStruct((s_len, RW_NB, RW_N, LANES), F32),
            jax.ShapeDtypeStruct((s_len, RW_N, LANES), F32),
        ] + x_out,
        scratch_shapes=[pltpu.VMEM((RW_NB, RW_N, LANES), F32)] + x_scr,
        compiler_params=_cparams(("arbitrary",)),
    )(r, w, k, a, b, v3, *x_in)


def rwkv_scan_bwd(r, w, k, a, b, v3, hist, sa3, dy3, rider=None):
    s_len = r.shape[0]
    nchunk = s_len // SCAN_CHUNK
    x_in, x_out, x_scr = _rider_parts(rider)
    ni, no = len(x_in), len(x_out)

    def body(*refs):
        r_ref, w_ref, k_ref, a_ref, b_ref, v_ref, hist_ref, prev_ref, sa_ref, dy_ref = refs[:10]
        dr_ref, dw_ref, dk_ref, da_ref, db_ref, dv_ref = refs[10 + ni:16 + ni]
        ds_ref = refs[16 + ni + no]
        ride = (refs[10:10 + ni], refs[16 + ni:16 + ni + no], *refs[17 + ni + no:])

        @pl.when(pl.program_id(0) == 0)
        def _():
            ds_ref[...] = jnp.zeros_like(ds_ref)
            if rider is not None:
                rider["start"](*ride)

        not_first = (pl.program_id(0) < nchunk - 1).astype(F32)

        def before(t, j):
            return prev_ref[0, j] * not_first if t == 0 else hist_ref[t - 1, j]

        for t in reversed(range(SCAN_CHUNK)):
            vt = v_ref[t]
            dyt = dy_ref[t]
            sat = sa_ref[t]
            dv_acc = jnp.zeros((RW_N, LANES), F32)
            dsa_acc = jnp.zeros((RW_N, LANES), F32)
            for j in range(RW_NB):
                ls = slice(j * LANES, (j + 1) * LANES)
                row = (slice(t, t + 1), ls)
                ds_j = ds_ref[j] + dyt * r_ref[row]
                ds_ref[j] = ds_j
                dr_ref[row] = jnp.sum(hist_ref[t, j] * dyt, axis=0, keepdims=True)
                dv_acc = dv_acc + ds_j * k_ref[row]
                dk_ref[row] = jnp.sum(ds_j * vt, axis=0, keepdims=True)
                dsa_acc = dsa_acc + ds_j * b_ref[row]
                db_ref[row] = jnp.sum(ds_j * sat, axis=0, keepdims=True)
                dw_ref[row] = jnp.sum(ds_j * before(t, j), axis=0, keepdims=True)
            dv_ref[t] = _fold(dv_acc)
            dsa = _fold(dsa_acc)
            for j in range(RW_NB):
                ls = slice(j * LANES, (j + 1) * LANES)
                row = (slice(t, t + 1), ls)
                da_ref[row] = jnp.sum(before(t, j) * dsa, axis=0, keepdims=True)
                ds_ref[j] = ds_ref[j] * w_ref[row] + dsa * a_ref[row]

        if rider is not None:
            @pl.when(pl.program_id(0) == nchunk - 1)
            def _():
                rider["finish"](*ride)

    rev = lambda i: (nchunk - 1 - i, 0)
    rev3 = lambda i: (nchunk - 1 - i, 0, 0)
    row = pl.BlockSpec((SCAN_CHUNK, RW_C), rev)
    til = pl.BlockSpec((SCAN_CHUNK, RW_N, LANES), rev3)
    rows = jax.ShapeDtypeStruct((s_len, RW_C), F32)
    return pl.pallas_call(
        body,
        name="rwkv_scan_bwd",
        grid=(nchunk,),
        in_specs=[row, row, row, row, row, til,
                  pl.BlockSpec((SCAN_CHUNK, RW_NB, RW_N, LANES), lambda i: (nchunk - 1 - i, 0, 0, 0)),
                  pl.BlockSpec((1, RW_NB, RW_N, LANES),
                               lambda i: (jnp.maximum((nchunk - 1 - i) * SCAN_CHUNK - 1, 0), 0, 0, 0)),
                  til, til] + [ANY] * ni,
        out_specs=[row, row, row, row, row, til] + [ANY] * no,
        out_shape=[rows, rows, rows, rows, rows, jax.ShapeDtypeStruct((s_len, RW_N, LANES), F32)] + x_out,
        scratch_shapes=[pltpu.VMEM((RW_NB, RW_N, LANES), F32)] + x_scr,
        compiler_params=_cparams(("arbitrary",)),
    )(r, w, k, a, b, v3, hist, hist, sa3, dy3, *x_in)


def _pick(n, cands):
    for c in cands:
        if n % c == 0:
            return c
    return n


MM_VMEM_BUDGET = 40 * 1024 * 1024
MM_FLOPS = 8.5e14
MM_HBM = 2.2e12
MM_STEP = 0.4e-6


def _mm_plan(m, n, k, out_bytes):
    divs = lambda d: [t for t in range(LANES, d + 1, LANES) if d % t == 0] or [d]
    best = None
    for tm in divs(m):
        for tn in divs(n):
            for tk in divs(k):
                nk = k // tk
                vmem = 4 * (tm * tk + tk * tn) + (4 * tm * tn if nk > 1 else 0) + 2 * tm * tn * out_bytes
                if vmem > MM_VMEM_BUDGET:
                    continue
                steps = (m // tm) * (n // tn) * nk
                for n_outer in (False, True):
                    if nk > 1:
                        traffic = steps * (tm * tk + tk * tn) * 2
                    elif n_outer:
                        traffic = (n // tn) * (k * tn + m * k) * 2
                    else:
                        traffic = (m // tm) * (tm * k + k * n) * 2
                    cost = max(2.0 * m * n * k / MM_FLOPS, (traffic + m * n * out_bytes) / MM_HBM) + steps * MM_STEP
                    if best is None or cost < best[0]:
                        best = (cost, tm, tn, tk, n_outer)
    return best[1:]


def matmul(a, b, *, ta=False, tb=False, out_dtype=F32, name="matmul", plan=None, rider=None):
    m, kdim = (a.shape[1], a.shape[0]) if ta else a.shape
    n = b.shape[0] if tb else b.shape[1]
    assert (b.shape[1] if tb else b.shape[0]) == kdim
    tm, tn, tk, n_outer = plan or _mm_plan(m, n, kdim, jnp.dtype(out_dtype).itemsize)
    nk = kdim // tk
    dims = (((0 if ta else 1,), (1 if tb else 0,)), ((), ()))
    grid = (n // tn, m // tm, nk) if n_outer else (m // tm, n // tn, nk)
    x_in, x_out, x_scr = _rider_parts(rider)
    ni, no = len(x_in), len(x_out)
    n_acc = 1 if nk > 1 else 0

    def body(*refs):
        a_ref, b_ref = refs[:2]
        o_ref = refs[2 + ni]
        ride = (refs[2:2 + ni], refs[3 + ni:3 + ni + no], *refs[3 + ni + no + n_acc:])
        ids = [pl.program_id(ax) for ax in range(3)]
        if rider is not None:
            @pl.when((ids[0] == 0) & (ids[1] == 0) & (ids[2] == 0))
            def _():
                rider["start"](*ride)

        prod = lax.dot_general(a_ref[...].astype(BF16), b_ref[...].astype(BF16), dims, preferred_element_type=F32)
        if nk == 1:
            o_ref[...] = prod.astype(o_ref.dtype)
        else:
            acc_ref = refs[3 + ni + no]
            kk = ids[2]

            @pl.when(kk == 0)
            def _():
                acc_ref[...] = prod

            @pl.when(kk > 0)
            def _():
                acc_ref[...] += prod

            @pl.when(kk == nk - 1)
            def _():
                o_ref[...] = acc_ref[...].astype(o_ref.dtype)

        if rider is not None:
            @pl.when((ids[0] == grid[0] - 1) & (ids[1] == grid[1] - 1) & (ids[2] == nk - 1))
            def _():
                rider["finish"](*ride)

    ij = (lambda p, q: (q, p)) if n_outer else (lambda p, q: (p, q))
    a_map = (lambda p, q, k: (k, ij(p, q)[0])) if ta else (lambda p, q, k: (ij(p, q)[0], k))
    b_map = (lambda p, q, k: (ij(p, q)[1], k)) if tb else (lambda p, q, k: (k, ij(p, q)[1]))
    out = pl.pallas_call(
        body,
        name=name,
        grid=grid,
        in_specs=[pl.BlockSpec((tk, tm) if ta else (tm, tk), a_map),
                  pl.BlockSpec((tn, tk) if tb else (tk, tn), b_map)] + [ANY] * ni,
        out_specs=[pl.BlockSpec((tm, tn), lambda p, q, k: ij(p, q))] + [ANY] * no,
        out_shape=[jax.ShapeDtypeStruct((m, n), out_dtype)] + x_out,
        scratch_shapes=([pltpu.VMEM((tm, tn), F32)] if nk > 1 else []) + x_scr,
        compiler_params=_cparams(("arbitrary",) * 3 if rider is not None else ("parallel", "parallel", "arbitrary")),
    )(a, b, *x_in)
    return out if rider is not None else out[0]


@jax.custom_vjp
def hsum(x):
    acc = x[:, 0:LANES]
    for j in range(1, RW_NB):
        acc = acc + x[:, j * LANES:(j + 1) * LANES]
    return _fold(acc)


def _hsum_fwd(x):
    return hsum(x), None


def _hsum_bwd(_, ct):
    return (jnp.concatenate([_fold(ct)] * RW_NB, axis=1),)


hsum.defvjp(_hsum_fwd, _hsum_bwd)


@jax.custom_vjp
def hbcast(s):
    return jnp.concatenate([s] * RW_NB, axis=1)


def _hbcast_fwd(s):
    return hbcast(s), None


def _hbcast_bwd(_, ct):
    acc = ct[:, 0:LANES]
    for j in range(1, RW_NB):
        acc = acc + ct[:, j * LANES:(j + 1) * LANES]
    return (acc,)


hbcast.defvjp(_hbcast_fwd, _hbcast_bwd)


@jax.custom_vjp
def bdot(x, w):
    return jnp.dot(x.astype(BF16), w, preferred_element_type=F32)


def _bdot_fwd(x, w):
    return bdot(x, w), w


def _bdot_bwd(w, ct):
    dx = lax.dot_general(ct.astype(BF16), w, (((1,), (1,)), ((), ())), preferred_element_type=F32)
    return dx, jnp.zeros_like(w)


bdot.defvjp(_bdot_fwd, _bdot_bwd)

RMS_EPS = 1e-6
GN_EPS = 64e-5


def f_rms(x, g):
    return x * lax.rsqrt(jnp.mean(x * x, axis=-1, keepdims=True) + RMS_EPS) * g


def _softplus(z):
    return jnp.maximum(z, 0.0) + jnp.log1p(jnp.exp(-jnp.abs(z)))


def f_pre(xk, xg, xw, xa, ew, ea, w0, a0, k_k, k_a, w2, a2, g2):
    tw = jnp.tanh(xw)
    sg = jax.nn.sigmoid(xg)
    wlog = -_softplus(-(w0 + bdot(tw, w2) + ew)) - 0.5
    decay = jnp.exp(-jnp.exp(wlog))
    a = jax.nn.sigmoid(a0 + bdot(xa, a2) + ea)
    g = bdot(sg, g2)
    kk0 = xk * k_k
    nrm = jnp.sqrt(hbcast(hsum(kk0 * kk0)))
    kk = kk0 / jnp.maximum(nrm, 1e-12)
    k = xk * (1.0 + (a - 1.0) * k_a)
    return decay, k, -kk, kk * a, g, tw, sg


def f_post(y, r, k, v, g, ln_w, ln_b, r_k):
    mu = hbcast(hsum(y)) * (1.0 / RW_N)
    yc = y - mu
    var = hbcast(hsum(yc * yc)) * (1.0 / RW_N)
    yn = yc * lax.rsqrt(var + GN_EPS) * ln_w + ln_b
    bonus = hbcast(hsum(r * k * r_k)) * v
    return (yn + bonus) * g


def f_merge(ga, gr, ab, rb):
    return jax.nn.sigmoid(ga) * ab + jax.nn.sigmoid(gr) * rb


def f_swiglu(gg, uu):
    return gg * jax.nn.sigmoid(gg) * uu


def _row(tt, width, cb=0, rev_n=None):
    if rev_n is None:
        return pl.BlockSpec((tt, width), lambda i: (i, cb))
    return pl.BlockSpec((tt, width), lambda i: (rev_n - 1 - i, cb))


def _full(arr):
    nd = arr.ndim
    return pl.BlockSpec(arr.shape, lambda i: (0,) * nd)


def _acc_init(i_first, *refs):
    @pl.when(i_first)
    def _():
        for r in refs:
            r[...] = jnp.zeros_like(r)


def rms_fwd(x, g, *, tt=128):
    s_len, d = x.shape

    def body(x_ref, g_ref, o_ref):
        o_ref[...] = f_rms(x_ref[...], g_ref[...]).astype(BF16)

    return pl.pallas_call(
        body, name="rms_fwd", grid=(s_len // tt,),
        in_specs=[_row(tt, d), _full(g)], out_specs=_row(tt, d),
        out_shape=jax.ShapeDtypeStruct((s_len, d), BF16),
        compiler_params=_cparams(("parallel",)),
    )(x, g)


def rwkv_pre_fwd(proj, mu, w0, a0, k_k, k_a, w2, a2, g2, *, tt=64):
    s_len, c = proj.shape
    nt = s_len // tt
    sub = tt // SUBLANES

    def body(p_ref, pb_ref, mu_ref, w0_ref, a0_ref, kk_ref, ka_ref, w2_ref, a2_ref, g2_ref,
             r_ref, dec_ref, k_ref, v_ref, av_ref, bv_ref, g_ref, tw_ref, xa_ref, sg_ref):
        i = pl.program_id(0)
        cur = p_ref[...]
        edge = jnp.where(i > 0, pb_ref[SUBLANES - 1:SUBLANES, :], 0.0)
        rows = lax.broadcasted_iota(jnp.int32, cur.shape, 0)
        prev = jnp.where(rows == 0, edge, pltpu.roll(cur, 1, axis=0))
        xs = cur + (prev - cur) * mu_ref[...]
        xr, xk, xv = xs[:, 0:RW_C], xs[:, RW_C:2 * RW_C], xs[:, 2 * RW_C:3 * RW_C]
        xg = xs[:, 3 * RW_C:3 * RW_C + 512]
        xw = xs[:, 3 * RW_C + 512:3 * RW_C + 640]
        xa = xs[:, 3 * RW_C + 640:3 * RW_C + 768]
        zero = jnp.zeros((tt, RW_C), F32)
        dec, k, av, bv, g, tw, sg = f_pre(xk, xg, xw, xa, zero, zero, w0_ref[...], a0_ref[...], kk_ref[...],
                                          ka_ref[...], w2_ref[...], a2_ref[...], g2_ref[...])
        r_ref[...] = xr
        dec_ref[...] = dec
        k_ref[...] = k
        v_ref[...] = xv
        av_ref[...] = av
        bv_ref[...] = bv
        g_ref[...] = g
        tw_ref[...] = tw.astype(BF16)
        xa_ref[...] = xa.astype(BF16)
        sg_ref[...] = sg.astype(BF16)

    rows_f = jax.ShapeDtypeStruct((s_len, RW_C), F32)
    prev_spec = pl.BlockSpec((SUBLANES, c), lambda i: (jnp.maximum(i * sub - 1, 0), 0))
    params = [mu, w0, a0, k_k, k_a, w2, a2, g2]
    return pl.pallas_call(
        body, name="rwkv_pre_fwd", grid=(nt,),
        in_specs=[_row(tt, c), prev_spec] + [_full(p) for p in params],
        out_specs=[_row(tt, RW_C)] * 7 + [_row(tt, 128), _row(tt, 128), _row(tt, 512)],
        out_shape=[rows_f] * 7 + [jax.ShapeDtypeStruct((s_len, 128), BF16), jax.ShapeDtypeStruct((s_len, 128), BF16),
                                  jax.ShapeDtypeStruct((s_len, 512), BF16)],
        compiler_params=_cparams(("parallel",)),
    )(proj, proj, *params)


def rwkv_pre_bwd(proj, mu, w0, a0, k_k, k_a, w2, a2, g2, d_r, d_dec, d_k, d_v, d_av, d_bv, d_g, d_r2, d_k2, d_v2,
                 *, tt=32):
    s_len, c = proj.shape
    nt = s_len // tt
    sub = tt // SUBLANES

    def body(p_ref, pb_ref, mu_ref, w0_ref, a0_ref, kk_ref, ka_ref, w2_ref, a2_ref, g2_ref,
             dr_ref, ddec_ref, dk_ref, dv_ref, dav_ref, dbv_ref, dg_ref, dr2_ref, dk2_ref, dv2_ref,
             dp_ref, dzw_ref, dza_ref, dmu_ref, dw0_ref, da0_ref, dkk_ref, dka_ref, carry_ref):
        step = pl.program_id(0)
        i = nt - 1 - step
        _acc_init(step == 0, dmu_ref, dw0_ref, da0_ref, dkk_ref, dka_ref, carry_ref)
        cur = p_ref[...]
        edge = jnp.where(i > 0, pb_ref[SUBLANES - 1:SUBLANES, :], 0.0)
        rows = lax.broadcasted_iota(jnp.int32, cur.shape, 0)
        prev = jnp.where(rows == 0, edge, pltpu.roll(cur, 1, axis=0))
        mu_v = mu_ref[...]
        xs = cur + (prev - cur) * mu_v
        xk = xs[:, RW_C:2 * RW_C]
        xg = xs[:, 3 * RW_C:3 * RW_C + 512]
        xw = xs[:, 3 * RW_C + 512:3 * RW_C + 640]
        xa = xs[:, 3 * RW_C + 640:3 * RW_C + 768]
        zero = jnp.zeros((tt, RW_C), F32)
        w2_v, a2_v, g2_v = w2_ref[...], a2_ref[...], g2_ref[...]

        def core(xk, xg, xw, xa, ew, ea, w0, a0, k_k, k_a):
            return f_pre(xk, xg, xw, xa, ew, ea, w0, a0, k_k, k_a, w2_v, a2_v, g2_v)[:5]

        _, vjp = jax.vjp(core, xk, xg, xw, xa, zero, zero, w0_ref[...], a0_ref[...], kk_ref[...], ka_ref[...])
        dxk, dxg, dxw, dxa, dzw, dza, dw0, da0, dkk, dka = vjp(
            (ddec_ref[...], dk_ref[...] + dk2_ref[...], dav_ref[...], dbv_ref[...], dg_ref[...]))
        dzw_ref[...] = dzw.astype(BF16)
        dza_ref[...] = dza.astype(BF16)
        dw0_ref[...] += dw0
        da0_ref[...] += da0
        dkk_ref[...] += dkk
        dka_ref[...] += dka
        dxs = jnp.concatenate([dr_ref[...] + dr2_ref[...], dxk, dv_ref[...] + dv2_ref[...], dxg, dxw, dxa], axis=1)
        dmu_ref[...] += jnp.sum(dxs * (prev - cur), axis=0, keepdims=True)
        to_prev = dxs * mu_v
        nxt = jnp.where(rows == tt - 1, carry_ref[...], pltpu.roll(to_prev, tt - 1, axis=0))
        carry_ref[...] = to_prev[0:1, :]
        dp_ref[...] = (dxs * (1.0 - mu_v) + nxt).astype(BF16)

    prev_spec = pl.BlockSpec((SUBLANES, c), lambda s: (jnp.maximum((nt - 1 - s) * sub - 1, 0), 0))
    params = [mu, w0, a0, k_k, k_a, w2, a2, g2]
    cts = [d_r, d_dec, d_k, d_v, d_av, d_bv, d_g, d_r2, d_k2, d_v2]
    vec = jax.ShapeDtypeStruct((1, RW_C), F32)
    acc = pl.BlockSpec((1, RW_C), lambda s: (0, 0))
    return pl.pallas_call(
        body, name="rwkv_pre_bwd", grid=(nt,),
        in_specs=[_row(tt, c, rev_n=nt), prev_spec] + [_full(p) for p in params] + [_row(tt, RW_C, rev_n=nt)] * 10,
        out_specs=[_row(tt, c, rev_n=nt), _row(tt, RW_C, rev_n=nt), _row(tt, RW_C, rev_n=nt),
                   pl.BlockSpec((1, c), lambda s: (0, 0)), acc, acc, acc, acc],
        out_shape=[jax.ShapeDtypeStruct((s_len, c), BF16), jax.ShapeDtypeStruct((s_len, RW_C), BF16),
                   jax.ShapeDtypeStruct((s_len, RW_C), BF16), jax.ShapeDtypeStruct((1, c), F32), vec, vec, vec, vec],
        scratch_shapes=[pltpu.VMEM((1, c), F32)],
        compiler_params=_cparams(("arbitrary",)),
    )(proj, proj, *params, *cts)


def rwkv_post_fwd(y, r, k, v, g, ln_w, ln_b, r_k, *, tt=64):
    s_len = y.shape[0]

    def body(y_ref, r_ref, k_ref, v_ref, g_ref, lw_ref, lb_ref, rk_ref, o_ref):
        o_ref[...] = f_post(y_ref[...], r_ref[...], k_ref[...], v_ref[...], g_ref[...],
                            lw_ref[...], lb_ref[...], rk_ref[...]).astype(BF16)

    return pl.pallas_call(
        body, name="rwkv_post_fwd", grid=(s_len // tt,),
        in_specs=[_row(tt, RW_C)] * 5 + [_full(ln_w), _full(ln_b), _full(r_k)],
        out_specs=_row(tt, RW_C), out_shape=jax.ShapeDtypeStruct((s_len, RW_C), BF16),
        compiler_params=_cparams(("parallel",)),
    )(y, r, k, v, g, ln_w, ln_b, r_k)


def rwkv_post_bwd(y, r, k, v, g, ln_w, ln_b, r_k, d_o, *, tt=32):
    s_len = y.shape[0]

    def body(y_ref, r_ref, k_ref, v_ref, g_ref, lw_ref, lb_ref, rk_ref, do_ref,
             dy_ref, dr_ref, dk_ref, dv_ref, dg_ref, dlw_ref, dlb_ref, drk_ref):
        _acc_init(pl.program_id(0) == 0, dlw_ref, dlb_ref, drk_ref)
        _, vjp = jax.vjp(f_post, y_ref[...], r_ref[...], k_ref[...], v_ref[...], g_ref[...],
                         lw_ref[...], lb_ref[...], rk_ref[...])
        dy, dr, dk, dv, dg, dlw, dlb, drk = vjp(do_ref[...].astype(F32))
        dy_ref[...] = dy
        dr_ref[...] = dr
        dk_ref[...] = dk
        dv_ref[...] = dv
        dg_ref[...] = dg
        dlw_ref[...] += dlw
        dlb_ref[...] += dlb
        drk_ref[...] += drk

    rows_f = jax.ShapeDtypeStruct((s_len, RW_C), F32)
    vec = jax.ShapeDtypeStruct((1, RW_C), F32)
    acc = pl.BlockSpec((1, RW_C), lambda s: (0, 0))
    return pl.pallas_call(
        body, name="rwkv_post_bwd", grid=(s_len // tt,),
        in_specs=[_row(tt, RW_C)] * 5 + [_full(ln_w), _full(ln_b), _full(r_k), _row(tt, RW_C)],
        out_specs=[_row(tt, RW_C)] * 5 + [acc] * 3, out_shape=[rows_f] * 5 + [vec] * 3,
        compiler_params=_cparams(("arbitrary",)),
    )(y, r, k, v, g, ln_w, ln_b, r_k, d_o)


def merge_fwd(gate, ab, rb, *, tt=128):
    s_len, d = ab.shape

    def body(ga_ref, gr_ref, a_ref, r_ref, o_ref):
        o_ref[...] = f_merge(*(t[...].astype(F32) for t in (ga_ref, gr_ref, a_ref, r_ref))).astype(BF16)

    return pl.pallas_call(
        body, name="merge_fwd", grid=(s_len // tt,),
        in_specs=[_row(tt, d, 0), _row(tt, d, 1), _row(tt, d), _row(tt, d)],
        out_specs=_row(tt, d), out_shape=jax.ShapeDtypeStruct((s_len, d), BF16),
        compiler_params=_cparams(("parallel",)),
    )(gate, gate, ab, rb)


def merge_bwd(gate, ab, rb, d_m, *, tt=64):
    s_len, d = ab.shape

    def body(ga_ref, gr_ref, a_ref, r_ref, dm_ref, dgate_ref, da_ref, dr_ref):
        _, vjp = jax.vjp(f_merge, *(t[...].astype(F32) for t in (ga_ref, gr_ref, a_ref, r_ref)))
        dga, dgr, da, dr = vjp(dm_ref[...].astype(F32))
        dgate_ref[:, 0:d] = dga.astype(BF16)
        dgate_ref[:, d:2 * d] = dgr.astype(BF16)
        da_ref[...] = da.astype(BF16)
        dr_ref[...] = dr.astype(BF16)

    return pl.pallas_call(
        body, name="merge_bwd", grid=(s_len // tt,),
        in_specs=[_row(tt, d, 0), _row(tt, d, 1), _row(tt, d), _row(tt, d), _row(tt, d)],
        out_specs=[_row(tt, 2 * d), _row(tt, d), _row(tt, d)],
        out_shape=[jax.ShapeDtypeStruct((s_len, 2 * d), BF16), jax.ShapeDtypeStruct((s_len, d), BF16),
                   jax.ShapeDtypeStruct((s_len, d), BF16)],
        compiler_params=_cparams(("parallel",)),
    )(gate, gate, ab, rb, d_m)


def swiglu_fwd(gg, uu, *, tt=64):
    s_len, f = gg.shape

    def body(g_ref, u_ref, o_ref):
        o_ref[...] = f_swiglu(g_ref[...].astype(F32), u_ref[...].astype(F32)).astype(BF16)

    return pl.pallas_call(
        body, name="swiglu_fwd", grid=(s_len // tt,),
        in_specs=[_row(tt, f), _row(tt, f)], out_specs=_row(tt, f),
        out_shape=jax.ShapeDtypeStruct((s_len, f), BF16),
        compiler_params=_cparams(("parallel",)),
    )(gg, uu)


def swiglu_bwd(gg, uu, d_act, *, tt=32):
    s_len, f = gg.shape

    def body(g_ref, u_ref, d_ref, dg_ref, du_ref):
        _, vjp = jax.vjp(f_swiglu, g_ref[...].astype(F32), u_ref[...].astype(F32))
        dg, du = vjp(d_ref[...].astype(F32))
        dg_ref[...] = dg.astype(BF16)
        du_ref[...] = du.astype(BF16)

    out = jax.ShapeDtypeStruct((s_len, f), BF16)
    return pl.pallas_call(
        body, name="swiglu_bwd", grid=(s_len // tt,),
        in_specs=[_row(tt, f)] * 3, out_specs=[_row(tt, f)] * 2, out_shape=[out, out],
        compiler_params=_cparams(("parallel",)),
    )(gg, uu, d_act)


def resid_norm_fwd(x, m2, g_post, g_pre, *, tt=128):
    s_len, d = x.shape

    def body(x_ref, m_ref, gp_ref, gn_ref, x1_ref, h_ref):
        x1 = x_ref[...] + f_rms(m_ref[...], gp_ref[...])
        x1_ref[...] = x1
        h_ref[...] = f_rms(x1, gn_ref[...]).astype(BF16)

    return pl.pallas_call(
        body, name="resid_norm_fwd", grid=(s_len // tt,),
        in_specs=[_row(tt, d), _row(tt, d), _full(g_post), _full(g_pre)],
        out_specs=[_row(tt, d), _row(tt, d)],
        out_shape=[jax.ShapeDtypeStruct((s_len, d), F32), jax.ShapeDtypeStruct((s_len, d), BF16)],
        compiler_params=_cparams(("parallel",)),
    )(x, m2, g_post, g_pre)


def loss_head(x1, ff, tgt, g_post, *, tt=64):
    s_len, d = x1.shape

    def body(x1_ref, f_ref, t_ref, g_ref, loss_ref, dy_ref, df_ref, dg_ref):
        _acc_init(pl.program_id(0) == 0, loss_ref, dg_ref)
        nrm, vjp = jax.vjp(f_rms, f_ref[...], g_ref[...])
        err = x1_ref[...] + nrm - t_ref[...]
        per_tok = jnp.mean(err * err, axis=-1, keepdims=True)
        loss_ref[...] += 0.5 * jnp.sum(per_tok, axis=0, keepdims=True)
        dy = err * (1.0 / d)
        dff, dg = vjp(dy)
        dy_ref[...] = dy
        df_ref[...] = dff.astype(BF16)
        dg_ref[...] += dg

    return pl.pallas_call(
        body, name="loss_head", grid=(s_len // tt,),
        in_specs=[_row(tt, d)] * 3 + [_full(g_post)],
        out_specs=[pl.BlockSpec((1, LANES), lambda s: (0, 0)), _row(tt, d), _row(tt, d),
                   pl.BlockSpec((1, d), lambda s: (0, 0))],
        out_shape=[jax.ShapeDtypeStruct((1, LANES), F32), jax.ShapeDtypeStruct((s_len, d), F32),
                   jax.ShapeDtypeStruct((s_len, d), BF16), jax.ShapeDtypeStruct((1, d), F32)],
        compiler_params=_cparams(("arbitrary",)),
    )(x1, ff, tgt, g_post)


def resid_norm_bwd(x1, dh_a, dh_b, g_pre, m2, g_post, dy, *, tt=64):
    s_len, d = x1.shape

    def body(x1_ref, da_ref, db_ref, gn_ref, m_ref, gp_ref, dy_ref, dx1_ref, dm_ref, dgn_ref, dgp_ref):
        _acc_init(pl.program_id(0) == 0, dgn_ref, dgp_ref)
        _, vjp_n = jax.vjp(f_rms, x1_ref[...], gn_ref[...])
        dx1_n, dgn = vjp_n(da_ref[...].astype(F32) + db_ref[...].astype(F32))
        dx1 = dy_ref[...] + dx1_n
        _, vjp_p = jax.vjp(f_rms, m_ref[...], gp_ref[...])
        dm, dgp = vjp_p(dx1)
        dx1_ref[...] = dx1
        dm_ref[...] = dm.astype(BF16)
        dgn_ref[...] += dgn
        dgp_ref[...] += dgp

    acc = pl.BlockSpec((1, d), lambda s: (0, 0))
    vec = jax.ShapeDtypeStruct((1, d), F32)
    return pl.pallas_call(
        body, name="resid_norm_bwd", grid=(s_len // tt,),
        in_specs=[_row(tt, d)] * 3 + [_full(g_pre), _row(tt, d), _full(g_post), _row(tt, d)],
        out_specs=[_row(tt, d), _row(tt, d), acc, acc],
        out_shape=[jax.ShapeDtypeStruct((s_len, d), F32), jax.ShapeDtypeStruct((s_len, d), BF16), vec, vec],
        compiler_params=_cparams(("arbitrary",)),
    )(x1, dh_a, dh_b, g_pre, m2, g_post, dy)


def rms_bwd(x, g, dh_a, dh_b, dh_c, dres, *, tt=64):
    s_len, d = x.shape

    def body(x_ref, g_ref, a_ref, b_ref, c_ref, r_ref, dx_ref, dg_ref):
        _acc_init(pl.program_id(0) == 0, dg_ref)
        _, vjp = jax.vjp(f_rms, x_ref[...], g_ref[...])
        dx, dg = vjp(a_ref[...].astype(F32) + b_ref[...].astype(F32) + c_ref[...].astype(F32))
        dx_ref[...] = r_ref[...] + dx
        dg_ref[...] += dg

    return pl.pallas_call(
        body, name="rms_bwd", grid=(s_len // tt,),
        in_specs=[_row(tt, d), _full(g)] + [_row(tt, d)] * 4,
        out_specs=[_row(tt, d), pl.BlockSpec((1, d), lambda s: (0, 0))],
        out_shape=[jax.ShapeDtypeStruct((s_len, d), F32), jax.ShapeDtypeStruct((1, d), F32)],
        compiler_params=_cparams(("arbitrary",)),
    )(x, g, dh_a, dh_b, dh_c, dres)


def colsum(a, *, tt=256):
    s_len, c = a.shape

    def body(a_ref, o_ref):
        _acc_init(pl.program_id(0) == 0, o_ref)
        o_ref[...] += jnp.sum(a_ref[...].astype(F32), axis=0, keepdims=True)

    return pl.pallas_call(
        body, name="colsum", grid=(s_len // tt,),
        in_specs=[_row(tt, c)], out_specs=pl.BlockSpec((1, c), lambda s: (0, 0)),
        out_shape=jax.ShapeDtypeStruct((1, c), F32),
        compiler_params=_cparams(("arbitrary",)),
    )(a)


AT_HD = 128
AT_GROUP = 4
AT_KVH = 8
AT_BLK = 128
AT_QW = AT_GROUP * AT_HD
AT_KCOL = AT_KVH * AT_GROUP
AT_VCOL = AT_KCOL + AT_KVH
NEG_INF = -1e30
AT_SCALE = AT_HD ** -0.5


def _rope(t, cos2, sin2):
    return t * cos2 + pltpu.roll(t, AT_HD // 2, axis=1) * sin2


def _rope_t(d, cos2, sin2):
    return d * cos2 + pltpu.roll(d * sin2, AT_HD // 2, axis=1)


def _att_specs():
    prev = lambda i: jnp.maximum(i - 1, 0)
    blk = (AT_BLK, AT_HD)
    return [
        pl.BlockSpec((AT_BLK, AT_QW), lambda h, i: (i, h)),
        pl.BlockSpec(blk, lambda h, i: (i, AT_KCOL + h)),
        pl.BlockSpec(blk, lambda h, i: (prev(i), AT_KCOL + h)),
        pl.BlockSpec(blk, lambda h, i: (i, AT_VCOL + h)),
        pl.BlockSpec(blk, lambda h, i: (prev(i), AT_VCOL + h)),
        pl.BlockSpec((1, AT_QW), lambda h, i: (0, h)),
        pl.BlockSpec((1, AT_HD), lambda h, i: (0, AT_KCOL + h)),
        pl.BlockSpec((1, AT_HD), lambda h, i: (0, AT_VCOL + h)),
        pl.BlockSpec((1, AT_GROUP, AT_HD), lambda h, i: (h, 0, 0)),
        pl.BlockSpec(blk, lambda h, i: (i, 0)),
        pl.BlockSpec(blk, lambda h, i: (i, 0)),
        pl.BlockSpec(blk, lambda h, i: (prev(i), 0)),
        pl.BlockSpec(blk, lambda h, i: (prev(i), 0)),
    ]


def _att_load(i, q_ref, kc_ref, kp_ref, vc_ref, vp_ref, bq_ref, bk_ref, bv_ref, cc_ref, sc_ref, cp_ref, sp_ref):
    cosc, sinc = cc_ref[...], sc_ref[...]
    q = q_ref[...] + bq_ref[...]
    kc = _rope(kc_ref[...] + bk_ref[...], cosc, sinc)
    kp = _rope(kp_ref[...] + bk_ref[...], cp_ref[...], sp_ref[...])
    kcat = jnp.concatenate([kp, kc], axis=0).astype(BF16)
    vcat = jnp.concatenate([vp_ref[...] + bv_ref[...], vc_ref[...] + bv_ref[...]], axis=0).astype(BF16)
    qi = lax.broadcasted_iota(jnp.int32, (AT_GROUP * AT_BLK, 2 * AT_BLK), 0) & (AT_BLK - 1)
    kj = lax.broadcasted_iota(jnp.int32, (AT_GROUP * AT_BLK, 2 * AT_BLK), 1)
    rel = qi + AT_BLK - kj
    mask = (rel >= 0) & (rel < AT_BLK) & ((kj >= AT_BLK) | (i > 0))
    return q, kcat, vcat, mask, cosc, sinc


AT_ROWS = AT_GROUP * AT_BLK


def _att_stack(q, cosc, sinc):
    return jnp.concatenate([_rope(q[:, g * AT_HD:(g + 1) * AT_HD], cosc, sinc) for g in range(AT_GROUP)], axis=0)


def _att_cols(sk_ref):
    head = lax.broadcasted_iota(jnp.int32, (AT_ROWS, 1), 0) >> 7
    sink = jnp.zeros((AT_ROWS, 1), F32)
    for g in range(AT_GROUP):
        sink = jnp.where(head == g, sk_ref[0, g:g + 1, 0:1], sink)
    return sink, head


def _att_probs(qs, kcat, mask, sink):
    s = lax.dot_general(qs, kcat, (((1,), (1,)), ((), ())), preferred_element_type=F32) * AT_SCALE
    s = jnp.where(mask, s, NEG_INF)
    m = jnp.maximum(jnp.max(s, axis=-1, keepdims=True), sink)
    p = jnp.exp(s - m)
    es = jnp.exp(sink - m)
    inv = 1.0 / (jnp.sum(p, axis=-1, keepdims=True) + es)
    return p * inv, es * inv


def attention_fwd(qkv, bias, sinks_b, cos2, sin2):
    s_len = qkv.shape[0]
    nb = s_len // AT_BLK

    def body(q_ref, kc_ref, kp_ref, vc_ref, vp_ref, bq_ref, bk_ref, bv_ref, sk_ref, cc_ref, sc_ref, cp_ref, sp_ref,
             o_ref):
        i = pl.program_id(1)
        q, kcat, vcat, mask, cosc, sinc = _att_load(i, q_ref, kc_ref, kp_ref, vc_ref, vp_ref, bq_ref, bk_ref,
                                                    bv_ref, cc_ref, sc_ref, cp_ref, sp_ref)
        sink, _ = _att_cols(sk_ref)
        probs, _ = _att_probs(_att_stack(q, cosc, sinc).astype(BF16), kcat, mask, sink)
        o = jnp.dot(probs.astype(BF16), vcat, preferred_element_type=F32).astype(BF16)
        for g in range(AT_GROUP):
            o_ref[:, g * AT_HD:(g + 1) * AT_HD] = o[g * AT_BLK:(g + 1) * AT_BLK, :]

    return pl.pallas_call(
        body, name="attention_fwd", grid=(AT_KVH, nb),
        in_specs=_att_specs(),
        out_specs=pl.BlockSpec((AT_BLK, AT_QW), lambda h, i: (i, h)),
        out_shape=jax.ShapeDtypeStruct((s_len, AT_KVH * AT_QW), BF16),
        compiler_params=_cparams(("parallel", "parallel")),
    )(qkv, qkv, qkv, qkv, qkv, bias, bias, bias, sinks_b, cos2, sin2, cos2, sin2)


def attention_bwd(qkv, bias, sinks_b, cos2, sin2, d_o, rider=None):
    s_len = qkv.shape[0]
    nb = s_len // AT_BLK
    x_in, x_out, x_scr = _rider_parts(rider)
    ni, no = len(x_in), len(x_out)

    def body(*refs):
        (q_ref, kc_ref, kp_ref, vc_ref, vp_ref, bq_ref, bk_ref, bv_ref, sk_ref, cc_ref, sc_ref, cp_ref, sp_ref,
         do_ref) = refs[:14]
        dq_ref, dk_ref, dv_ref, dsk_ref = refs[14 + ni:18 + ni]
        ride = (refs[14:14 + ni], refs[18 + ni:18 + ni + no], *refs[18 + ni + no:])
        i = pl.program_id(1)
        if rider is not None:
            @pl.when((pl.program_id(0) == 0) & (i == 0))
            def _():
                rider["start"](*ride)

        _acc_init(i == 0, dsk_ref)
        q, kcat, vcat, mask, cosc, sinc = _att_load(i, q_ref, kc_ref, kp_ref, vc_ref, vp_ref, bq_ref, bk_ref,
                                                    bv_ref, cc_ref, sc_ref, cp_ref, sp_ref)
        sink, head = _att_cols(sk_ref)
        qs = _att_stack(q, cosc, sinc).astype(BF16)
        probs, psink = _att_probs(qs, kcat, mask, sink)
        pb = probs.astype(BF16)
        do_s = jnp.concatenate([do_ref[:, g * AT_HD:(g + 1) * AT_HD] for g in range(AT_GROUP)], axis=0)
        do_f = do_s.astype(F32)
        do_b = do_s.astype(BF16)
        o_s = jnp.dot(pb, vcat, preferred_element_type=F32)
        dsum = jnp.sum(do_f * o_s, axis=-1, keepdims=True)
        dp = lax.dot_general(do_b, vcat, (((1,), (1,)), ((), ())), preferred_element_type=F32)
        ds = (probs * (dp - dsum) * AT_SCALE).astype(BF16)
        dv_cat = lax.dot_general(pb, do_b, (((0,), (0,)), ((), ())), preferred_element_type=F32)
        dk_cat = lax.dot_general(ds, qs, (((0,), (0,)), ((), ())), preferred_element_type=F32)
        dq_s = jnp.dot(ds, kcat, preferred_element_type=F32)
        lane = lax.broadcasted_iota(jnp.int32, (1, AT_HD), 1)
        dsk = jnp.zeros((1, AT_HD), F32)
        sink_term = psink * dsum
        for g in range(AT_GROUP):
            rows = slice(g * AT_BLK, (g + 1) * AT_BLK)
            dq_ref[:, g * AT_HD:(g + 1) * AT_HD] = _rope_t(dq_s[rows, :], cosc, sinc).astype(BF16)
            dsk = dsk + jnp.where(lane == g, -jnp.sum(sink_term[rows, :], axis=0, keepdims=True), 0.0)
        dsk_ref[0] += dsk
        cur = pl.ds(pl.multiple_of(i * AT_BLK, AT_BLK), AT_BLK)
        dk_ref[cur, :] = _rope_t(dk_cat[AT_BLK:], cosc, sinc)
        dv_ref[cur, :] = dv_cat[AT_BLK:]

        @pl.when(i > 0)
        def _():
            prv = pl.ds(pl.multiple_of((i - 1) * AT_BLK, AT_BLK), AT_BLK)
            dk_ref[prv, :] += _rope_t(dk_cat[:AT_BLK], cp_ref[...], sp_ref[...])
            dv_ref[prv, :] += dv_cat[:AT_BLK]

        if rider is not None:
            @pl.when((pl.program_id(0) == AT_KVH - 1) & (i == nb - 1))
            def _():
                rider["finish"](*ride)

    kv_out = pl.BlockSpec((s_len, AT_HD), lambda h, i: (0, h))
    return pl.pallas_call(
        body, name="attention_bwd", grid=(AT_KVH, nb),
        in_specs=_att_specs() + [pl.BlockSpec((AT_BLK, AT_QW), lambda h, i: (i, h))] + [ANY] * ni,
        out_specs=[pl.BlockSpec((AT_BLK, AT_QW), lambda h, i: (i, h)), kv_out, kv_out,
                   pl.BlockSpec((1, 1, AT_HD), lambda h, i: (h, 0, 0))] + [ANY] * no,
        out_shape=[jax.ShapeDtypeStruct((s_len, AT_KVH * AT_QW), BF16),
                   jax.ShapeDtypeStruct((s_len, AT_KVH * AT_HD), F32),
                   jax.ShapeDtypeStruct((s_len, AT_KVH * AT_HD), F32),
                   jax.ShapeDtypeStruct((AT_KVH, 1, AT_HD), F32)] + x_out,
        scratch_shapes=x_scr,
        compiler_params=_cparams(("arbitrary", "arbitrary")),
    )(qkv, qkv, qkv, qkv, qkv, bias, bias, bias, sinks_b, cos2, sin2, cos2, sin2, d_o, *x_in)


ATT_QKV = 6144
RW_SHIFT = 13024
RW_PAD = 13056
N_CHIPS = 4
D_GATE = 480
ROPE_THETA = 10000.0


def perm_cols(a):
    lead = a.shape[:-1]
    return jnp.swapaxes(a.reshape(lead + (RW_H, RW_N)), -1, -2).reshape(lead + (RW_C,))


def rw_reorder(a, pad_value=0):
    r, k, v = (perm_cols(a[..., i * RW_C:(i + 1) * RW_C]) for i in range(3))
    wd = a[..., 3 * RW_C:3 * RW_C + 128]
    ad = a[..., 3 * RW_C + 128:3 * RW_C + 256]
    gd = a[..., 3 * RW_C + 256:]
    pad = jnp.full(a.shape[:-1] + (512 - D_GATE,), pad_value, a.dtype)
    return jnp.concatenate([r, k, v, gd, pad, wd, ad], axis=-1)


def rw_restore(a):
    r, k, v = (perm_cols(a[..., i * RW_C:(i + 1) * RW_C]) for i in range(3))
    gd = a[..., 3 * RW_C:3 * RW_C + D_GATE]
    wd = a[..., 3 * RW_C + 512:3 * RW_C + 640]
    ad = a[..., 3 * RW_C + 640:3 * RW_C + 768]
    return jnp.concatenate([r, k, v, wd, ad, gd], axis=-1)


W_IN_PIECES = (ATT_QKV, RW_C, RW_C, RW_C, 128, 128, D_GATE, 2 * RW_C)


def pieces_to_blocks(pieces, n_blocks):
    bw = sum(p.shape[1] for p in pieces) // n_blocks
    blocks = []
    for s in range(n_blocks):
        parts, off = [], 0
        for p in pieces:
            lo, hi = max(s * bw, off), min((s + 1) * bw, off + p.shape[1])
            if lo < hi:
                parts.append(p[:, lo - off:hi - off])
            off += p.shape[1]
        blocks.append(jnp.concatenate(parts, axis=1))
    return jnp.stack(blocks)


def blocks_to_pieces(g4, widths):
    bw = g4.shape[2]
    pieces, off = [], 0
    for w in widths:
        parts = []
        for s in range(g4.shape[0]):
            lo, hi = max(off, s * bw), min(off + w, (s + 1) * bw)
            if lo < hi:
                parts.append(g4[s][:, lo - s * bw:hi - s * bw])
        pieces.append(parts[0] if len(parts) == 1 else jnp.concatenate(parts, axis=1))
        off += w
    return pieces


def to_tiles(a):
    t = a.reshape(a.shape[0], RW_N, RW_H)
    return jnp.concatenate([t, t], axis=-1)


def from_tiles(t):
    return t[:, :, :RW_H].reshape(t.shape[0], RW_C)


def rope_tables(s_len):
    pos = jnp.arange(s_len, dtype=F32)
    inv_freq = ROPE_THETA ** (-jnp.arange(0, AT_HD, 2, dtype=F32) / AT_HD)
    ang = pos[:, None] * inv_freq[None, :]
    cos, sin = jnp.cos(ang), jnp.sin(ang)
    return jnp.concatenate([cos, cos], axis=1), jnp.concatenate([-sin, sin], axis=1)


def local_step(x, tgt, small, big, fwd_rider, got_early, att_rider, bwd_rider, late_riders):
    s_len, d = x.shape
    w_qkv, p_r, p_k, p_v, p_wd, p_ad, p_gd, w_gate = blocks_to_pieces(big["w_in"], W_IN_PIECES)
    w_rw = jnp.concatenate([perm_cols(p_r), perm_cols(p_k), perm_cols(p_v), p_gd,
                            jnp.zeros((d, 512 - D_GATE), BF16), p_wd, p_ad], axis=1)
    w2 = perm_cols(big["w2"])
    a2 = perm_cols(big["a2"])
    g2 = jnp.pad(perm_cols(big["g2"]), ((0, 512 - D_GATE), (0, 0)))
    mu = rw_reorder(small["mu_shift"])
    w0, a0, k_k, k_a, ln_w, ln_b = (perm_cols(small[n]) for n in ("w0", "a0", "k_k", "k_a", "ln_x_w", "ln_x_b"))
    r_k = small["r_k"].reshape(RW_H, RW_N).T.reshape(1, RW_C)
    sinks_b = jnp.broadcast_to(small["att_sinks"].reshape(AT_KVH, AT_GROUP, 1), (AT_KVH, AT_GROUP, AT_HD))
    cos2, sin2 = rope_tables(s_len)
    bias = small["b_qkv"]

    h = rms_fwd(x, small["norm_mix_pre"])
    qkv = matmul(h, w_qkv, name="mm_qkv")
    prw = matmul(h, w_rw, name="mm_rw")
    gate = matmul(h, w_gate, out_dtype=BF16, name="mm_gate")
    o_att = attention_fwd(qkv, bias, sinks_b, cos2, sin2)
    pre_params = (mu, w0, a0, k_k, k_a, w2, a2, g2)
    r, dec, k, v, av, bv, g, tw, xa, sg = rwkv_pre_fwd(prw, *pre_params)
    v3 = to_tiles(v)
    y3, hist, sa3, *arrived = rwkv_scan_fwd(r, dec, k, av, bv, v3, rider=fwd_rider)
    big = {**big, **got_early(arrived)}
    w_rb = big["w_rwkv_branch"].reshape(RW_H, RW_N, d).swapaxes(0, 1).reshape(RW_C, d)
    y = from_tiles(y3)
    o_rw = rwkv_post_fwd(y, r, k, v, g, ln_w, ln_b, r_k)
    ab = matmul(o_att, big["w_att_branch"], out_dtype=BF16, name="mm_ab")
    rb = matmul(o_rw, w_rb, out_dtype=BF16, name="mm_rb")
    merged = merge_fwd(gate, ab, rb)
    m2 = matmul(merged, big["w_out"], name="mm_out")
    x1, h2 = resid_norm_fwd(x, m2, small["norm_mix_post"], small["norm_ffn_pre"])
    gg = matmul(h2, big["w_ffn_gate"], out_dtype=BF16, name="mm_fg")
    uu = matmul(h2, big["w_ffn_up"], out_dtype=BF16, name="mm_fu")
    act = swiglu_fwd(gg, uu)
    ff = matmul(act, big["w_ffn_down"], name="mm_fd")
    loss, dy, dff, d_nfp = loss_head(x1, ff, tgt, small["norm_ffn_post"])

    dact = matmul(dff, big["w_ffn_down"], tb=True, out_dtype=BF16, name="mm_dact")
    g_fd = matmul(act, dff, ta=True, out_dtype=BF16, name="mm_gfd")
    dgg, duu = swiglu_bwd(gg, uu, dact)
    g_fg = matmul(h2, dgg, ta=True, out_dtype=BF16, name="mm_gfg")
    g_fu = matmul(h2, duu, ta=True, out_dtype=BF16, name="mm_gfu")
    dh2a = matmul(dgg, big["w_ffn_gate"], tb=True, out_dtype=BF16, name="mm_dh2a")
    dh2b = matmul(duu, big["w_ffn_up"], tb=True, out_dtype=BF16, name="mm_dh2b")
    dx1, dm2, d_nfpre, d_nmpost = resid_norm_bwd(x1, dh2a, dh2b, small["norm_ffn_pre"], m2, small["norm_mix_post"], dy)
    dmerged = matmul(dm2, big["w_out"], tb=True, out_dtype=BF16, name="mm_dmerged")
    g_out = matmul(merged, dm2, ta=True, out_dtype=BF16, name="mm_gout")
    dgate, dab, drb = merge_bwd(gate, ab, rb, dmerged)
    do_att = matmul(dab, big["w_att_branch"], tb=True, out_dtype=BF16, name="mm_doatt")
    g_ab = matmul(o_att, dab, ta=True, out_dtype=BF16, name="mm_gab")
    do_rw = matmul(drb, w_rb, tb=True, out_dtype=BF16, name="mm_dorw")
    g_rb = matmul(o_rw, drb, ta=True, out_dtype=BF16, name="mm_grb")
    early = {"w_att_branch": g_ab, "w_rwkv_branch": g_rb.reshape(RW_N, RW_H, d).swapaxes(0, 1).reshape(RW_C, d),
             "w_out": g_out, "w_ffn_gate": g_fg, "w_ffn_up": g_fu, "w_ffn_down": g_fd}
    dq, dk_att, dv_att, dsk, *from_sibling = attention_bwd(qkv, bias, sinks_b, cos2, sin2, do_att,
                                                           rider=att_rider(early))
    dqkv = jnp.concatenate([dq, dk_att.astype(BF16), dv_att.astype(BF16)], axis=1)
    dy_s, dr_p, dk_p, dv_p, dg, d_lnw, d_lnb, d_rk = rwkv_post_bwd(y, r, k, v, g, ln_w, ln_b, r_k, do_rw)
    dr_s, ddec, dk_s, dav, dbv, dv3, *from_chips = rwkv_scan_bwd(r, dec, k, av, bv, v3, hist, sa3, to_tiles(dy_s),
                                                                  rider=bwd_rider(from_sibling))
    dprw, dzw, dza, dmu, dw0, da0, dkk, dka = rwkv_pre_bwd(
        prw, *pre_params, dr_p, ddec, dk_p, dv_p, dav, dbv, dg, dr_s, dk_s, from_tiles(dv3))
    g_w2 = matmul(tw, dzw, ta=True, out_dtype=BF16, name="mm_gw2")
    g_a2 = matmul(xa, dza, ta=True, out_dtype=BF16, name="mm_ga2")
    g_g2 = matmul(sg, dg.astype(BF16), ta=True, out_dtype=BF16, name="mm_gg2")
    g_qkv = matmul(h, dqkv, ta=True, out_dtype=BF16, name="mm_gqkv")
    g_rw = matmul(h, dprw, ta=True, out_dtype=BF16, name="mm_grw")
    g_gate = matmul(h, dgate, ta=True, out_dtype=BF16, name="mm_ggate")
    g_r, g_k, g_v = (perm_cols(g_rw[:, i * RW_C:(i + 1) * RW_C]) for i in range(3))
    gbig = {
        "w_in": pieces_to_blocks([g_qkv, g_r, g_k, g_v, g_rw[:, 3 * RW_C + 512:3 * RW_C + 640],
                                  g_rw[:, 3 * RW_C + 640:3 * RW_C + 768], g_rw[:, 3 * RW_C:3 * RW_C + D_GATE], g_gate],
                                 N_CHIPS),
        "w2": perm_cols(g_w2), "a2": perm_cols(g_a2), "g2": perm_cols(g_g2)[:D_GATE],
    }
    ride_a, ride_b, ride_c = late_riders(gbig)
    dh_a, *late_a = matmul(dqkv, w_qkv, tb=True, out_dtype=BF16, name="mm_dha", rider=ride_a)
    dh_b, *late_b = matmul(dprw, w_rw, tb=True, out_dtype=BF16, name="mm_dhb", rider=ride_b)
    dh_c, *late_c = matmul(dgate, w_gate, tb=True, out_dtype=BF16, name="mm_dhc", rider=ride_c)
    grad_x, d_nmpre = rms_bwd(x, small["norm_mix_pre"], dh_a, dh_b, dh_c, dx1)
    d_bias = colsum(dqkv)

    gsmall = {
        "norm_mix_pre": d_nmpre, "norm_mix_post": d_nmpost, "norm_ffn_pre": d_nfpre, "norm_ffn_post": d_nfp,
        "b_qkv": d_bias, "att_sinks": dsk[:, 0, :AT_GROUP].reshape(1, AT_KVH * AT_GROUP),
        "mu_shift": rw_restore(dmu), "w0": perm_cols(dw0), "a0": perm_cols(da0), "k_k": perm_cols(dkk),
        "k_a": perm_cols(dka), "r_k": d_rk.reshape(RW_N, RW_H).T.reshape(1, RW_C),
        "ln_x_w": perm_cols(d_lnw), "ln_x_b": perm_cols(d_lnb),
    }
    return loss, grad_x, gsmall, from_chips, (late_a, late_b, late_c)


def _place():
    x, y, c = lax.axis_index("x"), lax.axis_index("y"), lax.axis_index("c")
    chips = [(1 - x, y), (x, 1 - y), (1 - x, 1 - y)]
    return x, y, c, chips


def _remote(src, dst, send_sems, recv_sems, k, dev):
    return pltpu.make_async_remote_copy(src_ref=src, dst_ref=dst, send_sem=send_sems.at[k], recv_sem=recv_sems.at[k],
                                        device_id=dev, device_id_type=MESH)


def _gather_parts(n):
    def half(ref, which):
        hr = ref.shape[0] // 2
        return ref.at[pl.ds(which * hr, hr), :]

    def sends(ins, outs, send_sems, recv_sems):
        x, y, c, chips = _place()
        me = 2 * x + y
        return [_remote(half(ins[i], c), half(outs[i].at[me], c), send_sems, recv_sems, 6 * i + j, (*chip, c))
                for i in range(n) for j, chip in enumerate(chips)]

    def start(ins, outs, send_sems, recv_sems):
        for cp in sends(ins, outs, send_sems, recv_sems):
            cp.start()

    def finish(ins, outs, send_sems, recv_sems):
        x, y, c, chips = _place()
        sib = (x, y, 1 - c)
        passed = []
        for i in range(n):
            for j, chip in enumerate(chips):
                got = half(outs[i].at[2 * chip[0] + chip[1]], c)
                _remote(got, got, send_sems, recv_sems, 6 * i + j, sib).wait_recv()
                cp = _remote(got, got, send_sems, recv_sems, 6 * i + 3 + j, sib)
                cp.start()
                passed.append(cp)
        for i in range(n):
            for j, chip in enumerate(chips):
                got = half(outs[i].at[2 * chip[0] + chip[1]], 1 - c)
                _remote(got, got, send_sems, recv_sems, 6 * i + 3 + j, sib).wait_recv()
        for cp in sends(ins, outs, send_sems, recv_sems) + passed:
            cp.wait_send()

    return start, finish


def gather_rider(shards):
    n = len(shards)
    start, finish = _gather_parts(n)
    return {"ins": shards, "out_shapes": [jax.ShapeDtypeStruct((4,) + s.shape, s.dtype) for s in shards],
            "scratch": [pltpu.SemaphoreType.DMA((6 * n,)), pltpu.SemaphoreType.DMA((6 * n,))],
            "start": start, "finish": finish}


def gather_weights(shards):
    n = len(shards)
    start, finish = _gather_parts(n)

    def body(*refs):
        parts = (refs[:n], refs[n:2 * n], *refs[2 * n:])
        start(*parts)
        finish(*parts)

    return pl.pallas_call(
        body, name="gather_weights",
        in_specs=[ANY] * n, out_specs=[ANY] * n,
        out_shape=[jax.ShapeDtypeStruct((4,) + s.shape, s.dtype) for s in shards],
        scratch_shapes=[pltpu.SemaphoreType.DMA((6 * n,)), pltpu.SemaphoreType.DMA((6 * n,))],
    )(*shards)


def swap_with_sibling(blocks, name):
    n = len(blocks)

    def body(*refs):
        ins, outs = refs[:n], refs[n:2 * n]
        send_sems, recv_sems = refs[2 * n:]
        x, y, c, _ = _place()
        cps = [_remote(ins[i], outs[i], send_sems, recv_sems, i, (x, y, 1 - c)) for i in range(n)]
        for cp in cps:
            cp.start()
        for cp in cps:
            cp.wait()

    return pl.pallas_call(
        body, name=name, in_specs=[ANY] * n, out_specs=[ANY] * n,
        out_shape=[jax.ShapeDtypeStruct(b.shape, b.dtype) for b in blocks],
        scratch_shapes=[pltpu.SemaphoreType.DMA((n,)), pltpu.SemaphoreType.DMA((n,))],
    )(*blocks)


def _scatter_parts(n, rows=None):
    def piece(ref, i):
        return ref if rows is None or rows[i] is None else ref.at[pl.ds(rows[i][0], rows[i][1]), :]

    def copies(ins, outs, send_sems, recv_sems):
        x, y, c, chips = _place()
        return [_remote(piece(ins[i].at[2 * chip[0] + chip[1]], i), outs[i].at[j], send_sems, recv_sems, 3 * i + j,
                        (*chip, c))
                for i in range(n) for j, chip in enumerate(chips)]

    def start(*refs):
        for cp in copies(*refs):
            cp.start()

    def finish(*refs):
        for cp in copies(*refs):
            cp.wait()

    return start, finish


def scatter_rider(parts, rows=None):
    n = len(parts)
    start, finish = _scatter_parts(n, rows)
    nrows = [p.shape[1] if rows is None or rows[i] is None else rows[i][1] for i, p in enumerate(parts)]
    return {"ins": parts, "out_shapes": [jax.ShapeDtypeStruct((3, r, p.shape[2]), p.dtype) for p, r in zip(parts, nrows)],
            "scratch": [pltpu.SemaphoreType.DMA((3 * n,)), pltpu.SemaphoreType.DMA((3 * n,))],
            "start": start, "finish": finish}


def allreduce_small(v):
    rows = v.shape[0]

    def body(v_ref, o_ref, buf, send_sems, recv_sems):
        x, y, c, chips = _place()
        me, sib = (x, y, c), (x, y, 1 - c)

        def slot(px, py, pc):
            return buf.at[4 * px + 2 * py + pc]

        def copy(k, block, to, src=None):
            return _remote(slot(*block) if src is None else src, slot(*block), send_sems, recv_sems, k, to)

        buf[4 * x + 2 * y + c] = v_ref[...]
        first = [copy(0, me, sib, src=v_ref)]
        first += [copy(1 + j, me, (*chip, c), src=v_ref) for j, chip in enumerate(chips)]
        for cp in first:
            cp.start()
        passed = [copy(4 + j, (*chip, c), sib) for j, chip in enumerate(chips)]
        for j, chip in enumerate(chips):
            copy(1 + j, (*chip, c), me).wait_recv()
            passed[j].start()
        copy(0, sib, me).wait_recv()
        for j, chip in enumerate(chips):
            copy(4 + j, (*chip, 1 - c), me).wait_recv()
        for cp in first + passed:
            cp.wait_send()
        acc = buf[0]
        for k in range(1, 8):
            acc = acc + buf[k]
        o_ref[...] = acc

    vm = pl.BlockSpec(memory_space=pltpu.VMEM)
    return pl.pallas_call(
        body, name="allreduce_small", in_specs=[vm], out_specs=vm,
        out_shape=jax.ShapeDtypeStruct(v.shape, F32),
        scratch_shapes=[pltpu.VMEM((8, rows, LANES), F32), pltpu.SemaphoreType.DMA((7,)),
                        pltpu.SemaphoreType.DMA((7,))],
    )(v)


def _rows_tile(r, most=64):
    return _pick(r, tuple(t for t in (256, 128, 64, 32, 16, 8) if t <= most))


def _swap_parts(n):
    def copies(ins, outs, send_sems, recv_sems):
        x, y, c, _ = _place()
        return [_remote(ins[i].at[:, pl.ds((1 - c) * (ins[i].shape[1] // 2), ins[i].shape[1] // 2), :], outs[i],
                        send_sems, recv_sems, i, (x, y, 1 - c)) for i in range(n)]

    def start(*refs):
        for cp in copies(*refs):
            cp.start()

    def finish(*refs):
        for cp in copies(*refs):
            cp.wait()

    return start, finish


def swap_rider(blocks):
    n = len(blocks)
    start, finish = _swap_parts(n)
    return {"ins": blocks, "out_shapes": [jax.ShapeDtypeStruct((4, b.shape[1] // 2, b.shape[2]), b.dtype) for b in blocks],
            "scratch": [pltpu.SemaphoreType.DMA((n,)), pltpu.SemaphoreType.DMA((n,))], "start": start, "finish": finish}


def swap_halves(blocks, name):
    n = len(blocks)
    start, finish = _swap_parts(n)

    def body(*refs):
        parts = (refs[:n], refs[n:2 * n], *refs[2 * n:])
        start(*parts)
        finish(*parts)

    return pl.pallas_call(
        body, name=name, in_specs=[ANY] * n, out_specs=[ANY] * n,
        out_shape=[jax.ShapeDtypeStruct((4, b.shape[1] // 2, b.shape[2]), b.dtype) for b in blocks],
        scratch_shapes=[pltpu.SemaphoreType.DMA((n,)), pltpu.SemaphoreType.DMA((n,))],
    )(*blocks)


def add_pairs(g4, b, core):
    _, r, c = b.shape
    tr = _rows_tile(r, 256)
    nrt = r // tr

    def body(core_ref, a_ref, b_ref, o_ref):
        o_ref[...] = (a_ref[...].astype(F32) + b_ref[...].astype(F32)).astype(o_ref.dtype)

    spec = pl.BlockSpec((1, tr, c), lambda s, i, core_ref: (s, i, 0))
    return pl.pallas_call(
        body, name="add_pairs",
        grid_spec=pltpu.PrefetchScalarGridSpec(
            num_scalar_prefetch=1, grid=(4, nrt),
            in_specs=[pl.BlockSpec((1, tr, c), lambda s, i, core_ref: (s, core_ref[0] * nrt + i, 0)), spec],
            out_specs=spec),
        out_shape=jax.ShapeDtypeStruct(b.shape, b.dtype), compiler_params=_cparams(("parallel", "parallel")),
    )(core, g4, b)


def add_four(mine, others):
    r, c = mine.shape
    tr = _rows_tile(r, 128)

    def body(m_ref, o_ref, out_ref):
        acc = m_ref[...].astype(F32)
        for j in range(3):
            acc = acc + o_ref[j].astype(F32)
        out_ref[...] = acc

    return pl.pallas_call(
        body, name="add_four", grid=(r // tr,),
        in_specs=[pl.BlockSpec((tr, c), lambda i: (i, 0)), pl.BlockSpec((3, tr, c), lambda i: (0, i, 0))],
        out_specs=pl.BlockSpec((tr, c), lambda i: (i, 0)),
        out_shape=jax.ShapeDtypeStruct((r, c), F32), compiler_params=_cparams(("parallel",)),
    )(mine, others)


def pair_sums(blocks, from_sibling=None):
    core = lax.axis_index("c").astype(jnp.int32).reshape(1)
    if from_sibling is None:
        from_sibling = swap_halves(blocks, "swap_halves_late")
    return [add_pairs(g4, b, core) for g4, b in zip(blocks, from_sibling)]


def owner_sums(pair, from_chips):
    cx, cy, cc = lax.axis_index("x"), lax.axis_index("y"), lax.axis_index("c")
    me = 2 * cx + cy
    sums = [add_four(lax.dynamic_index_in_dim(p, me, 0, keepdims=False), t) for p, t in zip(pair, from_chips)]
    got = swap_with_sibling(sums, "swap_sums")
    return [jnp.concatenate([jnp.where(cc == 0, s, g), jnp.where(cc == 0, g, s)], axis=0) for s, g in zip(sums, got)]


ADAM_LR = 0.001
ADAM_B1 = 0.9
ADAM_B2 = 0.999
ADAM_EPS = 1e-08
ADAM_WD = 0.01
ADAM_STEP = 10


def adamw(w, g, m, v):
    r, c = w.shape
    tr = _rows_tile(r)
    spec = pl.BlockSpec((tr, c), lambda i: (i, 0))

    def body(w_ref, g_ref, m_ref, v_ref, d_ref, nm_ref, nv_ref):
        gv = g_ref[...]
        nm = ADAM_B1 * m_ref[...] + (1.0 - ADAM_B1) * gv
        nv = ADAM_B2 * v_ref[...] + (1.0 - ADAM_B2) * jnp.square(gv)
        m_hat = nm / (1.0 - ADAM_B1 ** ADAM_STEP)
        v_hat = nv / (1.0 - ADAM_B2 ** ADAM_STEP)
        d_ref[...] = -ADAM_LR * (m_hat / (jnp.sqrt(v_hat) + ADAM_EPS) + ADAM_WD * w_ref[...])
        nm_ref[...] = nm
        nv_ref[...] = nv

    out = jax.ShapeDtypeStruct((r, c), F32)
    return pl.pallas_call(
        body, name="adamw", grid=(r // tr,), in_specs=[spec] * 4, out_specs=[spec] * 3, out_shape=[out] * 3,
        compiler_params=_cparams(("parallel",)),
    )(w, g, m, v)


WEIGHTS = ["norm_mix_pre", "norm_mix_post", "norm_ffn_pre", "norm_ffn_post", "w_in", "b_qkv", "att_sinks", "mu_shift",
           "w0", "w2", "a0", "a2", "g2", "k_k", "k_a", "r_k", "ln_x_w", "ln_x_b", "w_att_branch", "w_rwkv_branch",
           "w_out", "w_ffn_gate", "w_ffn_up", "w_ffn_down"]
BIG = {"w_in": 1, "w2": 1, "a2": 1, "g2": 1, "w_att_branch": 0, "w_rwkv_branch": 0, "w_out": 0, "w_ffn_gate": 1,
       "w_ffn_up": 1, "w_ffn_down": 0}
SMALL = [n for n in WEIGHTS if n not in BIG]
LATE_CUTS = (13, 45)
LATE = ("w_in", "w2", "a2", "g2")


def _whole(g4, axis):
    if axis == 0:
        return g4.reshape(g4.shape[0] * g4.shape[1], g4.shape[2])
    return jnp.swapaxes(g4, 0, 1).reshape(g4.shape[1], g4.shape[0] * g4.shape[2])


def _by_shard(w, axis):
    if axis == 0:
        return w.reshape(N_CHIPS, w.shape[0] // N_CHIPS, w.shape[1])
    return jnp.swapaxes(w.reshape(w.shape[0], N_CHIPS, w.shape[1] // N_CHIPS), 0, 1)


def _pack(parts):
    flat = jnp.concatenate([parts[n].reshape(-1) for n in SMALL])
    rows = -(-flat.shape[0] // (LANES * SUBLANES)) * SUBLANES
    return jnp.pad(flat, (0, rows * LANES - flat.shape[0])).reshape(rows, LANES)


def _unpack(packed, like):
    flat = packed.reshape(-1)
    out, off = {}, 0
    for n in SMALL:
        size = like[n].size
        out[n] = flat[off:off + size].reshape(like[n].shape)
        off += size
    return out


def kernel(x, norm_mix_pre, norm_mix_post, norm_ffn_pre, norm_ffn_post, w_in, b_qkv, att_sinks, mu_shift, w0, w2, a0, a2, g2, k_k, k_a, r_k, ln_x_w, ln_x_b, w_att_branch, w_rwkv_branch, w_out, w_ffn_gate, w_ffn_up, w_ffn_down, loss_target, m_norm_mix_pre, m_norm_mix_post, m_norm_ffn_pre, m_norm_ffn_post, m_w_in, m_b_qkv, m_att_sinks, m_mu_shift, m_w0, m_w2, m_a0, m_a2, m_g2, m_k_k, m_k_a, m_r_k, m_ln_x_w, m_ln_x_b, m_w_att_branch, m_w_rwkv_branch, m_w_out, m_w_ffn_gate, m_w_ffn_up, m_w_ffn_down, v_norm_mix_pre, v_norm_mix_post, v_norm_ffn_pre, v_norm_ffn_post, v_w_in, v_b_qkv, v_att_sinks, v_mu_shift, v_w0, v_w2, v_a0, v_a2, v_g2, v_k_k, v_k_a, v_r_k, v_ln_x_w, v_ln_x_b, v_w_att_branch, v_w_rwkv_branch, v_w_out, v_w_ffn_gate, v_w_ffn_up, v_w_ffn_down):
    given = dict(locals())
    wts = {n: given[n] for n in WEIGHTS}
    mom1 = {n: given["m_" + n] for n in WEIGHTS}
    mom2 = {n: given["v_" + n] for n in WEIGHTS}
    early = [n for n in BIG if n not in LATE]
    me = 2 * lax.axis_index("x") + lax.axis_index("y")
    own = {n: wts[n][0].astype(BF16) for n in BIG}

    def placed(names, gathered):
        return {n: (lambda g4: g4 if n == "w_in" else _whole(g4, BIG[n]))(
            lax.dynamic_update_index_in_dim(g4, own[n], me, 0)) for n, g4 in zip(names, gathered)}

    small = {n: wts[n].reshape(1, -1) for n in SMALL}
    pairs = {}

    def att_rider(grads_early):
        pairs["blocks"] = [_by_shard(grads_early[n], BIG[n]) for n in early]
        return swap_rider(pairs["blocks"])

    def bwd_rider(from_sibling):
        pairs["early"] = pair_sums(pairs["blocks"], from_sibling)
        return scatter_rider(pairs["early"])

    def late_riders(grads_late):
        pairs["late"] = pair_sums([grads_late[n] if n == "w_in" else _by_shard(grads_late[n], BIG[n]) for n in LATE])
        half = pairs["late"][0].shape[1]
        cuts = [0] + [half * f // 64 // 16 * 16 for f in LATE_CUTS] + [half]
        spans = [(cuts[i], cuts[i + 1] - cuts[i]) for i in range(3)]
        return (scatter_rider(pairs["late"], [spans[0], None, None, None]),
                scatter_rider(pairs["late"][:1], [spans[1]]), scatter_rider(pairs["late"][:1], [spans[2]]))

    loss, grad_x, gsmall, chips_early, (late_a, late_b, late_c) = local_step(
        x[0], loss_target[0], small, placed(LATE, gather_weights([own[n] for n in LATE])),
        gather_rider([own[n] for n in early]), lambda arrived: placed(early, arrived), att_rider, bwd_rider,
        late_riders)

    chips_late = [jnp.concatenate([late_a[0], late_b[0], late_c[0]], axis=1)] + list(late_a[1:])
    wholes = owner_sums(pairs["early"] + pairs["late"], list(chips_early) + chips_late)
    grads = dict(zip(early + list(LATE), wholes))
    names = list(BIG)

    gsum = _unpack(allreduce_small(_pack(gsmall)), small)

    outs_g, outs_d, outs_m, outs_v = {}, {}, {}, {}
    for n in names:
        d, nm, nv = adamw(wts[n][0], grads[n], mom1[n][0], mom2[n][0])
        outs_g[n], outs_d[n], outs_m[n], outs_v[n] = (t[None] for t in (grads[n], d, nm, nv))
    pk = lambda src: _pack({n: src[n] for n in SMALL})
    d, nm, nv = adamw(pk(wts), _pack(gsum), pk(mom1), pk(mom2))
    du, mu, vu = _unpack(d, small), _unpack(nm, small), _unpack(nv, small)
    for n in SMALL:
        outs_g[n], outs_d[n], outs_m[n], outs_v[n] = (t[n].reshape(wts[n].shape) for t in (gsum, du, mu, vu))

    total = lax.psum(loss[0, 0], ("x", "y", "c"))
    return (total, grad_x[None], *[outs_g[n] for n in WEIGHTS], *[outs_d[n] for n in WEIGHTS],
            *[outs_m[n] for n in WEIGHTS], *[outs_v[n] for n in WEIGHTS])
```

```python
import jax
import jax.numpy as jnp
from jax import lax
from jax.experimental import pallas as pl
from jax.experimental.pallas import tpu as pltpu

F32 = jnp.float32
BF16 = jnp.bfloat16

LANES = 128
SUBLANES = 8
VMEM_LIMIT = 56 * 1024 * 1024

RW_H = 64
RW_N = 64
RW_C = RW_H * RW_N
RW_NB = RW_C // LANES
SCAN_CHUNK = 8


MESH = pl.DeviceIdType.MESH
ANY = pl.BlockSpec(memory_space=pl.ANY)


def _cparams(sem=None):
    return pltpu.CompilerParams(dimension_semantics=sem, vmem_limit_bytes=VMEM_LIMIT)


def _fold(x):
    return x + pltpu.roll(x, 64, axis=x.ndim - 1)


def _scan_step_fwd(t, src_ref, dst_ref, r_ref, w_ref, k_ref, a_ref, b_ref, v_ref):
    vt = v_ref[t]
    acc = jnp.zeros((RW_N, LANES), F32)
    for j in range(RW_NB):
        ls = slice(j * LANES, (j + 1) * LANES)
        acc = acc + src_ref[j] * a_ref[t:t + 1, ls]
    sa = _fold(acc)
    yacc = jnp.zeros((RW_N, LANES), F32)
    for j in range(RW_NB):
        ls = slice(j * LANES, (j + 1) * LANES)
        s_new = src_ref[j] * w_ref[t:t + 1, ls] + sa * b_ref[t:t + 1, ls] + vt * k_ref[t:t + 1, ls]
        dst_ref[j] = s_new
        yacc = yacc + s_new * r_ref[t:t + 1, ls]
    return _fold(yacc), sa


def _rider_parts(rider):
    if rider is None:
        return [], [], []
    return list(rider["ins"]), list(rider["out_shapes"]), list(rider["scratch"])


def rwkv_scan_fwd(r, w, k, a, b, v3, rider=None):
    s_len = r.shape[0]
    nchunk = s_len // SCAN_CHUNK
    x_in, x_out, x_scr = _rider_parts(rider)
    ni, no = len(x_in), len(x_out)

    def body(*refs):
        r_ref, w_ref, k_ref, a_ref, b_ref, v_ref = refs[:6]
        y_ref, hist_ref, sa_ref = refs[6 + ni:9 + ni]
        st_ref = refs[9 + ni + no]
        ride = (refs[6:6 + ni], refs[9 + ni:9 + ni + no], *refs[10 + ni + no:])

        @pl.when(pl.program_id(0) == 0)
        def _():
            st_ref[...] = jnp.zeros_like(st_ref)
            if rider is not None:
                rider["start"](*ride)

        for t in range(SCAN_CHUNK):
            y, sa = _scan_step_fwd(t, st_ref if t == 0 else hist_ref.at[t - 1], hist_ref.at[t],
                                   r_ref, w_ref, k_ref, a_ref, b_ref, v_ref)
            y_ref[t] = y
            sa_ref[t] = sa
        st_ref[...] = hist_ref[SCAN_CHUNK - 1]

        if rider is not None:
            @pl.when(pl.program_id(0) == nchunk - 1)
            def _():
                rider["finish"](*ride)

    row = pl.BlockSpec((SCAN_CHUNK, RW_C), lambda i: (i, 0))
    til = pl.BlockSpec((SCAN_CHUNK, RW_N, LANES), lambda i: (i, 0, 0))
    return pl.pallas_call(
        body,
        name="rwkv_scan_fwd",
        grid=(nchunk,),
        in_specs=[row, row, row, row, row, til] + [ANY] * ni,
        out_specs=[til, pl.BlockSpec((SCAN_CHUNK, RW_NB, RW_N, LANES), lambda i: (i, 0, 0, 0)), til] + [ANY] * no,
        out_shape=[
            jax.ShapeDtypeStruct((s_len, RW_N, LANES), F32),
            jax.ShapeDtypeStruct((s_len, RW_NB, RW_N, LANES), F32),
            jax.ShapeDtypeStruct((s_len, RW_N, LANES), F32),
        ] + x_out,
        scratch_shapes=[pltpu.VMEM((RW_NB, RW_N, LANES), F32)] + x_scr,
        compiler_params=_cparams(("arbitrary",)),
    )(r, w, k, a, b, v3, *x_in)


def rwkv_scan_bwd(r, w, k, a, b, v3, hist, sa3, dy3, rider=None):
    s_len = r.shape[0]
    nchunk = s_len // SCAN_CHUNK
    x_in, x_out, x_scr = _rider_parts(rider)
    ni, no = len(x_in), len(x_out)

    def body(*refs):
        r_ref, w_ref, k_ref, a_ref, b_ref, v_ref, hist_ref, prev_ref, sa_ref, dy_ref = refs[:10]
        dr_ref, dw_ref, dk_ref, da_ref, db_ref, dv_ref = refs[10 + ni:16 + ni]
        ds_ref = refs[16 + ni + no]
        ride = (refs[10:10 + ni], refs[16 + ni:16 + ni + no], *refs[17 + ni + no:])

        @pl.when(pl.program_id(0) == 0)
        def _():
            ds_ref[...] = jnp.zeros_like(ds_ref)
            if rider is not None:
                rider["start"](*ride)

        not_first = (pl.program_id(0) < nchunk - 1).astype(F32)

        def before(t, j):
            return prev_ref[0, j] * not_first if t == 0 else hist_ref[t - 1, j]

        for t in reversed(range(SCAN_CHUNK)):
            vt = v_ref[t]
            dyt = dy_ref[t]
            sat = sa_ref[t]
            dv_acc = jnp.zeros((RW_N, LANES), F32)
            dsa_acc = jnp.zeros((RW_N, LANES), F32)
            for j in range(RW_NB):
                ls = slice(j * LANES, (j + 1) * LANES)
                row = (slice(t, t + 1), ls)
                ds_j = ds_ref[j] + dyt * r_ref[row]
                ds_ref[j] = ds_j
                dr_ref[row] = jnp.sum(hist_ref[t, j] * dyt, axis=0, keepdims=True)
                dv_acc = dv_acc + ds_j * k_ref[row]
                dk_ref[row] = jnp.sum(ds_j * vt, axis=0, keepdims=True)
                dsa_acc = dsa_acc + ds_j * b_ref[row]
                db_ref[row] = jnp.sum(ds_j * sat, axis=0, keepdims=True)
                dw_ref[row] = jnp.sum(ds_j * before(t, j), axis=0, keepdims=True)
            dv_ref[t] = _fold(dv_acc)
            dsa = _fold(dsa_acc)
            for j in range(RW_NB):
                ls = slice(j * LANES, (j + 1) * LANES)
                row = (slice(t, t + 1), ls)
                da_ref[row] = jnp.sum(before(t, j) * dsa, axis=0, keepdims=True)
                ds_ref[j] = ds_ref[j] * w_ref[row] + dsa * a_ref[row]

        if rider is not None:
            @pl.when(pl.program_id(0) == nchunk - 1)
            def _():
                rider["finish"](*ride)

    rev = lambda i: (nchunk - 1 - i, 0)
    rev3 = lambda i: (nchunk - 1 - i, 0, 0)
    row = pl.BlockSpec((SCAN_CHUNK, RW_C), rev)
    til = pl.BlockSpec((SCAN_CHUNK, RW_N, LANES), rev3)
    rows = jax.ShapeDtypeStruct((s_len, RW_C), F32)
    return pl.pallas_call(
        body,
        name="rwkv_scan_bwd",
        grid=(nchunk,),
        in_specs=[row, row, row, row, row, til,
                  pl.BlockSpec((SCAN_CHUNK, RW_NB, RW_N, LANES), lambda i: (nchunk - 1 - i, 0, 0, 0)),
                  pl.BlockSpec((1, RW_NB, RW_N, LANES),
                               lambda i: (jnp.maximum((nchunk - 1 - i) * SCAN_CHUNK - 1, 0), 0, 0, 0)),
                  til, til] + [ANY] * ni,
        out_specs=[row, row, row, row, row, til] + [ANY] * no,
        out_shape=[rows, rows, rows, rows, rows, jax.ShapeDtypeStruct((s_len, RW_N, LANES), F32)] + x_out,
        scratch_shapes=[pltpu.VMEM((RW_NB, RW_N, LANES), F32)] + x_scr,
        compiler_params=_cparams(("arbitrary",)),
    )(r, w, k, a, b, v3, hist, hist, sa3, dy3, *x_in)


def _pick(n, cands):
    for c in cands:
        if n % c == 0:
            return c
    return n


MM_VMEM_BUDGET = 40 * 1024 * 1024
MM_FLOPS = 8.5e14
MM_HBM = 2.2e12
MM_STEP = 0.4e-6


def _mm_plan(m, n, k, out_bytes):
    divs = lambda d: [t for t in range(LANES, d + 1, LANES) if d % t == 0] or [d]
    best = None
    for tm in divs(m):
        for tn in divs(n):
            for tk in divs(k):
                nk = k // tk
                vmem = 4 * (tm * tk + tk * tn) + (4 * tm * tn if nk > 1 else 0) + 2 * tm * tn * out_bytes
                if vmem > MM_VMEM_BUDGET:
                    continue
                steps = (m // tm) * (n // tn) * nk
                for n_outer in (False, True):
                    if nk > 1:
                        traffic = steps * (tm * tk + tk * tn) * 2
                    elif n_outer:
                        traffic = (n // tn) * (k * tn + m * k) * 2
                    else:
                        traffic = (m // tm) * (tm * k + k * n) * 2
                    cost = max(2.0 * m * n * k / MM_FLOPS, (traffic + m * n * out_bytes) / MM_HBM) + steps * MM_STEP
                    if best is None or cost < best[0]:
                        best = (cost, tm, tn, tk, n_outer)
    return best[1:]


def matmul(a, b, *, ta=False, tb=False, out_dtype=F32, name="matmul", plan=None, rider=None):
    m, kdim = (a.shape[1], a.shape[0]) if ta else a.shape
    n = b.shape[0] if tb else b.shape[1]
    assert (b.shape[1] if tb else b.shape[0]) == kdim
    tm, tn, tk, n_outer = plan or _mm_plan(m, n, kdim, jnp.dtype(out_dtype).itemsize)
    nk = kdim // tk
    dims = (((0 if ta else 1,), (1 if tb else 0,)), ((), ()))
    grid = (n // tn, m // tm, nk) if n_outer else (m // tm, n // tn, nk)
    x_in, x_out, x_scr = _rider_parts(rider)
    ni, no = len(x_in), len(x_out)
    n_acc = 1 if nk > 1 else 0

    def body(*refs):
        a_ref, b_ref = refs[:2]
        o_ref = refs[2 + ni]
        ride = (refs[2:2 + ni], refs[3 + ni:3 + ni + no], *refs[3 + ni + no + n_acc:])
        ids = [pl.program_id(ax) for ax in range(3)]
        if rider is not None:
            @pl.when((ids[0] == 0) & (ids[1] == 0) & (ids[2] == 0))
            def _():
                rider["start"](*ride)

        prod = lax.dot_general(a_ref[...].astype(BF16), b_ref[...].astype(BF16), dims, preferred_element_type=F32)
        if nk == 1:
            o_ref[...] = prod.astype(o_ref.dtype)
        else:
            acc_ref = refs[3 + ni + no]
            kk = ids[2]

            @pl.when(kk == 0)
            def _():
                acc_ref[...] = prod

            @pl.when(kk > 0)
            def _():
                acc_ref[...] += prod

            @pl.when(kk == nk - 1)
            def _():
                o_ref[...] = acc_ref[...].astype(o_ref.dtype)

        if rider is not None:
            @pl.when((ids[0] == grid[0] - 1) & (ids[1] == grid[1] - 1) & (ids[2] == nk - 1))
            def _():
                rider["finish"](*ride)

    ij = (lambda p, q: (q, p)) if n_outer else (lambda p, q: (p, q))
    a_map = (lambda p, q, k: (k, ij(p, q)[0])) if ta else (lambda p, q, k: (ij(p, q)[0], k))
    b_map = (lambda p, q, k: (ij(p, q)[1], k)) if tb else (lambda p, q, k: (k, ij(p, q)[1]))
    out = pl.pallas_call(
        body,
        name=name,
        grid=grid,
        in_specs=[pl.BlockSpec((tk, tm) if ta else (tm, tk), a_map),
                  pl.BlockSpec((tn, tk) if tb else (tk, tn), b_map)] + [ANY] * ni,
        out_specs=[pl.BlockSpec((tm, tn), lambda p, q, k: ij(p, q))] + [ANY] * no,
        out_shape=[jax.ShapeDtypeStruct((m, n), out_dtype)] + x_out,
        scratch_shapes=([pltpu.VMEM((tm, tn), F32)] if nk > 1 else []) + x_scr,
        compiler_params=_cparams(("arbitrary",) * 3 if rider is not None else ("parallel", "parallel", "arbitrary")),
    )(a, b, *x_in)
    return out if rider is not None else out[0]


@jax.custom_vjp
def hsum(x):
    acc = x[:, 0:LANES]
    for j in range(1, RW_NB):
        acc = acc + x[:, j * LANES:(j + 1) * LANES]
    return _fold(acc)


def _hsum_fwd(x):
    return hsum(x), None


def _hsum_bwd(_, ct):
    return (jnp.concatenate([_fold(ct)] * RW_NB, axis=1),)


hsum.defvjp(_hsum_fwd, _hsum_bwd)


@jax.custom_vjp
def hbcast(s):
    return jnp.concatenate([s] * RW_NB, axis=1)


def _hbcast_fwd(s):
    return hbcast(s), None


def _hbcast_bwd(_, ct):
    acc = ct[:, 0:LANES]
    for j in range(1, RW_NB):
        acc = acc + ct[:, j * LANES:(j + 1) * LANES]
    return (acc,)


hbcast.defvjp(_hbcast_fwd, _hbcast_bwd)


@jax.custom_vjp
def bdot(x, w):
    return jnp.dot(x.astype(BF16), w, preferred_element_type=F32)


def _bdot_fwd(x, w):
    return bdot(x, w), w


def _bdot_bwd(w, ct):
    dx = lax.dot_general(ct.astype(BF16), w, (((1,), (1,)), ((), ())), preferred_element_type=F32)
    return dx, jnp.zeros_like(w)


bdot.defvjp(_bdot_fwd, _bdot_bwd)

RMS_EPS = 1e-6
GN_EPS = 64e-5


def f_rms(x, g):
    return x * lax.rsqrt(jnp.mean(x * x, axis=-1, keepdims=True) + RMS_EPS) * g


def _softplus(z):
    return jnp.maximum(z, 0.0) + jnp.log1p(jnp.exp(-jnp.abs(z)))


def f_pre(xk, xg, xw, xa, ew, ea, w0, a0, k_k, k_a, w2, a2, g2):
    tw = jnp.tanh(xw)
    sg = jax.nn.sigmoid(xg)
    wlog = -_softplus(-(w0 + bdot(tw, w2) + ew)) - 0.5
    decay = jnp.exp(-jnp.exp(wlog))
    a = jax.nn.sigmoid(a0 + bdot(xa, a2) + ea)
    g = bdot(sg, g2)
    kk0 = xk * k_k
    nrm = jnp.sqrt(hbcast(hsum(kk0 * kk0)))
    kk = kk0 / jnp.maximum(nrm, 1e-12)
    k = xk * (1.0 + (a - 1.0) * k_a)
    return decay, k, -kk, kk * a, g, tw, sg


def f_post(y, r, k, v, g, ln_w, ln_b, r_k):
    mu = hbcast(hsum(y)) * (1.0 / RW_N)
    yc = y - mu
    var = hbcast(hsum(yc * yc)) * (1.0 / RW_N)
    yn = yc * lax.rsqrt(var + GN_EPS) * ln_w + ln_b
    bonus = hbcast(hsum(r * k * r_k)) * v
    return (yn + bonus) * g


def f_merge(ga, gr, ab, rb):
    return jax.nn.sigmoid(ga) * ab + jax.nn.sigmoid(gr) * rb


def f_swiglu(gg, uu):
    return gg * jax.nn.sigmoid(gg) * uu


def _row(tt, width, cb=0, rev_n=None):
    if rev_n is None:
        return pl.BlockSpec((tt, width), lambda i: (i, cb))
    return pl.BlockSpec((tt, width), lambda i: (rev_n - 1 - i, cb))


def _full(arr):
    nd = arr.ndim
    return pl.BlockSpec(arr.shape, lambda i: (0,) * nd)


def _acc_init(i_first, *refs):
    @pl.when(i_first)
    def _():
        for r in refs:
            r[...] = jnp.zeros_like(r)


def rms_fwd(x, g, *, tt=128):
    s_len, d = x.shape

    def body(x_ref, g_ref, o_ref):
        o_ref[...] = f_rms(x_ref[...], g_ref[...]).astype(BF16)

    return pl.pallas_call(
        body, name="rms_fwd", grid=(s_len // tt,),
        in_specs=[_row(tt, d), _full(g)], out_specs=_row(tt, d),
        out_shape=jax.ShapeDtypeStruct((s_len, d), BF16),
        compiler_params=_cparams(("parallel",)),
    )(x, g)


def rwkv_pre_fwd(proj, mu, w0, a0, k_k, k_a, w2, a2, g2, *, tt=64):
    s_len, c = proj.shape
    nt = s_len // tt
    sub = tt // SUBLANES

    def body(p_ref, pb_ref, mu_ref, w0_ref, a0_ref, kk_ref, ka_ref, w2_ref, a2_ref, g2_ref,
             r_ref, dec_ref, k_ref, v_ref, av_ref, bv_ref, g_ref, tw_ref, xa_ref, sg_ref):
        i = pl.program_id(0)
        cur = p_ref[...]
        edge = jnp.where(i > 0, pb_ref[SUBLANES - 1:SUBLANES, :], 0.0)
        rows = lax.broadcasted_iota(jnp.int32, cur.shape, 0)
        prev = jnp.where(rows == 0, edge, pltpu.roll(cur, 1, axis=0))
        xs = cur + (prev - cur) * mu_ref[...]
        xr, xk, xv = xs[:, 0:RW_C], xs[:, RW_C:2 * RW_C], xs[:, 2 * RW_C:3 * RW_C]
        xg = xs[:, 3 * RW_C:3 * RW_C + 512]
        xw = xs[:, 3 * RW_C + 512:3 * RW_C + 640]
        xa = xs[:, 3 * RW_C + 640:3 * RW_C + 768]
        zero = jnp.zeros((tt, RW_C), F32)
        dec, k, av, bv, g, tw, sg = f_pre(xk, xg, xw, xa, zero, zero, w0_ref[...], a0_ref[...], kk_ref[...],
                                          ka_ref[...], w2_ref[...], a2_ref[...], g2_ref[...])
        r_ref[...] = xr
        dec_ref[...] = dec
        k_ref[...] = k
        v_ref[...] = xv
        av_ref[...] = av
        bv_ref[...] = bv
        g_ref[...] = g
        tw_ref[...] = tw.astype(BF16)
        xa_ref[...] = xa.astype(BF16)
        sg_ref[...] = sg.astype(BF16)

    rows_f = jax.ShapeDtypeStruct((s_len, RW_C), F32)
    prev_spec = pl.BlockSpec((SUBLANES, c), lambda i: (jnp.maximum(i * sub - 1, 0), 0))
    params = [mu, w0, a0, k_k, k_a, w2, a2, g2]
    return pl.pallas_call(
        body, name="rwkv_pre_fwd", grid=(nt,),
        in_specs=[_row(tt, c), prev_spec] + [_full(p) for p in params],
        out_specs=[_row(tt, RW_C)] * 7 + [_row(tt, 128), _row(tt, 128), _row(tt, 512)],
        out_shape=[rows_f] * 7 + [jax.ShapeDtypeStruct((s_len, 128), BF16), jax.ShapeDtypeStruct((s_len, 128), BF16),
                                  jax.ShapeDtypeStruct((s_len, 512), BF16)],
        compiler_params=_cparams(("parallel",)),
    )(proj, proj, *params)


def rwkv_pre_bwd(proj, mu, w0, a0, k_k, k_a, w2, a2, g2, d_r, d_dec, d_k, d_v, d_av, d_bv, d_g, d_r2, d_k2, d_v2,
                 *, tt=32):
    s_len, c = proj.shape
    nt = s_len // tt
    sub = tt // SUBLANES

    def body(p_ref, pb_ref, mu_ref, w0_ref, a0_ref, kk_ref, ka_ref, w2_ref, a2_ref, g2_ref,
             dr_ref, ddec_ref, dk_ref, dv_ref, dav_ref, dbv_ref, dg_ref, dr2_ref, dk2_ref, dv2_ref,
             dp_ref, dzw_ref, dza_ref, dmu_ref, dw0_ref, da0_ref, dkk_ref, dka_ref, carry_ref):
        step = pl.program_id(0)
        i = nt - 1 - step
        _acc_init(step == 0, dmu_ref, dw0_ref, da0_ref, dkk_ref, dka_ref, carry_ref)
        cur = p_ref[...]
        edge = jnp.where(i > 0, pb_ref[SUBLANES - 1:SUBLANES, :], 0.0)
        rows = lax.broadcasted_iota(jnp.int32, cur.shape, 0)
        prev = jnp.where(rows == 0, edge, pltpu.roll(cur, 1, axis=0))
        mu_v = mu_ref[...]
        xs = cur + (prev - cur) * mu_v
        xk = xs[:, RW_C:2 * RW_C]
        xg = xs[:, 3 * RW_C:3 * RW_C + 512]
        xw = xs[:, 3 * RW_C + 512:3 * RW_C + 640]
        xa = xs[:, 3 * RW_C + 640:3 * RW_C + 768]
        zero = jnp.zeros((tt, RW_C), F32)
        w2_v, a2_v, g2_v = w2_ref[...], a2_ref[...], g2_ref[...]

        def core(xk, xg, xw, xa, ew, ea, w0, a0, k_k, k_a):
            return f_pre(xk, xg, xw, xa, ew, ea, w0, a0, k_k, k_a, w2_v, a2_v, g2_v)[:5]

        _, vjp = jax.vjp(core, xk, xg, xw, xa, zero, zero, w0_ref[...], a0_ref[...], kk_ref[...], ka_ref[...])
        dxk, dxg, dxw, dxa, dzw, dza, dw0, da0, dkk, dka = vjp(
            (ddec_ref[...], dk_ref[...] + dk2_ref[...], dav_ref[...], dbv_ref[...], dg_ref[...]))
        dzw_ref[...] = dzw.astype(BF16)
        dza_ref[...] = dza.astype(BF16)
        dw0_ref[...] += dw0
        da0_ref[...] += da0
        dkk_ref[...] += dkk
        dka_ref[...] += dka
        dxs = jnp.concatenate([dr_ref[...] + dr2_ref[...], dxk, dv_ref[...] + dv2_ref[...], dxg, dxw, dxa], axis=1)
        dmu_ref[...] += jnp.sum(dxs * (prev - cur), axis=0, keepdims=True)
        to_prev = dxs * mu_v
        nxt = jnp.where(rows == tt - 1, carry_ref[...], pltpu.roll(to_prev, tt - 1, axis=0))
        carry_ref[...] = to_prev[0:1, :]
        dp_ref[...] = (dxs * (1.0 - mu_v) + nxt).astype(BF16)

    prev_spec = pl.BlockSpec((SUBLANES, c), lambda s: (jnp.maximum((nt - 1 - s) * sub - 1, 0), 0))
    params = [mu, w0, a0, k_k, k_a, w2, a2, g2]
    cts = [d_r, d_dec, d_k, d_v, d_av, d_bv, d_g, d_r2, d_k2, d_v2]
    vec = jax.ShapeDtypeStruct((1, RW_C), F32)
    acc = pl.BlockSpec((1, RW_C), lambda s: (0, 0))
    return pl.pallas_call(
        body, name="rwkv_pre_bwd", grid=(nt,),
        in_specs=[_row(tt, c, rev_n=nt), prev_spec] + [_full(p) for p in params] + [_row(tt, RW_C, rev_n=nt)] * 10,
        out_specs=[_row(tt, c, rev_n=nt), _row(tt, RW_C, rev_n=nt), _row(tt, RW_C, rev_n=nt),
                   pl.BlockSpec((1, c), lambda s: (0, 0)), acc, acc, acc, acc],
        out_shape=[jax.ShapeDtypeStruct((s_len, c), BF16), jax.ShapeDtypeStruct((s_len, RW_C), BF16),
                   jax.ShapeDtypeStruct((s_len, RW_C), BF16), jax.ShapeDtypeStruct((1, c), F32), vec, vec, vec, vec],
        scratch_shapes=[pltpu.VMEM((1, c), F32)],
        compiler_params=_cparams(("arbitrary",)),
    )(proj, proj, *params, *cts)


def rwkv_post_fwd(y, r, k, v, g, ln_w, ln_b, r_k, *, tt=64):
    s_len = y.shape[0]

    def body(y_ref, r_ref, k_ref, v_ref, g_ref, lw_ref, lb_ref, rk_ref, o_ref):
        o_ref[...] = f_post(y_ref[...], r_ref[...], k_ref[...], v_ref[...], g_ref[...],
                            lw_ref[...], lb_ref[...], rk_ref[...]).astype(BF16)

    return pl.pallas_call(
        body, name="rwkv_post_fwd", grid=(s_len // tt,),
        in_specs=[_row(tt, RW_C)] * 5 + [_full(ln_w), _full(ln_b), _full(r_k)],
        out_specs=_row(tt, RW_C), out_shape=jax.ShapeDtypeStruct((s_len, RW_C), BF16),
        compiler_params=_cparams(("parallel",)),
    )(y, r, k, v, g, ln_w, ln_b, r_k)


def rwkv_post_bwd(y, r, k, v, g, ln_w, ln_b, r_k, d_o, *, tt=32):
    s_len = y.shape[0]

    def body(y_ref, r_ref, k_ref, v_ref, g_ref, lw_ref, lb_ref, rk_ref, do_ref,
             dy_ref, dr_ref, dk_ref, dv_ref, dg_ref, dlw_ref, dlb_ref, drk_ref):
        _acc_init(pl.program_id(0) == 0, dlw_ref, dlb_ref, drk_ref)
        _, vjp = jax.vjp(f_post, y_ref[...], r_ref[...], k_ref[...], v_ref[...], g_ref[...],
                         lw_ref[...], lb_ref[...], rk_ref[...])
        dy, dr, dk, dv, dg, dlw, dlb, drk = vjp(do_ref[...].astype(F32))
        dy_ref[...] = dy
        dr_ref[...] = dr
        dk_ref[...] = dk
        dv_ref[...] = dv
        dg_ref[...] = dg
        dlw_ref[...] += dlw
        dlb_ref[...] += dlb
        drk_ref[...] += drk

    rows_f = jax.ShapeDtypeStruct((s_len, RW_C), F32)
    vec = jax.ShapeDtypeStruct((1, RW_C), F32)
    acc = pl.BlockSpec((1, RW_C), lambda s: (0, 0))
    return pl.pallas_call(
        body, name="rwkv_post_bwd", grid=(s_len // tt,),
        in_specs=[_row(tt, RW_C)] * 5 + [_full(ln_w), _full(ln_b), _full(r_k), _row(tt, RW_C)],
        out_specs=[_row(tt, RW_C)] * 5 + [acc] * 3, out_shape=[rows_f] * 5 + [vec] * 3,
        compiler_params=_cparams(("arbitrary",)),
    )(y, r, k, v, g, ln_w, ln_b, r_k, d_o)


def merge_fwd(gate, ab, rb, *, tt=128):
    s_len, d = ab.shape

    def body(ga_ref, gr_ref, a_ref, r_ref, o_ref):
        o_ref[...] = f_merge(*(t[...].astype(F32) for t in (ga_ref, gr_ref, a_ref, r_ref))).astype(BF16)

    return pl.pallas_call(
        body, name="merge_fwd", grid=(s_len // tt,),
        in_specs=[_row(tt, d, 0), _row(tt, d, 1), _row(tt, d), _row(tt, d)],
        out_specs=_row(tt, d), out_shape=jax.ShapeDtypeStruct((s_len, d), BF16),
        compiler_params=_cparams(("parallel",)),
    )(gate, gate, ab, rb)


def merge_bwd(gate, ab, rb, d_m, *, tt=64):
    s_len, d = ab.shape

    def body(ga_ref, gr_ref, a_ref, r_ref, dm_ref, dgate_ref, da_ref, dr_ref):
        _, vjp = jax.vjp(f_merge, *(t[...].astype(F32) for t in (ga_ref, gr_ref, a_ref, r_ref)))
        dga, dgr, da, dr = vjp(dm_ref[...].astype(F32))
        dgate_ref[:, 0:d] = dga.astype(BF16)
        dgate_ref[:, d:2 * d] = dgr.astype(BF16)
        da_ref[...] = da.astype(BF16)
        dr_ref[...] = dr.astype(BF16)

    return pl.pallas_call(
        body, name="merge_bwd", grid=(s_len // tt,),
        in_specs=[_row(tt, d, 0), _row(tt, d, 1), _row(tt, d), _row(tt, d), _row(tt, d)],
        out_specs=[_row(tt, 2 * d), _row(tt, d), _row(tt, d)],
        out_shape=[jax.ShapeDtypeStruct((s_len, 2 * d), BF16), jax.ShapeDtypeStruct((s_len, d), BF16),
                   jax.ShapeDtypeStruct((s_len, d), BF16)],
        compiler_params=_cparams(("parallel",)),
    )(gate, gate, ab, rb, d_m)


def swiglu_fwd(gg, uu, *, tt=64):
    s_len, f = gg.shape

    def body(g_ref, u_ref, o_ref):
        o_ref[...] = f_swiglu(g_ref[...].astype(F32), u_ref[...].astype(F32)).astype(BF16)

    return pl.pallas_call(
        body, name="swiglu_fwd", grid=(s_len // tt,),
        in_specs=[_row(tt, f), _row(tt, f)], out_specs=_row(tt, f),
        out_shape=jax.ShapeDtypeStruct((s_len, f), BF16),
        compiler_params=_cparams(("parallel",)),
    )(gg, uu)


def swiglu_bwd(gg, uu, d_act, *, tt=32):
    s_len, f = gg.shape

    def body(g_ref, u_ref, d_ref, dg_ref, du_ref):
        _, vjp = jax.vjp(f_swiglu, g_ref[...].astype(F32), u_ref[...].astype(F32))
        dg, du = vjp(d_ref[...].astype(F32))
        dg_ref[...] = dg.astype(BF16)
        du_ref[...] = du.astype(BF16)

    out = jax.ShapeDtypeStruct((s_len, f), BF16)
    return pl.pallas_call(
        body, name="swiglu_bwd", grid=(s_len // tt,),
        in_specs=[_row(tt, f)] * 3, out_specs=[_row(tt, f)] * 2, out_shape=[out, out],
        compiler_params=_cparams(("parallel",)),
    )(gg, uu, d_act)


def resid_norm_fwd(x, m2, g_post, g_pre, *, tt=128):
    s_len, d = x.shape

    def body(x_ref, m_ref, gp_ref, gn_ref, x1_ref, h_ref):
        x1 = x_ref[...] + f_rms(m_ref[...], gp_ref[...])
        x1_ref[...] = x1
        h_ref[...] = f_rms(x1, gn_ref[...]).astype(BF16)

    return pl.pallas_call(
        body, name="resid_norm_fwd", grid=(s_len // tt,),
        in_specs=[_row(tt, d), _row(tt, d), _full(g_post), _full(g_pre)],
        out_specs=[_row(tt, d), _row(tt, d)],
        out_shape=[jax.ShapeDtypeStruct((s_len, d), F32), jax.ShapeDtypeStruct((s_len, d), BF16)],
        compiler_params=_cparams(("parallel",)),
    )(x, m2, g_post, g_pre)


def loss_head(x1, ff, tgt, g_post, *, tt=64):
    s_len, d = x1.shape

    def body(x1_ref, f_ref, t_ref, g_ref, loss_ref, dy_ref, df_ref, dg_ref):
        _acc_init(pl.program_id(0) == 0, loss_ref, dg_ref)
        nrm, vjp = jax.vjp(f_rms, f_ref[...], g_ref[...])
        err = x1_ref[...] + nrm - t_ref[...]
        per_tok = jnp.mean(err * err, axis=-1, keepdims=True)
        loss_ref[...] += 0.5 * jnp.sum(per_tok, axis=0, keepdims=True)
        dy = err * (1.0 / d)
        dff, dg = vjp(dy)
        dy_ref[...] = dy
        df_ref[...] = dff.astype(BF16)
        dg_ref[...] += dg

    return pl.pallas_call(
        body, name="loss_head", grid=(s_len // tt,),
        in_specs=[_row(tt, d)] * 3 + [_full(g_post)],
        out_specs=[pl.BlockSpec((1, LANES), lambda s: (0, 0)), _row(tt, d), _row(tt, d),
                   pl.BlockSpec((1, d), lambda s: (0, 0))],
        out_shape=[jax.ShapeDtypeStruct((1, LANES), F32), jax.ShapeDtypeStruct((s_len, d), F32),
                   jax.ShapeDtypeStruct((s_len, d), BF16), jax.ShapeDtypeStruct((1, d), F32)],
        compiler_params=_cparams(("arbitrary",)),
    )(x1, ff, tgt, g_post)


def resid_norm_bwd(x1, dh_a, dh_b, g_pre, m2, g_post, dy, *, tt=64):
    s_len, d = x1.shape

    def body(x1_ref, da_ref, db_ref, gn_ref, m_ref, gp_ref, dy_ref, dx1_ref, dm_ref, dgn_ref, dgp_ref):
        _acc_init(pl.program_id(0) == 0, dgn_ref, dgp_ref)
        _, vjp_n = jax.vjp(f_rms, x1_ref[...], gn_ref[...])
        dx1_n, dgn = vjp_n(da_ref[...].astype(F32) + db_ref[...].astype(F32))
        dx1 = dy_ref[...] + dx1_n
        _, vjp_p = jax.vjp(f_rms, m_ref[...], gp_ref[...])
        dm, dgp = vjp_p(dx1)
        dx1_ref[...] = dx1
        dm_ref[...] = dm.astype(BF16)
        dgn_ref[...] += dgn
        dgp_ref[...] += dgp

    acc = pl.BlockSpec((1, d), lambda s: (0, 0))
    vec = jax.ShapeDtypeStruct((1, d), F32)
    return pl.pallas_call(
        body, name="resid_norm_bwd", grid=(s_len // tt,),
        in_specs=[_row(tt, d)] * 3 + [_full(g_pre), _row(tt, d), _full(g_post), _row(tt, d)],
        out_specs=[_row(tt, d), _row(tt, d), acc, acc],
        out_shape=[jax.ShapeDtypeStruct((s_len, d), F32), jax.ShapeDtypeStruct((s_len, d), BF16), vec, vec],
        compiler_params=_cparams(("arbitrary",)),
    )(x1, dh_a, dh_b, g_pre, m2, g_post, dy)


def rms_bwd(x, g, dh_a, dh_b, dh_c, dres, *, tt=64):
    s_len, d = x.shape

    def body(x_ref, g_ref, a_ref, b_ref, c_ref, r_ref, dx_ref, dg_ref):
        _acc_init(pl.program_id(0) == 0, dg_ref)
        _, vjp = jax.vjp(f_rms, x_ref[...], g_ref[...])
        dx, dg = vjp(a_ref[...].astype(F32) + b_ref[...].astype(F32) + c_ref[...].astype(F32))
        dx_ref[...] = r_ref[...] + dx
        dg_ref[...] += dg

    return pl.pallas_call(
        body, name="rms_bwd", grid=(s_len // tt,),
        in_specs=[_row(tt, d), _full(g)] + [_row(tt, d)] * 4,
        out_specs=[_row(tt, d), pl.BlockSpec((1, d), lambda s: (0, 0))],
        out_shape=[jax.ShapeDtypeStruct((s_len, d), F32), jax.ShapeDtypeStruct((1, d), F32)],
        compiler_params=_cparams(("arbitrary",)),
    )(x, g, dh_a, dh_b, dh_c, dres)


def colsum(a, *, tt=256):
    s_len, c = a.shape

    def body(a_ref, o_ref):
        _acc_init(pl.program_id(0) == 0, o_ref)
        o_ref[...] += jnp.sum(a_ref[...].astype(F32), axis=0, keepdims=True)

    return pl.pallas_call(
        body, name="colsum", grid=(s_len // tt,),
        in_specs=[_row(tt, c)], out_specs=pl.BlockSpec((1, c), lambda s: (0, 0)),
        out_shape=jax.ShapeDtypeStruct((1, c), F32),
        compiler_params=_cparams(("arbitrary",)),
    )(a)


AT_HD = 128
AT_GROUP = 4
AT_KVH = 8
AT_BLK = 128
AT_QW = AT_GROUP * AT_HD
AT_KCOL = AT_KVH * AT_GROUP
AT_VCOL = AT_KCOL + AT_KVH
NEG_INF = -1e30
AT_SCALE = AT_HD ** -0.5


def _rope(t, cos2, sin2):
    return t * cos2 + pltpu.roll(t, AT_HD // 2, axis=1) * sin2


def _rope_t(d, cos2, sin2):
    return d * cos2 + pltpu.roll(d * sin2, AT_HD // 2, axis=1)


def _att_specs():
    prev = lambda i: jnp.maximum(i - 1, 0)
    blk = (AT_BLK, AT_HD)
    return [
        pl.BlockSpec((AT_BLK, AT_QW), lambda h, i: (i, h)),
        pl.BlockSpec(blk, lambda h, i: (i, AT_KCOL + h)),
        pl.BlockSpec(blk, lambda h, i: (prev(i), AT_KCOL + h)),
        pl.BlockSpec(blk, lambda h, i: (i, AT_VCOL + h)),
        pl.BlockSpec(blk, lambda h, i: (prev(i), AT_VCOL + h)),
        pl.BlockSpec((1, AT_QW), lambda h, i: (0, h)),
        pl.BlockSpec((1, AT_HD), lambda h, i: (0, AT_KCOL + h)),
        pl.BlockSpec((1, AT_HD), lambda h, i: (0, AT_VCOL + h)),
        pl.BlockSpec((1, AT_GROUP, AT_HD), lambda h, i: (h, 0, 0)),
        pl.BlockSpec(blk, lambda h, i: (i, 0)),
        pl.BlockSpec(blk, lambda h, i: (i, 0)),
        pl.BlockSpec(blk, lambda h, i: (prev(i), 0)),
        pl.BlockSpec(blk, lambda h, i: (prev(i), 0)),
    ]


def _att_load(i, q_ref, kc_ref, kp_ref, vc_ref, vp_ref, bq_ref, bk_ref, bv_ref, cc_ref, sc_ref, cp_ref, sp_ref):
    cosc, sinc = cc_ref[...], sc_ref[...]
    q = q_ref[...] + bq_ref[...]
    kc = _rope(kc_ref[...] + bk_ref[...], cosc, sinc)
    kp = _rope(kp_ref[...] + bk_ref[...], cp_ref[...], sp_ref[...])
    kcat = jnp.concatenate([kp, kc], axis=0).astype(BF16)
    vcat = jnp.concatenate([vp_ref[...] + bv_ref[...], vc_ref[...] + bv_ref[...]], axis=0).astype(BF16)
    qi = lax.broadcasted_iota(jnp.int32, (AT_GROUP * AT_BLK, 2 * AT_BLK), 0) & (AT_BLK - 1)
    kj = lax.broadcasted_iota(jnp.int32, (AT_GROUP * AT_BLK, 2 * AT_BLK), 1)
    rel = qi + AT_BLK - kj
    mask = (rel >= 0) & (rel < AT_BLK) & ((kj >= AT_BLK) | (i > 0))
    return q, kcat, vcat, mask, cosc, sinc


AT_ROWS = AT_GROUP * AT_BLK


def _att_stack(q, cosc, sinc):
    return jnp.concatenate([_rope(q[:, g * AT_HD:(g + 1) * AT_HD], cosc, sinc) for g in range(AT_GROUP)], axis=0)


def _att_cols(sk_ref):
    head = lax.broadcasted_iota(jnp.int32, (AT_ROWS, 1), 0) >> 7
    sink = jnp.zeros((AT_ROWS, 1), F32)
    for g in range(AT_GROUP):
        sink = jnp.where(head == g, sk_ref[0, g:g + 1, 0:1], sink)
    return sink, head


def _att_probs(qs, kcat, mask, sink):
    s = lax.dot_general(qs, kcat, (((1,), (1,)), ((), ())), preferred_element_type=F32) * AT_SCALE
    s = jnp.where(mask, s, NEG_INF)
    m = jnp.maximum(jnp.max(s, axis=-1, keepdims=True), sink)
    p = jnp.exp(s - m)
    es = jnp.exp(sink - m)
    inv = 1.0 / (jnp.sum(p, axis=-1, keepdims=True) + es)
    return p * inv, es * inv


def attention_fwd(qkv, bias, sinks_b, cos2, sin2):
    s_len = qkv.shape[0]
    nb = s_len // AT_BLK

    def body(q_ref, kc_ref, kp_ref, vc_ref, vp_ref, bq_ref, bk_ref, bv_ref, sk_ref, cc_ref, sc_ref, cp_ref, sp_ref,
             o_ref):
        i = pl.program_id(1)
        q, kcat, vcat, mask, cosc, sinc = _att_load(i, q_ref, kc_ref, kp_ref, vc_ref, vp_ref, bq_ref, bk_ref,
                                                    bv_ref, cc_ref, sc_ref, cp_ref, sp_ref)
        sink, _ = _att_cols(sk_ref)
        probs, _ = _att_probs(_att_stack(q, cosc, sinc).astype(BF16), kcat, mask, sink)
        o = jnp.dot(probs.astype(BF16), vcat, preferred_element_type=F32).astype(BF16)
        for g in range(AT_GROUP):
            o_ref[:, g * AT_HD:(g + 1) * AT_HD] = o[g * AT_BLK:(g + 1) * AT_BLK, :]

    return pl.pallas_call(
        body, name="attention_fwd", grid=(AT_KVH, nb),
        in_specs=_att_specs(),
        out_specs=pl.BlockSpec((AT_BLK, AT_QW), lambda h, i: (i, h)),
        out_shape=jax.ShapeDtypeStruct((s_len, AT_KVH * AT_QW), BF16),
        compiler_params=_cparams(("parallel", "parallel")),
    )(qkv, qkv, qkv, qkv, qkv, bias, bias, bias, sinks_b, cos2, sin2, cos2, sin2)


def attention_bwd(qkv, bias, sinks_b, cos2, sin2, d_o, rider=None):
    s_len = qkv.shape[0]
    nb = s_len // AT_BLK
    x_in, x_out, x_scr = _rider_parts(rider)
    ni, no = len(x_in), len(x_out)

    def body(*refs):
        (q_ref, kc_ref, kp_ref, vc_ref, vp_ref, bq_ref, bk_ref, bv_ref, sk_ref, cc_ref, sc_ref, cp_ref, sp_ref,
         do_ref) = refs[:14]
        dq_ref, dk_ref, dv_ref, dsk_ref = refs[14 + ni:18 + ni]
        ride = (refs[14:14 + ni], refs[18 + ni:18 + ni + no], *refs[18 + ni + no:])
        i = pl.program_id(1)
        if rider is not None:
            @pl.when((pl.program_id(0) == 0) & (i == 0))
            def _():
                rider["start"](*ride)

        _acc_init(i == 0, dsk_ref)
        q, kcat, vcat, mask, cosc, sinc = _att_load(i, q_ref, kc_ref, kp_ref, vc_ref, vp_ref, bq_ref, bk_ref,
                                                    bv_ref, cc_ref, sc_ref, cp_ref, sp_ref)
        sink, head = _att_cols(sk_ref)
        qs = _att_stack(q, cosc, sinc).astype(BF16)
        probs, psink = _att_probs(qs, kcat, mask, sink)
        pb = probs.astype(BF16)
        do_s = jnp.concatenate([do_ref[:, g * AT_HD:(g + 1) * AT_HD] for g in range(AT_GROUP)], axis=0)
        do_f = do_s.astype(F32)
        do_b = do_s.astype(BF16)
        o_s = jnp.dot(pb, vcat, preferred_element_type=F32)
        dsum = jnp.sum(do_f * o_s, axis=-1, keepdims=True)
        dp = lax.dot_general(do_b, vcat, (((1,), (1,)), ((), ())), preferred_element_type=F32)
        ds = (probs * (dp - dsum) * AT_SCALE).astype(BF16)
        dv_cat = lax.dot_general(pb, do_b, (((0,), (0,)), ((), ())), preferred_element_type=F32)
        dk_cat = lax.dot_general(ds, qs, (((0,), (0,)), ((), ())), preferred_element_type=F32)
        dq_s = jnp.dot(ds, kcat, preferred_element_type=F32)
        lane = lax.broadcasted_iota(jnp.int32, (1, AT_HD), 1)
        dsk = jnp.zeros((1, AT_HD), F32)
        sink_term = psink * dsum
        for g in range(AT_GROUP):
            rows = slice(g * AT_BLK, (g + 1) * AT_BLK)
            dq_ref[:, g * AT_HD:(g + 1) * AT_HD] = _rope_t(dq_s[rows, :], cosc, sinc).astype(BF16)
            dsk = dsk + jnp.where(lane == g, -jnp.sum(sink_term[rows, :], axis=0, keepdims=True), 0.0)
        dsk_ref[0] += dsk
        cur = pl.ds(pl.multiple_of(i * AT_BLK, AT_BLK), AT_BLK)
        dk_ref[cur, :] = _rope_t(dk_cat[AT_BLK:], cosc, sinc)
        dv_ref[cur, :] = dv_cat[AT_BLK:]

        @pl.when(i > 0)
        def _():
            prv = pl.ds(pl.multiple_of((i - 1) * AT_BLK, AT_BLK), AT_BLK)
            dk_ref[prv, :] += _rope_t(dk_cat[:AT_BLK], cp_ref[...], sp_ref[...])
            dv_ref[prv, :] += dv_cat[:AT_BLK]

        if rider is not None:
            @pl.when((pl.program_id(0) == AT_KVH - 1) & (i == nb - 1))
            def _():
                rider["finish"](*ride)

    kv_out = pl.BlockSpec((s_len, AT_HD), lambda h, i: (0, h))
    return pl.pallas_call(
        body, name="attention_bwd", grid=(AT_KVH, nb),
        in_specs=_att_specs() + [pl.BlockSpec((AT_BLK, AT_QW), lambda h, i: (i, h))] + [ANY] * ni,
        out_specs=[pl.BlockSpec((AT_BLK, AT_QW), lambda h, i: (i, h)), kv_out, kv_out,
                   pl.BlockSpec((1, 1, AT_HD), lambda h, i: (h, 0, 0))] + [ANY] * no,
        out_shape=[jax.ShapeDtypeStruct((s_len, AT_KVH * AT_QW), BF16),
                   jax.ShapeDtypeStruct((s_len, AT_KVH * AT_HD), F32),
                   jax.ShapeDtypeStruct((s_len, AT_KVH * AT_HD), F32),
                   jax.ShapeDtypeStruct((AT_KVH, 1, AT_HD), F32)] + x_out,
        scratch_shapes=x_scr,
        compiler_params=_cparams(("arbitrary", "arbitrary")),
    )(qkv, qkv, qkv, qkv, qkv, bias, bias, bias, sinks_b, cos2, sin2, cos2, sin2, d_o, *x_in)


ATT_QKV = 6144
RW_SHIFT = 13024
RW_PAD = 13056
N_CHIPS = 4
D_GATE = 480
ROPE_THETA = 10000.0


def perm_cols(a):
    lead = a.shape[:-1]
    return jnp.swapaxes(a.reshape(lead + (RW_H, RW_N)), -1, -2).reshape(lead + (RW_C,))


def rw_reorder(a, pad_value=0):
    r, k, v = (perm_cols(a[..., i * RW_C:(i + 1) * RW_C]) for i in range(3))
    wd = a[..., 3 * RW_C:3 * RW_C + 128]
    ad = a[..., 3 * RW_C + 128:3 * RW_C + 256]
    gd = a[..., 3 * RW_C + 256:]
    pad = jnp.full(a.shape[:-1] + (512 - D_GATE,), pad_value, a.dtype)
    return jnp.concatenate([r, k, v, gd, pad, wd, ad], axis=-1)


def rw_restore(a):
    r, k, v = (perm_cols(a[..., i * RW_C:(i + 1) * RW_C]) for i in range(3))
    gd = a[..., 3 * RW_C:3 * RW_C + D_GATE]
    wd = a[..., 3 * RW_C + 512:3 * RW_C + 640]
    ad = a[..., 3 * RW_C + 640:3 * RW_C + 768]
    return jnp.concatenate([r, k, v, wd, ad, gd], axis=-1)


W_IN_PIECES = (ATT_QKV, RW_C, RW_C, RW_C, 128, 128, D_GATE, 2 * RW_C)


def pieces_to_blocks(pieces, n_blocks):
    bw = sum(p.shape[1] for p in pieces) // n_blocks
    blocks = []
    for s in range(n_blocks):
        parts, off = [], 0
        for p in pieces:
            lo, hi = max(s * bw, off), min((s + 1) * bw, off + p.shape[1])
            if lo < hi:
                parts.append(p[:, lo - off:hi - off])
            off += p.shape[1]
        blocks.append(jnp.concatenate(parts, axis=1))
    return jnp.stack(blocks)


def blocks_to_pieces(g4, widths):
    bw = g4.shape[2]
    pieces, off = [], 0
    for w in widths:
        parts = []
        for s in range(g4.shape[0]):
            lo, hi = max(off, s * bw), min(off + w, (s + 1) * bw)
            if lo < hi:
                parts.append(g4[s][:, lo - s * bw:hi - s * bw])
        pieces.append(parts[0] if len(parts) == 1 else jnp.concatenate(parts, axis=1))
        off += w
    return pieces


def to_tiles(a):
    t = a.reshape(a.shape[0], RW_N, RW_H)
    return jnp.concatenate([t, t], axis=-1)


def from_tiles(t):
    return t[:, :, :RW_H].reshape(t.shape[0], RW_C)


def rope_tables(s_len):
    pos = jnp.arange(s_len, dtype=F32)
    inv_freq = ROPE_THETA ** (-jnp.arange(0, AT_HD, 2, dtype=F32) / AT_HD)
    ang = pos[:, None] * inv_freq[None, :]
    cos, sin = jnp.cos(ang), jnp.sin(ang)
    return jnp.concatenate([cos, cos], axis=1), jnp.concatenate([-sin, sin], axis=1)


def local_step(x, tgt, small, big, fwd_rider, got_early, att_rider, bwd_rider, late_riders):
    s_len, d = x.shape
    w_qkv, p_r, p_k, p_v, p_wd, p_ad, p_gd, w_gate = blocks_to_pieces(big["w_in"], W_IN_PIECES)
    w_rw = jnp.concatenate([perm_cols(p_r), perm_cols(p_k), perm_cols(p_v), p_gd,
                            jnp.zeros((d, 512 - D_GATE), BF16), p_wd, p_ad], axis=1)
    w2 = perm_cols(big["w2"])
    a2 = perm_cols(big["a2"])
    g2 = jnp.pad(perm_cols(big["g2"]), ((0, 512 - D_GATE), (0, 0)))
    mu = rw_reorder(small["mu_shift"])
    w0, a0, k_k, k_a, ln_w, ln_b = (perm_cols(small[n]) for n in ("w0", "a0", "k_k", "k_a", "ln_x_w", "ln_x_b"))
    r_k = small["r_k"].reshape(RW_H, RW_N).T.reshape(1, RW_C)
    sinks_b = jnp.broadcast_to(small["att_sinks"].reshape(AT_KVH, AT_GROUP, 1), (AT_KVH, AT_GROUP, AT_HD))
    cos2, sin2 = rope_tables(s_len)
    bias = small["b_qkv"]

    h = rms_fwd(x, small["norm_mix_pre"])
    qkv = matmul(h, w_qkv, name="mm_qkv")
    prw = matmul(h, w_rw, name="mm_rw")
    gate = matmul(h, w_gate, out_dtype=BF16, name="mm_gate")
    o_att = attention_fwd(qkv, bias, sinks_b, cos2, sin2)
    pre_params = (mu, w0, a0, k_k, k_a, w2, a2, g2)
    r, dec, k, v, av, bv, g, tw, xa, sg = rwkv_pre_fwd(prw, *pre_params)
    v3 = to_tiles(v)
    y3, hist, sa3, *arrived = rwkv_scan_fwd(r, dec, k, av, bv, v3, rider=fwd_rider)
    big = {**big, **got_early(arrived)}
    w_rb = big["w_rwkv_branch"].reshape(RW_H, RW_N, d).swapaxes(0, 1).reshape(RW_C, d)
    y = from_tiles(y3)
    o_rw = rwkv_post_fwd(y, r, k, v, g, ln_w, ln_b, r_k)
    ab = matmul(o_att, big["w_att_branch"], out_dtype=BF16, name="mm_ab")
    rb = matmul(o_rw, w_rb, out_dtype=BF16, name="mm_rb")
    merged = merge_fwd(gate, ab, rb)
    m2 = matmul(merged, big["w_out"], name="mm_out")
    x1, h2 = resid_norm_fwd(x, m2, small["norm_mix_post"], small["norm_ffn_pre"])
    gg = matmul(h2, big["w_ffn_gate"], out_dtype=BF16, name="mm_fg")
    uu = matmul(h2, big["w_ffn_up"], out_dtype=BF16, name="mm_fu")
    act = swiglu_fwd(gg, uu)
    ff = matmul(act, big["w_ffn_down"], name="mm_fd")
    loss, dy, dff, d_nfp = loss_head(x1, ff, tgt, small["norm_ffn_post"])

    dact = matmul(dff, big["w_ffn_down"], tb=True, out_dtype=BF16, name="mm_dact")
    g_fd = matmul(act, dff, ta=True, out_dtype=BF16, name="mm_gfd")
    dgg, duu = swiglu_bwd(gg, uu, dact)
    g_fg = matmul(h2, dgg, ta=True, out_dtype=BF16, name="mm_gfg")
    g_fu = matmul(h2, duu, ta=True, out_dtype=BF16, name="mm_gfu")
    dh2a = matmul(dgg, big["w_ffn_gate"], tb=True, out_dtype=BF16, name="mm_dh2a")
    dh2b = matmul(duu, big["w_ffn_up"], tb=True, out_dtype=BF16, name="mm_dh2b")
    dx1, dm2, d_nfpre, d_nmpost = resid_norm_bwd(x1, dh2a, dh2b, small["norm_ffn_pre"], m2, small["norm_mix_post"], dy)
    dmerged = matmul(dm2, big["w_out"], tb=True, out_dtype=BF16, name="mm_dmerged")
    g_out = matmul(merged, dm2, ta=True, out_dtype=BF16, name="mm_gout")
    dgate, dab, drb = merge_bwd(gate, ab, rb, dmerged)
    do_att = matmul(dab, big["w_att_branch"], tb=True, out_dtype=BF16, name="mm_doatt")
    g_ab = matmul(o_att, dab, ta=True, out_dtype=BF16, name="mm_gab")
    do_rw = matmul(drb, w_rb, tb=True, out_dtype=BF16, name="mm_dorw")
    g_rb = matmul(o_rw, drb, ta=True, out_dtype=BF16, name="mm_grb")
    early = {"w_att_branch": g_ab, "w_rwkv_branch": g_rb.reshape(RW_N, RW_H, d).swapaxes(0, 1).reshape(RW_C, d),
             "w_out": g_out, "w_ffn_gate": g_fg, "w_ffn_up": g_fu, "w_ffn_down": g_fd}
    dq, dk_att, dv_att, dsk, *from_sibling = attention_bwd(qkv, bias, sinks_b, cos2, sin2, do_att,
                                                           rider=att_rider(early))
    dqkv = jnp.concatenate([dq, dk_att.astype(BF16), dv_att.astype(BF16)], axis=1)
    dy_s, dr_p, dk_p, dv_p, dg, d_lnw, d_lnb, d_rk = rwkv_post_bwd(y, r, k, v, g, ln_w, ln_b, r_k, do_rw)
    dr_s, ddec, dk_s, dav, dbv, dv3, *from_chips = rwkv_scan_bwd(r, dec, k, av, bv, v3, hist, sa3, to_tiles(dy_s),
                                                                  rider=bwd_rider(from_sibling))
    dprw, dzw, dza, dmu, dw0, da0, dkk, dka = rwkv_pre_bwd(
        prw, *pre_params, dr_p, ddec, dk_p, dv_p, dav, dbv, dg, dr_s, dk_s, from_tiles(dv3))
    g_w2 = matmul(tw, dzw, ta=True, out_dtype=BF16, name="mm_gw2")
    g_a2 = matmul(xa, dza, ta=True, out_dtype=BF16, name="mm_ga2")
    g_g2 = matmul(sg, dg.astype(BF16), ta=True, out_dtype=BF16, name="mm_gg2")
    g_qkv = matmul(h, dqkv, ta=True, out_dtype=BF16, name="mm_gqkv")
    g_rw = matmul(h, dprw, ta=True, out_dtype=BF16, name="mm_grw")
    g_gate = matmul(h, dgate, ta=True, out_dtype=BF16, name="mm_ggate")
    g_r, g_k, g_v = (perm_cols(g_rw[:, i * RW_C:(i + 1) * RW_C]) for i in range(3))
    gbig = {
        "w_in": pieces_to_blocks([g_qkv, g_r, g_k, g_v, g_rw[:, 3 * RW_C + 512:3 * RW_C + 640],
                                  g_rw[:, 3 * RW_C + 640:3 * RW_C + 768], g_rw[:, 3 * RW_C:3 * RW_C + D_GATE], g_gate],
                                 N_CHIPS),
        "w2": perm_cols(g_w2), "a2": perm_cols(g_a2), "g2": perm_cols(g_g2)[:D_GATE],
    }
    ride_a, ride_b, ride_c = late_riders(gbig)
    dh_a, *late_a = matmul(dqkv, w_qkv, tb=True, out_dtype=BF16, name="mm_dha", rider=ride_a)
    dh_b, *late_b = matmul(dprw, w_rw, tb=True, out_dtype=BF16, name="mm_dhb", rider=ride_b)
    dh_c, *late_c = matmul(dgate, w_gate, tb=True, out_dtype=BF16, name="mm_dhc", rider=ride_c)
    grad_x, d_nmpre = rms_bwd(x, small["norm_mix_pre"], dh_a, dh_b, dh_c, dx1)
    d_bias = colsum(dqkv)

    gsmall = {
        "norm_mix_pre": d_nmpre, "norm_mix_post": d_nmpost, "norm_ffn_pre": d_nfpre, "norm_ffn_post": d_nfp,
        "b_qkv": d_bias, "att_sinks": dsk[:, 0, :AT_GROUP].reshape(1, AT_KVH * AT_GROUP),
        "mu_shift": rw_restore(dmu), "w0": perm_cols(dw0), "a0": perm_cols(da0), "k_k": perm_cols(dkk),
        "k_a": perm_cols(dka), "r_k": d_rk.reshape(RW_N, RW_H).T.reshape(1, RW_C),
        "ln_x_w": perm_cols(d_lnw), "ln_x_b": perm_cols(d_lnb),
    }
    return loss, grad_x, gsmall, from_chips, (late_a, late_b, late_c)


def _place():
    x, y, c = lax.axis_index("x"), lax.axis_index("y"), lax.axis_index("c")
    chips = [(1 - x, y), (x, 1 - y), (1 - x, 1 - y)]
    return x, y, c, chips


def _remote(src, dst, send_sems, recv_sems, k, dev):
    return pltpu.make_async_remote_copy(src_ref=src, dst_ref=dst, send_sem=send_sems.at[k], recv_sem=recv_sems.at[k],
                                        device_id=dev, device_id_type=MESH)


GATHER_CHUNKS = 4


def _gather_parts(n):
    kc = GATHER_CHUNKS

    def pieces(ref, which):
        hr = ref.shape[0] // 2
        k = kc if hr % (16 * kc) == 0 else 1
        return [ref.at[pl.ds(which * hr + q * (hr // k), hr // k), :] for q in range(k)]

    def sends(ins, outs, send_sems, recv_sems):
        x, y, c, chips = _place()
        me = 2 * x + y
        return [_remote(src, dst, send_sems, recv_sems, kc * (6 * i + j) + q, (*chip, c))
                for i in range(n) for j, chip in enumerate(chips)
                for q, (src, dst) in enumerate(zip(pieces(ins[i], c), pieces(outs[i].at[me], c)))]

    def start(ins, outs, send_sems, recv_sems):
        for cp in sends(ins, outs, send_sems, recv_sems):
            cp.start()

    def finish(ins, outs, send_sems, recv_sems):
        x, y, c, chips = _place()
        sib = (x, y, 1 - c)
        passed = []
        for i in range(n):
            for j, chip in enumerate(chips):
                for q, got in enumerate(pieces(outs[i].at[2 * chip[0] + chip[1]], c)):
                    _remote(got, got, send_sems, recv_sems, kc * (6 * i + j) + q, sib).wait_recv()
                    cp = _remote(got, got, send_sems, recv_sems, kc * (6 * i + 3 + j) + q, sib)
                    cp.start()
                    passed.append(cp)
        for i in range(n):
            for j, chip in enumerate(chips):
                for q, got in enumerate(pieces(outs[i].at[2 * chip[0] + chip[1]], 1 - c)):
                    _remote(got, got, send_sems, recv_sems, kc * (6 * i + 3 + j) + q, sib).wait_recv()
        for cp in sends(ins, outs, send_sems, recv_sems) + passed:
            cp.wait_send()

    return start, finish


def gather_rider(shards):
    n = len(shards)
    start, finish = _gather_parts(n)
    return {"ins": shards, "out_shapes": [jax.ShapeDtypeStruct((4,) + s.shape, s.dtype) for s in shards],
            "scratch": [pltpu.SemaphoreType.DMA((6 * n * GATHER_CHUNKS,)), pltpu.SemaphoreType.DMA((6 * n * GATHER_CHUNKS,))],
            "start": start, "finish": finish}


def gather_weights(shards):
    n = len(shards)
    start, finish = _gather_parts(n)

    def body(*refs):
        parts = (refs[:n], refs[n:2 * n], *refs[2 * n:])
        start(*parts)
        finish(*parts)

    return pl.pallas_call(
        body, name="gather_weights",
        in_specs=[ANY] * n, out_specs=[ANY] * n,
        out_shape=[jax.ShapeDtypeStruct((4,) + s.shape, s.dtype) for s in shards],
        scratch_shapes=[pltpu.SemaphoreType.DMA((6 * n * GATHER_CHUNKS,)), pltpu.SemaphoreType.DMA((6 * n * GATHER_CHUNKS,))],
    )(*shards)


def swap_with_sibling(blocks, name):
    n = len(blocks)

    def body(*refs):
        ins, outs = refs[:n], refs[n:2 * n]
        send_sems, recv_sems = refs[2 * n:]
        x, y, c, _ = _place()
        cps = [_remote(ins[i], outs[i], send_sems, recv_sems, i, (x, y, 1 - c)) for i in range(n)]
        for cp in cps:
            cp.start()
        for cp in cps:
            cp.wait()

    return pl.pallas_call(
        body, name=name, in_specs=[ANY] * n, out_specs=[ANY] * n,
        out_shape=[jax.ShapeDtypeStruct(b.shape, b.dtype) for b in blocks],
        scratch_shapes=[pltpu.SemaphoreType.DMA((n,)), pltpu.SemaphoreType.DMA((n,))],
    )(*blocks)


def _scatter_parts(n, rows=None):
    def piece(ref, i):
        return ref if rows is None or rows[i] is None else ref.at[pl.ds(rows[i][0], rows[i][1]), :]

    def copies(ins, outs, send_sems, recv_sems):
        x, y, c, chips = _place()
        return [_remote(piece(ins[i].at[2 * chip[0] + chip[1]], i), outs[i].at[j], send_sems, recv_sems, 3 * i + j,
                        (*chip, c))
                for i in range(n) for j, chip in enumerate(chips)]

    def start(*refs):
        for cp in copies(*refs):
            cp.start()

    def finish(*refs):
        for cp in copies(*refs):
            cp.wait()

    return start, finish


def scatter_rider(parts, rows=None):
    n = len(parts)
    start, finish = _scatter_parts(n, rows)
    nrows = [p.shape[1] if rows is None or rows[i] is None else rows[i][1] for i, p in enumerate(parts)]
    return {"ins": parts, "out_shapes": [jax.ShapeDtypeStruct((3, r, p.shape[2]), p.dtype) for p, r in zip(parts, nrows)],
            "scratch": [pltpu.SemaphoreType.DMA((3 * n,)), pltpu.SemaphoreType.DMA((3 * n,))],
            "start": start, "finish": finish}


def allreduce_small(v):
    rows = v.shape[0]

    def body(v_ref, o_ref, buf, send_sems, recv_sems):
        x, y, c, chips = _place()
        me, sib = (x, y, c), (x, y, 1 - c)

        def slot(px, py, pc):
            return buf.at[4 * px + 2 * py + pc]

        def copy(k, block, to, src=None):
            return _remote(slot(*block) if src is None else src, slot(*block), send_sems, recv_sems, k, to)

        buf[4 * x + 2 * y + c] = v_ref[...]
        first = [copy(0, me, sib, src=v_ref)]
        first += [copy(1 + j, me, (*chip, c), src=v_ref) for j, chip in enumerate(chips)]
        for cp in first:
            cp.start()
        passed = [copy(4 + j, (*chip, c), sib) for j, chip in enumerate(chips)]
        for j, chip in enumerate(chips):
            copy(1 + j, (*chip, c), me).wait_recv()
            passed[j].start()
        copy(0, sib, me).wait_recv()
        for j, chip in enumerate(chips):
            copy(4 + j, (*chip, 1 - c), me).wait_recv()
        for cp in first + passed:
            cp.wait_send()
        acc = buf[0]
        for k in range(1, 8):
            acc = acc + buf[k]
        o_ref[...] = acc

    vm = pl.BlockSpec(memory_space=pltpu.VMEM)
    return pl.pallas_call(
        body, name="allreduce_small", in_specs=[vm], out_specs=vm,
        out_shape=jax.ShapeDtypeStruct(v.shape, F32),
        scratch_shapes=[pltpu.VMEM((8, rows, LANES), F32), pltpu.SemaphoreType.DMA((7,)),
                        pltpu.SemaphoreType.DMA((7,))],
    )(v)


def _rows_tile(r, most=64):
    return _pick(r, tuple(t for t in (256, 128, 64, 32, 16, 8) if t <= most))


def _swap_parts(n):
    def copies(ins, outs, send_sems, recv_sems):
        x, y, c, _ = _place()
        return [_remote(ins[i].at[:, pl.ds((1 - c) * (ins[i].shape[1] // 2), ins[i].shape[1] // 2), :], outs[i],
                        send_sems, recv_sems, i, (x, y, 1 - c)) for i in range(n)]

    def start(*refs):
        for cp in copies(*refs):
            cp.start()

    def finish(*refs):
        for cp in copies(*refs):
            cp.wait()

    return start, finish


def swap_rider(blocks):
    n = len(blocks)
    start, finish = _swap_parts(n)
    return {"ins": blocks, "out_shapes": [jax.ShapeDtypeStruct((4, b.shape[1] // 2, b.shape[2]), b.dtype) for b in blocks],
            "scratch": [pltpu.SemaphoreType.DMA((n,)), pltpu.SemaphoreType.DMA((n,))], "start": start, "finish": finish}


def swap_halves(blocks, name):
    n = len(blocks)
    start, finish = _swap_parts(n)

    def body(*refs):
        parts = (refs[:n], refs[n:2 * n], *refs[2 * n:])
        start(*parts)
        finish(*parts)

    return pl.pallas_call(
        body, name=name, in_specs=[ANY] * n, out_specs=[ANY] * n,
        out_shape=[jax.ShapeDtypeStruct((4, b.shape[1] // 2, b.shape[2]), b.dtype) for b in blocks],
        scratch_shapes=[pltpu.SemaphoreType.DMA((n,)), pltpu.SemaphoreType.DMA((n,))],
    )(*blocks)


def add_pairs(g4, b, core):
    _, r, c = b.shape
    tr = _rows_tile(r, 256)
    nrt = r // tr

    def body(core_ref, a_ref, b_ref, o_ref):
        o_ref[...] = (a_ref[...].astype(F32) + b_ref[...].astype(F32)).astype(o_ref.dtype)

    spec = pl.BlockSpec((1, tr, c), lambda s, i, core_ref: (s, i, 0))
    return pl.pallas_call(
        body, name="add_pairs",
        grid_spec=pltpu.PrefetchScalarGridSpec(
            num_scalar_prefetch=1, grid=(4, nrt),
            in_specs=[pl.BlockSpec((1, tr, c), lambda s, i, core_ref: (s, core_ref[0] * nrt + i, 0)), spec],
            out_specs=spec),
        out_shape=jax.ShapeDtypeStruct(b.shape, b.dtype), compiler_params=_cparams(("parallel", "parallel")),
    )(core, g4, b)


def add_four(mine, others):
    r, c = mine.shape
    tr = _rows_tile(r, 128)

    def body(m_ref, o_ref, out_ref):
        acc = m_ref[...].astype(F32)
        for j in range(3):
            acc = acc + o_ref[j].astype(F32)
        out_ref[...] = acc

    return pl.pallas_call(
        body, name="add_four", grid=(r // tr,),
        in_specs=[pl.BlockSpec((tr, c), lambda i: (i, 0)), pl.BlockSpec((3, tr, c), lambda i: (0, i, 0))],
        out_specs=pl.BlockSpec((tr, c), lambda i: (i, 0)),
        out_shape=jax.ShapeDtypeStruct((r, c), F32), compiler_params=_cparams(("parallel",)),
    )(mine, others)


def pair_sums(blocks, from_sibling=None):
    core = lax.axis_index("c").astype(jnp.int32).reshape(1)
    if from_sibling is None:
        from_sibling = swap_halves(blocks, "swap_halves_late")
    return [add_pairs(g4, b, core) for g4, b in zip(blocks, from_sibling)]


def owner_sums(pair, from_chips):
    cx, cy, cc = lax.axis_index("x"), lax.axis_index("y"), lax.axis_index("c")
    me = 2 * cx + cy
    sums = [add_four(lax.dynamic_index_in_dim(p, me, 0, keepdims=False), t) for p, t in zip(pair, from_chips)]
    got = swap_with_sibling(sums, "swap_sums")
    return [jnp.concatenate([jnp.where(cc == 0, s, g), jnp.where(cc == 0, g, s)], axis=0) for s, g in zip(sums, got)]


ADAM_LR = 0.001
ADAM_B1 = 0.9
ADAM_B2 = 0.999
ADAM_EPS = 1e-08
ADAM_WD = 0.01
ADAM_STEP = 10


def adamw(w, g, m, v):
    r, c = w.shape
    tr = _rows_tile(r)
    spec = pl.BlockSpec((tr, c), lambda i: (i, 0))

    def body(w_ref, g_ref, m_ref, v_ref, d_ref, nm_ref, nv_ref):
        gv = g_ref[...]
        nm = ADAM_B1 * m_ref[...] + (1.0 - ADAM_B1) * gv
        nv = ADAM_B2 * v_ref[...] + (1.0 - ADAM_B2) * jnp.square(gv)
        m_hat = nm / (1.0 - ADAM_B1 ** ADAM_STEP)
        v_hat = nv / (1.0 - ADAM_B2 ** ADAM_STEP)
        d_ref[...] = -ADAM_LR * (m_hat / (jnp.sqrt(v_hat) + ADAM_EPS) + ADAM_WD * w_ref[...])
        nm_ref[...] = nm
        nv_ref[...] = nv

    out = jax.ShapeDtypeStruct((r, c), F32)
    return pl.pallas_call(
        body, name="adamw", grid=(r // tr,), in_specs=[spec] * 4, out_specs=[spec] * 3, out_shape=[out] * 3,
        compiler_params=_cparams(("parallel",)),
    )(w, g, m, v)


WEIGHTS = ["norm_mix_pre", "norm_mix_post", "norm_ffn_pre", "norm_ffn_post", "w_in", "b_qkv", "att_sinks", "mu_shift",
           "w0", "w2", "a0", "a2", "g2", "k_k", "k_a", "r_k", "ln_x_w", "ln_x_b", "w_att_branch", "w_rwkv_branch",
           "w_out", "w_ffn_gate", "w_ffn_up", "w_ffn_down"]
BIG = {"w_in": 1, "w2": 1, "a2": 1, "g2": 1, "w_att_branch": 0, "w_rwkv_branch": 0, "w_out": 0, "w_ffn_gate": 1,
       "w_ffn_up": 1, "w_ffn_down": 0}
SMALL = [n for n in WEIGHTS if n not in BIG]
LATE_CUTS = (13, 45)
LATE = ("w_in", "w2", "a2", "g2")


def _whole(g4, axis):
    if axis == 0:
        return g4.reshape(g4.shape[0] * g4.shape[1], g4.shape[2])
    return jnp.swapaxes(g4, 0, 1).reshape(g4.shape[1], g4.shape[0] * g4.shape[2])


def _by_shard(w, axis):
    if axis == 0:
        return w.reshape(N_CHIPS, w.shape[0] // N_CHIPS, w.shape[1])
    return jnp.swapaxes(w.reshape(w.shape[0], N_CHIPS, w.shape[1] // N_CHIPS), 0, 1)


def _pack(parts):
    flat = jnp.concatenate([parts[n].reshape(-1) for n in SMALL])
    rows = -(-flat.shape[0] // (LANES * SUBLANES)) * SUBLANES
    return jnp.pad(flat, (0, rows * LANES - flat.shape[0])).reshape(rows, LANES)


def _unpack(packed, like):
    flat = packed.reshape(-1)
    out, off = {}, 0
    for n in SMALL:
        size = like[n].size
        out[n] = flat[off:off + size].reshape(like[n].shape)
        off += size
    return out


def kernel(x, norm_mix_pre, norm_mix_post, norm_ffn_pre, norm_ffn_post, w_in, b_qkv, att_sinks, mu_shift, w0, w2, a0, a2, g2, k_k, k_a, r_k, ln_x_w, ln_x_b, w_att_branch, w_rwkv_branch, w_out, w_ffn_gate, w_ffn_up, w_ffn_down, loss_target, m_norm_mix_pre, m_norm_mix_post, m_norm_ffn_pre, m_norm_ffn_post, m_w_in, m_b_qkv, m_att_sinks, m_mu_shift, m_w0, m_w2, m_a0, m_a2, m_g2, m_k_k, m_k_a, m_r_k, m_ln_x_w, m_ln_x_b, m_w_att_branch, m_w_rwkv_branch, m_w_out, m_w_ffn_gate, m_w_ffn_up, m_w_ffn_down, v_norm_mix_pre, v_norm_mix_post, v_norm_ffn_pre, v_norm_ffn_post, v_w_in, v_b_qkv, v_att_sinks, v_mu_shift, v_w0, v_w2, v_a0, v_a2, v_g2, v_k_k, v_k_a, v_r_k, v_ln_x_w, v_ln_x_b, v_w_att_branch, v_w_rwkv_branch, v_w_out, v_w_ffn_gate, v_w_ffn_up, v_w_ffn_down):
    given = dict(locals())
    wts = {n: given[n] for n in WEIGHTS}
    mom1 = {n: given["m_" + n] for n in WEIGHTS}
    mom2 = {n: given["v_" + n] for n in WEIGHTS}
    early = [n for n in BIG if n not in LATE]
    me = 2 * lax.axis_index("x") + lax.axis_index("y")
    own = {n: wts[n][0].astype(BF16) for n in BIG}

    def placed(names, gathered):
        return {n: (lambda g4: g4 if n == "w_in" else _whole(g4, BIG[n]))(
            lax.dynamic_update_index_in_dim(g4, own[n], me, 0)) for n, g4 in zip(names, gathered)}

    small = {n: wts[n].reshape(1, -1) for n in SMALL}
    pairs = {}

    def att_rider(grads_early):
        pairs["blocks"] = [_by_shard(grads_early[n], BIG[n]) for n in early]
        return swap_rider(pairs["blocks"])

    def bwd_rider(from_sibling):
        pairs["early"] = pair_sums(pairs["blocks"], from_sibling)
        return scatter_rider(pairs["early"])

    def late_riders(grads_late):
        pairs["late"] = pair_sums([grads_late[n] if n == "w_in" else _by_shard(grads_late[n], BIG[n]) for n in LATE])
        half = pairs["late"][0].shape[1]
        cuts = [0] + [half * f // 64 // 16 * 16 for f in LATE_CUTS] + [half]
        spans = [(cuts[i], cuts[i + 1] - cuts[i]) for i in range(3)]
        return (scatter_rider(pairs["late"], [spans[0], None, None, None]),
                scatter_rider(pairs["late"][:1], [spans[1]]), scatter_rider(pairs["late"][:1], [spans[2]]))

    loss, grad_x, gsmall, chips_early, (late_a, late_b, late_c) = local_step(
        x[0], loss_target[0], small, placed(LATE, gather_weights([own[n] for n in LATE])),
        gather_rider([own[n] for n in early]), lambda arrived: placed(early, arrived), att_rider, bwd_rider,
        late_riders)

    chips_late = [jnp.concatenate([late_a[0], late_b[0], late_c[0]], axis=1)] + list(late_a[1:])
    wholes = owner_sums(pairs["early"] + pairs["late"], list(chips_early) + chips_late)
    grads = dict(zip(early + list(LATE), wholes))
    names = list(BIG)

    gsum = _unpack(allreduce_small(_pack(gsmall)), small)

    outs_g, outs_d, outs_m, outs_v = {}, {}, {}, {}
    for n in names:
        d, nm, nv = adamw(wts[n][0], grads[n], mom1[n][0], mom2[n][0])
        outs_g[n], outs_d[n], outs_m[n], outs_v[n] = (t[None] for t in (grads[n], d, nm, nv))
    pk = lambda src: _pack({n: src[n] for n in SMALL})
    d, nm, nv = adamw(pk(wts), _pack(gsum), pk(mom1), pk(mom2))
    du, mu, vu = _unpack(d, small), _unpack(nm, small), _unpack(nv, small)
    for n in SMALL:
        outs_g[n], outs_d[n], outs_m[n], outs_v[n] = (t[n].reshape(wts[n].shape) for t in (gsum, du, mu, vu))

    total = lax.psum(loss[0, 0], ("x", "y", "c"))
    return (total, grad_x[None], *[outs_g[n] for n in WEIGHTS], *[outs_d[n] for n in WEIGHTS],
            *[outs_m[n] for n in WEIGHTS], *[outs_v[n] for n in WEIGHTS])
```
